```python
import math
import jax, jax.numpy as jnp
from jax import lax
import numpy as np

D_MODEL = 1024
BATCH = 2
SEQ = 8192
DEPTH = 2

GRID_W = 64
SSM_WIDTH = D_MODEL
SSM_GROUP = 16
SSM_GROUPS = SSM_WIDTH // SSM_GROUP
SSM_STATE = 64
SSM_DIRS = 2
LOG_STEP_MIN = math.log(1e-3)
LOG_STEP_MAX = math.log(1e-1)
N_HEADS = 16
N_KV_HEADS = 4
HEAD_DIM = D_MODEL // N_HEADS
Q_PER_KV = N_HEADS // N_KV_HEADS
ROPE_AXIS_DIM = HEAD_DIM // 2
ROPE_THETA = 10000.0
ATTN_BLOCK = 128
QKV_WIDTH = (N_HEADS + 2 * N_KV_HEADS) * HEAD_DIM
D_FF = ((8 * D_MODEL // 3 + 127) // 128) * 128
N_EXPERTS = 8
TOP_K = 2
D_FF_EXPERT = 7 * D_MODEL // 2
N_EVEN = (DEPTH + 1) // 2
N_ODD = DEPTH // 2
NORM_EPS = 1e-6

kernel_name = "hybrid_s5_axial_gqa_moe_encoder"


def rms_norm(x, gain):
    x32 = x.astype(jnp.float32)
    y = x32 * lax.rsqrt(jnp.mean(x32 * x32, axis=-1, keepdims=True) + NORM_EPS)
    return (y * gain.astype(jnp.float32)).astype(x.dtype)


def swiglu(h, w_gate, w_up, w_down):
    return (jax.nn.silu(h @ w_gate) * (h @ w_up)) @ w_down


def s5_mixer(h, w_in, lam_re, lam_im, log_step, b_re, b_im, c_re, c_im, d_skip, w_glu, w_out):
    bsz, seq, _ = h.shape
    u = h @ w_in
    uf = u.astype(jnp.float32).reshape(bsz, seq, SSM_GROUPS, SSM_GROUP)
    y = d_skip.astype(jnp.float32).reshape(SSM_GROUPS, SSM_GROUP) * uf

    def combine(e1, e2):
        a1r, a1i, b1r, b1i = e1
        a2r, a2i, b2r, b2i = e2
        return (a2r * a1r - a2i * a1i,
                a2r * a1i + a2i * a1r,
                a2r * b1r - a2i * b1i + b2r,
                a2r * b1i + a2i * b1r + b2i)

    for d in range(SSM_DIRS):
        lr = lam_re[d].astype(jnp.float32)
        li = lam_im[d].astype(jnp.float32)
        step = jnp.exp(log_step[d].astype(jnp.float32))[:, None]
        mag = jnp.exp(lr * step)
        ang = li * step
        ab_re = mag * jnp.cos(ang)
        ab_im = mag * jnp.sin(ang)
        nr, ni = ab_re - 1.0, ab_im
        den = lr * lr + li * li
        q_re = (nr * lr + ni * li) / den
        q_im = (ni * lr - nr * li) / den
        br = b_re[d].astype(jnp.float32)
        bi = b_im[d].astype(jnp.float32)
        bb_re = q_re[..., None] * br - q_im[..., None] * bi
        bb_im = q_re[..., None] * bi + q_im[..., None] * br
        bu_re = jnp.einsum('blgs,gps->blgp', uf, bb_re)
        bu_im = jnp.einsum('blgs,gps->blgp', uf, bb_im)
        a_re = jnp.broadcast_to(ab_re, bu_re.shape)
        a_im = jnp.broadcast_to(ab_im, bu_im.shape)
        _, _, xr, xi = lax.associative_scan(combine, (a_re, a_im, bu_re, bu_im),
                                            reverse=(d == 1), axis=1)
        y = y + (jnp.einsum('blgp,gsp->blgs', xr, c_re[d].astype(jnp.float32))
                 - jnp.einsum('blgp,gsp->blgs', xi, c_im[d].astype(jnp.float32)))
    y = y.reshape(bsz, seq, SSM_WIDTH).astype(h.dtype)
    g = jax.nn.gelu(y)
    g = g * jax.nn.sigmoid(g @ w_glu)
    return g @ w_out


def axial_rope_tables(seq):
    rows = seq // GRID_W
    freqs = ROPE_THETA ** (-jnp.arange(0, ROPE_AXIS_DIM, 2, dtype=jnp.float32) / ROPE_AXIS_DIM)
    n_f = freqs.shape[0]
    row_ang = jnp.arange(rows, dtype=jnp.float32)[:, None] * freqs
    col_ang = jnp.arange(GRID_W, dtype=jnp.float32)[:, None] * freqs
    ang = jnp.concatenate([
        jnp.broadcast_to(row_ang[:, None, :], (rows, GRID_W, n_f)),
        jnp.broadcast_to(col_ang[None, :, :], (rows, GRID_W, n_f))], axis=-1)
    ang = ang.reshape(seq, 2 * n_f)
    return jnp.cos(ang), jnp.sin(ang)


def apply_rope(x, cos, sin):
    x32 = x.astype(jnp.float32).reshape(*x.shape[:-1], HEAD_DIM // 2, 2)
    x1, x2 = x32[..., 0], x32[..., 1]
    c = cos[None, :, None, :]
    s = sin[None, :, None, :]
    out = jnp.stack([x1 * c - x2 * s, x1 * s + x2 * c], axis=-1)
    return out.reshape(x.shape).astype(x.dtype)


def axial_gqa_mixer(h, w_qkv, q_gain, k_gain, w_out):
    bsz, seq, _ = h.shape
    qkv = h @ w_qkv
    q = qkv[..., :N_HEADS * HEAD_DIM].reshape(bsz, seq, N_HEADS, HEAD_DIM)
    k = qkv[..., N_HEADS * HEAD_DIM:(N_HEADS + N_KV_HEADS) * HEAD_DIM].reshape(bsz, seq, N_KV_HEADS, HEAD_DIM)
    v = qkv[..., (N_HEADS + N_KV_HEADS) * HEAD_DIM:].reshape(bsz, seq, N_KV_HEADS, HEAD_DIM)
    q = rms_norm(q, q_gain)
    k = rms_norm(k, k_gain)
    cos, sin = axial_rope_tables(seq)
    q = apply_rope(q, cos, sin)
    k = apply_rope(k, cos, sin)
    n_blk = seq // ATTN_BLOCK
    qb = q.reshape(bsz, n_blk, ATTN_BLOCK, N_KV_HEADS, Q_PER_KV, HEAD_DIM).transpose(1, 0, 2, 3, 4, 5)
    scale = 1.0 / math.sqrt(HEAD_DIM)

    def attend(q_blk):
        s = jnp.einsum('bqkgd,bskd->bkgqs', q_blk, k, preferred_element_type=jnp.float32) * scale
        p = jax.nn.softmax(s, axis=-1).astype(v.dtype)
        return jnp.einsum('bkgqs,bskd->bqkgd', p, v)

    o = lax.map(attend, qb)
    o = o.transpose(1, 0, 2, 3, 4, 5).reshape(bsz, seq, N_HEADS * HEAD_DIM)
    return o @ w_out


def moe_swiglu(h, w_router, w_gate, w_up, w_down):
    bsz, seq, dm = h.shape
    hf = h.reshape(bsz * seq, dm)
    logits = (hf @ w_router).astype(jnp.float32)
    top_vals, top_idx = lax.top_k(logits, TOP_K)
    top_w = jax.nn.softmax(top_vals, axis=-1)
    gates = jnp.sum(jax.nn.one_hot(top_idx, N_EXPERTS, dtype=jnp.float32) * top_w[..., None], axis=1)
    gates = gates.astype(h.dtype)
    out = jnp.zeros_like(hf)
    for e in range(N_EXPERTS):
        out = out + gates[:, e:e + 1] * swiglu(hf, w_gate[e], w_up[e], w_down[e])
    return out.reshape(bsz, seq, dm)


def setup_inputs(seed: int = 0) -> dict:
    key = jax.random.key(seed)
    ks = iter(jax.random.split(key, 32))

    def nrm(shape, scale):
        return jax.random.normal(next(ks), shape, jnp.float32) * scale

    x = nrm((BATCH, SEQ, D_MODEL), 1.0)
    norm_gains = 1.0 + nrm((DEPTH, 4, D_MODEL), 0.1)
    ssm_w_in = nrm((N_EVEN, D_MODEL, SSM_WIDTH), D_MODEL ** -0.5)
    n_idx = jnp.arange(SSM_STATE, dtype=jnp.float32)
    ssm_lambda_re = -0.5 + nrm((N_EVEN, SSM_DIRS, SSM_GROUPS, SSM_STATE), 0.01)
    ssm_lambda_im = math.pi * n_idx + nrm((N_EVEN, SSM_DIRS, SSM_GROUPS, SSM_STATE), 0.01)
    ssm_log_step = jax.random.uniform(next(ks), (N_EVEN, SSM_DIRS, SSM_GROUPS), jnp.float32,
                                      LOG_STEP_MIN, LOG_STEP_MAX)
    b_scale = (2.0 * SSM_GROUP) ** -0.5
    c_scale = (2.0 * SSM_STATE) ** -0.5
    ssm_b_re = nrm((N_EVEN, SSM_DIRS, SSM_GROUPS, SSM_STATE, SSM_GROUP), b_scale)
    ssm_b_im = nrm((N_EVEN, SSM_DIRS, SSM_GROUPS, SSM_STATE, SSM_GROUP), b_scale)
    ssm_c_re = nrm((N_EVEN, SSM_DIRS, SSM_GROUPS, SSM_GROUP, SSM_STATE), c_scale)
    ssm_c_im = nrm((N_EVEN, SSM_DIRS, SSM_GROUPS, SSM_GROUP, SSM_STATE), c_scale)
    ssm_d = nrm((N_EVEN, SSM_WIDTH), 1.0)
    ssm_w_glu = nrm((N_EVEN, SSM_WIDTH, SSM_WIDTH), SSM_WIDTH ** -0.5)
    ssm_w_out = nrm((N_EVEN, SSM_WIDTH, D_MODEL), SSM_WIDTH ** -0.5)
    ffn_w_gate = nrm((N_EVEN, D_MODEL, D_FF), D_MODEL ** -0.5)
    ffn_w_up = nrm((N_EVEN, D_MODEL, D_FF), D_MODEL ** -0.5)
    ffn_w_down = nrm((N_EVEN, D_FF, D_MODEL), D_FF ** -0.5)
    attn_w_qkv = nrm((N_ODD, D_MODEL, QKV_WIDTH), D_MODEL ** -0.5)
    attn_q_gain = 1.0 + nrm((N_ODD, HEAD_DIM), 0.1)
    attn_k_gain = 1.0 + nrm((N_ODD, HEAD_DIM), 0.1)
    attn_w_out = nrm((N_ODD, N_HEADS * HEAD_DIM, D_MODEL), (N_HEADS * HEAD_DIM) ** -0.5)
    moe_w_router = nrm((N_ODD, D_MODEL, N_EXPERTS), D_MODEL ** -0.5)
    moe_w_gate = nrm((N_ODD, N_EXPERTS, D_MODEL, D_FF_EXPERT), D_MODEL ** -0.5)
    moe_w_up = nrm((N_ODD, N_EXPERTS, D_MODEL, D_FF_EXPERT), D_MODEL ** -0.5)
    moe_w_down = nrm((N_ODD, N_EXPERTS, D_FF_EXPERT, D_MODEL), D_FF_EXPERT ** -0.5)
    return {"x": x, "norm_gains": norm_gains,
            "ssm_w_in": ssm_w_in, "ssm_lambda_re": ssm_lambda_re, "ssm_lambda_im": ssm_lambda_im,
            "ssm_log_step": ssm_log_step, "ssm_b_re": ssm_b_re, "ssm_b_im": ssm_b_im,
            "ssm_c_re": ssm_c_re, "ssm_c_im": ssm_c_im, "ssm_d": ssm_d,
            "ssm_w_glu": ssm_w_glu, "ssm_w_out": ssm_w_out,
            "ffn_w_gate": ffn_w_gate, "ffn_w_up": ffn_w_up, "ffn_w_down": ffn_w_down,
            "attn_w_qkv": attn_w_qkv, "attn_q_gain": attn_q_gain, "attn_k_gain": attn_k_gain,
            "attn_w_out": attn_w_out,
            "moe_w_router": moe_w_router, "moe_w_gate": moe_w_gate, "moe_w_up": moe_w_up,
            "moe_w_down": moe_w_down}


def reference(x, norm_gains,
              ssm_w_in, ssm_lambda_re, ssm_lambda_im, ssm_log_step, ssm_b_re, ssm_b_im,
              ssm_c_re, ssm_c_im, ssm_d, ssm_w_glu, ssm_w_out,
              ffn_w_gate, ffn_w_up, ffn_w_down,
              attn_w_qkv, attn_q_gain, attn_k_gain, attn_w_out,
              moe_w_router, moe_w_gate, moe_w_up, moe_w_down):
    h = x
    for i in range(DEPTH):
        j = i // 2
        g = norm_gains[i]
        if i % 2 == 0:
            mix = s5_mixer(rms_norm(h, g[0]), ssm_w_in[j], ssm_lambda_re[j], ssm_lambda_im[j],
                           ssm_log_step[j], ssm_b_re[j], ssm_b_im[j], ssm_c_re[j], ssm_c_im[j],
                           ssm_d[j], ssm_w_glu[j], ssm_w_out[j])
            h = h + rms_norm(mix, g[1])
            ff = swiglu(rms_norm(h, g[2]), ffn_w_gate[j], ffn_w_up[j], ffn_w_down[j])
            h = h + rms_norm(ff, g[3])
        else:
            mix = axial_gqa_mixer(rms_norm(h, g[0]), attn_w_qkv[j], attn_q_gain[j],
                                  attn_k_gain[j], attn_w_out[j])
            h = h + rms_norm(mix, g[1])
            ff = moe_swiglu(rms_norm(h, g[2]), moe_w_router[j], moe_w_gate[j], moe_w_up[j],
                            moe_w_down[j])
            h = h + rms_norm(ff, g[3])
    return h
```

```python
import functools
import math

import jax
import jax.numpy as jnp
from jax import lax
from jax.experimental import pallas as pl
from jax.experimental.pallas import tpu as pltpu

F32 = jnp.float32
BF16 = jnp.bfloat16
NORM_EPS = 1e-6
ROPE_THETA = 10000.0
GRID_W = 64
N_HEADS = 16
N_KV_HEADS = 4
HEAD_DIM = 64
Q_PER_KV = N_HEADS // N_KV_HEADS
SSM_GROUP = 16
SSM_STATE = 64
S5_CHUNK = 16
S5_SEG_CHUNKS = 64
TOP_K = 2
LANES = 128
VMEM_LIMIT_BYTES = 56 * 1024 * 1024

_NT = (((1,), (1,)), ((), ()))


def _cparams(*sem):
    return pltpu.CompilerParams(dimension_semantics=sem, vmem_limit_bytes=VMEM_LIMIT_BYTES)


def _rms(x, gain):
    return x * lax.rsqrt(jnp.mean(x * x, axis=-1, keepdims=True) + NORM_EPS) * gain


def _sigmoid(x):
    return 1.0 / (1.0 + jnp.exp(-x))


def _gelu_tanh(x):
    return x * (0.5 * (1.0 + jnp.tanh(math.sqrt(2.0 / math.pi) * (x + 0.044715 * (x * x * x)))))


def _norm_matmul_kernel(x_ref, g_ref, w_ref, o_ref):
    xn = _rms(x_ref[...], g_ref[...]).astype(BF16)
    o_ref[...] = jnp.dot(xn, w_ref[...], preferred_element_type=F32).astype(o_ref.dtype)


def _norm_matmul(x, gain, w, tm):
    n, d = x.shape
    m = w.shape[1]
    return pl.pallas_call(
        _norm_matmul_kernel,
        grid=(n // tm,),
        in_specs=[pl.BlockSpec((tm, d), lambda i: (i, 0)),
                  pl.BlockSpec((1, d), lambda i: (0, 0)),
                  pl.BlockSpec((d, m), lambda i: (0, 0))],
        out_specs=pl.BlockSpec((tm, m), lambda i: (i, 0)),
        out_shape=jax.ShapeDtypeStruct((n, m), BF16),
        compiler_params=_cparams("parallel"),
        name="norm_matmul",
    )(x, gain.reshape(1, d), w)


def _s5_prep_kernel(lr_ref, li_ref, ls_ref, bre_ref, bim_ref, cre_ref, cim_ref, dt_ref,
                    m_ref, ws_ref, wyt_ref, pw_ref, aa_ref):
    t_n, s_n, p_n = S5_CHUNK, SSM_GROUP, SSM_STATE
    rows = t_n * s_n
    hi = lax.Precision.HIGHEST

    r_io = lax.broadcasted_iota(jnp.int32, (rows, t_n), 0)
    c_io = lax.broadcasted_iota(jnp.int32, (rows, t_n), 1)
    rep_t = ((r_io // s_n) == c_io).astype(F32)
    rep_s = ((r_io % s_n) == c_io).astype(F32)
    rr = lax.broadcasted_iota(jnp.int32, (rows, rows), 0) // s_n
    cc = lax.broadcasted_iota(jnp.int32, (rows, rows), 1) // s_n
    tvec = lax.broadcasted_iota(jnp.int32, (t_n, 1), 0).astype(F32)
    jvec = lax.broadcasted_iota(jnp.int32, (S5_SEG_CHUNKS, 1), 0).astype(F32)
    row8 = lax.broadcasted_iota(jnp.int32, (8, p_n), 0)

    def expand(tab, rep):
        return jnp.dot(rep, tab, precision=hi, preferred_element_type=F32)

    m_total = None
    for d in range(2):
        lr = lr_ref[0, d:d + 1, :]
        li = li_ref[0, d:d + 1, :]
        step = jnp.exp(ls_ref[0, d:d + 1, :])
        lsr = lr * step
        lsi = li * step

        def cpow(k):
            mag = jnp.exp(lsr * k)
            ang = lsi * k
            return mag * jnp.cos(ang), mag * jnp.sin(ang)

        a_re, a_im = cpow(jnp.ones((1, 1), F32))
        nr, ni = a_re - 1.0, a_im
        den = lr * lr + li * li
        q_re = (nr * lr + ni * li) / den
        q_im = (ni * lr - nr * li) / den
        br = bre_ref[0, d]
        bi = bim_ref[0, d]
        bb_re = q_re * br - q_im * bi
        bb_im = q_re * bi + q_im * br
        cr = cre_ref[0, d]
        ci = cim_ref[0, d]

        def outer(x_re, x_im, k):
            p_re, p_im = cpow(k)
            pe_re, pe_im = expand(p_re, rep_t), expand(p_im, rep_t)
            xe_re, xe_im = expand(x_re, rep_s), expand(x_im, rep_s)
            return xe_re * pe_re - xe_im * pe_im, xe_re * pe_im + xe_im * pe_re

        if d == 0:
            l_re, l_im = outer(bb_re, bb_im, -tvec)
            rt_re, rt_im = outer(cr, ci, tvec)
            ws_re, ws_im = outer(bb_re, bb_im, (t_n - 1.0) - tvec)
            wy_re, wy_im = outer(cr, ci, tvec + 1.0)
            mask = rr <= cc
            pw_re, pw_im = cpow(t_n * jvec)
        else:
            l_re, l_im = outer(bb_re, bb_im, tvec)
            rt_re, rt_im = outer(cr, ci, -tvec)
            ws_re, ws_im = l_re, l_im
            wy_re, wy_im = outer(cr, ci, t_n - tvec)
            mask = rr >= cc
            pw_re, pw_im = cpow(t_n * ((S5_SEG_CHUNKS - 1.0) - jvec))
        kern = (lax.dot_general(l_re, rt_re, _NT, precision=hi, preferred_element_type=F32)
                - lax.dot_general(l_im, rt_im, _NT, precision=hi, preferred_element_type=F32))
        kern = jnp.where(mask, kern, 0.0)
        m_total = kern if m_total is None else m_total + kern

        ws_ref[0, 2 * d] = ws_re
        ws_ref[0, 2 * d + 1] = ws_im
        wyt_ref[0, 2 * d] = wy_re
        wyt_ref[0, 2 * d + 1] = -wy_im
        pw_ref[0, 2 * d] = pw_re
        pw_ref[0, 2 * d + 1] = pw_im
        c_re, c_im = cpow(jnp.full((1, 1), float(t_n), F32))
        s_re, s_im = cpow(jnp.full((1, 1), float(t_n * S5_SEG_CHUNKS), F32))
        aa_ref[0, 2 * d] = jnp.where(row8 == 0, c_re, jnp.where(row8 == 1, s_re, 0.0))
        aa_ref[0, 2 * d + 1] = jnp.where(row8 == 0, c_im, jnp.where(row8 == 1, s_im, 0.0))

    diag = (lax.broadcasted_iota(jnp.int32, (rows, rows), 0)
            == lax.broadcasted_iota(jnp.int32, (rows, rows), 1))
    m_ref[0] = m_total + jnp.where(diag, dt_ref[0], 0.0)


def _s5_prep(lam_re, lam_im, log_step, b_re, b_im, c_re, c_im, d_skip):
    g_n = lam_re.shape[1]
    rows = S5_CHUNK * SSM_GROUP
    lr = jnp.transpose(lam_re, (1, 0, 2))
    li = jnp.transpose(lam_im, (1, 0, 2))
    ls = jnp.transpose(log_step, (1, 0))[:, :, None]
    bre = jnp.transpose(b_re, (1, 0, 3, 2))
    bim = jnp.transpose(b_im, (1, 0, 3, 2))
    cre = jnp.transpose(c_re, (1, 0, 2, 3))
    cim = jnp.transpose(c_im, (1, 0, 2, 3))
    dt = jnp.tile(d_skip.reshape(g_n, 1, SSM_GROUP), (1, 1, S5_CHUNK))

    def spec3(a, b):
        return pl.BlockSpec((1, a, b), lambda g: (g, 0, 0))

    def spec4(a, b, c):
        return pl.BlockSpec((1, a, b, c), lambda g: (g, 0, 0, 0))

    return pl.pallas_call(
        _s5_prep_kernel,
        grid=(g_n,),
        in_specs=[spec3(2, SSM_STATE), spec3(2, SSM_STATE), spec3(2, 1),
                  spec4(2, SSM_GROUP, SSM_STATE), spec4(2, SSM_GROUP, SSM_STATE),
                  spec4(2, SSM_GROUP, SSM_STATE), spec4(2, SSM_GROUP, SSM_STATE),
                  spec3(1, rows)],
        out_specs=[spec3(rows, rows), spec4(4, rows, SSM_STATE), spec4(4, rows, SSM_STATE),
                   spec4(4, S5_SEG_CHUNKS, SSM_STATE), spec4(4, 8, SSM_STATE)],
        out_shape=[jax.ShapeDtypeStruct((g_n, rows, rows), F32),
                   jax.ShapeDtypeStruct((g_n, 4, rows, SSM_STATE), F32),
                   jax.ShapeDtypeStruct((g_n, 4, rows, SSM_STATE), F32),
                   jax.ShapeDtypeStruct((g_n, 4, S5_SEG_CHUNKS, SSM_STATE), F32),
                   jax.ShapeDtypeStruct((g_n, 4, 8, SSM_STATE), F32)],
        compiler_params=_cparams("parallel"),
        name="s5_prep",
    )(lr, li, ls, bre, bim, cre, cim, dt)


def _s5_pair_layout(m, ws, wyt, pw, aa):
    g_n, rows, _ = m.shape
    pairs = g_n // 2
    eye = jnp.eye(2, dtype=F32)
    m_p = jnp.einsum('agrc,gh->agrhc', m.reshape(pairs, 2, rows, rows), eye)
    m_p = m_p.reshape(pairs, 2 * rows, 2 * rows)

    def spread(x):
        x = x.reshape(pairs, 2, 4, rows, SSM_STATE)
        x = jnp.einsum('agkrp,gh->agrkhp', x, eye)
        return x.reshape(pairs, 2 * rows, 8 * SSM_STATE)

    w1 = jnp.concatenate([m_p, spread(ws)], axis=-1).astype(BF16)
    w2t = spread(wyt).astype(BF16)

    def lanes(x):
        r = x.shape[2]
        x = x.reshape(pairs, 2, 4, r, SSM_STATE)
        return jnp.transpose(x, (0, 3, 2, 1, 4)).reshape(pairs, r, 8 * SSM_STATE)

    return w1, w2t, lanes(pw), lanes(aa)


def _s5_core_kernel(u_ref, w1_ref, w2t_ref, pw_ref, aa_ref, y_ref, z_s, x_s, xb_s,
                    *, nseq, nseg):
    n_chunks = S5_SEG_CHUNKS
    rows = n_chunks * nseq
    half = 2 * LANES
    ycols = y_ref.shape[-1]

    z_s[...] = jnp.dot(u_ref[0], w1_ref[0], preferred_element_type=F32)

    aa = aa_ref[0]
    a_fr, a_fi = aa[0:1, 0:LANES], aa[0:1, LANES:2 * LANES]
    a_br, a_bi = aa[0:1, 2 * LANES:3 * LANES], aa[0:1, 3 * LANES:4 * LANES]
    g_fr, g_fi = aa[1:2, 0:LANES], aa[1:2, LANES:2 * LANES]
    g_br, g_bi = aa[1:2, 2 * LANES:3 * LANES], aa[1:2, 3 * LANES:4 * LANES]

    x_s[nseq:2 * nseq, 0:half] = jnp.zeros((nseq, half), F32)
    x_s[rows:rows + nseq, half:2 * half] = jnp.zeros((nseq, half), F32)
    zero = jnp.zeros((nseq, LANES), F32)

    def fwd(j, carry):
        xr, xi = carry
        r0 = pl.multiple_of(j * nseq, nseq)
        sr = z_s[pl.ds(r0, nseq), ycols:ycols + LANES]
        si = z_s[pl.ds(r0, nseq), ycols + LANES:ycols + 2 * LANES]
        nr = a_fr * xr - a_fi * xi + sr
        ni = a_fr * xi + a_fi * xr + si
        w0 = pl.multiple_of(r0 + 2 * nseq, nseq)
        x_s[pl.ds(w0, nseq), 0:LANES] = nr
        x_s[pl.ds(w0, nseq), LANES:2 * LANES] = ni
        return nr, ni

    def bwd(i, carry):
        xr, xi = carry
        j = n_chunks - 1 - i
        r0 = pl.multiple_of(j * nseq, nseq)
        sr = z_s[pl.ds(r0, nseq), ycols + 2 * LANES:ycols + 3 * LANES]
        si = z_s[pl.ds(r0, nseq), ycols + 3 * LANES:ycols + 4 * LANES]
        nr = a_br * xr - a_bi * xi + sr
        ni = a_br * xi + a_bi * xr + si
        x_s[pl.ds(r0, nseq), 2 * LANES:3 * LANES] = nr
        x_s[pl.ds(r0, nseq), 3 * LANES:4 * LANES] = ni
        return nr, ni

    ef_r, ef_i = lax.fori_loop(0, n_chunks, fwd, (zero, zero))
    eb_r, eb_i = lax.fori_loop(0, n_chunks, bwd, (zero, zero))

    def seg_carries(e_r, e_i, g_r, g_i, reverse):
        out_r = [None] * nseq
        out_i = [None] * nseq
        for b in range(nseq // nseg):
            c_r = jnp.zeros((1, LANES), F32)
            c_i = jnp.zeros((1, LANES), F32)
            order = range(nseg - 1, -1, -1) if reverse else range(nseg)
            for s in order:
                r = b * nseg + s
                out_r[r], out_i[r] = c_r, c_i
                n_r = g_r * c_r - g_i * c_i + e_r[r:r + 1, :]
                n_i = g_r * c_i + g_i * c_r + e_i[r:r + 1, :]
                c_r, c_i = n_r, n_i
        return jnp.concatenate(out_r, axis=0), jnp.concatenate(out_i, axis=0)

    cf_r, cf_i = seg_carries(ef_r, ef_i, g_fr, g_fi, False)
    cb_r, cb_i = seg_carries(eb_r, eb_i, g_br, g_bi, True)

    def fix(j, _):
        r0 = pl.multiple_of(j * nseq, nseq)
        rd = pl.multiple_of(r0 + nseq, nseq)
        p = pw_ref[0, pl.ds(j, 1), :]
        p_fr, p_fi = p[:, 0:LANES], p[:, LANES:2 * LANES]
        p_br, p_bi = p[:, 2 * LANES:3 * LANES], p[:, 3 * LANES:4 * LANES]
        xfr = x_s[pl.ds(rd, nseq), 0:LANES] + (p_fr * cf_r - p_fi * cf_i)
        xfi = x_s[pl.ds(rd, nseq), LANES:2 * LANES] + (p_fr * cf_i + p_fi * cf_r)
        xbr = x_s[pl.ds(rd, nseq), 2 * LANES:3 * LANES] + (p_br * cb_r - p_bi * cb_i)
        xbi = x_s[pl.ds(rd, nseq), 3 * LANES:4 * LANES] + (p_br * cb_i + p_bi * cb_r)
        xb_s[pl.ds(r0, nseq), 0:LANES] = xfr.astype(BF16)
        xb_s[pl.ds(r0, nseq), LANES:2 * LANES] = xfi.astype(BF16)
        xb_s[pl.ds(r0, nseq), 2 * LANES:3 * LANES] = xbr.astype(BF16)
        xb_s[pl.ds(r0, nseq), 3 * LANES:4 * LANES] = xbi.astype(BF16)
        return 0

    lax.fori_loop(0, n_chunks, fix, 0)

    inter = lax.dot_general(xb_s[...], w2t_ref[0], _NT, preferred_element_type=F32)
    y_ref[0] = (z_s[:, 0:ycols] + inter).astype(y_ref.dtype)


def _s5_core(u_p, w1, w2t, pw, aa, nseq, nseg):
    pairs, rows, width = u_p.shape
    return pl.pallas_call(
        functools.partial(_s5_core_kernel, nseq=nseq, nseg=nseg),
        grid=(pairs,),
        in_specs=[pl.BlockSpec((1, rows, width), lambda g: (g, 0, 0)),
                  pl.BlockSpec((1, width, w1.shape[2]), lambda g: (g, 0, 0)),
                  pl.BlockSpec((1, width, w2t.shape[2]), lambda g: (g, 0, 0)),
                  pl.BlockSpec((1, S5_SEG_CHUNKS, pw.shape[2]), lambda g: (g, 0, 0)),
                  pl.BlockSpec((1, 8, aa.shape[2]), lambda g: (g, 0, 0))],
        out_specs=pl.BlockSpec((1, rows, width), lambda g: (g, 0, 0)),
        out_shape=jax.ShapeDtypeStruct((pairs, rows, width), BF16),
        scratch_shapes=[pltpu.VMEM((rows, w1.shape[2]), F32),
                        pltpu.VMEM((rows + 2 * nseq, 4 * LANES), F32),
                        pltpu.VMEM((rows, 4 * LANES), BF16)],
        compiler_params=_cparams("parallel"),
        name="s5_core",
    )(u_p, w1, w2t, pw, aa)


def _s5_out_kernel(y_ref, h_ref, wglu_ref, wout_ref, g_ref, o_ref):
    g = _gelu_tanh(y_ref[...].astype(F32))
    z = jnp.dot(g.astype(BF16), wglu_ref[...], preferred_element_type=F32)
    g2 = g * _sigmoid(z)
    mix = jnp.dot(g2.astype(BF16), wout_ref[...], preferred_element_type=F32)
    o_ref[...] = h_ref[...] + _rms(mix, g_ref[...])


def _s5_out(y, h, w_glu, w_out, gain, tm):
    n, d = h.shape
    return pl.pallas_call(
        _s5_out_kernel,
        grid=(n // tm,),
        in_specs=[pl.BlockSpec((tm, d), lambda i: (i, 0)),
                  pl.BlockSpec((tm, d), lambda i: (i, 0)),
                  pl.BlockSpec((d, d), lambda i: (0, 0)),
                  pl.BlockSpec((d, d), lambda i: (0, 0)),
                  pl.BlockSpec((1, d), lambda i: (0, 0))],
        out_specs=pl.BlockSpec((tm, d), lambda i: (i, 0)),
        out_shape=jax.ShapeDtypeStruct((n, d), F32),
        compiler_params=_cparams("parallel"),
        name="s5_out",
    )(y, h, w_glu, w_out, gain.reshape(1, d))


def _ffn_kernel(h_ref, gpre_ref, gpost_ref, wr_ref, wg_ref, wu_ref, wd_ref, o_ref,
                hn_s, gate_s, acc_s, *, n_exp):
    e = pl.program_id(1)
    f = pl.program_id(2)

    @pl.when((e == 0) & (f == 0))
    def _():
        xn = _rms(h_ref[...], gpre_ref[...])
        hn_s[...] = xn.astype(BF16)
        acc_s[...] = jnp.zeros_like(acc_s)
        if n_exp > 1:
            logits = jnp.dot(xn, wr_ref[...], precision=lax.Precision.HIGHEST,
                             preferred_element_type=F32)
            lane = lax.broadcasted_iota(jnp.int32, logits.shape, 1)
            neg = jnp.float32(-jnp.inf)
            logits = jnp.where(lane < n_exp, logits, neg)
            m1 = jnp.max(logits, axis=-1, keepdims=True)
            i1 = jnp.min(jnp.where(logits == m1, lane, LANES), axis=-1, keepdims=True)
            rest = jnp.where(lane == i1, neg, logits)
            m2 = jnp.max(rest, axis=-1, keepdims=True)
            i2 = jnp.min(jnp.where(rest == m2, lane, LANES), axis=-1, keepdims=True)
            e2 = jnp.exp(m2 - m1)
            w1 = 1.0 / (1.0 + e2)
            w2 = e2 / (1.0 + e2)
            gate_s[...] = jnp.where(lane == i1, w1, 0.0) + jnp.where(lane == i2, w2, 0.0)

    hn = hn_s[...]
    a = jnp.dot(hn, wg_ref[0], preferred_element_type=F32)
    u = jnp.dot(hn, wu_ref[0], preferred_element_type=F32)
    act = a * _sigmoid(a) * u
    if n_exp > 1:
        lane = lax.broadcasted_iota(jnp.int32, gate_s.shape, 1)
        gcol = jnp.sum(jnp.where(lane == e, gate_s[...], 0.0), axis=-1, keepdims=True)
        act = act * gcol
    acc_s[...] += jnp.dot(act.astype(BF16), wd_ref[0], preferred_element_type=F32)

    @pl.when((e == n_exp - 1) & (f == pl.num_programs(2) - 1))
    def _():
        o_ref[...] = h_ref[...] + _rms(acc_s[...], gpost_ref[...])


def _ffn(h, gain_pre, gain_post, w_router, w_gate, w_up, w_down, tm, tf):
    n, d = h.shape
    n_exp, _, d_ff = w_gate.shape
    return pl.pallas_call(
        functools.partial(_ffn_kernel, n_exp=n_exp),
        grid=(n // tm, n_exp, d_ff // tf),
        in_specs=[pl.BlockSpec((tm, d), lambda i, e, f: (i, 0)),
                  pl.BlockSpec((1, d), lambda i, e, f: (0, 0)),
                  pl.BlockSpec((1, d), lambda i, e, f: (0, 0)),
                  pl.BlockSpec((d, LANES), lambda i, e, f: (0, 0)),
                  pl.BlockSpec((1, d, tf), lambda i, e, f: (e, 0, f)),
                  pl.BlockSpec((1, d, tf), lambda i, e, f: (e, 0, f)),
                  pl.BlockSpec((1, tf, d), lambda i, e, f: (e, f, 0))],
        out_specs=pl.BlockSpec((tm, d), lambda i, e, f: (i, 0)),
        out_shape=jax.ShapeDtypeStruct((n, d), F32),
        scratch_shapes=[pltpu.VMEM((tm, d), BF16),
                        pltpu.VMEM((tm, LANES), F32),
                        pltpu.VMEM((tm, d), F32)],
        compiler_params=_cparams("parallel", "arbitrary", "arbitrary"),
        name="ffn" if n_exp == 1 else "moe_ffn",
    )(h, gain_pre.reshape(1, d), gain_post.reshape(1, d), w_router, w_gate, w_up, w_down)


def _qkv_kernel(h_ref, g_ref, w_ref, qg_ref, kg_ref, cs_ref, sn_ref, bd_ref,
                q_ref, k_ref, v_ref):
    xn = _rms(h_ref[...], g_ref[...]).astype(BF16)
    qkv = jnp.dot(xn, w_ref[...], preferred_element_type=F32)
    cs = cs_ref[...]
    sn = sn_ref[...]
    bd = bd_ref[...]
    lane = lax.broadcasted_iota(jnp.int32, cs.shape, 1)
    first_half = (lane % HEAD_DIM) < (HEAD_DIM // 2)
    scale = 1.0 / math.sqrt(HEAD_DIM)

    def norm_rope(x, gain):
        ms = jnp.dot(x * x, bd, precision=lax.Precision.HIGHEST, preferred_element_type=F32)
        y = x * lax.rsqrt(ms + NORM_EPS) * gain
        partner = jnp.where(first_half,
                            pltpu.roll(y, LANES - HEAD_DIM // 2, axis=1),
                            pltpu.roll(y, HEAD_DIM // 2, axis=1))
        return y * cs + partner * sn

    n_q_tiles = N_HEADS * HEAD_DIM // LANES
    for t in range(n_q_tiles):
        y = norm_rope(qkv[:, t * LANES:(t + 1) * LANES], qg_ref[...]) * scale
        q_ref[0, 2 * t] = y[:, 0:HEAD_DIM].astype(BF16)
        q_ref[0, 2 * t + 1] = y[:, HEAD_DIM:LANES].astype(BF16)
    k0 = N_HEADS * HEAD_DIM
    for t in range(N_KV_HEADS * HEAD_DIM // LANES):
        y = norm_rope(qkv[:, k0 + t * LANES:k0 + (t + 1) * LANES], kg_ref[...])
        k_ref[0, 2 * t] = y[:, 0:HEAD_DIM].astype(BF16)
        k_ref[0, 2 * t + 1] = y[:, HEAD_DIM:LANES].astype(BF16)
    v0 = (N_HEADS + N_KV_HEADS) * HEAD_DIM
    for j in range(N_KV_HEADS):
        v_ref[0, j] = qkv[:, v0 + j * HEAD_DIM:v0 + (j + 1) * HEAD_DIM].astype(BF16)


def _rope_tables(seq):
    axis_dim = HEAD_DIM // 2
    freqs = ROPE_THETA ** (-jnp.arange(0, axis_dim, 2, dtype=F32) / axis_dim)
    rows = seq // GRID_W
    row_ang = jnp.arange(rows, dtype=F32)[:, None] * freqs
    col_ang = jnp.arange(GRID_W, dtype=F32)[:, None] * freqs
    ang = jnp.concatenate([
        jnp.broadcast_to(row_ang[:, None, :], (rows, GRID_W, freqs.shape[0])),
        jnp.broadcast_to(col_ang[None, :, :], (rows, GRID_W, freqs.shape[0]))], axis=-1)
    ang = ang.reshape(seq, HEAD_DIM // 2)
    cos, sin = jnp.cos(ang), jnp.sin(ang)
    cs = jnp.tile(jnp.concatenate([cos, cos], axis=-1), (1, LANES // HEAD_DIM))
    sn = jnp.tile(jnp.concatenate([-sin, sin], axis=-1), (1, LANES // HEAD_DIM))
    return cs, sn


def _qkv(h, gain, w_qkv, q_gain, k_gain, bsz, seq, tm):
    n, d = h.shape
    width = w_qkv.shape[1]
    perm = jnp.concatenate([jnp.arange(0, HEAD_DIM, 2), jnp.arange(1, HEAD_DIM, 2)])
    n_rot = N_HEADS + N_KV_HEADS
    cols = (jnp.arange(n_rot)[:, None] * HEAD_DIM + perm[None, :]).reshape(-1)
    cols = jnp.concatenate([cols, jnp.arange(n_rot * HEAD_DIM, width)])
    w = w_qkv[:, cols].astype(BF16)
    qg = jnp.tile(q_gain[perm], LANES // HEAD_DIM).reshape(1, LANES)
    kg = jnp.tile(k_gain[perm], LANES // HEAD_DIM).reshape(1, LANES)
    cs, sn = _rope_tables(seq)
    blk = jnp.arange(LANES) // HEAD_DIM
    bd = (blk[:, None] == blk[None, :]).astype(F32) / HEAD_DIM
    per_seq = seq // tm
    return pl.pallas_call(
        _qkv_kernel,
        grid=(n // tm,),
        in_specs=[pl.BlockSpec((tm, d), lambda i: (i, 0)),
                  pl.BlockSpec((1, d), lambda i: (0, 0)),
                  pl.BlockSpec((d, width), lambda i: (0, 0)),
                  pl.BlockSpec((1, LANES), lambda i: (0, 0)),
                  pl.BlockSpec((1, LANES), lambda i: (0, 0)),
                  pl.BlockSpec((tm, LANES), lambda i: (i % per_seq, 0)),
                  pl.BlockSpec((tm, LANES), lambda i: (i % per_seq, 0)),
                  pl.BlockSpec((LANES, LANES), lambda i: (0, 0))],
        out_specs=[pl.BlockSpec((1, N_HEADS, tm, HEAD_DIM),
                                lambda i: (i // per_seq, 0, i % per_seq, 0)),
                   pl.BlockSpec((1, N_KV_HEADS, tm, HEAD_DIM),
                                lambda i: (i // per_seq, 0, i % per_seq, 0)),
                   pl.BlockSpec((1, N_KV_HEADS, tm, HEAD_DIM),
                                lambda i: (i // per_seq, 0, i % per_seq, 0))],
        out_shape=[jax.ShapeDtypeStruct((bsz, N_HEADS, seq, HEAD_DIM), BF16),
                   jax.ShapeDtypeStruct((bsz, N_KV_HEADS, seq, HEAD_DIM), BF16),
                   jax.ShapeDtypeStruct((bsz, N_KV_HEADS, seq, HEAD_DIM), BF16)],
        compiler_params=_cparams("parallel"),
        name="qkv_rope",
    )(h, gain.reshape(1, d), w, qg, kg, cs, sn, bd)


def _attn_kernel(q_ref, k_ref, v_ref, o_ref, m_s, l_s, acc_s, *, tq):
    kv = pl.program_id(3)

    @pl.when(kv == 0)
    def _():
        m_s[...] = jnp.full_like(m_s, -jnp.inf)
        l_s[...] = jnp.zeros_like(l_s)
        acc_s[...] = jnp.zeros_like(acc_s)

    q = q_ref[0].reshape(Q_PER_KV * tq, HEAD_DIM)
    s = lax.dot_general(q, k_ref[0, 0], _NT, preferred_element_type=F32)
    m_prev = m_s[...]
    m_new = jnp.maximum(m_prev, jnp.max(s, axis=-1, keepdims=True))
    alpha = jnp.exp(m_prev - m_new)
    p = jnp.exp(s - m_new)
    l_s[...] = alpha * l_s[...] + jnp.sum(p, axis=-1, keepdims=True)
    acc_s[...] = alpha * acc_s[...] + jnp.dot(p.astype(BF16), v_ref[0, 0],
                                              preferred_element_type=F32)
    m_s[...] = m_new

    @pl.when(kv == pl.num_programs(3) - 1)
    def _():
        o = acc_s[...] / l_s[...]
        o_ref[...] = jnp.concatenate(
            [o[g * tq:(g + 1) * tq, :] for g in range(Q_PER_KV)], axis=-1).astype(o_ref.dtype)


def _attention(q, k, v, tq, tk):
    bsz, _, seq, _ = q.shape
    n_q = seq // tq
    return pl.pallas_call(
        functools.partial(_attn_kernel, tq=tq),
        grid=(bsz, N_KV_HEADS, n_q, seq // tk),
        in_specs=[pl.BlockSpec((1, Q_PER_KV, tq, HEAD_DIM), lambda b, j, i, t: (b, j, i, 0)),
                  pl.BlockSpec((1, 1, tk, HEAD_DIM), lambda b, j, i, t: (b, j, t, 0)),
                  pl.BlockSpec((1, 1, tk, HEAD_DIM), lambda b, j, i, t: (b, j, t, 0))],
        out_specs=pl.BlockSpec((tq, Q_PER_KV * HEAD_DIM), lambda b, j, i, t: (b * n_q + i, j)),
        out_shape=jax.ShapeDtypeStruct((bsz * seq, N_HEADS * HEAD_DIM), BF16),
        scratch_shapes=[pltpu.VMEM((Q_PER_KV * tq, 1), F32),
                        pltpu.VMEM((Q_PER_KV * tq, 1), F32),
                        pltpu.VMEM((Q_PER_KV * tq, HEAD_DIM), F32)],
        compiler_params=_cparams("parallel", "parallel", "parallel", "arbitrary"),
        name="flash_attn",
    )(q, k, v)


def _proj_res_kernel(x_ref, h_ref, w_ref, g_ref, o_ref):
    mix = jnp.dot(x_ref[...], w_ref[...], preferred_element_type=F32)
    o_ref[...] = h_ref[...] + _rms(mix, g_ref[...])


def _proj_res(x, h, w, gain, tm):
    n, d = h.shape
    k = x.shape[1]
    return pl.pallas_call(
        _proj_res_kernel,
        grid=(n // tm,),
        in_specs=[pl.BlockSpec((tm, k), lambda i: (i, 0)),
                  pl.BlockSpec((tm, d), lambda i: (i, 0)),
                  pl.BlockSpec((k, d), lambda i: (0, 0)),
                  pl.BlockSpec((1, d), lambda i: (0, 0))],
        out_specs=pl.BlockSpec((tm, d), lambda i: (i, 0)),
        out_shape=jax.ShapeDtypeStruct((n, d), F32),
        compiler_params=_cparams("parallel"),
        name="proj_res",
    )(x, h, w, gain.reshape(1, d))


def _s5_layer(h, bsz, seq, gains, w_in, lam_re, lam_im, log_step, b_re, b_im, c_re, c_im,
              d_skip, w_glu, w_out):
    n, d = h.shape
    seg_tokens = S5_CHUNK * S5_SEG_CHUNKS
    nseg = seq // seg_tokens
    nseq = bsz * nseg
    pairs = d // (2 * SSM_GROUP)
    u = _norm_matmul(h, gains[0], w_in.astype(BF16), tm=512)
    u_p = u.reshape(nseq, S5_SEG_CHUNKS, S5_CHUNK, pairs, 2, SSM_GROUP)
    u_p = jnp.transpose(u_p, (3, 1, 0, 4, 2, 5)).reshape(
        pairs, S5_SEG_CHUNKS * nseq, 2 * S5_CHUNK * SSM_GROUP)
    m, ws, wyt, pw, aa = _s5_prep(lam_re, lam_im, log_step, b_re, b_im, c_re, c_im, d_skip)
    w1, w2t, pw_p, aa_p = _s5_pair_layout(m, ws, wyt, pw, aa)
    y_p = _s5_core(u_p, w1, w2t, pw_p, aa_p, nseq, nseg)
    y = y_p.reshape(pairs, S5_SEG_CHUNKS, nseq, 2, S5_CHUNK, SSM_GROUP)
    y = jnp.transpose(y, (2, 1, 4, 0, 3, 5)).reshape(n, d)
    return _s5_out(y, h, w_glu.astype(BF16), w_out.astype(BF16), gains[1], tm=512)


def _attn_layer(h, bsz, seq, gains, w_qkv, q_gain, k_gain, w_out):
    q, k, v = _qkv(h, gains[0], w_qkv, q_gain, k_gain, bsz, seq, tm=512)
    o = _attention(q, k, v, tq=256, tk=512)
    return _proj_res(o, h, w_out.astype(BF16), gains[1], tm=512)


def _moe(h, gain_pre, gain_post, w_router, w_gate, w_up, w_down):
    n_exp = w_router.shape[-1]
    w_r = jnp.pad(w_router, ((0, 0), (0, LANES - n_exp)))
    return _ffn(h, gain_pre, gain_post, w_r, w_gate.astype(BF16), w_up.astype(BF16),
                w_down.astype(BF16), tm=512, tf=896)


def kernel(x, norm_gains, ssm_w_in, ssm_lambda_re, ssm_lambda_im, ssm_log_step, ssm_b_re,
           ssm_b_im, ssm_c_re, ssm_c_im, ssm_d, ssm_w_glu, ssm_w_out, ffn_w_gate, ffn_w_up,
           ffn_w_down, attn_w_qkv, attn_q_gain, attn_k_gain, attn_w_out, moe_w_router,
           moe_w_gate, moe_w_up, moe_w_down):
    bsz, seq, d = x.shape
    depth = norm_gains.shape[0]
    h = x.reshape(bsz * seq, d)
    no_router = jnp.zeros((d, LANES), F32)
    for i in range(depth):
        j = i // 2
        g = norm_gains[i]
        if i % 2 == 0:
            h = _s5_layer(h, bsz, seq, g, ssm_w_in[j], ssm_lambda_re[j], ssm_lambda_im[j],
                          ssm_log_step[j], ssm_b_re[j], ssm_b_im[j], ssm_c_re[j], ssm_c_im[j],
                          ssm_d[j], ssm_w_glu[j], ssm_w_out[j])
            h = _ffn(h, g[2], g[3], no_router, ffn_w_gate[j][None].astype(BF16),
                     ffn_w_up[j][None].astype(BF16), ffn_w_down[j][None].astype(BF16),
                     tm=512, tf=1408)
        else:
            h = _attn_layer(h, bsz, seq, g, attn_w_qkv[j], attn_q_gain[j], attn_k_gain[j],
                            attn_w_out[j])
            h = _moe(h, g[2], g[3], moe_w_router[j], moe_w_gate[j], moe_w_up[j], moe_w_down[j])
    return h.reshape(bsz, seq, d)
```

```python
import functools
import math

import jax
import jax.numpy as jnp
from jax import lax
from jax.experimental import pallas as pl
from jax.experimental.pallas import tpu as pltpu

F32 = jnp.float32
BF16 = jnp.bfloat16
NORM_EPS = 1e-6
ROPE_THETA = 10000.0
GRID_W = 64
N_HEADS = 16
N_KV_HEADS = 4
HEAD_DIM = 64
Q_PER_KV = N_HEADS // N_KV_HEADS
SSM_GROUP = 16
SSM_STATE = 64
S5_CHUNK = 16
S5_SEG_CHUNKS = 64
TOP_K = 2
LANES = 128
VMEM_LIMIT_BYTES = 56 * 1024 * 1024

_NT = (((1,), (1,)), ((), ()))


def _cparams(*sem):
    return pltpu.CompilerParams(dimension_semantics=sem, vmem_limit_bytes=VMEM_LIMIT_BYTES)


def _rms(x, gain):
    return x * lax.rsqrt(jnp.mean(x * x, axis=-1, keepdims=True) + NORM_EPS) * gain


def _sigmoid(x):
    return 1.0 / (1.0 + jnp.exp(-x))


def _gelu_tanh(x):
    return x * (0.5 * (1.0 + jnp.tanh(math.sqrt(2.0 / math.pi) * (x + 0.044715 * (x * x * x)))))


def _norm_matmul_kernel(x_ref, g_ref, w_ref, o_ref):
    xn = _rms(x_ref[...], g_ref[...]).astype(BF16)
    o_ref[...] = jnp.dot(xn, w_ref[...], preferred_element_type=F32).astype(o_ref.dtype)


def _norm_matmul(x, gain, w, tm):
    n, d = x.shape
    m = w.shape[1]
    return pl.pallas_call(
        _norm_matmul_kernel,
        grid=(n // tm,),
        in_specs=[pl.BlockSpec((tm, d), lambda i: (i, 0)),
                  pl.BlockSpec((1, d), lambda i: (0, 0)),
                  pl.BlockSpec((d, m), lambda i: (0, 0))],
        out_specs=pl.BlockSpec((tm, m), lambda i: (i, 0)),
        out_shape=jax.ShapeDtypeStruct((n, m), BF16),
        compiler_params=_cparams("parallel"),
        name="norm_matmul",
    )(x, gain.reshape(1, d), w)


def _s5_prep_kernel(lr_ref, li_ref, ls_ref, bre_ref, bim_ref, cre_ref, cim_ref, dt_ref,
                    m_ref, ws_ref, wyt_ref, pw_ref, aa_ref):
    t_n, s_n, p_n = S5_CHUNK, SSM_GROUP, SSM_STATE
    rows = t_n * s_n
    hi = lax.Precision.HIGHEST

    r_io = lax.broadcasted_iota(jnp.int32, (rows, t_n), 0)
    c_io = lax.broadcasted_iota(jnp.int32, (rows, t_n), 1)
    rep_t = ((r_io // s_n) == c_io).astype(F32)
    rep_s = ((r_io % s_n) == c_io).astype(F32)
    rr = lax.broadcasted_iota(jnp.int32, (rows, rows), 0) // s_n
    cc = lax.broadcasted_iota(jnp.int32, (rows, rows), 1) // s_n
    tvec = lax.broadcasted_iota(jnp.int32, (t_n, 1), 0).astype(F32)
    jvec = lax.broadcasted_iota(jnp.int32, (S5_SEG_CHUNKS, 1), 0).astype(F32)
    row8 = lax.broadcasted_iota(jnp.int32, (8, p_n), 0)

    def expand(tab, rep):
        return jnp.dot(rep, tab, precision=hi, preferred_element_type=F32)

    m_total = None
    for d in range(2):
        lr = lr_ref[0, d:d + 1, :]
        li = li_ref[0, d:d + 1, :]
        step = jnp.exp(ls_ref[0, d:d + 1, :])
        lsr = lr * step
        lsi = li * step

        def cpow(k):
            mag = jnp.exp(lsr * k)
            ang = lsi * k
            return mag * jnp.cos(ang), mag * jnp.sin(ang)

        a_re, a_im = cpow(jnp.ones((1, 1), F32))
        nr, ni = a_re - 1.0, a_im
        den = lr * lr + li * li
        q_re = (nr * lr + ni * li) / den
        q_im = (ni * lr - nr * li) / den
        br = bre_ref[0, d]
        bi = bim_ref[0, d]
        bb_re = q_re * br - q_im * bi
        bb_im = q_re * bi + q_im * br
        cr = cre_ref[0, d]
        ci = cim_ref[0, d]

        def outer(x_re, x_im, k):
            p_re, p_im = cpow(k)
            pe_re, pe_im = expand(p_re, rep_t), expand(p_im, rep_t)
            xe_re, xe_im = expand(x_re, rep_s), expand(x_im, rep_s)
            return xe_re * pe_re - xe_im * pe_im, xe_re * pe_im + xe_im * pe_re

        if d == 0:
            l_re, l_im = outer(bb_re, bb_im, -tvec)
            rt_re, rt_im = outer(cr, ci, tvec)
            ws_re, ws_im = outer(bb_re, bb_im, (t_n - 1.0) - tvec)
            wy_re, wy_im = outer(cr, ci, tvec + 1.0)
            mask = rr <= cc
            pw_re, pw_im = cpow(t_n * jvec)
        else:
            l_re, l_im = outer(bb_re, bb_im, tvec)
            rt_re, rt_im = outer(cr, ci, -tvec)
            ws_re, ws_im = l_re, l_im
            wy_re, wy_im = outer(cr, ci, t_n - tvec)
            mask = rr >= cc
            pw_re, pw_im = cpow(t_n * ((S5_SEG_CHUNKS - 1.0) - jvec))
        kern = (lax.dot_general(l_re, rt_re, _NT, precision=hi, preferred_element_type=F32)
                - lax.dot_general(l_im, rt_im, _NT, precision=hi, preferred_element_type=F32))
        kern = jnp.where(mask, kern, 0.0)
        m_total = kern if m_total is None else m_total + kern

        ws_ref[0, 2 * d] = ws_re
        ws_ref[0, 2 * d + 1] = ws_im
        wyt_ref[0, 2 * d] = wy_re
        wyt_ref[0, 2 * d + 1] = -wy_im
        pw_ref[0, 2 * d] = pw_re
        pw_ref[0, 2 * d + 1] = pw_im
        c_re, c_im = cpow(jnp.full((1, 1), float(t_n), F32))
        s_re, s_im = cpow(jnp.full((1, 1), float(t_n * S5_SEG_CHUNKS), F32))
        aa_ref[0, 2 * d] = jnp.where(row8 == 0, c_re, jnp.where(row8 == 1, s_re, 0.0))
        aa_ref[0, 2 * d + 1] = jnp.where(row8 == 0, c_im, jnp.where(row8 == 1, s_im, 0.0))

    diag = (lax.broadcasted_iota(jnp.int32, (rows, rows), 0)
            == lax.broadcasted_iota(jnp.int32, (rows, rows), 1))
    m_ref[0] = m_total + jnp.where(diag, dt_ref[0], 0.0)


def _s5_prep(lam_re, lam_im, log_step, b_re, b_im, c_re, c_im, d_skip):
    g_n = lam_re.shape[1]
    rows = S5_CHUNK * SSM_GROUP
    lr = jnp.transpose(lam_re, (1, 0, 2))
    li = jnp.transpose(lam_im, (1, 0, 2))
    ls = jnp.transpose(log_step, (1, 0))[:, :, None]
    bre = jnp.transpose(b_re, (1, 0, 3, 2))
    bim = jnp.transpose(b_im, (1, 0, 3, 2))
    cre = jnp.transpose(c_re, (1, 0, 2, 3))
    cim = jnp.transpose(c_im, (1, 0, 2, 3))
    dt = jnp.tile(d_skip.reshape(g_n, 1, SSM_GROUP), (1, 1, S5_CHUNK))

    def spec3(a, b):
        return pl.BlockSpec((1, a, b), lambda g: (g, 0, 0))

    def spec4(a, b, c):
        return pl.BlockSpec((1, a, b, c), lambda g: (g, 0, 0, 0))

    return pl.pallas_call(
        _s5_prep_kernel,
        grid=(g_n,),
        in_specs=[spec3(2, SSM_STATE), spec3(2, SSM_STATE), spec3(2, 1),
                  spec4(2, SSM_GROUP, SSM_STATE), spec4(2, SSM_GROUP, SSM_STATE),
                  spec4(2, SSM_GROUP, SSM_STATE), spec4(2, SSM_GROUP, SSM_STATE),
                  spec3(1, rows)],
        out_specs=[spec3(rows, rows), spec4(4, rows, SSM_STATE), spec4(4, rows, SSM_STATE),
                   spec4(4, S5_SEG_CHUNKS, SSM_STATE), spec4(4, 8, SSM_STATE)],
        out_shape=[jax.ShapeDtypeStruct((g_n, rows, rows), F32),
                   jax.ShapeDtypeStruct((g_n, 4, rows, SSM_STATE), F32),
                   jax.ShapeDtypeStruct((g_n, 4, rows, SSM_STATE), F32),
                   jax.ShapeDtypeStruct((g_n, 4, S5_SEG_CHUNKS, SSM_STATE), F32),
                   jax.ShapeDtypeStruct((g_n, 4, 8, SSM_STATE), F32)],
        compiler_params=_cparams("parallel"),
        name="s5_prep",
    )(lr, li, ls, bre, bim, cre, cim, dt)


def _s5_pair_layout(m, ws, wyt, pw, aa):
    g_n, rows, _ = m.shape
    pairs = g_n // 2
    eye = jnp.eye(2, dtype=F32)
    m_p = jnp.einsum('agrc,gh->agrhc', m.reshape(pairs, 2, rows, rows), eye)
    m_p = m_p.reshape(pairs, 2 * rows, 2 * rows)

    def spread(x):
        x = x.reshape(pairs, 2, 4, rows, SSM_STATE)
        x = jnp.einsum('agkrp,gh->agrkhp', x, eye)
        return x.reshape(pairs, 2 * rows, 8 * SSM_STATE)

    w1 = jnp.concatenate([m_p, spread(ws)], axis=-1).astype(BF16)
    w2t = spread(wyt).astype(BF16)

    def lanes(x):
        r = x.shape[2]
        x = x.reshape(pairs, 2, 4, r, SSM_STATE)
        return jnp.transpose(x, (0, 3, 2, 1, 4)).reshape(pairs, r, 8 * SSM_STATE)

    return w1, w2t, lanes(pw), lanes(aa)


def _s5_core_kernel(u_ref, w1_ref, w2t_ref, pw_ref, aa_ref, y_ref, z_s, x_s, xb_s,
                    *, nseq, nseg):
    n_chunks = S5_SEG_CHUNKS
    rows = n_chunks * nseq
    half = 2 * LANES
    ycols = y_ref.shape[-1]

    z_s[...] = jnp.dot(u_ref[0], w1_ref[0], preferred_element_type=F32)

    aa = aa_ref[0]
    a_fr, a_fi = aa[0:1, 0:LANES], aa[0:1, LANES:2 * LANES]
    a_br, a_bi = aa[0:1, 2 * LANES:3 * LANES], aa[0:1, 3 * LANES:4 * LANES]
    g_fr, g_fi = aa[1:2, 0:LANES], aa[1:2, LANES:2 * LANES]
    g_br, g_bi = aa[1:2, 2 * LANES:3 * LANES], aa[1:2, 3 * LANES:4 * LANES]

    x_s[nseq:2 * nseq, 0:half] = jnp.zeros((nseq, half), F32)
    x_s[rows:rows + nseq, half:2 * half] = jnp.zeros((nseq, half), F32)
    zero = jnp.zeros((nseq, LANES), F32)

    def fwd(j, carry):
        xr, xi = carry
        r0 = pl.multiple_of(j * nseq, nseq)
        sr = z_s[pl.ds(r0, nseq), ycols:ycols + LANES]
        si = z_s[pl.ds(r0, nseq), ycols + LANES:ycols + 2 * LANES]
        nr = a_fr * xr - a_fi * xi + sr
        ni = a_fr * xi + a_fi * xr + si
        w0 = pl.multiple_of(r0 + 2 * nseq, nseq)
        x_s[pl.ds(w0, nseq), 0:LANES] = nr
        x_s[pl.ds(w0, nseq), LANES:2 * LANES] = ni
        return nr, ni

    def bwd(i, carry):
        xr, xi = carry
        j = n_chunks - 1 - i
        r0 = pl.multiple_of(j * nseq, nseq)
        sr = z_s[pl.ds(r0, nseq), ycols + 2 * LANES:ycols + 3 * LANES]
        si = z_s[pl.ds(r0, nseq), ycols + 3 * LANES:ycols + 4 * LANES]
        nr = a_br * xr - a_bi * xi + sr
        ni = a_br * xi + a_bi * xr + si
        x_s[pl.ds(r0, nseq), 2 * LANES:3 * LANES] = nr
        x_s[pl.ds(r0, nseq), 3 * LANES:4 * LANES] = ni
        return nr, ni

    ef_r, ef_i = lax.fori_loop(0, n_chunks, fwd, (zero, zero))
    eb_r, eb_i = lax.fori_loop(0, n_chunks, bwd, (zero, zero))

    def seg_carries(e_r, e_i, g_r, g_i, reverse):
        out_r = [None] * nseq
        out_i = [None] * nseq
        for b in range(nseq // nseg):
            c_r = jnp.zeros((1, LANES), F32)
            c_i = jnp.zeros((1, LANES), F32)
            order = range(nseg - 1, -1, -1) if reverse else range(nseg)
            for s in order:
                r = b * nseg + s
                out_r[r], out_i[r] = c_r, c_i
                n_r = g_r * c_r - g_i * c_i + e_r[r:r + 1, :]
                n_i = g_r * c_i + g_i * c_r + e_i[r:r + 1, :]
                c_r, c_i = n_r, n_i
        return jnp.concatenate(out_r, axis=0), jnp.concatenate(out_i, axis=0)

    cf_r, cf_i = seg_carries(ef_r, ef_i, g_fr, g_fi, False)
    cb_r, cb_i = seg_carries(eb_r, eb_i, g_br, g_bi, True)

    def fix(j, _):
        r0 = pl.multiple_of(j * nseq, nseq)
        rd = pl.multiple_of(r0 + nseq, nseq)
        p = pw_ref[0, pl.ds(j, 1), :]
        p_fr, p_fi = p[:, 0:LANES], p[:, LANES:2 * LANES]
        p_br, p_bi = p[:, 2 * LANES:3 * LANES], p[:, 3 * LANES:4 * LANES]
        xfr = x_s[pl.ds(rd, nseq), 0:LANES] + (p_fr * cf_r - p_fi * cf_i)
        xfi = x_s[pl.ds(rd, nseq), LANES:2 * LANES] + (p_fr * cf_i + p_fi * cf_r)
        xbr = x_s[pl.ds(rd, nseq), 2 * LANES:3 * LANES] + (p_br * cb_r - p_bi * cb_i)
        xbi = x_s[pl.ds(rd, nseq), 3 * LANES:4 * LANES] + (p_br * cb_i + p_bi * cb_r)
        xb_s[pl.ds(r0, nseq), 0:LANES] = xfr.astype(BF16)
        xb_s[pl.ds(r0, nseq), LANES:2 * LANES] = xfi.astype(BF16)
        xb_s[pl.ds(r0, nseq), 2 * LANES:3 * LANES] = xbr.astype(BF16)
        xb_s[pl.ds(r0, nseq), 3 * LANES:4 * LANES] = xbi.astype(BF16)
        return 0

    lax.fori_loop(0, n_chunks, fix, 0)

    inter = lax.dot_general(xb_s[...], w2t_ref[0], _NT, preferred_element_type=F32)
    y_ref[0] = (z_s[:, 0:ycols] + inter).astype(y_ref.dtype)


def _s5_core(u_p, w1, w2t, pw, aa, nseq, nseg):
    pairs, rows, width = u_p.shape
    return pl.pallas_call(
        functools.partial(_s5_core_kernel, nseq=nseq, nseg=nseg),
        grid=(pairs,),
        in_specs=[pl.BlockSpec((1, rows, width), lambda g: (g, 0, 0)),
                  pl.BlockSpec((1, width, w1.shape[2]), lambda g: (g, 0, 0)),
                  pl.BlockSpec((1, width, w2t.shape[2]), lambda g: (g, 0, 0)),
                  pl.BlockSpec((1, S5_SEG_CHUNKS, pw.shape[2]), lambda g: (g, 0, 0)),
                  pl.BlockSpec((1, 8, aa.shape[2]), lambda g: (g, 0, 0))],
        out_specs=pl.BlockSpec((1, rows, width), lambda g: (g, 0, 0)),
        out_shape=jax.ShapeDtypeStruct((pairs, rows, width), BF16),
        scratch_shapes=[pltpu.VMEM((rows, w1.shape[2]), F32),
                        pltpu.VMEM((rows + 2 * nseq, 4 * LANES), F32),
                        pltpu.VMEM((rows, 4 * LANES), BF16)],
        compiler_params=_cparams("parallel"),
        name="s5_core",
    )(u_p, w1, w2t, pw, aa)


def _s5_out_kernel(y_ref, h_ref, wglu_ref, wout_ref, g_ref, o_ref):
    g = _gelu_tanh(y_ref[...].astype(F32))
    z = jnp.dot(g.astype(BF16), wglu_ref[...], preferred_element_type=F32)
    g2 = g * _sigmoid(z)
    mix = jnp.dot(g2.astype(BF16), wout_ref[...], preferred_element_type=F32)
    o_ref[...] = h_ref[...] + _rms(mix, g_ref[...])


def _s5_out(y, h, w_glu, w_out, gain, tm):
    n, d = h.shape
    return pl.pallas_call(
        _s5_out_kernel,
        grid=(n // tm,),
        in_specs=[pl.BlockSpec((tm, d), lambda i: (i, 0)),
                  pl.BlockSpec((tm, d), lambda i: (i, 0)),
                  pl.BlockSpec((d, d), lambda i: (0, 0)),
                  pl.BlockSpec((d, d), lambda i: (0, 0)),
                  pl.BlockSpec((1, d), lambda i: (0, 0))],
        out_specs=pl.BlockSpec((tm, d), lambda i: (i, 0)),
        out_shape=jax.ShapeDtypeStruct((n, d), F32),
        compiler_params=_cparams("parallel"),
        name="s5_out",
    )(y, h, w_glu, w_out, gain.reshape(1, d))


def _ffn_kernel(h_ref, gpre_ref, gpost_ref, wr_ref, wg_ref, wu_ref, wd_ref, o_ref,
                hn_s, gate_s, acc_s, *, n_exp):
    e = pl.program_id(1)
    f = pl.program_id(2)

    @pl.when((e == 0) & (f == 0))
    def _():
        xn = _rms(h_ref[...], gpre_ref[...])
        hn_s[...] = xn.astype(BF16)
        acc_s[...] = jnp.zeros_like(acc_s)
        if n_exp > 1:
            logits = jnp.dot(xn, wr_ref[...], precision=lax.Precision.HIGHEST,
                             preferred_element_type=F32)
            lane = lax.broadcasted_iota(jnp.int32, logits.shape, 1)
            neg = jnp.float32(-jnp.inf)
            logits = jnp.where(lane < n_exp, logits, neg)
            m1 = jnp.max(logits, axis=-1, keepdims=True)
            i1 = jnp.min(jnp.where(logits == m1, lane, LANES), axis=-1, keepdims=True)
            rest = jnp.where(lane == i1, neg, logits)
            m2 = jnp.max(rest, axis=-1, keepdims=True)
            i2 = jnp.min(jnp.where(rest == m2, lane, LANES), axis=-1, keepdims=True)
            e2 = jnp.exp(m2 - m1)
            w1 = 1.0 / (1.0 + e2)
            w2 = e2 / (1.0 + e2)
            gate_s[...] = jnp.where(lane == i1, w1, 0.0) + jnp.where(lane == i2, w2, 0.0)

    hn = hn_s[...]
    a = jnp.dot(hn, wg_ref[0], preferred_element_type=F32)
    u = jnp.dot(hn, wu_ref[0], preferred_element_type=F32)
    act = a * _sigmoid(a) * u
    if n_exp > 1:
        lane = lax.broadcasted_iota(jnp.int32, gate_s.shape, 1)
        gcol = jnp.sum(jnp.where(lane == e, gate_s[...], 0.0), axis=-1, keepdims=True)
        act = act * gcol
    acc_s[...] += jnp.dot(act.astype(BF16), wd_ref[0], preferred_element_type=F32)

    @pl.when((e == n_exp - 1) & (f == pl.num_programs(2) - 1))
    def _():
        o_ref[...] = h_ref[...] + _rms(acc_s[...], gpost_ref[...])


def _ffn(h, gain_pre, gain_post, w_router, w_gate, w_up, w_down, tm, tf):
    n, d = h.shape
    n_exp, _, d_ff = w_gate.shape
    return pl.pallas_call(
        functools.partial(_ffn_kernel, n_exp=n_exp),
        grid=(n // tm, n_exp, d_ff // tf),
        in_specs=[pl.BlockSpec((tm, d), lambda i, e, f: (i, 0)),
                  pl.BlockSpec((1, d), lambda i, e, f: (0, 0)),
                  pl.BlockSpec((1, d), lambda i, e, f: (0, 0)),
                  pl.BlockSpec((d, LANES), lambda i, e, f: (0, 0)),
                  pl.BlockSpec((1, d, tf), lambda i, e, f: (e, 0, f)),
                  pl.BlockSpec((1, d, tf), lambda i, e, f: (e, 0, f)),
                  pl.BlockSpec((1, tf, d), lambda i, e, f: (e, f, 0))],
        out_specs=pl.BlockSpec((tm, d), lambda i, e, f: (i, 0)),
        out_shape=jax.ShapeDtypeStruct((n, d), F32),
        scratch_shapes=[pltpu.VMEM((tm, d), BF16),
                        pltpu.VMEM((tm, LANES), F32),
                        pltpu.VMEM((tm, d), F32)],
        compiler_params=_cparams("parallel", "arbitrary", "arbitrary"),
        name="ffn" if n_exp == 1 else "moe_ffn",
    )(h, gain_pre.reshape(1, d), gain_post.reshape(1, d), w_router, w_gate, w_up, w_down)


def _qkv_kernel(h_ref, g_ref, w_ref, qg_ref, kg_ref, cs_ref, sn_ref, bd_ref,
                q_ref, k_ref, v_ref):
    xn = _rms(h_ref[...], g_ref[...]).astype(BF16)
    qkv = jnp.dot(xn, w_ref[...], preferred_element_type=F32)
    cs = cs_ref[...]
    sn = sn_ref[...]
    bd = bd_ref[...]
    lane = lax.broadcasted_iota(jnp.int32, cs.shape, 1)
    first_half = (lane % HEAD_DIM) < (HEAD_DIM // 2)
    scale = math.log2(math.e) / math.sqrt(HEAD_DIM)

    def norm_rope(x, gain):
        ms = jnp.dot(x * x, bd, precision=lax.Precision.HIGHEST, preferred_element_type=F32)
        y = x * lax.rsqrt(ms + NORM_EPS) * gain
        partner = jnp.where(first_half,
                            pltpu.roll(y, LANES - HEAD_DIM // 2, axis=1),
                            pltpu.roll(y, HEAD_DIM // 2, axis=1))
        return y * cs + partner * sn

    n_q_tiles = N_HEADS * HEAD_DIM // LANES
    for t in range(n_q_tiles):
        y = norm_rope(qkv[:, t * LANES:(t + 1) * LANES], qg_ref[...]) * scale
        q_ref[0, 2 * t] = y[:, 0:HEAD_DIM].astype(BF16)
        q_ref[0, 2 * t + 1] = y[:, HEAD_DIM:LANES].astype(BF16)
    k0 = N_HEADS * HEAD_DIM
    for t in range(N_KV_HEADS * HEAD_DIM // LANES):
        y = norm_rope(qkv[:, k0 + t * LANES:k0 + (t + 1) * LANES], kg_ref[...])
        k_ref[0, 2 * t] = y[:, 0:HEAD_DIM].astype(BF16)
        k_ref[0, 2 * t + 1] = y[:, HEAD_DIM:LANES].astype(BF16)
    v0 = (N_HEADS + N_KV_HEADS) * HEAD_DIM
    ones = jnp.ones((qkv.shape[0], LANES - HEAD_DIM), BF16)
    for j in range(N_KV_HEADS):
        vj = qkv[:, v0 + j * HEAD_DIM:v0 + (j + 1) * HEAD_DIM].astype(BF16)
        v_ref[0, j] = jnp.concatenate([vj, ones], axis=-1)


def _rope_tables(seq):
    axis_dim = HEAD_DIM // 2
    freqs = ROPE_THETA ** (-jnp.arange(0, axis_dim, 2, dtype=F32) / axis_dim)
    rows = seq // GRID_W
    row_ang = jnp.arange(rows, dtype=F32)[:, None] * freqs
    col_ang = jnp.arange(GRID_W, dtype=F32)[:, None] * freqs
    ang = jnp.concatenate([
        jnp.broadcast_to(row_ang[:, None, :], (rows, GRID_W, freqs.shape[0])),
        jnp.broadcast_to(col_ang[None, :, :], (rows, GRID_W, freqs.shape[0]))], axis=-1)
    ang = ang.reshape(seq, HEAD_DIM // 2)
    cos, sin = jnp.cos(ang), jnp.sin(ang)
    cs = jnp.tile(jnp.concatenate([cos, cos], axis=-1), (1, LANES // HEAD_DIM))
    sn = jnp.tile(jnp.concatenate([-sin, sin], axis=-1), (1, LANES // HEAD_DIM))
    return cs, sn


def _qkv(h, gain, w_qkv, q_gain, k_gain, bsz, seq, tm):
    n, d = h.shape
    width = w_qkv.shape[1]
    perm = jnp.concatenate([jnp.arange(0, HEAD_DIM, 2), jnp.arange(1, HEAD_DIM, 2)])
    n_rot = N_HEADS + N_KV_HEADS
    cols = (jnp.arange(n_rot)[:, None] * HEAD_DIM + perm[None, :]).reshape(-1)
    cols = jnp.concatenate([cols, jnp.arange(n_rot * HEAD_DIM, width)])
    w = w_qkv[:, cols].astype(BF16)
    qg = jnp.tile(q_gain[perm], LANES // HEAD_DIM).reshape(1, LANES)
    kg = jnp.tile(k_gain[perm], LANES // HEAD_DIM).reshape(1, LANES)
    cs, sn = _rope_tables(seq)
    blk = jnp.arange(LANES) // HEAD_DIM
    bd = (blk[:, None] == blk[None, :]).astype(F32) / HEAD_DIM
    per_seq = seq // tm
    return pl.pallas_call(
        _qkv_kernel,
        grid=(n // tm,),
        in_specs=[pl.BlockSpec((tm, d), lambda i: (i, 0)),
                  pl.BlockSpec((1, d), lambda i: (0, 0)),
                  pl.BlockSpec((d, width), lambda i: (0, 0)),
                  pl.BlockSpec((1, LANES), lambda i: (0, 0)),
                  pl.BlockSpec((1, LANES), lambda i: (0, 0)),
                  pl.BlockSpec((tm, LANES), lambda i: (i % per_seq, 0)),
                  pl.BlockSpec((tm, LANES), lambda i: (i % per_seq, 0)),
                  pl.BlockSpec((LANES, LANES), lambda i: (0, 0))],
        out_specs=[pl.BlockSpec((1, N_HEADS, tm, HEAD_DIM),
                                lambda i: (i // per_seq, 0, i % per_seq, 0)),
                   pl.BlockSpec((1, N_KV_HEADS, tm, HEAD_DIM),
                                lambda i: (i // per_seq, 0, i % per_seq, 0)),
                   pl.BlockSpec((1, N_KV_HEADS, tm, LANES),
                                lambda i: (i // per_seq, 0, i % per_seq, 0))],
        out_shape=[jax.ShapeDtypeStruct((bsz, N_HEADS, seq, HEAD_DIM), BF16),
                   jax.ShapeDtypeStruct((bsz, N_KV_HEADS, seq, HEAD_DIM), BF16),
                   jax.ShapeDtypeStruct((bsz, N_KV_HEADS, seq, LANES), BF16)],
        compiler_params=_cparams("parallel"),
        name="qkv_rope",
    )(h, gain.reshape(1, d), w, qg, kg, cs, sn, bd)


def _attn_kernel(q_ref, k_ref, v_ref, o_ref, m_s, acc_s, s_buf, p_buf, a_buf, *, tq, rc):
    kv = pl.program_id(3)
    tk = k_ref.shape[2]
    chunks_per_head = tq // rc
    n_chunks = Q_PER_KV * chunks_per_head

    @pl.when(kv == 0)
    def _():
        m_s[...] = jnp.full_like(m_s, -jnp.inf)
        acc_s[...] = jnp.zeros_like(acc_s)

    def where(c):
        return c // chunks_per_head, pl.multiple_of((c % chunks_per_head) * rc, rc)

    def scores(c):
        g, r0 = where(c)
        q = q_ref[0, g, pl.ds(r0, rc), :]
        s_buf[...] = lax.dot_general(q, k_ref[0, 0], _NT, preferred_element_type=F32)

    def softmax(c):
        g, r0 = where(c)
        s = s_buf[...]
        m_prev = m_s[g, pl.ds(r0, rc), :]
        m_new = jnp.maximum(m_prev, jnp.max(s, axis=-1, keepdims=True))
        a_buf[...] = jnp.exp2(m_prev - m_new)
        for t in range(tk // LANES):
            p_buf[:, t * LANES:(t + 1) * LANES] = jnp.exp2(
                s[:, t * LANES:(t + 1) * LANES] - m_new).astype(BF16)
        m_s[g, pl.ds(r0, rc), :] = m_new

    def values(c):
        g, r0 = where(c)
        pv = jnp.dot(p_buf[...], v_ref[0, 0], preferred_element_type=F32)
        acc_s[g, pl.ds(r0, rc), :] = a_buf[...] * acc_s[g, pl.ds(r0, rc), :] + pv

    scores(0)
    softmax(0)
    scores(1)

    def body(c, _):
        values(c)
        softmax(c + 1)
        scores(c + 2)
        return 0

    lax.fori_loop(0, n_chunks - 2, body, 0)
    values(n_chunks - 2)
    softmax(n_chunks - 1)
    values(n_chunks - 1)

    @pl.when(kv == pl.num_programs(3) - 1)
    def _():
        outs = []
        for g in range(Q_PER_KV):
            acc = acc_s[g]
            o = acc / pltpu.roll(acc, HEAD_DIM, axis=1)
            outs.append(o[:, 0:HEAD_DIM])
        o_ref[...] = jnp.concatenate(outs, axis=-1).astype(o_ref.dtype)


def _attention(q, k, v, tq, tk, rc):
    bsz, _, seq, _ = q.shape
    n_q = seq // tq
    return pl.pallas_call(
        functools.partial(_attn_kernel, tq=tq, rc=rc),
        grid=(bsz, N_KV_HEADS, n_q, seq // tk),
        in_specs=[pl.BlockSpec((1, Q_PER_KV, tq, HEAD_DIM), lambda b, j, i, t: (b, j, i, 0)),
                  pl.BlockSpec((1, 1, tk, HEAD_DIM), lambda b, j, i, t: (b, j, t, 0)),
                  pl.BlockSpec((1, 1, tk, LANES), lambda b, j, i, t: (b, j, t, 0))],
        out_specs=pl.BlockSpec((tq, Q_PER_KV * HEAD_DIM), lambda b, j, i, t: (b * n_q + i, j)),
        out_shape=jax.ShapeDtypeStruct((bsz * seq, N_HEADS * HEAD_DIM), BF16),
        scratch_shapes=[pltpu.VMEM((Q_PER_KV, tq, LANES), F32),
                        pltpu.VMEM((Q_PER_KV, tq, LANES), F32),
                        pltpu.VMEM((rc, tk), F32),
                        pltpu.VMEM((rc, tk), BF16),
                        pltpu.VMEM((rc, LANES), F32)],
        compiler_params=_cparams("parallel", "parallel", "parallel", "arbitrary"),
        name="flash_attn",
    )(q, k, v)


def _proj_res_kernel(x_ref, h_ref, w_ref, g_ref, o_ref):
    mix = jnp.dot(x_ref[...], w_ref[...], preferred_element_type=F32)
    o_ref[...] = h_ref[...] + _rms(mix, g_ref[...])


def _proj_res(x, h, w, gain, tm):
    n, d = h.shape
    k = x.shape[1]
    return pl.pallas_call(
        _proj_res_kernel,
        grid=(n // tm,),
        in_specs=[pl.BlockSpec((tm, k), lambda i: (i, 0)),
                  pl.BlockSpec((tm, d), lambda i: (i, 0)),
                  pl.BlockSpec((k, d), lambda i: (0, 0)),
                  pl.BlockSpec((1, d), lambda i: (0, 0))],
        out_specs=pl.BlockSpec((tm, d), lambda i: (i, 0)),
        out_shape=jax.ShapeDtypeStruct((n, d), F32),
        compiler_params=_cparams("parallel"),
        name="proj_res",
    )(x, h, w, gain.reshape(1, d))


def _s5_layer(h, bsz, seq, gains, w_in, lam_re, lam_im, log_step, b_re, b_im, c_re, c_im,
              d_skip, w_glu, w_out):
    n, d = h.shape
    seg_tokens = S5_CHUNK * S5_SEG_CHUNKS
    nseg = seq // seg_tokens
    nseq = bsz * nseg
    pairs = d // (2 * SSM_GROUP)
    u = _norm_matmul(h, gains[0], w_in.astype(BF16), tm=512)
    u_p = u.reshape(nseq, S5_SEG_CHUNKS, S5_CHUNK, pairs, 2, SSM_GROUP)
    u_p = jnp.transpose(u_p, (3, 1, 0, 4, 2, 5)).reshape(
        pairs, S5_SEG_CHUNKS * nseq, 2 * S5_CHUNK * SSM_GROUP)
    m, ws, wyt, pw, aa = _s5_prep(lam_re, lam_im, log_step, b_re, b_im, c_re, c_im, d_skip)
    w1, w2t, pw_p, aa_p = _s5_pair_layout(m, ws, wyt, pw, aa)
    y_p = _s5_core(u_p, w1, w2t, pw_p, aa_p, nseq, nseg)
    y = y_p.reshape(pairs, S5_SEG_CHUNKS, nseq, 2, S5_CHUNK, SSM_GROUP)
    y = jnp.transpose(y, (2, 1, 4, 0, 3, 5)).reshape(n, d)
    return _s5_out(y, h, w_glu.astype(BF16), w_out.astype(BF16), gains[1], tm=512)


def _attn_layer(h, bsz, seq, gains, w_qkv, q_gain, k_gain, w_out):
    q, k, v = _qkv(h, gains[0], w_qkv, q_gain, k_gain, bsz, seq, tm=512)
    o = _attention(q, k, v, tq=1024, tk=512, rc=256)
    return _proj_res(o, h, w_out.astype(BF16), gains[1], tm=512)


def _moe(h, gain_pre, gain_post, w_router, w_gate, w_up, w_down):
    n_exp = w_router.shape[-1]
    w_r = jnp.pad(w_router, ((0, 0), (0, LANES - n_exp)))
    return _ffn(h, gain_pre, gain_post, w_r, w_gate.astype(BF16), w_up.astype(BF16),
                w_down.astype(BF16), tm=512, tf=896)


def kernel(x, norm_gains, ssm_w_in, ssm_lambda_re, ssm_lambda_im, ssm_log_step, ssm_b_re,
           ssm_b_im, ssm_c_re, ssm_c_im, ssm_d, ssm_w_glu, ssm_w_out, ffn_w_gate, ffn_w_up,
           ffn_w_down, attn_w_qkv, attn_q_gain, attn_k_gain, attn_w_out, moe_w_router,
           moe_w_gate, moe_w_up, moe_w_down):
    bsz, seq, d = x.shape
    depth = norm_gains.shape[0]
    h = x.reshape(bsz * seq, d)
    no_router = jnp.zeros((d, LANES), F32)
    for i in range(depth):
        j = i // 2
        g = norm_gains[i]
        if i % 2 == 0:
            h = _s5_layer(h, bsz, seq, g, ssm_w_in[j], ssm_lambda_re[j], ssm_lambda_im[j],
                          ssm_log_step[j], ssm_b_re[j], ssm_b_im[j], ssm_c_re[j], ssm_c_im[j],
                          ssm_d[j], ssm_w_glu[j], ssm_w_out[j])
            h = _ffn(h, g[2], g[3], no_router, ffn_w_gate[j][None].astype(BF16),
                     ffn_w_up[j][None].astype(BF16), ffn_w_down[j][None].astype(BF16),
                     tm=512, tf=1408)
        else:
            h = _attn_layer(h, bsz, seq, g, attn_w_qkv[j], attn_q_gain[j], attn_k_gain[j],
                            attn_w_out[j])
            h = _moe(h, g[2], g[3], moe_w_router[j], moe_w_gate[j], moe_w_up[j], moe_w_down[j])
    return h.reshape(bsz, seq, d)
```

```python
import functools
import math

import jax
import jax.numpy as jnp
from jax import lax
from jax.experimental import pallas as pl
from jax.experimental.pallas import tpu as pltpu

F32 = jnp.float32
BF16 = jnp.bfloat16
NORM_EPS = 1e-6
ROPE_THETA = 10000.0
GRID_W = 64
N_HEADS = 16
N_KV_HEADS = 4
HEAD_DIM = 64
Q_PER_KV = N_HEADS // N_KV_HEADS
SSM_GROUP = 16
SSM_STATE = 64
S5_CHUNK = 16
S5_SEG_CHUNKS = 64
TOP_K = 2
LANES = 128
VMEM_LIMIT_BYTES = 56 * 1024 * 1024

_NT = (((1,), (1,)), ((), ()))


def _cparams(*sem):
    return pltpu.CompilerParams(dimension_semantics=sem, vmem_limit_bytes=VMEM_LIMIT_BYTES)


def _rms(x, gain):
    return x * lax.rsqrt(jnp.mean(x * x, axis=-1, keepdims=True) + NORM_EPS) * gain


def _sigmoid(x):
    return 1.0 / (1.0 + jnp.exp(-x))


def _gelu_tanh(x):
    return x * (0.5 * (1.0 + jnp.tanh(math.sqrt(2.0 / math.pi) * (x + 0.044715 * (x * x * x)))))


def _norm_matmul_kernel(x_ref, g_ref, w_ref, o_ref):
    xn = _rms(x_ref[...], g_ref[...]).astype(BF16)
    o_ref[...] = jnp.dot(xn, w_ref[...], preferred_element_type=F32).astype(o_ref.dtype)


def _norm_matmul(x, gain, w, tm):
    n, d = x.shape
    m = w.shape[1]
    return pl.pallas_call(
        _norm_matmul_kernel,
        grid=(n // tm,),
        in_specs=[pl.BlockSpec((tm, d), lambda i: (i, 0)),
                  pl.BlockSpec((1, d), lambda i: (0, 0)),
                  pl.BlockSpec((d, m), lambda i: (0, 0))],
        out_specs=pl.BlockSpec((tm, m), lambda i: (i, 0)),
        out_shape=jax.ShapeDtypeStruct((n, m), BF16),
        compiler_params=_cparams("parallel"),
        name="norm_matmul",
    )(x, gain.reshape(1, d), w)


def _s5_prep_kernel(lr_ref, li_ref, ls_ref, bre_ref, bim_ref, cre_ref, cim_ref, dt_ref,
                    m_ref, ws_ref, wyt_ref, pw_ref, aa_ref):
    t_n, s_n, p_n = S5_CHUNK, SSM_GROUP, SSM_STATE
    rows = t_n * s_n
    hi = lax.Precision.HIGHEST

    r_io = lax.broadcasted_iota(jnp.int32, (rows, t_n), 0)
    c_io = lax.broadcasted_iota(jnp.int32, (rows, t_n), 1)
    rep_t = ((r_io // s_n) == c_io).astype(F32)
    rep_s = ((r_io % s_n) == c_io).astype(F32)
    rr = lax.broadcasted_iota(jnp.int32, (rows, rows), 0) // s_n
    cc = lax.broadcasted_iota(jnp.int32, (rows, rows), 1) // s_n
    tvec = lax.broadcasted_iota(jnp.int32, (t_n, 1), 0).astype(F32)
    jvec = lax.broadcasted_iota(jnp.int32, (S5_SEG_CHUNKS, 1), 0).astype(F32)
    row8 = lax.broadcasted_iota(jnp.int32, (8, p_n), 0)

    def expand(tab, rep):
        return jnp.dot(rep, tab, precision=hi, preferred_element_type=F32)

    m_total = None
    for d in range(2):
        lr = lr_ref[0, d:d + 1, :]
        li = li_ref[0, d:d + 1, :]
        step = jnp.exp(ls_ref[0, d:d + 1, :])
        lsr = lr * step
        lsi = li * step

        def cpow(k):
            mag = jnp.exp(lsr * k)
            ang = lsi * k
            return mag * jnp.cos(ang), mag * jnp.sin(ang)

        a_re, a_im = cpow(jnp.ones((1, 1), F32))
        nr, ni = a_re - 1.0, a_im
        den = lr * lr + li * li
        q_re = (nr * lr + ni * li) / den
        q_im = (ni * lr - nr * li) / den
        br = bre_ref[0, d]
        bi = bim_ref[0, d]
        bb_re = q_re * br - q_im * bi
        bb_im = q_re * bi + q_im * br
        cr = cre_ref[0, d]
        ci = cim_ref[0, d]

        def outer(x_re, x_im, k):
            p_re, p_im = cpow(k)
            pe_re, pe_im = expand(p_re, rep_t), expand(p_im, rep_t)
            xe_re, xe_im = expand(x_re, rep_s), expand(x_im, rep_s)
            return xe_re * pe_re - xe_im * pe_im, xe_re * pe_im + xe_im * pe_re

        if d == 0:
            l_re, l_im = outer(bb_re, bb_im, -tvec)
            rt_re, rt_im = outer(cr, ci, tvec)
            ws_re, ws_im = outer(bb_re, bb_im, (t_n - 1.0) - tvec)
            wy_re, wy_im = outer(cr, ci, tvec + 1.0)
            mask = rr <= cc
            pw_re, pw_im = cpow(t_n * jvec)
        else:
            l_re, l_im = outer(bb_re, bb_im, tvec)
            rt_re, rt_im = outer(cr, ci, -tvec)
            ws_re, ws_im = l_re, l_im
            wy_re, wy_im = outer(cr, ci, t_n - tvec)
            mask = rr >= cc
            pw_re, pw_im = cpow(t_n * ((S5_SEG_CHUNKS - 1.0) - jvec))
        kern = (lax.dot_general(l_re, rt_re, _NT, precision=hi, preferred_element_type=F32)
                - lax.dot_general(l_im, rt_im, _NT, precision=hi, preferred_element_type=F32))
        kern = jnp.where(mask, kern, 0.0)
        m_total = kern if m_total is None else m_total + kern

        ws_ref[0, 2 * d] = ws_re
        ws_ref[0, 2 * d + 1] = ws_im
        wyt_ref[0, 2 * d] = wy_re
        wyt_ref[0, 2 * d + 1] = -wy_im
        pw_ref[0, 2 * d] = pw_re
        pw_ref[0, 2 * d + 1] = pw_im
        c_re, c_im = cpow(jnp.full((1, 1), float(t_n), F32))
        s_re, s_im = cpow(jnp.full((1, 1), float(t_n * S5_SEG_CHUNKS), F32))
        aa_ref[0, 2 * d] = jnp.where(row8 == 0, c_re, jnp.where(row8 == 1, s_re, 0.0))
        aa_ref[0, 2 * d + 1] = jnp.where(row8 == 0, c_im, jnp.where(row8 == 1, s_im, 0.0))

    diag = (lax.broadcasted_iota(jnp.int32, (rows, rows), 0)
            == lax.broadcasted_iota(jnp.int32, (rows, rows), 1))
    m_ref[0] = m_total + jnp.where(diag, dt_ref[0], 0.0)


def _s5_prep(lam_re, lam_im, log_step, b_re, b_im, c_re, c_im, d_skip):
    g_n = lam_re.shape[1]
    rows = S5_CHUNK * SSM_GROUP
    lr = jnp.transpose(lam_re, (1, 0, 2))
    li = jnp.transpose(lam_im, (1, 0, 2))
    ls = jnp.transpose(log_step, (1, 0))[:, :, None]
    bre = jnp.transpose(b_re, (1, 0, 3, 2))
    bim = jnp.transpose(b_im, (1, 0, 3, 2))
    cre = jnp.transpose(c_re, (1, 0, 2, 3))
    cim = jnp.transpose(c_im, (1, 0, 2, 3))
    dt = jnp.tile(d_skip.reshape(g_n, 1, SSM_GROUP), (1, 1, S5_CHUNK))

    def spec3(a, b):
        return pl.BlockSpec((1, a, b), lambda g: (g, 0, 0))

    def spec4(a, b, c):
        return pl.BlockSpec((1, a, b, c), lambda g: (g, 0, 0, 0))

    return pl.pallas_call(
        _s5_prep_kernel,
        grid=(g_n,),
        in_specs=[spec3(2, SSM_STATE), spec3(2, SSM_STATE), spec3(2, 1),
                  spec4(2, SSM_GROUP, SSM_STATE), spec4(2, SSM_GROUP, SSM_STATE),
                  spec4(2, SSM_GROUP, SSM_STATE), spec4(2, SSM_GROUP, SSM_STATE),
                  spec3(1, rows)],
        out_specs=[spec3(rows, rows), spec4(4, rows, SSM_STATE), spec4(4, rows, SSM_STATE),
                   spec4(4, S5_SEG_CHUNKS, SSM_STATE), spec4(4, 8, SSM_STATE)],
        out_shape=[jax.ShapeDtypeStruct((g_n, rows, rows), F32),
                   jax.ShapeDtypeStruct((g_n, 4, rows, SSM_STATE), F32),
                   jax.ShapeDtypeStruct((g_n, 4, rows, SSM_STATE), F32),
                   jax.ShapeDtypeStruct((g_n, 4, S5_SEG_CHUNKS, SSM_STATE), F32),
                   jax.ShapeDtypeStruct((g_n, 4, 8, SSM_STATE), F32)],
        compiler_params=_cparams("parallel"),
        name="s5_prep",
    )(lr, li, ls, bre, bim, cre, cim, dt)


def _s5_pair_layout(m, ws, wyt, pw, aa):
    g_n, rows, _ = m.shape
    pairs = g_n // 2
    eye = jnp.eye(2, dtype=F32)
    m_p = jnp.einsum('agrc,gh->agrhc', m.reshape(pairs, 2, rows, rows), eye)
    m_p = m_p.reshape(pairs, 2 * rows, 2 * rows)

    def spread(x):
        x = x.reshape(pairs, 2, 4, rows, SSM_STATE)
        x = jnp.einsum('agkrp,gh->agrkhp', x, eye)
        return x.reshape(pairs, 2 * rows, 8 * SSM_STATE)

    w1 = jnp.concatenate([m_p, spread(ws)], axis=-1).astype(BF16)
    w2t = spread(wyt).astype(BF16)

    def lanes(x):
        r = x.shape[2]
        x = x.reshape(pairs, 2, 4, r, SSM_STATE)
        return jnp.transpose(x, (0, 3, 2, 1, 4)).reshape(pairs, r, 8 * SSM_STATE)

    return w1, w2t, lanes(pw), lanes(aa)


def _s5_core_kernel(u_ref, w1_ref, w2t_ref, pw_ref, aa_ref, y_ref, z_s, x_s, xb_s,
                    *, nseq, nseg):
    n_chunks = S5_SEG_CHUNKS
    rows = n_chunks * nseq
    half = 2 * LANES
    ycols = y_ref.shape[-1]

    z_s[...] = jnp.dot(u_ref[0], w1_ref[0], preferred_element_type=F32)

    aa = aa_ref[0]
    a_fr, a_fi = aa[0:1, 0:LANES], aa[0:1, LANES:2 * LANES]
    a_br, a_bi = aa[0:1, 2 * LANES:3 * LANES], aa[0:1, 3 * LANES:4 * LANES]
    g_fr, g_fi = aa[1:2, 0:LANES], aa[1:2, LANES:2 * LANES]
    g_br, g_bi = aa[1:2, 2 * LANES:3 * LANES], aa[1:2, 3 * LANES:4 * LANES]

    x_s[nseq:2 * nseq, 0:half] = jnp.zeros((nseq, half), F32)
    x_s[rows:rows + nseq, half:2 * half] = jnp.zeros((nseq, half), F32)
    zero = jnp.zeros((nseq, LANES), F32)

    def fwd(j, carry):
        xr, xi = carry
        r0 = pl.multiple_of(j * nseq, nseq)
        sr = z_s[pl.ds(r0, nseq), ycols:ycols + LANES]
        si = z_s[pl.ds(r0, nseq), ycols + LANES:ycols + 2 * LANES]
        nr = a_fr * xr - a_fi * xi + sr
        ni = a_fr * xi + a_fi * xr + si
        w0 = pl.multiple_of(r0 + 2 * nseq, nseq)
        x_s[pl.ds(w0, nseq), 0:LANES] = nr
        x_s[pl.ds(w0, nseq), LANES:2 * LANES] = ni
        return nr, ni

    def bwd(i, carry):
        xr, xi = carry
        j = n_chunks - 1 - i
        r0 = pl.multiple_of(j * nseq, nseq)
        sr = z_s[pl.ds(r0, nseq), ycols + 2 * LANES:ycols + 3 * LANES]
        si = z_s[pl.ds(r0, nseq), ycols + 3 * LANES:ycols + 4 * LANES]
        nr = a_br * xr - a_bi * xi + sr
        ni = a_br * xi + a_bi * xr + si
        x_s[pl.ds(r0, nseq), 2 * LANES:3 * LANES] = nr
        x_s[pl.ds(r0, nseq), 3 * LANES:4 * LANES] = ni
        return nr, ni

    ef_r, ef_i = lax.fori_loop(0, n_chunks, fwd, (zero, zero))
    eb_r, eb_i = lax.fori_loop(0, n_chunks, bwd, (zero, zero))

    def seg_carries(e_r, e_i, g_r, g_i, reverse):
        out_r = [None] * nseq
        out_i = [None] * nseq
        for b in range(nseq // nseg):
            c_r = jnp.zeros((1, LANES), F32)
            c_i = jnp.zeros((1, LANES), F32)
            order = range(nseg - 1, -1, -1) if reverse else range(nseg)
            for s in order:
                r = b * nseg + s
                out_r[r], out_i[r] = c_r, c_i
                n_r = g_r * c_r - g_i * c_i + e_r[r:r + 1, :]
                n_i = g_r * c_i + g_i * c_r + e_i[r:r + 1, :]
                c_r, c_i = n_r, n_i
        return jnp.concatenate(out_r, axis=0), jnp.concatenate(out_i, axis=0)

    cf_r, cf_i = seg_carries(ef_r, ef_i, g_fr, g_fi, False)
    cb_r, cb_i = seg_carries(eb_r, eb_i, g_br, g_bi, True)

    def fix(j, _):
        r0 = pl.multiple_of(j * nseq, nseq)
        rd = pl.multiple_of(r0 + nseq, nseq)
        p = pw_ref[0, pl.ds(j, 1), :]
        p_fr, p_fi = p[:, 0:LANES], p[:, LANES:2 * LANES]
        p_br, p_bi = p[:, 2 * LANES:3 * LANES], p[:, 3 * LANES:4 * LANES]
        xfr = x_s[pl.ds(rd, nseq), 0:LANES] + (p_fr * cf_r - p_fi * cf_i)
        xfi = x_s[pl.ds(rd, nseq), LANES:2 * LANES] + (p_fr * cf_i + p_fi * cf_r)
        xbr = x_s[pl.ds(rd, nseq), 2 * LANES:3 * LANES] + (p_br * cb_r - p_bi * cb_i)
        xbi = x_s[pl.ds(rd, nseq), 3 * LANES:4 * LANES] + (p_br * cb_i + p_bi * cb_r)
        xb_s[pl.ds(r0, nseq), 0:LANES] = xfr.astype(BF16)
        xb_s[pl.ds(r0, nseq), LANES:2 * LANES] = xfi.astype(BF16)
        xb_s[pl.ds(r0, nseq), 2 * LANES:3 * LANES] = xbr.astype(BF16)
        xb_s[pl.ds(r0, nseq), 3 * LANES:4 * LANES] = xbi.astype(BF16)
        return 0

    lax.fori_loop(0, n_chunks, fix, 0)

    inter = lax.dot_general(xb_s[...], w2t_ref[0], _NT, preferred_element_type=F32)
    y_ref[0] = (z_s[:, 0:ycols] + inter).astype(y_ref.dtype)


def _s5_core(u_p, w1, w2t, pw, aa, nseq, nseg):
    pairs, rows, width = u_p.shape
    return pl.pallas_call(
        functools.partial(_s5_core_kernel, nseq=nseq, nseg=nseg),
        grid=(pairs,),
        in_specs=[pl.BlockSpec((1, rows, width), lambda g: (g, 0, 0)),
                  pl.BlockSpec((1, width, w1.shape[2]), lambda g: (g, 0, 0)),
                  pl.BlockSpec((1, width, w2t.shape[2]), lambda g: (g, 0, 0)),
                  pl.BlockSpec((1, S5_SEG_CHUNKS, pw.shape[2]), lambda g: (g, 0, 0)),
                  pl.BlockSpec((1, 8, aa.shape[2]), lambda g: (g, 0, 0))],
        out_specs=pl.BlockSpec((1, rows, width), lambda g: (g, 0, 0)),
        out_shape=jax.ShapeDtypeStruct((pairs, rows, width), BF16),
        scratch_shapes=[pltpu.VMEM((rows, w1.shape[2]), F32),
                        pltpu.VMEM((rows + 2 * nseq, 4 * LANES), F32),
                        pltpu.VMEM((rows, 4 * LANES), BF16)],
        compiler_params=_cparams("parallel"),
        name="s5_core",
    )(u_p, w1, w2t, pw, aa)


def _s5_out_kernel(y_ref, h_ref, wglu_ref, wout_ref, g_ref, o_ref):
    g = _gelu_tanh(y_ref[...].astype(F32))
    z = jnp.dot(g.astype(BF16), wglu_ref[...], preferred_element_type=F32)
    g2 = g * _sigmoid(z)
    mix = jnp.dot(g2.astype(BF16), wout_ref[...], preferred_element_type=F32)
    o_ref[...] = h_ref[...] + _rms(mix, g_ref[...])


def _s5_out(y, h, w_glu, w_out, gain, tm):
    n, d = h.shape
    return pl.pallas_call(
        _s5_out_kernel,
        grid=(n // tm,),
        in_specs=[pl.BlockSpec((tm, d), lambda i: (i, 0)),
                  pl.BlockSpec((tm, d), lambda i: (i, 0)),
                  pl.BlockSpec((d, d), lambda i: (0, 0)),
                  pl.BlockSpec((d, d), lambda i: (0, 0)),
                  pl.BlockSpec((1, d), lambda i: (0, 0))],
        out_specs=pl.BlockSpec((tm, d), lambda i: (i, 0)),
        out_shape=jax.ShapeDtypeStruct((n, d), F32),
        compiler_params=_cparams("parallel"),
        name="s5_out",
    )(y, h, w_glu, w_out, gain.reshape(1, d))


def _ffn_kernel(h_ref, gpre_ref, gpost_ref, wr_ref, wg_ref, wu_ref, wd_ref, o_ref,
                hn_s, gate_s, acc_s, *, n_exp):
    e = pl.program_id(1)
    f = pl.program_id(2)

    @pl.when((e == 0) & (f == 0))
    def _():
        xn = _rms(h_ref[...], gpre_ref[...])
        hn_s[...] = xn.astype(BF16)
        acc_s[...] = jnp.zeros_like(acc_s)
        if n_exp > 1:
            logits = jnp.dot(xn, wr_ref[...], precision=lax.Precision.HIGHEST,
                             preferred_element_type=F32)
            lane = lax.broadcasted_iota(jnp.int32, logits.shape, 1)
            neg = jnp.float32(-jnp.inf)
            logits = jnp.where(lane < n_exp, logits, neg)
            m1 = jnp.max(logits, axis=-1, keepdims=True)
            i1 = jnp.min(jnp.where(logits == m1, lane, LANES), axis=-1, keepdims=True)
            rest = jnp.where(lane == i1, neg, logits)
            m2 = jnp.max(rest, axis=-1, keepdims=True)
            i2 = jnp.min(jnp.where(rest == m2, lane, LANES), axis=-1, keepdims=True)
            e2 = jnp.exp(m2 - m1)
            w1 = 1.0 / (1.0 + e2)
            w2 = e2 / (1.0 + e2)
            gate_s[...] = jnp.where(lane == i1, w1, 0.0) + jnp.where(lane == i2, w2, 0.0)

    hn = hn_s[...]
    a = jnp.dot(hn, wg_ref[0], preferred_element_type=F32)
    u = jnp.dot(hn, wu_ref[0], preferred_element_type=F32)
    act = a * _sigmoid(a) * u
    if n_exp > 1:
        lane = lax.broadcasted_iota(jnp.int32, gate_s.shape, 1)
        gcol = jnp.sum(jnp.where(lane == e, gate_s[...], 0.0), axis=-1, keepdims=True)
        act = act * gcol
    acc_s[...] += jnp.dot(act.astype(BF16), wd_ref[0], preferred_element_type=F32)

    @pl.when((e == n_exp - 1) & (f == pl.num_programs(2) - 1))
    def _():
        o_ref[...] = h_ref[...] + _rms(acc_s[...], gpost_ref[...])


def _ffn(h, gain_pre, gain_post, w_router, w_gate, w_up, w_down, tm, tf):
    n, d = h.shape
    n_exp, _, d_ff = w_gate.shape
    return pl.pallas_call(
        functools.partial(_ffn_kernel, n_exp=n_exp),
        grid=(n // tm, n_exp, d_ff // tf),
        in_specs=[pl.BlockSpec((tm, d), lambda i, e, f: (i, 0)),
                  pl.BlockSpec((1, d), lambda i, e, f: (0, 0)),
                  pl.BlockSpec((1, d), lambda i, e, f: (0, 0)),
                  pl.BlockSpec((d, LANES), lambda i, e, f: (0, 0)),
                  pl.BlockSpec((1, d, tf), lambda i, e, f: (e, 0, f)),
                  pl.BlockSpec((1, d, tf), lambda i, e, f: (e, 0, f)),
                  pl.BlockSpec((1, tf, d), lambda i, e, f: (e, f, 0))],
        out_specs=pl.BlockSpec((tm, d), lambda i, e, f: (i, 0)),
        out_shape=jax.ShapeDtypeStruct((n, d), F32),
        scratch_shapes=[pltpu.VMEM((tm, d), BF16),
                        pltpu.VMEM((tm, LANES), F32),
                        pltpu.VMEM((tm, d), F32)],
        compiler_params=_cparams("parallel", "arbitrary", "arbitrary"),
        name="ffn" if n_exp == 1 else "moe_ffn",
    )(h, gain_pre.reshape(1, d), gain_post.reshape(1, d), w_router, w_gate, w_up, w_down)


def _qkv_kernel(h_ref, g_ref, w_ref, qg_ref, kg_ref, cs_ref, sn_ref, bd_ref,
                q_ref, k_ref, v_ref):
    xn = _rms(h_ref[...], g_ref[...]).astype(BF16)
    qkv = jnp.dot(xn, w_ref[...], preferred_element_type=F32)
    cs = cs_ref[...]
    sn = sn_ref[...]
    bd = bd_ref[...]
    lane = lax.broadcasted_iota(jnp.int32, cs.shape, 1)
    first_half = (lane % HEAD_DIM) < (HEAD_DIM // 2)
    scale = math.log2(math.e) / math.sqrt(HEAD_DIM)

    def norm_rope(x, gain):
        ms = jnp.dot(x * x, bd, precision=lax.Precision.HIGHEST, preferred_element_type=F32)
        y = x * lax.rsqrt(ms + NORM_EPS) * gain
        partner = jnp.where(first_half,
                            pltpu.roll(y, LANES - HEAD_DIM // 2, axis=1),
                            pltpu.roll(y, HEAD_DIM // 2, axis=1))
        return y * cs + partner * sn

    n_q_tiles = N_HEADS * HEAD_DIM // LANES
    for t in range(n_q_tiles):
        y = norm_rope(qkv[:, t * LANES:(t + 1) * LANES], qg_ref[...]) * scale
        q_ref[0, 2 * t] = y[:, 0:HEAD_DIM].astype(BF16)
        q_ref[0, 2 * t + 1] = y[:, HEAD_DIM:LANES].astype(BF16)
    k0 = N_HEADS * HEAD_DIM
    for t in range(N_KV_HEADS * HEAD_DIM // LANES):
        y = norm_rope(qkv[:, k0 + t * LANES:k0 + (t + 1) * LANES], kg_ref[...])
        k_ref[0, 2 * t] = y[:, 0:HEAD_DIM].astype(BF16)
        k_ref[0, 2 * t + 1] = y[:, HEAD_DIM:LANES].astype(BF16)
    v0 = (N_HEADS + N_KV_HEADS) * HEAD_DIM
    ones = jnp.ones((qkv.shape[0], LANES - HEAD_DIM), BF16)
    for j in range(N_KV_HEADS):
        vj = qkv[:, v0 + j * HEAD_DIM:v0 + (j + 1) * HEAD_DIM].astype(BF16)
        v_ref[0, j] = jnp.concatenate([vj, ones], axis=-1)


def _rope_tables(seq):
    axis_dim = HEAD_DIM // 2
    freqs = ROPE_THETA ** (-jnp.arange(0, axis_dim, 2, dtype=F32) / axis_dim)
    rows = seq // GRID_W
    row_ang = jnp.arange(rows, dtype=F32)[:, None] * freqs
    col_ang = jnp.arange(GRID_W, dtype=F32)[:, None] * freqs
    ang = jnp.concatenate([
        jnp.broadcast_to(row_ang[:, None, :], (rows, GRID_W, freqs.shape[0])),
        jnp.broadcast_to(col_ang[None, :, :], (rows, GRID_W, freqs.shape[0]))], axis=-1)
    ang = ang.reshape(seq, HEAD_DIM // 2)
    cos, sin = jnp.cos(ang), jnp.sin(ang)
    cs = jnp.tile(jnp.concatenate([cos, cos], axis=-1), (1, LANES // HEAD_DIM))
    sn = jnp.tile(jnp.concatenate([-sin, sin], axis=-1), (1, LANES // HEAD_DIM))
    return cs, sn


def _qkv(h, gain, w_qkv, q_gain, k_gain, bsz, seq, tm):
    n, d = h.shape
    width = w_qkv.shape[1]
    perm = jnp.concatenate([jnp.arange(0, HEAD_DIM, 2), jnp.arange(1, HEAD_DIM, 2)])
    n_rot = N_HEADS + N_KV_HEADS
    cols = (jnp.arange(n_rot)[:, None] * HEAD_DIM + perm[None, :]).reshape(-1)
    cols = jnp.concatenate([cols, jnp.arange(n_rot * HEAD_DIM, width)])
    w = w_qkv[:, cols].astype(BF16)
    qg = jnp.tile(q_gain[perm], LANES // HEAD_DIM).reshape(1, LANES)
    kg = jnp.tile(k_gain[perm], LANES // HEAD_DIM).reshape(1, LANES)
    cs, sn = _rope_tables(seq)
    blk = jnp.arange(LANES) // HEAD_DIM
    bd = (blk[:, None] == blk[None, :]).astype(F32) / HEAD_DIM
    per_seq = seq // tm
    return pl.pallas_call(
        _qkv_kernel,
        grid=(n // tm,),
        in_specs=[pl.BlockSpec((tm, d), lambda i: (i, 0)),
                  pl.BlockSpec((1, d), lambda i: (0, 0)),
                  pl.BlockSpec((d, width), lambda i: (0, 0)),
                  pl.BlockSpec((1, LANES), lambda i: (0, 0)),
                  pl.BlockSpec((1, LANES), lambda i: (0, 0)),
                  pl.BlockSpec((tm, LANES), lambda i: (i % per_seq, 0)),
                  pl.BlockSpec((tm, LANES), lambda i: (i % per_seq, 0)),
                  pl.BlockSpec((LANES, LANES), lambda i: (0, 0))],
        out_specs=[pl.BlockSpec((1, N_HEADS, tm, HEAD_DIM),
                                lambda i: (i // per_seq, 0, i % per_seq, 0)),
                   pl.BlockSpec((1, N_KV_HEADS, tm, HEAD_DIM),
                                lambda i: (i // per_seq, 0, i % per_seq, 0)),
                   pl.BlockSpec((1, N_KV_HEADS, tm, LANES),
                                lambda i: (i // per_seq, 0, i % per_seq, 0))],
        out_shape=[jax.ShapeDtypeStruct((bsz, N_HEADS, seq, HEAD_DIM), BF16),
                   jax.ShapeDtypeStruct((bsz, N_KV_HEADS, seq, HEAD_DIM), BF16),
                   jax.ShapeDtypeStruct((bsz, N_KV_HEADS, seq, LANES), BF16)],
        compiler_params=_cparams("parallel"),
        name="qkv_rope",
    )(h, gain.reshape(1, d), w, qg, kg, cs, sn, bd)


def _attn_kernel(q_ref, k_ref, v_ref, o_ref, m_s, acc_s, s_buf, p_buf, a_buf, *, tq, rc):
    kv = pl.program_id(3)
    tk = k_ref.shape[2]
    chunks_per_head = tq // rc
    n_chunks = Q_PER_KV * chunks_per_head

    @pl.when(kv == 0)
    def _():
        m_s[...] = jnp.full_like(m_s, -jnp.inf)
        acc_s[...] = jnp.zeros_like(acc_s)

    def where(c):
        return c // chunks_per_head, pl.multiple_of((c % chunks_per_head) * rc, rc)

    def scores(c):
        g, r0 = where(c)
        q = q_ref[0, g, pl.ds(r0, rc), :]
        s_buf[...] = lax.dot_general(q, k_ref[0, 0], _NT, preferred_element_type=F32)

    def softmax(c):
        g, r0 = where(c)
        s = s_buf[...]
        m_prev = m_s[g, pl.ds(r0, rc), :]
        m_new = jnp.maximum(m_prev, jnp.max(s, axis=-1, keepdims=True))
        a_buf[...] = jnp.exp2(m_prev - m_new)
        for t in range(tk // LANES):
            p_buf[:, t * LANES:(t + 1) * LANES] = jnp.exp2(
                s[:, t * LANES:(t + 1) * LANES] - m_new).astype(BF16)
        m_s[g, pl.ds(r0, rc), :] = m_new

    def values(c):
        g, r0 = where(c)
        pv = jnp.dot(p_buf[...], v_ref[0, 0], preferred_element_type=F32)
        acc_s[g, pl.ds(r0, rc), :] = a_buf[...] * acc_s[g, pl.ds(r0, rc), :] + pv

    scores(0)
    softmax(0)
    scores(1)

    def body(c, _):
        values(c)
        softmax(c + 1)
        scores(c + 2)
        return 0

    lax.fori_loop(0, n_chunks - 2, body, 0, unroll=5)
    values(n_chunks - 2)
    softmax(n_chunks - 1)
    values(n_chunks - 1)

    @pl.when(kv == pl.num_programs(3) - 1)
    def _():
        outs = []
        for g in range(Q_PER_KV):
            acc = acc_s[g]
            o = acc / pltpu.roll(acc, HEAD_DIM, axis=1)
            outs.append(o[:, 0:HEAD_DIM])
        o_ref[...] = jnp.concatenate(outs, axis=-1).astype(o_ref.dtype)


def _attention(q, k, v, tq, tk, rc):
    bsz, _, seq, _ = q.shape
    n_q = seq // tq
    return pl.pallas_call(
        functools.partial(_attn_kernel, tq=tq, rc=rc),
        grid=(bsz, N_KV_HEADS, n_q, seq // tk),
        in_specs=[pl.BlockSpec((1, Q_PER_KV, tq, HEAD_DIM), lambda b, j, i, t: (b, j, i, 0)),
                  pl.BlockSpec((1, 1, tk, HEAD_DIM), lambda b, j, i, t: (b, j, t, 0)),
                  pl.BlockSpec((1, 1, tk, LANES), lambda b, j, i, t: (b, j, t, 0))],
        out_specs=pl.BlockSpec((tq, Q_PER_KV * HEAD_DIM), lambda b, j, i, t: (b * n_q + i, j)),
        out_shape=jax.ShapeDtypeStruct((bsz * seq, N_HEADS * HEAD_DIM), BF16),
        scratch_shapes=[pltpu.VMEM((Q_PER_KV, tq, LANES), F32),
                        pltpu.VMEM((Q_PER_KV, tq, LANES), F32),
                        pltpu.VMEM((rc, tk), F32),
                        pltpu.VMEM((rc, tk), BF16),
                        pltpu.VMEM((rc, LANES), F32)],
        compiler_params=_cparams("parallel", "parallel", "parallel", "arbitrary"),
        name="flash_attn",
    )(q, k, v)


def _proj_res_kernel(x_ref, h_ref, w_ref, g_ref, o_ref):
    mix = jnp.dot(x_ref[...], w_ref[...], preferred_element_type=F32)
    o_ref[...] = h_ref[...] + _rms(mix, g_ref[...])


def _proj_res(x, h, w, gain, tm):
    n, d = h.shape
    k = x.shape[1]
    return pl.pallas_call(
        _proj_res_kernel,
        grid=(n // tm,),
        in_specs=[pl.BlockSpec((tm, k), lambda i: (i, 0)),
                  pl.BlockSpec((tm, d), lambda i: (i, 0)),
                  pl.BlockSpec((k, d), lambda i: (0, 0)),
                  pl.BlockSpec((1, d), lambda i: (0, 0))],
        out_specs=pl.BlockSpec((tm, d), lambda i: (i, 0)),
        out_shape=jax.ShapeDtypeStruct((n, d), F32),
        compiler_params=_cparams("parallel"),
        name="proj_res",
    )(x, h, w, gain.reshape(1, d))


def _s5_layer(h, bsz, seq, gains, w_in, lam_re, lam_im, log_step, b_re, b_im, c_re, c_im,
              d_skip, w_glu, w_out):
    n, d = h.shape
    seg_tokens = S5_CHUNK * S5_SEG_CHUNKS
    nseg = seq // seg_tokens
    nseq = bsz * nseg
    pairs = d // (2 * SSM_GROUP)
    u = _norm_matmul(h, gains[0], w_in.astype(BF16), tm=512)
    u_p = u.reshape(nseq, S5_SEG_CHUNKS, S5_CHUNK, pairs, 2, SSM_GROUP)
    u_p = jnp.transpose(u_p, (3, 1, 0, 4, 2, 5)).reshape(
        pairs, S5_SEG_CHUNKS * nseq, 2 * S5_CHUNK * SSM_GROUP)
    m, ws, wyt, pw, aa = _s5_prep(lam_re, lam_im, log_step, b_re, b_im, c_re, c_im, d_skip)
    w1, w2t, pw_p, aa_p = _s5_pair_layout(m, ws, wyt, pw, aa)
    y_p = _s5_core(u_p, w1, w2t, pw_p, aa_p, nseq, nseg)
    y = y_p.reshape(pairs, S5_SEG_CHUNKS, nseq, 2, S5_CHUNK, SSM_GROUP)
    y = jnp.transpose(y, (2, 1, 4, 0, 3, 5)).reshape(n, d)
    return _s5_out(y, h, w_glu.astype(BF16), w_out.astype(BF16), gains[1], tm=512)


def _attn_layer(h, bsz, seq, gains, w_qkv, q_gain, k_gain, w_out):
    q, k, v = _qkv(h, gains[0], w_qkv, q_gain, k_gain, bsz, seq, tm=512)
    o = _attention(q, k, v, tq=2048, tk=512, rc=256)
    return _proj_res(o, h, w_out.astype(BF16), gains[1], tm=512)


def _moe(h, gain_pre, gain_post, w_router, w_gate, w_up, w_down):
    n_exp = w_router.shape[-1]
    w_r = jnp.pad(w_router, ((0, 0), (0, LANES - n_exp)))
    return _ffn(h, gain_pre, gain_post, w_r, w_gate.astype(BF16), w_up.astype(BF16),
                w_down.astype(BF16), tm=512, tf=896)


def kernel(x, norm_gains, ssm_w_in, ssm_lambda_re, ssm_lambda_im, ssm_log_step, ssm_b_re,
           ssm_b_im, ssm_c_re, ssm_c_im, ssm_d, ssm_w_glu, ssm_w_out, ffn_w_gate, ffn_w_up,
           ffn_w_down, attn_w_qkv, attn_q_gain, attn_k_gain, attn_w_out, moe_w_router,
           moe_w_gate, moe_w_up, moe_w_down):
    bsz, seq, d = x.shape
    depth = norm_gains.shape[0]
    h = x.reshape(bsz * seq, d)
    no_router = jnp.zeros((d, LANES), F32)
    for i in range(depth):
        j = i // 2
        g = norm_gains[i]
        if i % 2 == 0:
            h = _s5_layer(h, bsz, seq, g, ssm_w_in[j], ssm_lambda_re[j], ssm_lambda_im[j],
                          ssm_log_step[j], ssm_b_re[j], ssm_b_im[j], ssm_c_re[j], ssm_c_im[j],
                          ssm_d[j], ssm_w_glu[j], ssm_w_out[j])
            h = _ffn(h, g[2], g[3], no_router, ffn_w_gate[j][None].astype(BF16),
                     ffn_w_up[j][None].astype(BF16), ffn_w_down[j][None].astype(BF16),
                     tm=512, tf=1408)
        else:
            h = _attn_layer(h, bsz, seq, g, attn_w_qkv[j], attn_q_gain[j], attn_k_gain[j],
                            attn_w_out[j])
            h = _moe(h, g[2], g[3], moe_w_router[j], moe_w_gate[j], moe_w_up[j], moe_w_down[j])
    return h.reshape(bsz, seq, d)
```

```python
import functools
import math

import jax
import jax.numpy as jnp
from jax import lax
from jax.experimental import pallas as pl
from jax.experimental.pallas import tpu as pltpu

F32 = jnp.float32
BF16 = jnp.bfloat16
NORM_EPS = 1e-6
ROPE_THETA = 10000.0
GRID_W = 64
N_HEADS = 16
N_KV_HEADS = 4
HEAD_DIM = 64
Q_PER_KV = N_HEADS // N_KV_HEADS
SSM_GROUP = 16
SSM_STATE = 64
S5_CHUNK = 16
S5_SEG_CHUNKS = 64
TOP_K = 2
LANES = 128
VMEM_LIMIT_BYTES = 56 * 1024 * 1024
MOE_VMEM_LIMIT_BYTES = 60 * 1024 * 1024

_NT = (((1,), (1,)), ((), ()))


def _cparams(*sem):
    return pltpu.CompilerParams(dimension_semantics=sem, vmem_limit_bytes=VMEM_LIMIT_BYTES)


def _rms(x, gain):
    return x * lax.rsqrt(jnp.mean(x * x, axis=-1, keepdims=True) + NORM_EPS) * gain


def _sigmoid(x):
    return 1.0 / (1.0 + jnp.exp(-x))


def _gelu_tanh(x):
    return x * (0.5 * (1.0 + jnp.tanh(math.sqrt(2.0 / math.pi) * (x + 0.044715 * (x * x * x)))))


def _norm_matmul_kernel(x_ref, g_ref, w_ref, o_ref):
    xn = _rms(x_ref[...], g_ref[...]).astype(BF16)
    o_ref[...] = jnp.dot(xn, w_ref[...], preferred_element_type=F32).astype(o_ref.dtype)


def _norm_matmul(x, gain, w, tm):
    n, d = x.shape
    m = w.shape[1]
    return pl.pallas_call(
        _norm_matmul_kernel,
        grid=(n // tm,),
        in_specs=[pl.BlockSpec((tm, d), lambda i: (i, 0)),
                  pl.BlockSpec((1, d), lambda i: (0, 0)),
                  pl.BlockSpec((d, m), lambda i: (0, 0))],
        out_specs=pl.BlockSpec((tm, m), lambda i: (i, 0)),
        out_shape=jax.ShapeDtypeStruct((n, m), BF16),
        compiler_params=_cparams("parallel"),
        name="norm_matmul",
    )(x, gain.reshape(1, d), w)


def _s5_prep_kernel(lr_ref, li_ref, ls_ref, bre_ref, bim_ref, cre_ref, cim_ref, dt_ref,
                    m_ref, ws_ref, wyt_ref, pw_ref, aa_ref):
    t_n, s_n, p_n = S5_CHUNK, SSM_GROUP, SSM_STATE
    rows = t_n * s_n
    hi = lax.Precision.HIGHEST

    r_io = lax.broadcasted_iota(jnp.int32, (rows, t_n), 0)
    c_io = lax.broadcasted_iota(jnp.int32, (rows, t_n), 1)
    rep_t = ((r_io // s_n) == c_io).astype(F32)
    rep_s = ((r_io % s_n) == c_io).astype(F32)
    rr = lax.broadcasted_iota(jnp.int32, (rows, rows), 0) // s_n
    cc = lax.broadcasted_iota(jnp.int32, (rows, rows), 1) // s_n
    tvec = lax.broadcasted_iota(jnp.int32, (t_n, 1), 0).astype(F32)
    jvec = lax.broadcasted_iota(jnp.int32, (S5_SEG_CHUNKS, 1), 0).astype(F32)
    row8 = lax.broadcasted_iota(jnp.int32, (8, p_n), 0)

    def expand(tab, rep):
        return jnp.dot(rep, tab, precision=hi, preferred_element_type=F32)

    m_total = None
    for d in range(2):
        lr = lr_ref[0, d:d + 1, :]
        li = li_ref[0, d:d + 1, :]
        step = jnp.exp(ls_ref[0, d:d + 1, :])
        lsr = lr * step
        lsi = li * step

        def cpow(k):
            mag = jnp.exp(lsr * k)
            ang = lsi * k
            return mag * jnp.cos(ang), mag * jnp.sin(ang)

        a_re, a_im = cpow(jnp.ones((1, 1), F32))
        nr, ni = a_re - 1.0, a_im
        den = lr * lr + li * li
        q_re = (nr * lr + ni * li) / den
        q_im = (ni * lr - nr * li) / den
        br = bre_ref[0, d]
        bi = bim_ref[0, d]
        bb_re = q_re * br - q_im * bi
        bb_im = q_re * bi + q_im * br
        cr = cre_ref[0, d]
        ci = cim_ref[0, d]

        def outer(x_re, x_im, k):
            p_re, p_im = cpow(k)
            pe_re, pe_im = expand(p_re, rep_t), expand(p_im, rep_t)
            xe_re, xe_im = expand(x_re, rep_s), expand(x_im, rep_s)
            return xe_re * pe_re - xe_im * pe_im, xe_re * pe_im + xe_im * pe_re

        if d == 0:
            l_re, l_im = outer(bb_re, bb_im, -tvec)
            rt_re, rt_im = outer(cr, ci, tvec)
            ws_re, ws_im = outer(bb_re, bb_im, (t_n - 1.0) - tvec)
            wy_re, wy_im = outer(cr, ci, tvec + 1.0)
            mask = rr <= cc
            pw_re, pw_im = cpow(t_n * jvec)
        else:
            l_re, l_im = outer(bb_re, bb_im, tvec)
            rt_re, rt_im = outer(cr, ci, -tvec)
            ws_re, ws_im = l_re, l_im
            wy_re, wy_im = outer(cr, ci, t_n - tvec)
            mask = rr >= cc
            pw_re, pw_im = cpow(t_n * ((S5_SEG_CHUNKS - 1.0) - jvec))
        kern = (lax.dot_general(l_re, rt_re, _NT, precision=hi, preferred_element_type=F32)
                - lax.dot_general(l_im, rt_im, _NT, precision=hi, preferred_element_type=F32))
        kern = jnp.where(mask, kern, 0.0)
        m_total = kern if m_total is None else m_total + kern

        ws_ref[0, 2 * d] = ws_re
        ws_ref[0, 2 * d + 1] = ws_im
        wyt_ref[0, 2 * d] = wy_re
        wyt_ref[0, 2 * d + 1] = -wy_im
        pw_ref[0, 2 * d] = pw_re
        pw_ref[0, 2 * d + 1] = pw_im
        c_re, c_im = cpow(jnp.full((1, 1), float(t_n), F32))
        s_re, s_im = cpow(jnp.full((1, 1), float(t_n * S5_SEG_CHUNKS), F32))
        aa_ref[0, 2 * d] = jnp.where(row8 == 0, c_re, jnp.where(row8 == 1, s_re, 0.0))
        aa_ref[0, 2 * d + 1] = jnp.where(row8 == 0, c_im, jnp.where(row8 == 1, s_im, 0.0))

    diag = (lax.broadcasted_iota(jnp.int32, (rows, rows), 0)
            == lax.broadcasted_iota(jnp.int32, (rows, rows), 1))
    m_ref[0] = m_total + jnp.where(diag, dt_ref[0], 0.0)


def _s5_prep(lam_re, lam_im, log_step, b_re, b_im, c_re, c_im, d_skip):
    g_n = lam_re.shape[1]
    rows = S5_CHUNK * SSM_GROUP
    lr = jnp.transpose(lam_re, (1, 0, 2))
    li = jnp.transpose(lam_im, (1, 0, 2))
    ls = jnp.transpose(log_step, (1, 0))[:, :, None]
    bre = jnp.transpose(b_re, (1, 0, 3, 2))
    bim = jnp.transpose(b_im, (1, 0, 3, 2))
    cre = jnp.transpose(c_re, (1, 0, 2, 3))
    cim = jnp.transpose(c_im, (1, 0, 2, 3))
    dt = jnp.tile(d_skip.reshape(g_n, 1, SSM_GROUP), (1, 1, S5_CHUNK))

    def spec3(a, b):
        return pl.BlockSpec((1, a, b), lambda g: (g, 0, 0))

    def spec4(a, b, c):
        return pl.BlockSpec((1, a, b, c), lambda g: (g, 0, 0, 0))

    return pl.pallas_call(
        _s5_prep_kernel,
        grid=(g_n,),
        in_specs=[spec3(2, SSM_STATE), spec3(2, SSM_STATE), spec3(2, 1),
                  spec4(2, SSM_GROUP, SSM_STATE), spec4(2, SSM_GROUP, SSM_STATE),
                  spec4(2, SSM_GROUP, SSM_STATE), spec4(2, SSM_GROUP, SSM_STATE),
                  spec3(1, rows)],
        out_specs=[spec3(rows, rows), spec4(4, rows, SSM_STATE), spec4(4, rows, SSM_STATE),
                   spec4(4, S5_SEG_CHUNKS, SSM_STATE), spec4(4, 8, SSM_STATE)],
        out_shape=[jax.ShapeDtypeStruct((g_n, rows, rows), F32),
                   jax.ShapeDtypeStruct((g_n, 4, rows, SSM_STATE), F32),
                   jax.ShapeDtypeStruct((g_n, 4, rows, SSM_STATE), F32),
                   jax.ShapeDtypeStruct((g_n, 4, S5_SEG_CHUNKS, SSM_STATE), F32),
                   jax.ShapeDtypeStruct((g_n, 4, 8, SSM_STATE), F32)],
        compiler_params=_cparams("parallel"),
        name="s5_prep",
    )(lr, li, ls, bre, bim, cre, cim, dt)


def _s5_pair_layout(m, ws, wyt, pw, aa):
    g_n, rows, _ = m.shape
    pairs = g_n // 2
    eye = jnp.eye(2, dtype=F32)
    m_p = jnp.einsum('agrc,gh->agrhc', m.reshape(pairs, 2, rows, rows), eye)
    m_p = m_p.reshape(pairs, 2 * rows, 2 * rows)

    def spread(x):
        x = x.reshape(pairs, 2, 4, rows, SSM_STATE)
        x = jnp.einsum('agkrp,gh->agrkhp', x, eye)
        return x.reshape(pairs, 2 * rows, 8 * SSM_STATE)

    w1 = jnp.concatenate([m_p, spread(ws)], axis=-1).astype(BF16)
    w2t = spread(wyt).astype(BF16)

    def lanes(x):
        r = x.shape[2]
        x = x.reshape(pairs, 2, 4, r, SSM_STATE)
        return jnp.transpose(x, (0, 3, 2, 1, 4)).reshape(pairs, r, 8 * SSM_STATE)

    return w1, w2t, lanes(pw), lanes(aa)


def _s5_core_kernel(u_ref, w1_ref, w2t_ref, pw_ref, aa_ref, y_ref, z_s, x_s, xb_s,
                    *, nseq, nseg):
    n_chunks = S5_SEG_CHUNKS
    rows = n_chunks * nseq
    half = 2 * LANES
    ycols = y_ref.shape[-1]

    z_s[...] = jnp.dot(u_ref[0], w1_ref[0], preferred_element_type=F32)

    aa = aa_ref[0]
    a_fr, a_fi = aa[0:1, 0:LANES], aa[0:1, LANES:2 * LANES]
    a_br, a_bi = aa[0:1, 2 * LANES:3 * LANES], aa[0:1, 3 * LANES:4 * LANES]
    g_fr, g_fi = aa[1:2, 0:LANES], aa[1:2, LANES:2 * LANES]
    g_br, g_bi = aa[1:2, 2 * LANES:3 * LANES], aa[1:2, 3 * LANES:4 * LANES]

    x_s[nseq:2 * nseq, 0:half] = jnp.zeros((nseq, half), F32)
    x_s[rows:rows + nseq, half:2 * half] = jnp.zeros((nseq, half), F32)
    zero = jnp.zeros((nseq, LANES), F32)

    def fwd(j, carry):
        xr, xi = carry
        r0 = pl.multiple_of(j * nseq, nseq)
        sr = z_s[pl.ds(r0, nseq), ycols:ycols + LANES]
        si = z_s[pl.ds(r0, nseq), ycols + LANES:ycols + 2 * LANES]
        nr = a_fr * xr - a_fi * xi + sr
        ni = a_fr * xi + a_fi * xr + si
        w0 = pl.multiple_of(r0 + 2 * nseq, nseq)
        x_s[pl.ds(w0, nseq), 0:LANES] = nr
        x_s[pl.ds(w0, nseq), LANES:2 * LANES] = ni
        return nr, ni

    def bwd(i, carry):
        xr, xi = carry
        j = n_chunks - 1 - i
        r0 = pl.multiple_of(j * nseq, nseq)
        sr = z_s[pl.ds(r0, nseq), ycols + 2 * LANES:ycols + 3 * LANES]
        si = z_s[pl.ds(r0, nseq), ycols + 3 * LANES:ycols + 4 * LANES]
        nr = a_br * xr - a_bi * xi + sr
        ni = a_br * xi + a_bi * xr + si
        x_s[pl.ds(r0, nseq), 2 * LANES:3 * LANES] = nr
        x_s[pl.ds(r0, nseq), 3 * LANES:4 * LANES] = ni
        return nr, ni

    ef_r, ef_i = lax.fori_loop(0, n_chunks, fwd, (zero, zero))
    eb_r, eb_i = lax.fori_loop(0, n_chunks, bwd, (zero, zero))

    def seg_carries(e_r, e_i, g_r, g_i, reverse):
        out_r = [None] * nseq
        out_i = [None] * nseq
        for b in range(nseq // nseg):
            c_r = jnp.zeros((1, LANES), F32)
            c_i = jnp.zeros((1, LANES), F32)
            order = range(nseg - 1, -1, -1) if reverse else range(nseg)
            for s in order:
                r = b * nseg + s
                out_r[r], out_i[r] = c_r, c_i
                n_r = g_r * c_r - g_i * c_i + e_r[r:r + 1, :]
                n_i = g_r * c_i + g_i * c_r + e_i[r:r + 1, :]
                c_r, c_i = n_r, n_i
        return jnp.concatenate(out_r, axis=0), jnp.concatenate(out_i, axis=0)

    cf_r, cf_i = seg_carries(ef_r, ef_i, g_fr, g_fi, False)
    cb_r, cb_i = seg_carries(eb_r, eb_i, g_br, g_bi, True)

    def fix(j, _):
        r0 = pl.multiple_of(j * nseq, nseq)
        rd = pl.multiple_of(r0 + nseq, nseq)
        p = pw_ref[0, pl.ds(j, 1), :]
        p_fr, p_fi = p[:, 0:LANES], p[:, LANES:2 * LANES]
        p_br, p_bi = p[:, 2 * LANES:3 * LANES], p[:, 3 * LANES:4 * LANES]
        xfr = x_s[pl.ds(rd, nseq), 0:LANES] + (p_fr * cf_r - p_fi * cf_i)
        xfi = x_s[pl.ds(rd, nseq), LANES:2 * LANES] + (p_fr * cf_i + p_fi * cf_r)
        xbr = x_s[pl.ds(rd, nseq), 2 * LANES:3 * LANES] + (p_br * cb_r - p_bi * cb_i)
        xbi = x_s[pl.ds(rd, nseq), 3 * LANES:4 * LANES] + (p_br * cb_i + p_bi * cb_r)
        xb_s[pl.ds(r0, nseq), 0:LANES] = xfr.astype(BF16)
        xb_s[pl.ds(r0, nseq), LANES:2 * LANES] = xfi.astype(BF16)
        xb_s[pl.ds(r0, nseq), 2 * LANES:3 * LANES] = xbr.astype(BF16)
        xb_s[pl.ds(r0, nseq), 3 * LANES:4 * LANES] = xbi.astype(BF16)
        return 0

    lax.fori_loop(0, n_chunks, fix, 0)

    inter = lax.dot_general(xb_s[...], w2t_ref[0], _NT, preferred_element_type=F32)
    y_ref[0] = (z_s[:, 0:ycols] + inter).astype(y_ref.dtype)


def _s5_core(u_p, w1, w2t, pw, aa, nseq, nseg):
    pairs, rows, width = u_p.shape
    return pl.pallas_call(
        functools.partial(_s5_core_kernel, nseq=nseq, nseg=nseg),
        grid=(pairs,),
        in_specs=[pl.BlockSpec((1, rows, width), lambda g: (g, 0, 0)),
                  pl.BlockSpec((1, width, w1.shape[2]), lambda g: (g, 0, 0)),
                  pl.BlockSpec((1, width, w2t.shape[2]), lambda g: (g, 0, 0)),
                  pl.BlockSpec((1, S5_SEG_CHUNKS, pw.shape[2]), lambda g: (g, 0, 0)),
                  pl.BlockSpec((1, 8, aa.shape[2]), lambda g: (g, 0, 0))],
        out_specs=pl.BlockSpec((1, rows, width), lambda g: (g, 0, 0)),
        out_shape=jax.ShapeDtypeStruct((pairs, rows, width), BF16),
        scratch_shapes=[pltpu.VMEM((rows, w1.shape[2]), F32),
                        pltpu.VMEM((rows + 2 * nseq, 4 * LANES), F32),
                        pltpu.VMEM((rows, 4 * LANES), BF16)],
        compiler_params=_cparams("parallel"),
        name="s5_core",
    )(u_p, w1, w2t, pw, aa)


def _s5_out_kernel(y_ref, h_ref, wglu_ref, wout_ref, g_ref, o_ref):
    g = _gelu_tanh(y_ref[...].astype(F32))
    z = jnp.dot(g.astype(BF16), wglu_ref[...], preferred_element_type=F32)
    g2 = g * _sigmoid(z)
    mix = jnp.dot(g2.astype(BF16), wout_ref[...], preferred_element_type=F32)
    o_ref[...] = h_ref[...] + _rms(mix, g_ref[...])


def _s5_out(y, h, w_glu, w_out, gain, tm):
    n, d = h.shape
    return pl.pallas_call(
        _s5_out_kernel,
        grid=(n // tm,),
        in_specs=[pl.BlockSpec((tm, d), lambda i: (i, 0)),
                  pl.BlockSpec((tm, d), lambda i: (i, 0)),
                  pl.BlockSpec((d, d), lambda i: (0, 0)),
                  pl.BlockSpec((d, d), lambda i: (0, 0)),
                  pl.BlockSpec((1, d), lambda i: (0, 0))],
        out_specs=pl.BlockSpec((tm, d), lambda i: (i, 0)),
        out_shape=jax.ShapeDtypeStruct((n, d), F32),
        compiler_params=_cparams("parallel"),
        name="s5_out",
    )(y, h, w_glu, w_out, gain.reshape(1, d))


def _ffn_kernel(h_ref, gpre_ref, gpost_ref, wg_ref, wu_ref, wd_ref, o_ref, hn_s, acc_s):
    f = pl.program_id(1)

    @pl.when(f == 0)
    def _():
        hn_s[...] = _rms(h_ref[...], gpre_ref[...]).astype(BF16)
        acc_s[...] = jnp.zeros_like(acc_s)

    hn = hn_s[...]
    a = jnp.dot(hn, wg_ref[...], preferred_element_type=F32)
    u = jnp.dot(hn, wu_ref[...], preferred_element_type=F32)
    act = a * _sigmoid(a) * u
    acc_s[...] += jnp.dot(act.astype(BF16), wd_ref[...], preferred_element_type=F32)

    @pl.when(f == pl.num_programs(1) - 1)
    def _():
        o_ref[...] = h_ref[...] + _rms(acc_s[...], gpost_ref[...])


def _ffn(h, gain_pre, gain_post, w_gate, w_up, w_down, tm, tf):
    n, d = h.shape
    d_ff = w_gate.shape[1]
    return pl.pallas_call(
        _ffn_kernel,
        grid=(n // tm, d_ff // tf),
        in_specs=[pl.BlockSpec((tm, d), lambda i, f: (i, 0)),
                  pl.BlockSpec((1, d), lambda i, f: (0, 0)),
                  pl.BlockSpec((1, d), lambda i, f: (0, 0)),
                  pl.BlockSpec((d, tf), lambda i, f: (0, f)),
                  pl.BlockSpec((d, tf), lambda i, f: (0, f)),
                  pl.BlockSpec((tf, d), lambda i, f: (f, 0))],
        out_specs=pl.BlockSpec((tm, d), lambda i, f: (i, 0)),
        out_shape=jax.ShapeDtypeStruct((n, d), F32),
        scratch_shapes=[pltpu.VMEM((tm, d), BF16), pltpu.VMEM((tm, d), F32)],
        compiler_params=_cparams("parallel", "arbitrary"),
        name="ffn",
    )(h, gain_pre.reshape(1, d), gain_post.reshape(1, d), w_gate, w_up, w_down)


def _moe_kernel(h_ref, gpre_ref, gpost_ref, wr_ref, wg_ref, wu_ref, wd_ref, o_ref,
                hn_s, gate_s, sel_s, pos_s, post_s, xc_s, yc_s, cnt_s, *, n_exp, rt, strip):
    e = pl.program_id(1)
    f = pl.program_id(2)
    tb, d = hn_s.shape
    n_f = pl.num_programs(2)

    @pl.when((e == 0) & (f == 0))
    def _():
        def route(r, total):
            rows = pl.ds(pl.multiple_of(r * strip, strip), strip)
            xn = _rms(h_ref[rows, :], gpre_ref[...])
            hn_s[rows, :] = xn.astype(BF16)
            o_ref[rows, :] = jnp.zeros((strip, d), F32)
            logits = jnp.dot(xn, wr_ref[...], precision=lax.Precision.HIGHEST,
                             preferred_element_type=F32)
            lane = lax.broadcasted_iota(jnp.int32, logits.shape, 1)
            neg = jnp.float32(-jnp.inf)
            logits = jnp.where(lane < n_exp, logits, neg)
            m1 = jnp.max(logits, axis=-1, keepdims=True)
            i1 = jnp.min(jnp.where(logits == m1, lane, LANES), axis=-1, keepdims=True)
            rest = jnp.where(lane == i1, neg, logits)
            m2 = jnp.max(rest, axis=-1, keepdims=True)
            i2 = jnp.min(jnp.where(rest == m2, lane, LANES), axis=-1, keepdims=True)
            e2 = jnp.exp(m2 - m1)
            gate_s[rows, :] = (jnp.where(lane == i1, 1.0 / (1.0 + e2), 0.0)
                               + jnp.where(lane == i2, e2 / (1.0 + e2), 0.0))
            sel = ((lane == i1) | (lane == i2)).astype(F32)
            sel_s[rows, :] = sel.astype(BF16)
            return total + jnp.sum(sel, axis=0, keepdims=True)

        total = lax.fori_loop(0, tb // strip, route, jnp.zeros((1, LANES), F32))
        lane1 = lax.broadcasted_iota(jnp.int32, total.shape, 1)
        for x in range(n_exp):
            cnt_s[x] = jnp.sum(jnp.where(lane1 == x, total, 0.0)).astype(jnp.int32)

        def rank(r, _):
            r0 = pl.multiple_of(r * strip, strip)
            rows = lax.broadcasted_iota(jnp.int32, (strip, tb), 0) + r0
            cols = lax.broadcasted_iota(jnp.int32, (strip, tb), 1)
            before = (cols < rows).astype(BF16)
            cnt = jnp.dot(before, sel_s[...], preferred_element_type=F32)
            chosen = sel_s[pl.ds(r0, strip), :] > 0
            pos_s[pl.ds(r0, strip), :] = jnp.where(chosen, cnt, -1.0)
            return 0

        lax.fori_loop(0, tb // strip, rank, 0)
        for r in range(tb // strip):
            post_s[:, r * strip:(r + 1) * strip] = pos_s[r * strip:(r + 1) * strip, :].T

    n_tiles = (cnt_s[e] + rt - 1) // rt
    lane = lax.broadcasted_iota(jnp.int32, (tb, LANES), 1)
    pos_col = jnp.sum(jnp.where(lane == e, pos_s[...], 0.0), axis=-1, keepdims=True)
    gate_col = jnp.sum(jnp.where(lane == e, gate_s[...], 0.0), axis=-1, keepdims=True)

    def tile(i, _):
        r0 = pl.multiple_of(i * rt, rt)

        @pl.when(f == 0)
        def _():
            slot = (lax.broadcasted_iota(jnp.int32, (rt, tb), 0) + r0).astype(F32)
            pick = (post_s[pl.ds(e, 1), :] == slot).astype(BF16)
            xc_s[pl.ds(r0, rt), :] = jnp.dot(pick, hn_s[...],
                                             preferred_element_type=F32).astype(BF16)

        xc = xc_s[pl.ds(r0, rt), :]
        a = jnp.dot(xc, wg_ref[0], preferred_element_type=F32)
        u = jnp.dot(xc, wu_ref[0], preferred_element_type=F32)
        act = (a * _sigmoid(a) * u).astype(BF16)
        y = jnp.dot(act, wd_ref[0], preferred_element_type=F32)
        prev = jnp.where(f == 0, 0.0, yc_s[pl.ds(r0, rt), :])
        yc_s[pl.ds(r0, rt), :] = prev + y

        @pl.when(f == n_f - 1)
        def _():
            yc = yc_s[pl.ds(r0, rt), :].astype(BF16)
            slot = (lax.broadcasted_iota(jnp.int32, (strip, rt), 1) + r0).astype(F32)
            for r in range(tb // strip):
                rows = slice(r * strip, (r + 1) * strip)
                put = (pos_col[rows] == slot).astype(BF16)
                back = jnp.dot(put, yc, preferred_element_type=F32)
                o_ref[rows, :] += gate_col[rows] * back
        return 0

    lax.fori_loop(0, n_tiles, tile, 0)

    @pl.when((e == n_exp - 1) & (f == n_f - 1))
    def _():
        def finish(r, _):
            rows = pl.ds(pl.multiple_of(r * strip, strip), strip)
            o_ref[rows, :] = h_ref[rows, :] + _rms(o_ref[rows, :], gpost_ref[...])
            return 0

        lax.fori_loop(0, tb // strip, finish, 0)


def _moe(h, gain_pre, gain_post, w_router, w_gate, w_up, w_down, tb, tf, rt):
    n, d = h.shape
    n_exp, _, d_ff = w_gate.shape
    w_r = jnp.pad(w_router, ((0, 0), (0, LANES - n_exp)))
    once = pl.Buffered(1)
    return pl.pallas_call(
        functools.partial(_moe_kernel, n_exp=n_exp, rt=rt, strip=256),
        grid=(n // tb, n_exp, d_ff // tf),
        in_specs=[pl.BlockSpec((tb, d), lambda i, e, f: (i, 0), pipeline_mode=once),
                  pl.BlockSpec((1, d), lambda i, e, f: (0, 0)),
                  pl.BlockSpec((1, d), lambda i, e, f: (0, 0)),
                  pl.BlockSpec((d, LANES), lambda i, e, f: (0, 0)),
                  pl.BlockSpec((1, d, tf), lambda i, e, f: (e, 0, f)),
                  pl.BlockSpec((1, d, tf), lambda i, e, f: (e, 0, f)),
                  pl.BlockSpec((1, tf, d), lambda i, e, f: (e, f, 0))],
        out_specs=pl.BlockSpec((tb, d), lambda i, e, f: (i, 0), pipeline_mode=once),
        out_shape=jax.ShapeDtypeStruct((n, d), F32),
        scratch_shapes=[pltpu.VMEM((tb, d), BF16),
                        pltpu.VMEM((tb, LANES), F32),
                        pltpu.VMEM((tb, LANES), BF16),
                        pltpu.VMEM((tb, LANES), F32),
                        pltpu.VMEM((LANES, tb), F32),
                        pltpu.VMEM((tb, d), BF16),
                        pltpu.VMEM((tb, d), F32),
                        pltpu.SMEM((n_exp,), jnp.int32)],
        compiler_params=pltpu.CompilerParams(
            dimension_semantics=("parallel", "arbitrary", "arbitrary"),
            vmem_limit_bytes=MOE_VMEM_LIMIT_BYTES),
        name="moe_ffn",
    )(h, gain_pre.reshape(1, d), gain_post.reshape(1, d), w_r, w_gate, w_up, w_down)


def _qkv_kernel(h_ref, g_ref, w_ref, qg_ref, kg_ref, cs_ref, sn_ref, bd_ref,
                q_ref, k_ref, v_ref):
    xn = _rms(h_ref[...], g_ref[...]).astype(BF16)
    qkv = jnp.dot(xn, w_ref[...], preferred_element_type=F32)
    cs = cs_ref[...]
    sn = sn_ref[...]
    bd = bd_ref[...]
    lane = lax.broadcasted_iota(jnp.int32, cs.shape, 1)
    first_half = (lane % HEAD_DIM) < (HEAD_DIM // 2)
    scale = math.log2(math.e) / math.sqrt(HEAD_DIM)

    def norm_rope(x, gain):
        ms = jnp.dot(x * x, bd, precision=lax.Precision.HIGHEST, preferred_element_type=F32)
        y = x * lax.rsqrt(ms + NORM_EPS) * gain
        partner = jnp.where(first_half,
                            pltpu.roll(y, LANES - HEAD_DIM // 2, axis=1),
                            pltpu.roll(y, HEAD_DIM // 2, axis=1))
        return y * cs + partner * sn

    n_q_tiles = N_HEADS * HEAD_DIM // LANES
    for t in range(n_q_tiles):
        y = norm_rope(qkv[:, t * LANES:(t + 1) * LANES], qg_ref[...]) * scale
        q_ref[0, 2 * t] = y[:, 0:HEAD_DIM].astype(BF16)
        q_ref[0, 2 * t + 1] = y[:, HEAD_DIM:LANES].astype(BF16)
    k0 = N_HEADS * HEAD_DIM
    for t in range(N_KV_HEADS * HEAD_DIM // LANES):
        y = norm_rope(qkv[:, k0 + t * LANES:k0 + (t + 1) * LANES], kg_ref[...])
        k_ref[0, 2 * t] = y[:, 0:HEAD_DIM].astype(BF16)
        k_ref[0, 2 * t + 1] = y[:, HEAD_DIM:LANES].astype(BF16)
    v0 = (N_HEADS + N_KV_HEADS) * HEAD_DIM
    ones = jnp.ones((qkv.shape[0], LANES - HEAD_DIM), BF16)
    for j in range(N_KV_HEADS):
        vj = qkv[:, v0 + j * HEAD_DIM:v0 + (j + 1) * HEAD_DIM].astype(BF16)
        v_ref[0, j] = jnp.concatenate([vj, ones], axis=-1)


def _rope_tables(seq):
    axis_dim = HEAD_DIM // 2
    freqs = ROPE_THETA ** (-jnp.arange(0, axis_dim, 2, dtype=F32) / axis_dim)
    rows = seq // GRID_W
    row_ang = jnp.arange(rows, dtype=F32)[:, None] * freqs
    col_ang = jnp.arange(GRID_W, dtype=F32)[:, None] * freqs
    ang = jnp.concatenate([
        jnp.broadcast_to(row_ang[:, None, :], (rows, GRID_W, freqs.shape[0])),
        jnp.broadcast_to(col_ang[None, :, :], (rows, GRID_W, freqs.shape[0]))], axis=-1)
    ang = ang.reshape(seq, HEAD_DIM // 2)
    cos, sin = jnp.cos(ang), jnp.sin(ang)
    cs = jnp.tile(jnp.concatenate([cos, cos], axis=-1), (1, LANES // HEAD_DIM))
    sn = jnp.tile(jnp.concatenate([-sin, sin], axis=-1), (1, LANES // HEAD_DIM))
    return cs, sn


def _qkv(h, gain, w_qkv, q_gain, k_gain, bsz, seq, tm):
    n, d = h.shape
    width = w_qkv.shape[1]
    perm = jnp.concatenate([jnp.arange(0, HEAD_DIM, 2), jnp.arange(1, HEAD_DIM, 2)])
    n_rot = N_HEADS + N_KV_HEADS
    cols = (jnp.arange(n_rot)[:, None] * HEAD_DIM + perm[None, :]).reshape(-1)
    cols = jnp.concatenate([cols, jnp.arange(n_rot * HEAD_DIM, width)])
    w = w_qkv[:, cols].astype(BF16)
    qg = jnp.tile(q_gain[perm], LANES // HEAD_DIM).reshape(1, LANES)
    kg = jnp.tile(k_gain[perm], LANES // HEAD_DIM).reshape(1, LANES)
    cs, sn = _rope_tables(seq)
    blk = jnp.arange(LANES) // HEAD_DIM
    bd = (blk[:, None] == blk[None, :]).astype(F32) / HEAD_DIM
    per_seq = seq // tm
    return pl.pallas_call(
        _qkv_kernel,
        grid=(n // tm,),
        in_specs=[pl.BlockSpec((tm, d), lambda i: (i, 0)),
                  pl.BlockSpec((1, d), lambda i: (0, 0)),
                  pl.BlockSpec((d, width), lambda i: (0, 0)),
                  pl.BlockSpec((1, LANES), lambda i: (0, 0)),
                  pl.BlockSpec((1, LANES), lambda i: (0, 0)),
                  pl.BlockSpec((tm, LANES), lambda i: (i % per_seq, 0)),
                  pl.BlockSpec((tm, LANES), lambda i: (i % per_seq, 0)),
                  pl.BlockSpec((LANES, LANES), lambda i: (0, 0))],
        out_specs=[pl.BlockSpec((1, N_HEADS, tm, HEAD_DIM),
                                lambda i: (i // per_seq, 0, i % per_seq, 0)),
                   pl.BlockSpec((1, N_KV_HEADS, tm, HEAD_DIM),
                                lambda i: (i // per_seq, 0, i % per_seq, 0)),
                   pl.BlockSpec((1, N_KV_HEADS, tm, LANES),
                                lambda i: (i // per_seq, 0, i % per_seq, 0))],
        out_shape=[jax.ShapeDtypeStruct((bsz, N_HEADS, seq, HEAD_DIM), BF16),
                   jax.ShapeDtypeStruct((bsz, N_KV_HEADS, seq, HEAD_DIM), BF16),
                   jax.ShapeDtypeStruct((bsz, N_KV_HEADS, seq, LANES), BF16)],
        compiler_params=_cparams("parallel"),
        name="qkv_rope",
    )(h, gain.reshape(1, d), w, qg, kg, cs, sn, bd)


def _attn_kernel(q_ref, k_ref, v_ref, o_ref, m_s, acc_s, s_buf, p_buf, a_buf, *, tq, rc):
    kv = pl.program_id(3)
    tk = k_ref.shape[2]
    chunks_per_head = tq // rc
    n_chunks = Q_PER_KV * chunks_per_head

    @pl.when(kv == 0)
    def _():
        m_s[...] = jnp.full_like(m_s, -jnp.inf)
        acc_s[...] = jnp.zeros_like(acc_s)

    def where(c):
        return c // chunks_per_head, pl.multiple_of((c % chunks_per_head) * rc, rc)

    def scores(c):
        g, r0 = where(c)
        q = q_ref[0, g, pl.ds(r0, rc), :]
        s_buf[...] = lax.dot_general(q, k_ref[0, 0], _NT, preferred_element_type=F32)

    def softmax(c):
        g, r0 = where(c)
        s = s_buf[...]
        m_prev = m_s[g, pl.ds(r0, rc), :]
        m_new = jnp.maximum(m_prev, jnp.max(s, axis=-1, keepdims=True))
        a_buf[...] = jnp.exp2(m_prev - m_new)
        for t in range(tk // LANES):
            p_buf[:, t * LANES:(t + 1) * LANES] = jnp.exp2(
                s[:, t * LANES:(t + 1) * LANES] - m_new).astype(BF16)
        m_s[g, pl.ds(r0, rc), :] = m_new

    def values(c):
        g, r0 = where(c)
        pv = jnp.dot(p_buf[...], v_ref[0, 0], preferred_element_type=F32)
        acc_s[g, pl.ds(r0, rc), :] = a_buf[...] * acc_s[g, pl.ds(r0, rc), :] + pv

    scores(0)
    softmax(0)
    scores(1)

    def body(c, _):
        values(c)
        softmax(c + 1)
        scores(c + 2)
        return 0

    lax.fori_loop(0, n_chunks - 2, body, 0, unroll=5)
    values(n_chunks - 2)
    softmax(n_chunks - 1)
    values(n_chunks - 1)

    @pl.when(kv == pl.num_programs(3) - 1)
    def _():
        outs = []
        for g in range(Q_PER_KV):
            acc = acc_s[g]
            o = acc / pltpu.roll(acc, HEAD_DIM, axis=1)
            outs.append(o[:, 0:HEAD_DIM])
        o_ref[...] = jnp.concatenate(outs, axis=-1).astype(o_ref.dtype)


def _attention(q, k, v, tq, tk, rc):
    bsz, _, seq, _ = q.shape
    n_q = seq // tq
    return pl.pallas_call(
        functools.partial(_attn_kernel, tq=tq, rc=rc),
        grid=(bsz, N_KV_HEADS, n_q, seq // tk),
        in_specs=[pl.BlockSpec((1, Q_PER_KV, tq, HEAD_DIM), lambda b, j, i, t: (b, j, i, 0)),
                  pl.BlockSpec((1, 1, tk, HEAD_DIM), lambda b, j, i, t: (b, j, t, 0)),
                  pl.BlockSpec((1, 1, tk, LANES), lambda b, j, i, t: (b, j, t, 0))],
        out_specs=pl.BlockSpec((tq, Q_PER_KV * HEAD_DIM), lambda b, j, i, t: (b * n_q + i, j)),
        out_shape=jax.ShapeDtypeStruct((bsz * seq, N_HEADS * HEAD_DIM), BF16),
        scratch_shapes=[pltpu.VMEM((Q_PER_KV, tq, LANES), F32),
                        pltpu.VMEM((Q_PER_KV, tq, LANES), F32),
                        pltpu.VMEM((rc, tk), F32),
                        pltpu.VMEM((rc, tk), BF16),
                        pltpu.VMEM((rc, LANES), F32)],
        compiler_params=_cparams("parallel", "parallel", "parallel", "arbitrary"),
        name="flash_attn",
    )(q, k, v)


def _proj_res_kernel(x_ref, h_ref, w_ref, g_ref, o_ref):
    mix = jnp.dot(x_ref[...], w_ref[...], preferred_element_type=F32)
    o_ref[...] = h_ref[...] + _rms(mix, g_ref[...])


def _proj_res(x, h, w, gain, tm):
    n, d = h.shape
    k = x.shape[1]
    return pl.pallas_call(
        _proj_res_kernel,
        grid=(n // tm,),
        in_specs=[pl.BlockSpec((tm, k), lambda i: (i, 0)),
                  pl.BlockSpec((tm, d), lambda i: (i, 0)),
                  pl.BlockSpec((k, d), lambda i: (0, 0)),
                  pl.BlockSpec((1, d), lambda i: (0, 0))],
        out_specs=pl.BlockSpec((tm, d), lambda i: (i, 0)),
        out_shape=jax.ShapeDtypeStruct((n, d), F32),
        compiler_params=_cparams("parallel"),
        name="proj_res",
    )(x, h, w, gain.reshape(1, d))


def _s5_layer(h, bsz, seq, gains, w_in, lam_re, lam_im, log_step, b_re, b_im, c_re, c_im,
              d_skip, w_glu, w_out):
    n, d = h.shape
    seg_tokens = S5_CHUNK * S5_SEG_CHUNKS
    nseg = seq // seg_tokens
    nseq = bsz * nseg
    pairs = d // (2 * SSM_GROUP)
    u = _norm_matmul(h, gains[0], w_in.astype(BF16), tm=512)
    u_p = u.reshape(nseq, S5_SEG_CHUNKS, S5_CHUNK, pairs, 2, SSM_GROUP)
    u_p = jnp.transpose(u_p, (3, 1, 0, 4, 2, 5)).reshape(
        pairs, S5_SEG_CHUNKS * nseq, 2 * S5_CHUNK * SSM_GROUP)
    m, ws, wyt, pw, aa = _s5_prep(lam_re, lam_im, log_step, b_re, b_im, c_re, c_im, d_skip)
    w1, w2t, pw_p, aa_p = _s5_pair_layout(m, ws, wyt, pw, aa)
    y_p = _s5_core(u_p, w1, w2t, pw_p, aa_p, nseq, nseg)
    y = y_p.reshape(pairs, S5_SEG_CHUNKS, nseq, 2, S5_CHUNK, SSM_GROUP)
    y = jnp.transpose(y, (2, 1, 4, 0, 3, 5)).reshape(n, d)
    return _s5_out(y, h, w_glu.astype(BF16), w_out.astype(BF16), gains[1], tm=512)


def _attn_layer(h, bsz, seq, gains, w_qkv, q_gain, k_gain, w_out):
    q, k, v = _qkv(h, gains[0], w_qkv, q_gain, k_gain, bsz, seq, tm=512)
    o = _attention(q, k, v, tq=2048, tk=512, rc=256)
    return _proj_res(o, h, w_out.astype(BF16), gains[1], tm=512)


def kernel(x, norm_gains, ssm_w_in, ssm_lambda_re, ssm_lambda_im, ssm_log_step, ssm_b_re,
           ssm_b_im, ssm_c_re, ssm_c_im, ssm_d, ssm_w_glu, ssm_w_out, ffn_w_gate, ffn_w_up,
           ffn_w_down, attn_w_qkv, attn_q_gain, attn_k_gain, attn_w_out, moe_w_router,
           moe_w_gate, moe_w_up, moe_w_down):
    bsz, seq, d = x.shape
    depth = norm_gains.shape[0]
    h = x.reshape(bsz * seq, d)
    for i in range(depth):
        j = i // 2
        g = norm_gains[i]
        if i % 2 == 0:
            h = _s5_layer(h, bsz, seq, g, ssm_w_in[j], ssm_lambda_re[j], ssm_lambda_im[j],
                          ssm_log_step[j], ssm_b_re[j], ssm_b_im[j], ssm_c_re[j], ssm_c_im[j],
                          ssm_d[j], ssm_w_glu[j], ssm_w_out[j])
            h = _ffn(h, g[2], g[3], ffn_w_gate[j].astype(BF16), ffn_w_up[j].astype(BF16),
                     ffn_w_down[j].astype(BF16), tm=512, tf=1408)
        else:
            h = _attn_layer(h, bsz, seq, g, attn_w_qkv[j], attn_q_gain[j], attn_k_gain[j],
                            attn_w_out[j])
            h = _moe(h, g[2], g[3], moe_w_router[j], moe_w_gate[j].astype(BF16),
                     moe_w_up[j].astype(BF16), moe_w_down[j].astype(BF16),
                     tb=2048, tf=896, rt=256)
    return h.reshape(bsz, seq, d)
```

```python
import functools
import math

import jax
import jax.numpy as jnp
from jax import lax
from jax.experimental import pallas as pl
from jax.experimental.pallas import tpu as pltpu

F32 = jnp.float32
BF16 = jnp.bfloat16
NORM_EPS = 1e-6
ROPE_THETA = 10000.0
GRID_W = 64
N_HEADS = 16
N_KV_HEADS = 4
HEAD_DIM = 64
Q_PER_KV = N_HEADS // N_KV_HEADS
SSM_GROUP = 16
SSM_STATE = 64
S5_CHUNK = 16
S5_SEG_CHUNKS = 64
TOP_K = 2
LANES = 128
VMEM_LIMIT_BYTES = 56 * 1024 * 1024
MOE_VMEM_LIMIT_BYTES = 60 * 1024 * 1024

_NT = (((1,), (1,)), ((), ()))


def _cparams(*sem):
    return pltpu.CompilerParams(dimension_semantics=sem, vmem_limit_bytes=VMEM_LIMIT_BYTES)


def _rms(x, gain):
    return x * lax.rsqrt(jnp.mean(x * x, axis=-1, keepdims=True) + NORM_EPS) * gain


def _sigmoid(x):
    return 1.0 / (1.0 + jnp.exp(-x))


def _gelu_tanh(x):
    return x * (0.5 * (1.0 + jnp.tanh(math.sqrt(2.0 / math.pi) * (x + 0.044715 * (x * x * x)))))


def _norm_matmul_kernel(x_ref, g_ref, w_ref, o_ref):
    xn = _rms(x_ref[...], g_ref[...]).astype(BF16)
    o_ref[0] = jnp.dot(xn, w_ref[...], preferred_element_type=F32).astype(o_ref.dtype)


def _norm_matmul(x, gain, w, rc):
    n, d = x.shape
    m = w.shape[1]
    chunks = n // S5_CHUNK
    return pl.pallas_call(
        _norm_matmul_kernel,
        grid=(chunks // rc, S5_CHUNK),
        in_specs=[pl.BlockSpec((rc, d), lambda i, t: (i, t)),
                  pl.BlockSpec((1, d), lambda i, t: (0, 0)),
                  pl.BlockSpec((d, m), lambda i, t: (0, 0))],
        out_specs=pl.BlockSpec((1, rc, m), lambda i, t: (t, i, 0)),
        out_shape=jax.ShapeDtypeStruct((S5_CHUNK, chunks, m), BF16),
        compiler_params=_cparams("parallel", "parallel"),
        name="norm_matmul",
    )(x.reshape(chunks, S5_CHUNK * d), gain.reshape(1, d), w)


def _s5_prep_kernel(lr_ref, li_ref, ls_ref, bre_ref, bim_ref, cre_ref, cim_ref, dt_ref,
                    m_ref, ws_ref, wyt_ref, pw_ref, aa_ref):
    t_n, s_n, p_n = S5_CHUNK, SSM_GROUP, SSM_STATE
    rows = t_n * s_n
    hi = lax.Precision.HIGHEST

    r_io = lax.broadcasted_iota(jnp.int32, (rows, t_n), 0)
    c_io = lax.broadcasted_iota(jnp.int32, (rows, t_n), 1)
    rep_t = ((r_io // s_n) == c_io).astype(F32)
    rep_s = ((r_io % s_n) == c_io).astype(F32)
    rr = lax.broadcasted_iota(jnp.int32, (rows, rows), 0) // s_n
    cc = lax.broadcasted_iota(jnp.int32, (rows, rows), 1) // s_n
    tvec = lax.broadcasted_iota(jnp.int32, (t_n, 1), 0).astype(F32)
    jvec = lax.broadcasted_iota(jnp.int32, (S5_SEG_CHUNKS, 1), 0).astype(F32)
    row8 = lax.broadcasted_iota(jnp.int32, (8, p_n), 0)

    def expand(tab, rep):
        return jnp.dot(rep, tab, precision=hi, preferred_element_type=F32)

    m_total = None
    for d in range(2):
        lr = lr_ref[0, d:d + 1, :]
        li = li_ref[0, d:d + 1, :]
        step = jnp.exp(ls_ref[0, d:d + 1, :])
        lsr = lr * step
        lsi = li * step

        def cpow(k):
            mag = jnp.exp(lsr * k)
            ang = lsi * k
            return mag * jnp.cos(ang), mag * jnp.sin(ang)

        a_re, a_im = cpow(jnp.ones((1, 1), F32))
        nr, ni = a_re - 1.0, a_im
        den = lr * lr + li * li
        q_re = (nr * lr + ni * li) / den
        q_im = (ni * lr - nr * li) / den
        br = bre_ref[0, d]
        bi = bim_ref[0, d]
        bb_re = q_re * br - q_im * bi
        bb_im = q_re * bi + q_im * br
        cr = cre_ref[0, d]
        ci = cim_ref[0, d]

        def outer(x_re, x_im, k):
            p_re, p_im = cpow(k)
            pe_re, pe_im = expand(p_re, rep_t), expand(p_im, rep_t)
            xe_re, xe_im = expand(x_re, rep_s), expand(x_im, rep_s)
            return xe_re * pe_re - xe_im * pe_im, xe_re * pe_im + xe_im * pe_re

        if d == 0:
            l_re, l_im = outer(bb_re, bb_im, -tvec)
            rt_re, rt_im = outer(cr, ci, tvec)
            ws_re, ws_im = outer(bb_re, bb_im, (t_n - 1.0) - tvec)
            wy_re, wy_im = outer(cr, ci, tvec + 1.0)
            mask = rr <= cc
            pw_re, pw_im = cpow(t_n * jvec)
        else:
            l_re, l_im = outer(bb_re, bb_im, tvec)
            rt_re, rt_im = outer(cr, ci, -tvec)
            ws_re, ws_im = l_re, l_im
            wy_re, wy_im = outer(cr, ci, t_n - tvec)
            mask = rr >= cc
            pw_re, pw_im = cpow(t_n * ((S5_SEG_CHUNKS - 1.0) - jvec))
        kern = (lax.dot_general(l_re, rt_re, _NT, precision=hi, preferred_element_type=F32)
                - lax.dot_general(l_im, rt_im, _NT, precision=hi, preferred_element_type=F32))
        kern = jnp.where(mask, kern, 0.0)
        m_total = kern if m_total is None else m_total + kern

        ws_ref[0, 2 * d] = ws_re
        ws_ref[0, 2 * d + 1] = ws_im
        wyt_ref[0, 2 * d] = wy_re
        wyt_ref[0, 2 * d + 1] = -wy_im
        pw_ref[0, 2 * d] = pw_re
        pw_ref[0, 2 * d + 1] = pw_im
        c_re, c_im = cpow(jnp.full((1, 1), float(t_n), F32))
        s_re, s_im = cpow(jnp.full((1, 1), float(t_n * S5_SEG_CHUNKS), F32))
        aa_ref[0, 2 * d] = jnp.where(row8 == 0, c_re, jnp.where(row8 == 1, s_re, 0.0))
        aa_ref[0, 2 * d + 1] = jnp.where(row8 == 0, c_im, jnp.where(row8 == 1, s_im, 0.0))

    diag = (lax.broadcasted_iota(jnp.int32, (rows, rows), 0)
            == lax.broadcasted_iota(jnp.int32, (rows, rows), 1))
    m_ref[0] = m_total + jnp.where(diag, dt_ref[0], 0.0)


def _s5_prep(lam_re, lam_im, log_step, b_re, b_im, c_re, c_im, d_skip):
    g_n = lam_re.shape[1]
    rows = S5_CHUNK * SSM_GROUP
    lr = jnp.transpose(lam_re, (1, 0, 2))
    li = jnp.transpose(lam_im, (1, 0, 2))
    ls = jnp.transpose(log_step, (1, 0))[:, :, None]
    bre = jnp.transpose(b_re, (1, 0, 3, 2))
    bim = jnp.transpose(b_im, (1, 0, 3, 2))
    cre = jnp.transpose(c_re, (1, 0, 2, 3))
    cim = jnp.transpose(c_im, (1, 0, 2, 3))
    dt = jnp.tile(d_skip.reshape(g_n, 1, SSM_GROUP), (1, 1, S5_CHUNK))

    def spec3(a, b):
        return pl.BlockSpec((1, a, b), lambda g: (g, 0, 0))

    def spec4(a, b, c):
        return pl.BlockSpec((1, a, b, c), lambda g: (g, 0, 0, 0))

    return pl.pallas_call(
        _s5_prep_kernel,
        grid=(g_n,),
        in_specs=[spec3(2, SSM_STATE), spec3(2, SSM_STATE), spec3(2, 1),
                  spec4(2, SSM_GROUP, SSM_STATE), spec4(2, SSM_GROUP, SSM_STATE),
                  spec4(2, SSM_GROUP, SSM_STATE), spec4(2, SSM_GROUP, SSM_STATE),
                  spec3(1, rows)],
        out_specs=[spec3(rows, rows), spec4(4, rows, SSM_STATE), spec4(4, rows, SSM_STATE),
                   spec4(4, S5_SEG_CHUNKS, SSM_STATE), spec4(4, 8, SSM_STATE)],
        out_shape=[jax.ShapeDtypeStruct((g_n, rows, rows), F32),
                   jax.ShapeDtypeStruct((g_n, 4, rows, SSM_STATE), F32),
                   jax.ShapeDtypeStruct((g_n, 4, rows, SSM_STATE), F32),
                   jax.ShapeDtypeStruct((g_n, 4, S5_SEG_CHUNKS, SSM_STATE), F32),
                   jax.ShapeDtypeStruct((g_n, 4, 8, SSM_STATE), F32)],
        compiler_params=_cparams("parallel"),
        name="s5_prep",
    )(lr, li, ls, bre, bim, cre, cim, dt)


def _s5_pair_layout(m, ws, wyt, pw, aa):
    g_n, rows, _ = m.shape
    pairs = g_n // 2
    t_n, s_n = S5_CHUNK, SSM_GROUP
    eye = jnp.eye(2, dtype=F32)
    m_p = jnp.einsum('agtsuv,gh->atgsuhv', m.reshape(pairs, 2, t_n, s_n, t_n, s_n), eye)
    m_p = m_p.reshape(pairs, 2 * rows, 2 * rows)

    def spread(x):
        x = x.reshape(pairs, 2, 4, t_n, s_n, SSM_STATE)
        x = jnp.einsum('agktsp,gh->atgskhp', x, eye)
        return x.reshape(pairs, 2 * rows, 8 * SSM_STATE)

    w1 = jnp.concatenate([m_p, spread(ws)], axis=-1).astype(BF16)
    w2t = spread(wyt).astype(BF16)

    def lanes(x):
        r = x.shape[2]
        x = x.reshape(pairs, 2, 4, r, SSM_STATE)
        return jnp.transpose(x, (0, 3, 2, 1, 4)).reshape(pairs, r, 8 * SSM_STATE)

    return w1, w2t, lanes(pw), lanes(aa)


S5_PAIR_LANES = 2 * SSM_GROUP
S5_PAIRS_PER_SLAB = LANES // S5_PAIR_LANES
S5_SEG_PITCH = S5_SEG_CHUNKS + 8
S5_SEG_BASE = 8


def _s5_core_kernel(x_ref, w1_ref, w2t_ref, pw_ref, aa_ref, y_ref, z_s, zs_s, x_s, u_s, xb_s,
                    *, nseq, nseg, strip):
    n_chunks = S5_SEG_CHUNKS
    rows = n_chunks * nseq
    ycols = u_s.shape[1]
    per_tile = LANES // S5_PAIR_LANES
    lane_grp = lax.broadcasted_iota(jnp.int32, (strip, LANES), 1) // S5_PAIR_LANES
    zero = jnp.zeros((nseq, LANES), F32)

    def seg_rows(j):
        return pl.ds(S5_SEG_BASE + j, nseq, stride=S5_SEG_PITCH)

    for q in range(S5_PAIRS_PER_SLAB):
        def gather(r, _, q=q):
            rws = pl.ds(pl.multiple_of(r * strip, strip), strip)
            for j in range(ycols // LANES):
                acc = None
                for i in range(per_tile):
                    xt = x_ref[per_tile * j + i, rws, :].astype(F32)
                    shift = (S5_PAIR_LANES * (i - q)) % LANES
                    if shift:
                        xt = pltpu.roll(xt, shift, axis=1)
                    acc = xt if acc is None else jnp.where(lane_grp == i, xt, acc)
                u_s[rws, j * LANES:(j + 1) * LANES] = acc.astype(BF16)
            return 0

        lax.fori_loop(0, rows // strip, gather, 0)
        z_s[...] = jnp.dot(u_s[...], w1_ref[q], preferred_element_type=F32)

        for b in range(nseq):
            dst = slice(S5_SEG_BASE + b * S5_SEG_PITCH, S5_SEG_BASE + b * S5_SEG_PITCH + n_chunks)
            for k in range(4):
                zs_s[k, dst, :] = z_s[b * n_chunks:(b + 1) * n_chunks,
                                      ycols + k * LANES:ycols + (k + 1) * LANES]
        x_s[0, seg_rows(0), :] = zero
        x_s[1, seg_rows(0), :] = zero
        x_s[2, seg_rows(n_chunks - 1), :] = zero
        x_s[3, seg_rows(n_chunks - 1), :] = zero

        aa = aa_ref[q]
        a_fr, a_fi = aa[0:1, 0:LANES], aa[0:1, LANES:2 * LANES]
        a_br, a_bi = aa[0:1, 2 * LANES:3 * LANES], aa[0:1, 3 * LANES:4 * LANES]
        g_fr, g_fi = aa[1:2, 0:LANES], aa[1:2, LANES:2 * LANES]
        g_br, g_bi = aa[1:2, 2 * LANES:3 * LANES], aa[1:2, 3 * LANES:4 * LANES]

        def fwd(j, carry):
            xr, xi = carry
            nr = a_fr * xr - a_fi * xi + zs_s[0, seg_rows(j), :]
            ni = a_fr * xi + a_fi * xr + zs_s[1, seg_rows(j), :]
            x_s[0, seg_rows(j + 1), :] = nr
            x_s[1, seg_rows(j + 1), :] = ni
            return nr, ni

        def bwd(i, carry):
            xr, xi = carry
            j = n_chunks - 1 - i
            nr = a_br * xr - a_bi * xi + zs_s[2, seg_rows(j), :]
            ni = a_br * xi + a_bi * xr + zs_s[3, seg_rows(j), :]
            x_s[2, seg_rows(j - 1), :] = nr
            x_s[3, seg_rows(j - 1), :] = ni
            return nr, ni

        ef_r, ef_i = lax.fori_loop(0, n_chunks, fwd, (zero, zero))
        eb_r, eb_i = lax.fori_loop(0, n_chunks, bwd, (zero, zero))

        def seg_carries(e_r, e_i, g_r, g_i, reverse):
            out_r = [None] * nseq
            out_i = [None] * nseq
            for b in range(nseq // nseg):
                c_r = jnp.zeros((1, LANES), F32)
                c_i = jnp.zeros((1, LANES), F32)
                order = range(nseg - 1, -1, -1) if reverse else range(nseg)
                for s in order:
                    r = b * nseg + s
                    out_r[r], out_i[r] = c_r, c_i
                    n_r = g_r * c_r - g_i * c_i + e_r[r:r + 1, :]
                    n_i = g_r * c_i + g_i * c_r + e_i[r:r + 1, :]
                    c_r, c_i = n_r, n_i
            return out_r, out_i

        cf_r, cf_i = seg_carries(ef_r, ef_i, g_fr, g_fi, False)
        cb_r, cb_i = seg_carries(eb_r, eb_i, g_br, g_bi, True)

        p = pw_ref[q]
        p_fr, p_fi = p[:, 0:LANES], p[:, LANES:2 * LANES]
        p_br, p_bi = p[:, 2 * LANES:3 * LANES], p[:, 3 * LANES:4 * LANES]
        for b in range(nseq):
            src = slice(S5_SEG_BASE + b * S5_SEG_PITCH, S5_SEG_BASE + b * S5_SEG_PITCH + n_chunks)
            dst = slice(b * n_chunks, (b + 1) * n_chunks)
            xb_s[dst, 0:LANES] = (x_s[0, src, :] + (p_fr * cf_r[b] - p_fi * cf_i[b])).astype(BF16)
            xb_s[dst, LANES:2 * LANES] = (
                x_s[1, src, :] + (p_fr * cf_i[b] + p_fi * cf_r[b])).astype(BF16)
            xb_s[dst, 2 * LANES:3 * LANES] = (
                x_s[2, src, :] + (p_br * cb_r[b] - p_bi * cb_i[b])).astype(BF16)
            xb_s[dst, 3 * LANES:4 * LANES] = (
                x_s[3, src, :] + (p_br * cb_i[b] + p_bi * cb_r[b])).astype(BF16)

        z_s[:, 0:ycols] += lax.dot_general(xb_s[...], w2t_ref[q], _NT,
                                           preferred_element_type=F32)

        def scatter(r, _, q=q):
            rws = pl.ds(pl.multiple_of(r * strip, strip), strip)
            for j in range(ycols // LANES):
                yq = z_s[rws, j * LANES:(j + 1) * LANES]
                for i in range(per_tile):
                    shift = (S5_PAIR_LANES * (q - i)) % LANES
                    yt = pltpu.roll(yq, shift, axis=1) if shift else yq
                    lanes_q = slice(q * S5_PAIR_LANES, (q + 1) * S5_PAIR_LANES)
                    y_ref[per_tile * j + i, rws, lanes_q] = yt[:, lanes_q].astype(y_ref.dtype)
            return 0

        lax.fori_loop(0, rows // strip, scatter, 0)


def _s5_core(u_t, w1, w2t, pw, aa, nseq, nseg):
    t_n, rows, d = u_t.shape
    width = w1.shape[1]
    pps = S5_PAIRS_PER_SLAB
    slab_rows = S5_SEG_BASE + nseq * S5_SEG_PITCH
    return pl.pallas_call(
        functools.partial(_s5_core_kernel, nseq=nseq, nseg=nseg, strip=128),
        grid=(d // LANES,),
        in_specs=[pl.BlockSpec((t_n, rows, LANES), lambda o: (0, 0, o)),
                  pl.BlockSpec((pps, width, w1.shape[2]), lambda o: (o, 0, 0)),
                  pl.BlockSpec((pps, width, w2t.shape[2]), lambda o: (o, 0, 0)),
                  pl.BlockSpec((pps, S5_SEG_CHUNKS, pw.shape[2]), lambda o: (o, 0, 0)),
                  pl.BlockSpec((pps, 8, aa.shape[2]), lambda o: (o, 0, 0))],
        out_specs=pl.BlockSpec((t_n, rows, LANES), lambda o: (0, 0, o)),
        out_shape=jax.ShapeDtypeStruct((t_n, rows, d), BF16),
        scratch_shapes=[pltpu.VMEM((rows, w1.shape[2]), F32),
                        pltpu.VMEM((4, slab_rows, LANES), F32),
                        pltpu.VMEM((4, slab_rows, LANES), F32),
                        pltpu.VMEM((rows, width), BF16),
                        pltpu.VMEM((rows, 4 * LANES), BF16)],
        compiler_params=_cparams("parallel"),
        name="s5_core",
    )(u_t, w1, w2t, pw, aa)


def _s5_out_kernel(y_ref, h_ref, wglu_ref, wout_ref, g_ref, o_ref):
    g = _gelu_tanh(y_ref[0].astype(F32))
    z = jnp.dot(g.astype(BF16), wglu_ref[...], preferred_element_type=F32)
    g2 = g * _sigmoid(z)
    mix = jnp.dot(g2.astype(BF16), wout_ref[...], preferred_element_type=F32)
    o_ref[...] = h_ref[...] + _rms(mix, g_ref[...])


def _s5_out(y_t, h, w_glu, w_out, gain, rc):
    n, d = h.shape
    chunks = n // S5_CHUNK
    out = pl.pallas_call(
        _s5_out_kernel,
        grid=(chunks // rc, S5_CHUNK),
        in_specs=[pl.BlockSpec((1, rc, d), lambda i, t: (t, i, 0)),
                  pl.BlockSpec((rc, d), lambda i, t: (i, t)),
                  pl.BlockSpec((d, d), lambda i, t: (0, 0)),
                  pl.BlockSpec((d, d), lambda i, t: (0, 0)),
                  pl.BlockSpec((1, d), lambda i, t: (0, 0))],
        out_specs=pl.BlockSpec((rc, d), lambda i, t: (i, t)),
        out_shape=jax.ShapeDtypeStruct((chunks, S5_CHUNK * d), F32),
        compiler_params=_cparams("parallel", "parallel"),
        name="s5_out",
    )(y_t, h.reshape(chunks, S5_CHUNK * d), w_glu, w_out, gain.reshape(1, d))
    return out.reshape(n, d)


def _ffn_kernel(h_ref, gpre_ref, gpost_ref, wg_ref, wu_ref, wd_ref, o_ref, hn_s, acc_s):
    f = pl.program_id(1)

    @pl.when(f == 0)
    def _():
        hn_s[...] = _rms(h_ref[...], gpre_ref[...]).astype(BF16)
        acc_s[...] = jnp.zeros_like(acc_s)

    hn = hn_s[...]
    a = jnp.dot(hn, wg_ref[...], preferred_element_type=F32)
    u = jnp.dot(hn, wu_ref[...], preferred_element_type=F32)
    act = a * _sigmoid(a) * u
    acc_s[...] += jnp.dot(act.astype(BF16), wd_ref[...], preferred_element_type=F32)

    @pl.when(f == pl.num_programs(1) - 1)
    def _():
        o_ref[...] = h_ref[...] + _rms(acc_s[...], gpost_ref[...])


def _ffn(h, gain_pre, gain_post, w_gate, w_up, w_down, tm, tf):
    n, d = h.shape
    d_ff = w_gate.shape[1]
    return pl.pallas_call(
        _ffn_kernel,
        grid=(n // tm, d_ff // tf),
        in_specs=[pl.BlockSpec((tm, d), lambda i, f: (i, 0)),
                  pl.BlockSpec((1, d), lambda i, f: (0, 0)),
                  pl.BlockSpec((1, d), lambda i, f: (0, 0)),
                  pl.BlockSpec((d, tf), lambda i, f: (0, f)),
                  pl.BlockSpec((d, tf), lambda i, f: (0, f)),
                  pl.BlockSpec((tf, d), lambda i, f: (f, 0))],
        out_specs=pl.BlockSpec((tm, d), lambda i, f: (i, 0)),
        out_shape=jax.ShapeDtypeStruct((n, d), F32),
        scratch_shapes=[pltpu.VMEM((tm, d), BF16), pltpu.VMEM((tm, d), F32)],
        compiler_params=_cparams("parallel", "arbitrary"),
        name="ffn",
    )(h, gain_pre.reshape(1, d), gain_post.reshape(1, d), w_gate, w_up, w_down)


def _moe_kernel(h_ref, gpre_ref, gpost_ref, wr_ref, wg_ref, wu_ref, wd_ref, o_ref,
                hn_s, gate_s, sel_s, pos_s, post_s, xc_s, yc_s, cnt_s, *, n_exp, rt, strip):
    e = pl.program_id(1)
    f = pl.program_id(2)
    tb, d = hn_s.shape
    n_f = pl.num_programs(2)

    @pl.when((e == 0) & (f == 0))
    def _():
        def route(r, total):
            rows = pl.ds(pl.multiple_of(r * strip, strip), strip)
            xn = _rms(h_ref[rows, :], gpre_ref[...])
            hn_s[rows, :] = xn.astype(BF16)
            o_ref[rows, :] = jnp.zeros((strip, d), F32)
            logits = jnp.dot(xn, wr_ref[...], precision=lax.Precision.HIGHEST,
                             preferred_element_type=F32)
            lane = lax.broadcasted_iota(jnp.int32, logits.shape, 1)
            neg = jnp.float32(-jnp.inf)
            logits = jnp.where(lane < n_exp, logits, neg)
            m1 = jnp.max(logits, axis=-1, keepdims=True)
            i1 = jnp.min(jnp.where(logits == m1, lane, LANES), axis=-1, keepdims=True)
            rest = jnp.where(lane == i1, neg, logits)
            m2 = jnp.max(rest, axis=-1, keepdims=True)
            i2 = jnp.min(jnp.where(rest == m2, lane, LANES), axis=-1, keepdims=True)
            e2 = jnp.exp(m2 - m1)
            gate_s[rows, :] = (jnp.where(lane == i1, 1.0 / (1.0 + e2), 0.0)
                               + jnp.where(lane == i2, e2 / (1.0 + e2), 0.0))
            sel = ((lane == i1) | (lane == i2)).astype(F32)
            sel_s[rows, :] = sel.astype(BF16)
            return total + jnp.sum(sel, axis=0, keepdims=True)

        total = lax.fori_loop(0, tb // strip, route, jnp.zeros((1, LANES), F32))
        lane1 = lax.broadcasted_iota(jnp.int32, total.shape, 1)
        for x in range(n_exp):
            cnt_s[x] = jnp.sum(jnp.where(lane1 == x, total, 0.0)).astype(jnp.int32)

        def rank(r, _):
            r0 = pl.multiple_of(r * strip, strip)
            rows = lax.broadcasted_iota(jnp.int32, (strip, tb), 0) + r0
            cols = lax.broadcasted_iota(jnp.int32, (strip, tb), 1)
            before = (cols < rows).astype(BF16)
            cnt = jnp.dot(before, sel_s[...], preferred_element_type=F32)
            chosen = sel_s[pl.ds(r0, strip), :] > 0
            pos_s[pl.ds(r0, strip), :] = jnp.where(chosen, cnt, -1.0)
            return 0

        lax.fori_loop(0, tb // strip, rank, 0)
        for r in range(tb // strip):
            post_s[:, r * strip:(r + 1) * strip] = pos_s[r * strip:(r + 1) * strip, :].T

    n_tiles = (cnt_s[e] + rt - 1) // rt
    lane = lax.broadcasted_iota(jnp.int32, (tb, LANES), 1)
    pos_col = jnp.sum(jnp.where(lane == e, pos_s[...], 0.0), axis=-1, keepdims=True)
    gate_col = jnp.sum(jnp.where(lane == e, gate_s[...], 0.0), axis=-1, keepdims=True)

    def tile(i, _):
        r0 = pl.multiple_of(i * rt, rt)

        @pl.when(f == 0)
        def _():
            slot = (lax.broadcasted_iota(jnp.int32, (rt, tb), 0) + r0).astype(F32)
            pick = (post_s[pl.ds(e, 1), :] == slot).astype(BF16)
            xc_s[pl.ds(r0, rt), :] = jnp.dot(pick, hn_s[...],
                                             preferred_element_type=F32).astype(BF16)

        xc = xc_s[pl.ds(r0, rt), :]
        a = jnp.dot(xc, wg_ref[0], preferred_element_type=F32)
        u = jnp.dot(xc, wu_ref[0], preferred_element_type=F32)
        act = (a * _sigmoid(a) * u).astype(BF16)
        y = jnp.dot(act, wd_ref[0], preferred_element_type=F32)
        prev = jnp.where(f == 0, 0.0, yc_s[pl.ds(r0, rt), :])
        yc_s[pl.ds(r0, rt), :] = prev + y

        @pl.when(f == n_f - 1)
        def _():
            yc = yc_s[pl.ds(r0, rt), :].astype(BF16)
            slot = (lax.broadcasted_iota(jnp.int32, (strip, rt), 1) + r0).astype(F32)
            for r in range(tb // strip):
                rows = slice(r * strip, (r + 1) * strip)
                put = (pos_col[rows] == slot).astype(BF16)
                back = jnp.dot(put, yc, preferred_element_type=F32)
                o_ref[rows, :] += gate_col[rows] * back
        return 0

    lax.fori_loop(0, n_tiles, tile, 0)

    @pl.when((e == n_exp - 1) & (f == n_f - 1))
    def _():
        def finish(r, _):
            rows = pl.ds(pl.multiple_of(r * strip, strip), strip)
            o_ref[rows, :] = h_ref[rows, :] + _rms(o_ref[rows, :], gpost_ref[...])
            return 0

        lax.fori_loop(0, tb // strip, finish, 0)


def _moe(h, gain_pre, gain_post, w_router, w_gate, w_up, w_down, tb, tf, rt):
    n, d = h.shape
    n_exp, _, d_ff = w_gate.shape
    w_r = jnp.pad(w_router, ((0, 0), (0, LANES - n_exp)))
    once = pl.Buffered(1)
    return pl.pallas_call(
        functools.partial(_moe_kernel, n_exp=n_exp, rt=rt, strip=256),
        grid=(n // tb, n_exp, d_ff // tf),
        in_specs=[pl.BlockSpec((tb, d), lambda i, e, f: (i, 0), pipeline_mode=once),
                  pl.BlockSpec((1, d), lambda i, e, f: (0, 0)),
                  pl.BlockSpec((1, d), lambda i, e, f: (0, 0)),
                  pl.BlockSpec((d, LANES), lambda i, e, f: (0, 0)),
                  pl.BlockSpec((1, d, tf), lambda i, e, f: (e, 0, f)),
                  pl.BlockSpec((1, d, tf), lambda i, e, f: (e, 0, f)),
                  pl.BlockSpec((1, tf, d), lambda i, e, f: (e, f, 0))],
        out_specs=pl.BlockSpec((tb, d), lambda i, e, f: (i, 0), pipeline_mode=once),
        out_shape=jax.ShapeDtypeStruct((n, d), F32),
        scratch_shapes=[pltpu.VMEM((tb, d), BF16),
                        pltpu.VMEM((tb, LANES), F32),
                        pltpu.VMEM((tb, LANES), BF16),
                        pltpu.VMEM((tb, LANES), F32),
                        pltpu.VMEM((LANES, tb), F32),
                        pltpu.VMEM((tb, d), BF16),
                        pltpu.VMEM((tb, d), F32),
                        pltpu.SMEM((n_exp,), jnp.int32)],
        compiler_params=pltpu.CompilerParams(
            dimension_semantics=("parallel", "arbitrary", "arbitrary"),
            vmem_limit_bytes=MOE_VMEM_LIMIT_BYTES),
        name="moe_ffn",
    )(h, gain_pre.reshape(1, d), gain_post.reshape(1, d), w_r, w_gate, w_up, w_down)


def _qkv_kernel(h_ref, g_ref, w_ref, qg_ref, kg_ref, cs_ref, sn_ref, bd_ref,
                q_ref, k_ref, v_ref):
    xn = _rms(h_ref[...], g_ref[...]).astype(BF16)
    qkv = jnp.dot(xn, w_ref[...], preferred_element_type=F32)
    cs = cs_ref[...]
    sn = sn_ref[...]
    bd = bd_ref[...]
    lane = lax.broadcasted_iota(jnp.int32, cs.shape, 1)
    first_half = (lane % HEAD_DIM) < (HEAD_DIM // 2)
    scale = math.log2(math.e) / math.sqrt(HEAD_DIM)

    def norm_rope(x, gain):
        ms = jnp.dot(x * x, bd, precision=lax.Precision.HIGHEST, preferred_element_type=F32)
        y = x * lax.rsqrt(ms + NORM_EPS) * gain
        partner = jnp.where(first_half,
                            pltpu.roll(y, LANES - HEAD_DIM // 2, axis=1),
                            pltpu.roll(y, HEAD_DIM // 2, axis=1))
        return y * cs + partner * sn

    n_q_tiles = N_HEADS * HEAD_DIM // LANES
    for t in range(n_q_tiles):
        y = norm_rope(qkv[:, t * LANES:(t + 1) * LANES], qg_ref[...]) * scale
        q_ref[0, 2 * t] = y[:, 0:HEAD_DIM].astype(BF16)
        q_ref[0, 2 * t + 1] = y[:, HEAD_DIM:LANES].astype(BF16)
    k0 = N_HEADS * HEAD_DIM
    for t in range(N_KV_HEADS * HEAD_DIM // LANES):
        y = norm_rope(qkv[:, k0 + t * LANES:k0 + (t + 1) * LANES], kg_ref[...])
        k_ref[0, 2 * t] = y[:, 0:HEAD_DIM].astype(BF16)
        k_ref[0, 2 * t + 1] = y[:, HEAD_DIM:LANES].astype(BF16)
    v0 = (N_HEADS + N_KV_HEADS) * HEAD_DIM
    ones = jnp.ones((qkv.shape[0], LANES - HEAD_DIM), BF16)
    for j in range(N_KV_HEADS):
        vj = qkv[:, v0 + j * HEAD_DIM:v0 + (j + 1) * HEAD_DIM].astype(BF16)
        v_ref[0, j] = jnp.concatenate([vj, ones], axis=-1)


def _rope_tables(seq):
    axis_dim = HEAD_DIM // 2
    freqs = ROPE_THETA ** (-jnp.arange(0, axis_dim, 2, dtype=F32) / axis_dim)
    rows = seq // GRID_W
    row_ang = jnp.arange(rows, dtype=F32)[:, None] * freqs
    col_ang = jnp.arange(GRID_W, dtype=F32)[:, None] * freqs
    ang = jnp.concatenate([
        jnp.broadcast_to(row_ang[:, None, :], (rows, GRID_W, freqs.shape[0])),
        jnp.broadcast_to(col_ang[None, :, :], (rows, GRID_W, freqs.shape[0]))], axis=-1)
    ang = ang.reshape(seq, HEAD_DIM // 2)
    cos, sin = jnp.cos(ang), jnp.sin(ang)
    cs = jnp.tile(jnp.concatenate([cos, cos], axis=-1), (1, LANES // HEAD_DIM))
    sn = jnp.tile(jnp.concatenate([-sin, sin], axis=-1), (1, LANES // HEAD_DIM))
    return cs, sn


def _qkv(h, gain, w_qkv, q_gain, k_gain, bsz, seq, tm):
    n, d = h.shape
    width = w_qkv.shape[1]
    perm = jnp.concatenate([jnp.arange(0, HEAD_DIM, 2), jnp.arange(1, HEAD_DIM, 2)])
    n_rot = N_HEADS + N_KV_HEADS
    cols = (jnp.arange(n_rot)[:, None] * HEAD_DIM + perm[None, :]).reshape(-1)
    cols = jnp.concatenate([cols, jnp.arange(n_rot * HEAD_DIM, width)])
    w = w_qkv[:, cols].astype(BF16)
    qg = jnp.tile(q_gain[perm], LANES // HEAD_DIM).reshape(1, LANES)
    kg = jnp.tile(k_gain[perm], LANES // HEAD_DIM).reshape(1, LANES)
    cs, sn = _rope_tables(seq)
    blk = jnp.arange(LANES) // HEAD_DIM
    bd = (blk[:, None] == blk[None, :]).astype(F32) / HEAD_DIM
    per_seq = seq // tm
    return pl.pallas_call(
        _qkv_kernel,
        grid=(n // tm,),
        in_specs=[pl.BlockSpec((tm, d), lambda i: (i, 0)),
                  pl.BlockSpec((1, d), lambda i: (0, 0)),
                  pl.BlockSpec((d, width), lambda i: (0, 0)),
                  pl.BlockSpec((1, LANES), lambda i: (0, 0)),
                  pl.BlockSpec((1, LANES), lambda i: (0, 0)),
                  pl.BlockSpec((tm, LANES), lambda i: (i % per_seq, 0)),
                  pl.BlockSpec((tm, LANES), lambda i: (i % per_seq, 0)),
                  pl.BlockSpec((LANES, LANES), lambda i: (0, 0))],
        out_specs=[pl.BlockSpec((1, N_HEADS, tm, HEAD_DIM),
                                lambda i: (i // per_seq, 0, i % per_seq, 0)),
                   pl.BlockSpec((1, N_KV_HEADS, tm, HEAD_DIM),
                                lambda i: (i // per_seq, 0, i % per_seq, 0)),
                   pl.BlockSpec((1, N_KV_HEADS, tm, LANES),
                                lambda i: (i // per_seq, 0, i % per_seq, 0))],
        out_shape=[jax.ShapeDtypeStruct((bsz, N_HEADS, seq, HEAD_DIM), BF16),
                   jax.ShapeDtypeStruct((bsz, N_KV_HEADS, seq, HEAD_DIM), BF16),
                   jax.ShapeDtypeStruct((bsz, N_KV_HEADS, seq, LANES), BF16)],
        compiler_params=_cparams("parallel"),
        name="qkv_rope",
    )(h, gain.reshape(1, d), w, qg, kg, cs, sn, bd)


def _attn_kernel(q_ref, k_ref, v_ref, o_ref, m_s, acc_s, s_buf, p_buf, a_buf, *, tq, rc):
    kv = pl.program_id(3)
    tk = k_ref.shape[2]
    chunks_per_head = tq // rc
    n_chunks = Q_PER_KV * chunks_per_head

    @pl.when(kv == 0)
    def _():
        m_s[...] = jnp.full_like(m_s, -jnp.inf)
        acc_s[...] = jnp.zeros_like(acc_s)

    def where(c):
        return c // chunks_per_head, pl.multiple_of((c % chunks_per_head) * rc, rc)

    def scores(c):
        g, r0 = where(c)
        q = q_ref[0, g, pl.ds(r0, rc), :]
        s_buf[...] = lax.dot_general(q, k_ref[0, 0], _NT, preferred_element_type=F32)

    def softmax(c):
        g, r0 = where(c)
        s = s_buf[...]
        m_prev = m_s[g, pl.ds(r0, rc), :]
        m_new = jnp.maximum(m_prev, jnp.max(s, axis=-1, keepdims=True))
        a_buf[...] = jnp.exp2(m_prev - m_new)
        for t in range(tk // LANES):
            p_buf[:, t * LANES:(t + 1) * LANES] = jnp.exp2(
                s[:, t * LANES:(t + 1) * LANES] - m_new).astype(BF16)
        m_s[g, pl.ds(r0, rc), :] = m_new

    def values(c):
        g, r0 = where(c)
        pv = jnp.dot(p_buf[...], v_ref[0, 0], preferred_element_type=F32)
        acc_s[g, pl.ds(r0, rc), :] = a_buf[...] * acc_s[g, pl.ds(r0, rc), :] + pv

    scores(0)
    softmax(0)
    scores(1)

    def body(c, _):
        values(c)
        softmax(c + 1)
        scores(c + 2)
        return 0

    lax.fori_loop(0, n_chunks - 2, body, 0, unroll=5)
    values(n_chunks - 2)
    softmax(n_chunks - 1)
    values(n_chunks - 1)

    @pl.when(kv == pl.num_programs(3) - 1)
    def _():
        outs = []
        for g in range(Q_PER_KV):
            acc = acc_s[g]
            o = acc / pltpu.roll(acc, HEAD_DIM, axis=1)
            outs.append(o[:, 0:HEAD_DIM])
        o_ref[...] = jnp.concatenate(outs, axis=-1).astype(o_ref.dtype)


def _attention(q, k, v, tq, tk, rc):
    bsz, _, seq, _ = q.shape
    n_q = seq // tq
    return pl.pallas_call(
        functools.partial(_attn_kernel, tq=tq, rc=rc),
        grid=(bsz, N_KV_HEADS, n_q, seq // tk),
        in_specs=[pl.BlockSpec((1, Q_PER_KV, tq, HEAD_DIM), lambda b, j, i, t: (b, j, i, 0)),
                  pl.BlockSpec((1, 1, tk, HEAD_DIM), lambda b, j, i, t: (b, j, t, 0)),
                  pl.BlockSpec((1, 1, tk, LANES), lambda b, j, i, t: (b, j, t, 0))],
        out_specs=pl.BlockSpec((tq, Q_PER_KV * HEAD_DIM), lambda b, j, i, t: (b * n_q + i, j)),
        out_shape=jax.ShapeDtypeStruct((bsz * seq, N_HEADS * HEAD_DIM), BF16),
        scratch_shapes=[pltpu.VMEM((Q_PER_KV, tq, LANES), F32),
                        pltpu.VMEM((Q_PER_KV, tq, LANES), F32),
                        pltpu.VMEM((rc, tk), F32),
                        pltpu.VMEM((rc, tk), BF16),
                        pltpu.VMEM((rc, LANES), F32)],
        compiler_params=_cparams("parallel", "parallel", "parallel", "arbitrary"),
        name="flash_attn",
    )(q, k, v)


def _proj_res_kernel(x_ref, h_ref, w_ref, g_ref, o_ref):
    mix = jnp.dot(x_ref[...], w_ref[...], preferred_element_type=F32)
    o_ref[...] = h_ref[...] + _rms(mix, g_ref[...])


def _proj_res(x, h, w, gain, tm):
    n, d = h.shape
    k = x.shape[1]
    return pl.pallas_call(
        _proj_res_kernel,
        grid=(n // tm,),
        in_specs=[pl.BlockSpec((tm, k), lambda i: (i, 0)),
                  pl.BlockSpec((tm, d), lambda i: (i, 0)),
                  pl.BlockSpec((k, d), lambda i: (0, 0)),
                  pl.BlockSpec((1, d), lambda i: (0, 0))],
        out_specs=pl.BlockSpec((tm, d), lambda i: (i, 0)),
        out_shape=jax.ShapeDtypeStruct((n, d), F32),
        compiler_params=_cparams("parallel"),
        name="proj_res",
    )(x, h, w, gain.reshape(1, d))


def _s5_layer(h, bsz, seq, gains, w_in, lam_re, lam_im, log_step, b_re, b_im, c_re, c_im,
              d_skip, w_glu, w_out):
    seg_tokens = S5_CHUNK * S5_SEG_CHUNKS
    nseg = seq // seg_tokens
    nseq = bsz * nseg
    rc = min(512, h.shape[0] // S5_CHUNK)
    u_t = _norm_matmul(h, gains[0], w_in.astype(BF16), rc=rc)
    m, ws, wyt, pw, aa = _s5_prep(lam_re, lam_im, log_step, b_re, b_im, c_re, c_im, d_skip)
    w1, w2t, pw_p, aa_p = _s5_pair_layout(m, ws, wyt, pw, aa)
    y_t = _s5_core(u_t, w1, w2t, pw_p, aa_p, nseq, nseg)
    return _s5_out(y_t, h, w_glu.astype(BF16), w_out.astype(BF16), gains[1], rc=rc)


def _attn_layer(h, bsz, seq, gains, w_qkv, q_gain, k_gain, w_out):
    q, k, v = _qkv(h, gains[0], w_qkv, q_gain, k_gain, bsz, seq, tm=512)
    o = _attention(q, k, v, tq=2048, tk=512, rc=256)
    return _proj_res(o, h, w_out.astype(BF16), gains[1], tm=512)


def kernel(x, norm_gains, ssm_w_in, ssm_lambda_re, ssm_lambda_im, ssm_log_step, ssm_b_re,
           ssm_b_im, ssm_c_re, ssm_c_im, ssm_d, ssm_w_glu, ssm_w_out, ffn_w_gate, ffn_w_up,
           ffn_w_down, attn_w_qkv, attn_q_gain, attn_k_gain, attn_w_out, moe_w_router,
           moe_w_gate, moe_w_up, moe_w_down):
    bsz, seq, d = x.shape
    depth = norm_gains.shape[0]
    h = x.reshape(bsz * seq, d)
    for i in range(depth):
        j = i // 2
        g = norm_gains[i]
        if i % 2 == 0:
            h = _s5_layer(h, bsz, seq, g, ssm_w_in[j], ssm_lambda_re[j], ssm_lambda_im[j],
                          ssm_log_step[j], ssm_b_re[j], ssm_b_im[j], ssm_c_re[j], ssm_c_im[j],
                          ssm_d[j], ssm_w_glu[j], ssm_w_out[j])
            h = _ffn(h, g[2], g[3], ffn_w_gate[j].astype(BF16), ffn_w_up[j].astype(BF16),
                     ffn_w_down[j].astype(BF16), tm=512, tf=1408)
        else:
            h = _attn_layer(h, bsz, seq, g, attn_w_qkv[j], attn_q_gain[j], attn_k_gain[j],
                            attn_w_out[j])
            h = _moe(h, g[2], g[3], moe_w_router[j], moe_w_gate[j].astype(BF16),
                     moe_w_up[j].astype(BF16), moe_w_down[j].astype(BF16),
                     tb=2048, tf=896, rt=256)
    return h.reshape(bsz, seq, d)
```

```python
import functools
import math

import jax
import jax.numpy as jnp
from jax import lax
from jax.experimental import pallas as pl
from jax.experimental.pallas import tpu as pltpu

F32 = jnp.float32
BF16 = jnp.bfloat16
NORM_EPS = 1e-6
ROPE_THETA = 10000.0
GRID_W = 64
N_HEADS = 16
N_KV_HEADS = 4
HEAD_DIM = 64
Q_PER_KV = N_HEADS // N_KV_HEADS
SSM_GROUP = 16
SSM_STATE = 64
S5_CHUNK = 16
S5_SEG_CHUNKS = 64
TOP_K = 2
LANES = 128
VMEM_LIMIT_BYTES = 56 * 1024 * 1024
MOE_VMEM_LIMIT_BYTES = 60 * 1024 * 1024

_NT = (((1,), (1,)), ((), ()))


def _cparams(*sem):
    return pltpu.CompilerParams(dimension_semantics=sem, vmem_limit_bytes=VMEM_LIMIT_BYTES)


def _rms(x, gain):
    return x * lax.rsqrt(jnp.mean(x * x, axis=-1, keepdims=True) + NORM_EPS) * gain


def _sigmoid(x):
    return 1.0 / (1.0 + jnp.exp(-x))


def _gelu_tanh(x):
    return x * (0.5 * (1.0 + jnp.tanh(math.sqrt(2.0 / math.pi) * (x + 0.044715 * (x * x * x)))))


def _norm_matmul_kernel(x_ref, g_ref, w_ref, o_ref):
    xn = _rms(x_ref[...], g_ref[...]).astype(BF16)
    o_ref[0] = jnp.dot(xn, w_ref[...], preferred_element_type=F32).astype(o_ref.dtype)


def _norm_matmul(x, gain, w, rc):
    n, d = x.shape
    m = w.shape[1]
    chunks = n // S5_CHUNK
    return pl.pallas_call(
        _norm_matmul_kernel,
        grid=(chunks // rc, S5_CHUNK),
        in_specs=[pl.BlockSpec((rc, d), lambda i, t: (i, t)),
                  pl.BlockSpec((1, d), lambda i, t: (0, 0)),
                  pl.BlockSpec((d, m), lambda i, t: (0, 0))],
        out_specs=pl.BlockSpec((1, rc, m), lambda i, t: (t, i, 0)),
        out_shape=jax.ShapeDtypeStruct((S5_CHUNK, chunks, m), BF16),
        compiler_params=_cparams("parallel", "parallel"),
        name="norm_matmul",
    )(x.reshape(chunks, S5_CHUNK * d), gain.reshape(1, d), w)


def _s5_prep_kernel(lr_ref, li_ref, ls_ref, bt_re_ref, bt_im_ref, c_re_ref, c_im_ref, dt_ref,
                    w1_ref, w2t_ref, pw_ref, aa_ref):
    t_n, s_n, p_n = S5_CHUNK, SSM_GROUP, SSM_STATE
    pair = 2 * s_n
    rows = t_n * pair
    hi = lax.Precision.HIGHEST

    rr = lax.broadcasted_iota(jnp.int32, (rows, rows), 0)
    cc = lax.broadcasted_iota(jnp.int32, (rows, rows), 1)
    same_group = ((rr // s_n) % 2) == ((cc // s_n) % 2)
    tvec = lax.broadcasted_iota(jnp.int32, (t_n, 1), 0).astype(F32)
    jvec = lax.broadcasted_iota(jnp.int32, (S5_SEG_CHUNKS, 1), 0).astype(F32)
    row8 = lax.broadcasted_iota(jnp.int32, (8, LANES), 0)
    own_lanes = ((lax.broadcasted_iota(jnp.int32, (rows, LANES), 1) // p_n)
                 == ((lax.broadcasted_iota(jnp.int32, (rows, LANES), 0) // s_n) % 2))

    def lanes_by_group(tab0, tab1):
        lane = lax.broadcasted_iota(jnp.int32, tab0.shape, 1)
        return jnp.where(lane < p_n, tab0, tab1)

    m_total = None
    for d in range(2):
        lsr, lsi, q_re, q_im = [], [], [], []
        for g in range(2):
            lr = lr_ref[0, d, g:g + 1, :]
            li = li_ref[0, d, g:g + 1, :]
            step = jnp.exp(ls_ref[0, d, g:g + 1, :])
            lsr.append(lr * step)
            lsi.append(li * step)
            mag = jnp.exp(lsr[g])
            a_re, a_im = mag * jnp.cos(lsi[g]), mag * jnp.sin(lsi[g])
            nr, ni = a_re - 1.0, a_im
            den = lr * lr + li * li
            q_re.append((nr * lr + ni * li) / den)
            q_im.append((ni * lr - nr * li) / den)

        def cpow(k, g):
            mag = jnp.exp(lsr[g] * k)
            ang = lsi[g] * k
            return mag * jnp.cos(ang), mag * jnp.sin(ang)

        def table(k):
            t0, t1 = cpow(k, 0), cpow(k, 1)
            return tuple(
                jnp.concatenate([jnp.broadcast_to(tg[part][t:t + 1, :], (s_n, LANES))
                                 for t in range(t_n) for tg in (t0, t1)], axis=0)
                for part in range(2))

        br, bi = bt_re_ref[0, d], bt_im_ref[0, d]
        qr = jnp.concatenate([jnp.broadcast_to(q_re[g], (s_n, LANES)) for g in range(2)], axis=0)
        qi = jnp.concatenate([jnp.broadcast_to(q_im[g], (s_n, LANES)) for g in range(2)], axis=0)
        bb_re = qr * br - qi * bi
        bb_im = qr * bi + qi * br
        cr, ci = c_re_ref[0, d], c_im_ref[0, d]

        def outer(x_re, x_im, k):
            pe_re, pe_im = table(k)
            xe_re = jnp.concatenate([x_re] * t_n, axis=0)
            xe_im = jnp.concatenate([x_im] * t_n, axis=0)
            return xe_re * pe_re - xe_im * pe_im, xe_re * pe_im + xe_im * pe_re

        if d == 0:
            l_re, l_im = outer(bb_re, bb_im, -tvec)
            rt_re, rt_im = outer(cr, ci, tvec)
            ws_re, ws_im = outer(bb_re, bb_im, (t_n - 1.0) - tvec)
            wy_re, wy_im = outer(cr, ci, tvec + 1.0)
            mask = (rr // pair) <= (cc // pair)
            kseg = t_n * jvec
        else:
            l_re, l_im = outer(bb_re, bb_im, tvec)
            rt_re, rt_im = outer(cr, ci, -tvec)
            ws_re, ws_im = l_re, l_im
            wy_re, wy_im = outer(cr, ci, t_n - tvec)
            mask = (rr // pair) >= (cc // pair)
            kseg = t_n * ((S5_SEG_CHUNKS - 1.0) - jvec)
        kern = (lax.dot_general(l_re[:, 0:p_n], rt_re[:, 0:p_n], _NT, precision=hi,
                                preferred_element_type=F32)
                - lax.dot_general(l_im[:, 0:p_n], rt_im[:, 0:p_n], _NT, precision=hi,
                                  preferred_element_type=F32))
        kern = jnp.where(mask & same_group, kern, 0.0)
        m_total = kern if m_total is None else m_total + kern

        ycols = rows
        for k2, val in ((2 * d, ws_re), (2 * d + 1, ws_im)):
            w1_ref[0, :, ycols + k2 * LANES:ycols + (k2 + 1) * LANES] = jnp.where(
                own_lanes, val, 0.0).astype(w1_ref.dtype)
        for k2, val in ((2 * d, wy_re), (2 * d + 1, -wy_im)):
            w2t_ref[0, :, k2 * LANES:(k2 + 1) * LANES] = jnp.where(
                own_lanes, val, 0.0).astype(w2t_ref.dtype)
        p0, p1 = cpow(kseg, 0), cpow(kseg, 1)
        pw_ref[0, :, 2 * d * LANES:(2 * d + 1) * LANES] = lanes_by_group(p0[0], p1[0])
        pw_ref[0, :, (2 * d + 1) * LANES:(2 * d + 2) * LANES] = lanes_by_group(p0[1], p1[1])
        one = jnp.ones((1, 1), F32)
        c0, c1 = cpow(float(t_n) * one, 0), cpow(float(t_n) * one, 1)
        s0, s1 = (cpow(float(t_n * S5_SEG_CHUNKS) * one, 0),
                  cpow(float(t_n * S5_SEG_CHUNKS) * one, 1))
        for part in range(2):
            chunk_pow = lanes_by_group(c0[part], c1[part])
            seg_pow = lanes_by_group(s0[part], s1[part])
            aa_ref[0, :, (2 * d + part) * LANES:(2 * d + part + 1) * LANES] = jnp.where(
                row8 == 0, chunk_pow, jnp.where(row8 == 1, seg_pow, 0.0))

    w1_ref[0, :, 0:rows] = (m_total + jnp.where(rr == cc, dt_ref[0], 0.0)).astype(w1_ref.dtype)


def _s5_prep(lam_re, lam_im, log_step, b_re, b_im, c_re, c_im, d_skip):
    g_n = lam_re.shape[1]
    pairs = g_n // 2
    pair = 2 * SSM_GROUP
    rows = S5_CHUNK * pair

    def states(x):
        x = jnp.transpose(x.reshape(2, pairs, 2, SSM_STATE), (1, 0, 2, 3))
        return jnp.tile(x, (1, 1, 1, LANES // SSM_STATE))

    def per_channel(x):
        x = jnp.transpose(x.reshape(2, pairs, pair, SSM_STATE), (1, 0, 2, 3))
        return jnp.tile(x, (1, 1, 1, LANES // SSM_STATE))

    ls = jnp.transpose(log_step.reshape(2, pairs, 2), (1, 0, 2))[..., None]
    bt_re = per_channel(jnp.swapaxes(b_re, 2, 3))
    bt_im = per_channel(jnp.swapaxes(b_im, 2, 3))
    dt = jnp.tile(d_skip.reshape(pairs, 1, pair), (1, 1, S5_CHUNK))

    def spec(*blk):
        return pl.BlockSpec((1,) + blk, lambda g: (g,) + (0,) * len(blk))

    return pl.pallas_call(
        _s5_prep_kernel,
        grid=(pairs,),
        in_specs=[spec(2, 2, LANES), spec(2, 2, LANES), spec(2, 2, 1),
                  spec(2, pair, LANES), spec(2, pair, LANES),
                  spec(2, pair, LANES), spec(2, pair, LANES), spec(1, rows)],
        out_specs=[spec(rows, rows + 4 * LANES), spec(rows, 4 * LANES),
                   spec(S5_SEG_CHUNKS, 4 * LANES), spec(8, 4 * LANES)],
        out_shape=[jax.ShapeDtypeStruct((pairs, rows, rows + 4 * LANES), BF16),
                   jax.ShapeDtypeStruct((pairs, rows, 4 * LANES), BF16),
                   jax.ShapeDtypeStruct((pairs, S5_SEG_CHUNKS, 4 * LANES), F32),
                   jax.ShapeDtypeStruct((pairs, 8, 4 * LANES), F32)],
        compiler_params=_cparams("parallel"),
        name="s5_prep",
    )(states(lam_re), states(lam_im), ls, bt_re, bt_im, per_channel(c_re), per_channel(c_im), dt)


S5_PAIR_LANES = 2 * SSM_GROUP
S5_PAIRS_PER_SLAB = LANES // S5_PAIR_LANES
S5_SEG_PITCH = S5_SEG_CHUNKS + 8
S5_SEG_BASE = 8


def _s5_core_kernel(x_ref, w1_ref, w2t_ref, pw_ref, aa_ref, y_ref, z_s, zs_s, x_s, u_s, xb_s,
                    *, nseq, nseg, strip):
    n_chunks = S5_SEG_CHUNKS
    rows = n_chunks * nseq
    ycols = u_s.shape[1]
    per_tile = LANES // S5_PAIR_LANES
    lane_grp = lax.broadcasted_iota(jnp.int32, (strip, LANES), 1) // S5_PAIR_LANES
    zero = jnp.zeros((nseq, LANES), F32)

    def seg_rows(j):
        return pl.ds(S5_SEG_BASE + j, nseq, stride=S5_SEG_PITCH)

    for q in range(S5_PAIRS_PER_SLAB):
        def gather(r, _, q=q):
            rws = pl.ds(pl.multiple_of(r * strip, strip), strip)
            for j in range(ycols // LANES):
                acc = None
                for i in range(per_tile):
                    xt = x_ref[per_tile * j + i, rws, :].astype(F32)
                    shift = (S5_PAIR_LANES * (i - q)) % LANES
                    if shift:
                        xt = pltpu.roll(xt, shift, axis=1)
                    acc = xt if acc is None else jnp.where(lane_grp == i, xt, acc)
                u_s[rws, j * LANES:(j + 1) * LANES] = acc.astype(BF16)
            return 0

        lax.fori_loop(0, rows // strip, gather, 0)
        z_s[...] = jnp.dot(u_s[...], w1_ref[q], preferred_element_type=F32)

        for b in range(nseq):
            dst = slice(S5_SEG_BASE + b * S5_SEG_PITCH, S5_SEG_BASE + b * S5_SEG_PITCH + n_chunks)
            for k in range(4):
                zs_s[k, dst, :] = z_s[b * n_chunks:(b + 1) * n_chunks,
                                      ycols + k * LANES:ycols + (k + 1) * LANES]
        x_s[0, seg_rows(0), :] = zero
        x_s[1, seg_rows(0), :] = zero
        x_s[2, seg_rows(n_chunks - 1), :] = zero
        x_s[3, seg_rows(n_chunks - 1), :] = zero

        aa = aa_ref[q]
        a_fr, a_fi = aa[0:1, 0:LANES], aa[0:1, LANES:2 * LANES]
        a_br, a_bi = aa[0:1, 2 * LANES:3 * LANES], aa[0:1, 3 * LANES:4 * LANES]
        g_fr, g_fi = aa[1:2, 0:LANES], aa[1:2, LANES:2 * LANES]
        g_br, g_bi = aa[1:2, 2 * LANES:3 * LANES], aa[1:2, 3 * LANES:4 * LANES]

        def fwd(j, carry):
            xr, xi = carry
            nr = a_fr * xr - a_fi * xi + zs_s[0, seg_rows(j), :]
            ni = a_fr * xi + a_fi * xr + zs_s[1, seg_rows(j), :]
            x_s[0, seg_rows(j + 1), :] = nr
            x_s[1, seg_rows(j + 1), :] = ni
            return nr, ni

        def bwd(i, carry):
            xr, xi = carry
            j = n_chunks - 1 - i
            nr = a_br * xr - a_bi * xi + zs_s[2, seg_rows(j), :]
            ni = a_br * xi + a_bi * xr + zs_s[3, seg_rows(j), :]
            x_s[2, seg_rows(j - 1), :] = nr
            x_s[3, seg_rows(j - 1), :] = ni
            return nr, ni

        ef_r, ef_i = lax.fori_loop(0, n_chunks, fwd, (zero, zero))
        eb_r, eb_i = lax.fori_loop(0, n_chunks, bwd, (zero, zero))

        def seg_carries(e_r, e_i, g_r, g_i, reverse):
            out_r = [None] * nseq
            out_i = [None] * nseq
            for b in range(nseq // nseg):
                c_r = jnp.zeros((1, LANES), F32)
                c_i = jnp.zeros((1, LANES), F32)
                order = range(nseg - 1, -1, -1) if reverse else range(nseg)
                for s in order:
                    r = b * nseg + s
                    out_r[r], out_i[r] = c_r, c_i
                    n_r = g_r * c_r - g_i * c_i + e_r[r:r + 1, :]
                    n_i = g_r * c_i + g_i * c_r + e_i[r:r + 1, :]
                    c_r, c_i = n_r, n_i
            return out_r, out_i

        cf_r, cf_i = seg_carries(ef_r, ef_i, g_fr, g_fi, False)
        cb_r, cb_i = seg_carries(eb_r, eb_i, g_br, g_bi, True)

        p = pw_ref[q]
        p_fr, p_fi = p[:, 0:LANES], p[:, LANES:2 * LANES]
        p_br, p_bi = p[:, 2 * LANES:3 * LANES], p[:, 3 * LANES:4 * LANES]
        for b in range(nseq):
            src = slice(S5_SEG_BASE + b * S5_SEG_PITCH, S5_SEG_BASE + b * S5_SEG_PITCH + n_chunks)
            dst = slice(b * n_chunks, (b + 1) * n_chunks)
            xb_s[dst, 0:LANES] = (x_s[0, src, :] + (p_fr * cf_r[b] - p_fi * cf_i[b])).astype(BF16)
            xb_s[dst, LANES:2 * LANES] = (
                x_s[1, src, :] + (p_fr * cf_i[b] + p_fi * cf_r[b])).astype(BF16)
            xb_s[dst, 2 * LANES:3 * LANES] = (
                x_s[2, src, :] + (p_br * cb_r[b] - p_bi * cb_i[b])).astype(BF16)
            xb_s[dst, 3 * LANES:4 * LANES] = (
                x_s[3, src, :] + (p_br * cb_i[b] + p_bi * cb_r[b])).astype(BF16)

        z_s[:, 0:ycols] += lax.dot_general(xb_s[...], w2t_ref[q], _NT,
                                           preferred_element_type=F32)

        def scatter(r, _, q=q):
            rws = pl.ds(pl.multiple_of(r * strip, strip), strip)
            for j in range(ycols // LANES):
                yq = z_s[rws, j * LANES:(j + 1) * LANES]
                for i in range(per_tile):
                    shift = (S5_PAIR_LANES * (q - i)) % LANES
                    yt = pltpu.roll(yq, shift, axis=1) if shift else yq
                    lanes_q = slice(q * S5_PAIR_LANES, (q + 1) * S5_PAIR_LANES)
                    y_ref[per_tile * j + i, rws, lanes_q] = yt[:, lanes_q].astype(y_ref.dtype)
            return 0

        lax.fori_loop(0, rows // strip, scatter, 0)


def _s5_core(u_t, w1, w2t, pw, aa, nseq, nseg):
    t_n, rows, d = u_t.shape
    width = w1.shape[1]
    pps = S5_PAIRS_PER_SLAB
    slab_rows = S5_SEG_BASE + nseq * S5_SEG_PITCH
    return pl.pallas_call(
        functools.partial(_s5_core_kernel, nseq=nseq, nseg=nseg, strip=128),
        grid=(d // LANES,),
        in_specs=[pl.BlockSpec((t_n, rows, LANES), lambda o: (0, 0, o)),
                  pl.BlockSpec((pps, width, w1.shape[2]), lambda o: (o, 0, 0)),
                  pl.BlockSpec((pps, width, w2t.shape[2]), lambda o: (o, 0, 0)),
                  pl.BlockSpec((pps, S5_SEG_CHUNKS, pw.shape[2]), lambda o: (o, 0, 0)),
                  pl.BlockSpec((pps, 8, aa.shape[2]), lambda o: (o, 0, 0))],
        out_specs=pl.BlockSpec((t_n, rows, LANES), lambda o: (0, 0, o)),
        out_shape=jax.ShapeDtypeStruct((t_n, rows, d), BF16),
        scratch_shapes=[pltpu.VMEM((rows, w1.shape[2]), F32),
                        pltpu.VMEM((4, slab_rows, LANES), F32),
                        pltpu.VMEM((4, slab_rows, LANES), F32),
                        pltpu.VMEM((rows, width), BF16),
                        pltpu.VMEM((rows, 4 * LANES), BF16)],
        compiler_params=_cparams("parallel"),
        name="s5_core",
    )(u_t, w1, w2t, pw, aa)


def _s5_out_kernel(y_ref, h_ref, wglu_ref, wout_ref, g_ref, o_ref):
    g = _gelu_tanh(y_ref[0].astype(F32))
    z = jnp.dot(g.astype(BF16), wglu_ref[...], preferred_element_type=F32)
    g2 = g * _sigmoid(z)
    mix = jnp.dot(g2.astype(BF16), wout_ref[...], preferred_element_type=F32)
    o_ref[...] = h_ref[...] + _rms(mix, g_ref[...])


def _s5_out(y_t, h, w_glu, w_out, gain, rc):
    n, d = h.shape
    chunks = n // S5_CHUNK
    out = pl.pallas_call(
        _s5_out_kernel,
        grid=(chunks // rc, S5_CHUNK),
        in_specs=[pl.BlockSpec((1, rc, d), lambda i, t: (t, i, 0)),
                  pl.BlockSpec((rc, d), lambda i, t: (i, t)),
                  pl.BlockSpec((d, d), lambda i, t: (0, 0)),
                  pl.BlockSpec((d, d), lambda i, t: (0, 0)),
                  pl.BlockSpec((1, d), lambda i, t: (0, 0))],
        out_specs=pl.BlockSpec((rc, d), lambda i, t: (i, t)),
        out_shape=jax.ShapeDtypeStruct((chunks, S5_CHUNK * d), F32),
        compiler_params=_cparams("parallel", "parallel"),
        name="s5_out",
    )(y_t, h.reshape(chunks, S5_CHUNK * d), w_glu, w_out, gain.reshape(1, d))
    return out.reshape(n, d)


def _ffn_kernel(h_ref, gpre_ref, gpost_ref, wg_ref, wu_ref, wd_ref, o_ref, hn_s, acc_s):
    f = pl.program_id(1)

    @pl.when(f == 0)
    def _():
        hn_s[...] = _rms(h_ref[...], gpre_ref[...]).astype(BF16)
        acc_s[...] = jnp.zeros_like(acc_s)

    hn = hn_s[...]
    a = jnp.dot(hn, wg_ref[...], preferred_element_type=F32)
    u = jnp.dot(hn, wu_ref[...], preferred_element_type=F32)
    act = a * _sigmoid(a) * u
    acc_s[...] += jnp.dot(act.astype(BF16), wd_ref[...], preferred_element_type=F32)

    @pl.when(f == pl.num_programs(1) - 1)
    def _():
        o_ref[...] = h_ref[...] + _rms(acc_s[...], gpost_ref[...])


def _ffn(h, gain_pre, gain_post, w_gate, w_up, w_down, tm, tf):
    n, d = h.shape
    d_ff = w_gate.shape[1]
    return pl.pallas_call(
        _ffn_kernel,
        grid=(n // tm, d_ff // tf),
        in_specs=[pl.BlockSpec((tm, d), lambda i, f: (i, 0)),
                  pl.BlockSpec((1, d), lambda i, f: (0, 0)),
                  pl.BlockSpec((1, d), lambda i, f: (0, 0)),
                  pl.BlockSpec((d, tf), lambda i, f: (0, f)),
                  pl.BlockSpec((d, tf), lambda i, f: (0, f)),
                  pl.BlockSpec((tf, d), lambda i, f: (f, 0))],
        out_specs=pl.BlockSpec((tm, d), lambda i, f: (i, 0)),
        out_shape=jax.ShapeDtypeStruct((n, d), F32),
        scratch_shapes=[pltpu.VMEM((tm, d), BF16), pltpu.VMEM((tm, d), F32)],
        compiler_params=_cparams("parallel", "arbitrary"),
        name="ffn",
    )(h, gain_pre.reshape(1, d), gain_post.reshape(1, d), w_gate, w_up, w_down)


def _moe_kernel(h_ref, gpre_ref, gpost_ref, wr_ref, wg_ref, wu_ref, wd_ref, o_ref,
                hn_s, gate_s, sel_s, pos_s, post_s, xc_s, yc_s, cnt_s, *, n_exp, rt, strip):
    e = pl.program_id(1)
    f = pl.program_id(2)
    tb, d = hn_s.shape
    n_f = pl.num_programs(2)

    @pl.when((e == 0) & (f == 0))
    def _():
        def route(r, total):
            rows = pl.ds(pl.multiple_of(r * strip, strip), strip)
            xn = _rms(h_ref[rows, :], gpre_ref[...])
            hn_s[rows, :] = xn.astype(BF16)
            o_ref[rows, :] = jnp.zeros((strip, d), F32)
            logits = jnp.dot(xn, wr_ref[...], precision=lax.Precision.HIGHEST,
                             preferred_element_type=F32)
            lane = lax.broadcasted_iota(jnp.int32, logits.shape, 1)
            neg = jnp.float32(-jnp.inf)
            logits = jnp.where(lane < n_exp, logits, neg)
            m1 = jnp.max(logits, axis=-1, keepdims=True)
            i1 = jnp.min(jnp.where(logits == m1, lane, LANES), axis=-1, keepdims=True)
            rest = jnp.where(lane == i1, neg, logits)
            m2 = jnp.max(rest, axis=-1, keepdims=True)
            i2 = jnp.min(jnp.where(rest == m2, lane, LANES), axis=-1, keepdims=True)
            e2 = jnp.exp(m2 - m1)
            gate_s[rows, :] = (jnp.where(lane == i1, 1.0 / (1.0 + e2), 0.0)
                               + jnp.where(lane == i2, e2 / (1.0 + e2), 0.0))
            sel = ((lane == i1) | (lane == i2)).astype(F32)
            sel_s[rows, :] = sel.astype(BF16)
            return total + jnp.sum(sel, axis=0, keepdims=True)

        total = lax.fori_loop(0, tb // strip, route, jnp.zeros((1, LANES), F32))
        lane1 = lax.broadcasted_iota(jnp.int32, total.shape, 1)
        for x in range(n_exp):
            cnt_s[x] = jnp.sum(jnp.where(lane1 == x, total, 0.0)).astype(jnp.int32)

        def rank(r, _):
            r0 = pl.multiple_of(r * strip, strip)
            rows = lax.broadcasted_iota(jnp.int32, (strip, tb), 0) + r0
            cols = lax.broadcasted_iota(jnp.int32, (strip, tb), 1)
            before = (cols < rows).astype(BF16)
            cnt = jnp.dot(before, sel_s[...], preferred_element_type=F32)
            chosen = sel_s[pl.ds(r0, strip), :] > 0
            pos_s[pl.ds(r0, strip), :] = jnp.where(chosen, cnt, -1.0)
            return 0

        lax.fori_loop(0, tb // strip, rank, 0)
        for r in range(tb // strip):
            post_s[:, r * strip:(r + 1) * strip] = pos_s[r * strip:(r + 1) * strip, :].T

    n_tiles = (cnt_s[e] + rt - 1) // rt
    lane = lax.broadcasted_iota(jnp.int32, (tb, LANES), 1)
    pos_col = jnp.sum(jnp.where(lane == e, pos_s[...], 0.0), axis=-1, keepdims=True)
    gate_col = jnp.sum(jnp.where(lane == e, gate_s[...], 0.0), axis=-1, keepdims=True)

    def tile(i, _):
        r0 = pl.multiple_of(i * rt, rt)

        @pl.when(f == 0)
        def _():
            slot = (lax.broadcasted_iota(jnp.int32, (rt, tb), 0) + r0).astype(F32)
            pick = (post_s[pl.ds(e, 1), :] == slot).astype(BF16)
            xc_s[pl.ds(r0, rt), :] = jnp.dot(pick, hn_s[...],
                                             preferred_element_type=F32).astype(BF16)

        xc = xc_s[pl.ds(r0, rt), :]
        a = jnp.dot(xc, wg_ref[0], preferred_element_type=F32)
        u = jnp.dot(xc, wu_ref[0], preferred_element_type=F32)
        act = (a * _sigmoid(a) * u).astype(BF16)
        y = jnp.dot(act, wd_ref[0], preferred_element_type=F32)
        prev = jnp.where(f == 0, 0.0, yc_s[pl.ds(r0, rt), :])
        yc_s[pl.ds(r0, rt), :] = prev + y

        @pl.when(f == n_f - 1)
        def _():
            yc = yc_s[pl.ds(r0, rt), :].astype(BF16)
            slot = (lax.broadcasted_iota(jnp.int32, (strip, rt), 1) + r0).astype(F32)
            for r in range(tb // strip):
                rows = slice(r * strip, (r + 1) * strip)
                put = (pos_col[rows] == slot).astype(BF16)
                back = jnp.dot(put, yc, preferred_element_type=F32)
                o_ref[rows, :] += gate_col[rows] * back
        return 0

    lax.fori_loop(0, n_tiles, tile, 0)

    @pl.when((e == n_exp - 1) & (f == n_f - 1))
    def _():
        def finish(r, _):
            rows = pl.ds(pl.multiple_of(r * strip, strip), strip)
            o_ref[rows, :] = h_ref[rows, :] + _rms(o_ref[rows, :], gpost_ref[...])
            return 0

        lax.fori_loop(0, tb // strip, finish, 0)


def _moe(h, gain_pre, gain_post, w_router, w_gate, w_up, w_down, tb, tf, rt):
    n, d = h.shape
    n_exp, _, d_ff = w_gate.shape
    w_r = jnp.pad(w_router, ((0, 0), (0, LANES - n_exp)))
    once = pl.Buffered(1)
    return pl.pallas_call(
        functools.partial(_moe_kernel, n_exp=n_exp, rt=rt, strip=256),
        grid=(n // tb, n_exp, d_ff // tf),
        in_specs=[pl.BlockSpec((tb, d), lambda i, e, f: (i, 0), pipeline_mode=once),
                  pl.BlockSpec((1, d), lambda i, e, f: (0, 0)),
                  pl.BlockSpec((1, d), lambda i, e, f: (0, 0)),
                  pl.BlockSpec((d, LANES), lambda i, e, f: (0, 0)),
                  pl.BlockSpec((1, d, tf), lambda i, e, f: (e, 0, f)),
                  pl.BlockSpec((1, d, tf), lambda i, e, f: (e, 0, f)),
                  pl.BlockSpec((1, tf, d), lambda i, e, f: (e, f, 0))],
        out_specs=pl.BlockSpec((tb, d), lambda i, e, f: (i, 0), pipeline_mode=once),
        out_shape=jax.ShapeDtypeStruct((n, d), F32),
        scratch_shapes=[pltpu.VMEM((tb, d), BF16),
                        pltpu.VMEM((tb, LANES), F32),
                        pltpu.VMEM((tb, LANES), BF16),
                        pltpu.VMEM((tb, LANES), F32),
                        pltpu.VMEM((LANES, tb), F32),
                        pltpu.VMEM((tb, d), BF16),
                        pltpu.VMEM((tb, d), F32),
                        pltpu.SMEM((n_exp,), jnp.int32)],
        compiler_params=pltpu.CompilerParams(
            dimension_semantics=("parallel", "arbitrary", "arbitrary"),
            vmem_limit_bytes=MOE_VMEM_LIMIT_BYTES),
        name="moe_ffn",
    )(h, gain_pre.reshape(1, d), gain_post.reshape(1, d), w_r, w_gate, w_up, w_down)


def _qkv_kernel(h_ref, g_ref, w_ref, qg_ref, kg_ref, cs_ref, sn_ref, bd_ref,
                q_ref, k_ref, v_ref):
    xn = _rms(h_ref[...], g_ref[...]).astype(BF16)
    qkv = jnp.dot(xn, w_ref[...], preferred_element_type=F32)
    cs = cs_ref[...]
    sn = sn_ref[...]
    bd = bd_ref[...]
    lane = lax.broadcasted_iota(jnp.int32, cs.shape, 1)
    first_half = (lane % HEAD_DIM) < (HEAD_DIM // 2)
    scale = math.log2(math.e) / math.sqrt(HEAD_DIM)

    def norm_rope(x, gain):
        ms = jnp.dot(x * x, bd, precision=lax.Precision.HIGHEST, preferred_element_type=F32)
        y = x * lax.rsqrt(ms + NORM_EPS) * gain
        partner = jnp.where(first_half,
                            pltpu.roll(y, LANES - HEAD_DIM // 2, axis=1),
                            pltpu.roll(y, HEAD_DIM // 2, axis=1))
        return y * cs + partner * sn

    n_q_tiles = N_HEADS * HEAD_DIM // LANES
    for t in range(n_q_tiles):
        y = norm_rope(qkv[:, t * LANES:(t + 1) * LANES], qg_ref[...]) * scale
        q_ref[0, 2 * t] = y[:, 0:HEAD_DIM].astype(BF16)
        q_ref[0, 2 * t + 1] = y[:, HEAD_DIM:LANES].astype(BF16)
    k0 = N_HEADS * HEAD_DIM
    for t in range(N_KV_HEADS * HEAD_DIM // LANES):
        y = norm_rope(qkv[:, k0 + t * LANES:k0 + (t + 1) * LANES], kg_ref[...])
        k_ref[0, 2 * t] = y[:, 0:HEAD_DIM].astype(BF16)
        k_ref[0, 2 * t + 1] = y[:, HEAD_DIM:LANES].astype(BF16)
    v0 = (N_HEADS + N_KV_HEADS) * HEAD_DIM
    ones = jnp.ones((qkv.shape[0], LANES - HEAD_DIM), BF16)
    for j in range(N_KV_HEADS):
        vj = qkv[:, v0 + j * HEAD_DIM:v0 + (j + 1) * HEAD_DIM].astype(BF16)
        v_ref[0, j] = jnp.concatenate([vj, ones], axis=-1)


def _rope_tables(seq):
    axis_dim = HEAD_DIM // 2
    freqs = ROPE_THETA ** (-jnp.arange(0, axis_dim, 2, dtype=F32) / axis_dim)
    rows = seq // GRID_W
    row_ang = jnp.arange(rows, dtype=F32)[:, None] * freqs
    col_ang = jnp.arange(GRID_W, dtype=F32)[:, None] * freqs
    ang = jnp.concatenate([
        jnp.broadcast_to(row_ang[:, None, :], (rows, GRID_W, freqs.shape[0])),
        jnp.broadcast_to(col_ang[None, :, :], (rows, GRID_W, freqs.shape[0]))], axis=-1)
    ang = ang.reshape(seq, HEAD_DIM // 2)
    cos, sin = jnp.cos(ang), jnp.sin(ang)
    cs = jnp.tile(jnp.concatenate([cos, cos], axis=-1), (1, LANES // HEAD_DIM))
    sn = jnp.tile(jnp.concatenate([-sin, sin], axis=-1), (1, LANES // HEAD_DIM))
    return cs, sn


def _qkv(h, gain, w_qkv, q_gain, k_gain, bsz, seq, tm):
    n, d = h.shape
    width = w_qkv.shape[1]
    perm = jnp.concatenate([jnp.arange(0, HEAD_DIM, 2), jnp.arange(1, HEAD_DIM, 2)])
    n_rot = N_HEADS + N_KV_HEADS
    cols = (jnp.arange(n_rot)[:, None] * HEAD_DIM + perm[None, :]).reshape(-1)
    cols = jnp.concatenate([cols, jnp.arange(n_rot * HEAD_DIM, width)])
    w = w_qkv[:, cols].astype(BF16)
    qg = jnp.tile(q_gain[perm], LANES // HEAD_DIM).reshape(1, LANES)
    kg = jnp.tile(k_gain[perm], LANES // HEAD_DIM).reshape(1, LANES)
    cs, sn = _rope_tables(seq)
    blk = jnp.arange(LANES) // HEAD_DIM
    bd = (blk[:, None] == blk[None, :]).astype(F32) / HEAD_DIM
    per_seq = seq // tm
    return pl.pallas_call(
        _qkv_kernel,
        grid=(n // tm,),
        in_specs=[pl.BlockSpec((tm, d), lambda i: (i, 0)),
                  pl.BlockSpec((1, d), lambda i: (0, 0)),
                  pl.BlockSpec((d, width), lambda i: (0, 0)),
                  pl.BlockSpec((1, LANES), lambda i: (0, 0)),
                  pl.BlockSpec((1, LANES), lambda i: (0, 0)),
                  pl.BlockSpec((tm, LANES), lambda i: (i % per_seq, 0)),
                  pl.BlockSpec((tm, LANES), lambda i: (i % per_seq, 0)),
                  pl.BlockSpec((LANES, LANES), lambda i: (0, 0))],
        out_specs=[pl.BlockSpec((1, N_HEADS, tm, HEAD_DIM),
                                lambda i: (i // per_seq, 0, i % per_seq, 0)),
                   pl.BlockSpec((1, N_KV_HEADS, tm, HEAD_DIM),
                                lambda i: (i // per_seq, 0, i % per_seq, 0)),
                   pl.BlockSpec((1, N_KV_HEADS, tm, LANES),
                                lambda i: (i // per_seq, 0, i % per_seq, 0))],
        out_shape=[jax.ShapeDtypeStruct((bsz, N_HEADS, seq, HEAD_DIM), BF16),
                   jax.ShapeDtypeStruct((bsz, N_KV_HEADS, seq, HEAD_DIM), BF16),
                   jax.ShapeDtypeStruct((bsz, N_KV_HEADS, seq, LANES), BF16)],
        compiler_params=_cparams("parallel"),
        name="qkv_rope",
    )(h, gain.reshape(1, d), w, qg, kg, cs, sn, bd)


def _attn_kernel(q_ref, k_ref, v_ref, o_ref, m_s, acc_s, s_buf, p_buf, a_buf,
                 *, tq, tk, rc, unroll):
    seq = k_ref.shape[2]
    chunks_per_head = tq // rc
    n_chunks = Q_PER_KV * chunks_per_head
    n_steps = (seq // tk) * n_chunks

    m_s[...] = jnp.full_like(m_s, -jnp.inf)
    acc_s[...] = jnp.zeros_like(acc_s)

    def where(n):
        c = n % n_chunks
        return (pl.multiple_of((n // n_chunks) * tk, tk), c // chunks_per_head,
                pl.multiple_of((c % chunks_per_head) * rc, rc))

    def scores(n):
        k0, g, r0 = where(n)
        q = q_ref[0, g, pl.ds(r0, rc), :]
        s_buf[...] = lax.dot_general(q, k_ref[0, 0, pl.ds(k0, tk), :], _NT,
                                     preferred_element_type=F32)

    def softmax(n):
        _, g, r0 = where(n)
        s = s_buf[...]
        m_prev = m_s[g, pl.ds(r0, rc), :]
        m_new = jnp.maximum(m_prev, jnp.max(s, axis=-1, keepdims=True))
        a_buf[...] = jnp.exp2(m_prev - m_new)
        for t in range(tk // LANES):
            p_buf[:, t * LANES:(t + 1) * LANES] = jnp.exp2(
                s[:, t * LANES:(t + 1) * LANES] - m_new).astype(BF16)
        m_s[g, pl.ds(r0, rc), :] = m_new

    def values(n):
        k0, g, r0 = where(n)
        pv = jnp.dot(p_buf[...], v_ref[0, 0, pl.ds(k0, tk), :],
                     preferred_element_type=F32)
        acc_s[g, pl.ds(r0, rc), :] = a_buf[...] * acc_s[g, pl.ds(r0, rc), :] + pv

    scores(0)
    softmax(0)
    scores(1)

    def body(n, _):
        values(n)
        softmax(n + 1)
        scores(n + 2)
        return 0

    lax.fori_loop(0, n_steps - 2, body, 0, unroll=unroll)
    values(n_steps - 2)
    softmax(n_steps - 1)
    values(n_steps - 1)

    outs = []
    for g in range(Q_PER_KV):
        acc = acc_s[g]
        o = acc / pltpu.roll(acc, HEAD_DIM, axis=1)
        outs.append(o[:, 0:HEAD_DIM])
    o_ref[...] = jnp.concatenate(outs, axis=-1).astype(o_ref.dtype)


def _attention(q, k, v, tq, tk, rc):
    bsz, _, seq, _ = q.shape
    n_q = seq // tq
    n_steps = (seq // tk) * Q_PER_KV * (tq // rc)
    unroll = next(u for u in (5, 4, 3, 2, 1) if (n_steps - 2) % u == 0)
    return pl.pallas_call(
        functools.partial(_attn_kernel, tq=tq, tk=tk, rc=rc, unroll=unroll),
        grid=(bsz, N_KV_HEADS, n_q),
        in_specs=[pl.BlockSpec((1, Q_PER_KV, tq, HEAD_DIM), lambda b, j, i: (b, j, i, 0)),
                  pl.BlockSpec((1, 1, seq, HEAD_DIM), lambda b, j, i: (b, j, 0, 0)),
                  pl.BlockSpec((1, 1, seq, LANES), lambda b, j, i: (b, j, 0, 0))],
        out_specs=pl.BlockSpec((tq, Q_PER_KV * HEAD_DIM), lambda b, j, i: (b * n_q + i, j)),
        out_shape=jax.ShapeDtypeStruct((bsz * seq, N_HEADS * HEAD_DIM), BF16),
        scratch_shapes=[pltpu.VMEM((Q_PER_KV, tq, LANES), F32),
                        pltpu.VMEM((Q_PER_KV, tq, LANES), F32),
                        pltpu.VMEM((rc, tk), F32),
                        pltpu.VMEM((rc, tk), BF16),
                        pltpu.VMEM((rc, LANES), F32)],
        compiler_params=_cparams("parallel", "parallel", "parallel"),
        name="flash_attn",
    )(q, k, v)


def _proj_res_kernel(x_ref, h_ref, w_ref, g_ref, o_ref):
    mix = jnp.dot(x_ref[...], w_ref[...], preferred_element_type=F32)
    o_ref[...] = h_ref[...] + _rms(mix, g_ref[...])


def _proj_res(x, h, w, gain, tm):
    n, d = h.shape
    k = x.shape[1]
    return pl.pallas_call(
        _proj_res_kernel,
        grid=(n // tm,),
        in_specs=[pl.BlockSpec((tm, k), lambda i: (i, 0)),
                  pl.BlockSpec((tm, d), lambda i: (i, 0)),
                  pl.BlockSpec((k, d), lambda i: (0, 0)),
                  pl.BlockSpec((1, d), lambda i: (0, 0))],
        out_specs=pl.BlockSpec((tm, d), lambda i: (i, 0)),
        out_shape=jax.ShapeDtypeStruct((n, d), F32),
        compiler_params=_cparams("parallel"),
        name="proj_res",
    )(x, h, w, gain.reshape(1, d))


def _s5_layer(h, bsz, seq, gains, w_in, lam_re, lam_im, log_step, b_re, b_im, c_re, c_im,
              d_skip, w_glu, w_out):
    seg_tokens = S5_CHUNK * S5_SEG_CHUNKS
    nseg = seq // seg_tokens
    nseq = bsz * nseg
    rc = min(512, h.shape[0] // S5_CHUNK)
    u_t = _norm_matmul(h, gains[0], w_in.astype(BF16), rc=rc)
    w1, w2t, pw_p, aa_p = _s5_prep(lam_re, lam_im, log_step, b_re, b_im, c_re, c_im, d_skip)
    y_t = _s5_core(u_t, w1, w2t, pw_p, aa_p, nseq, nseg)
    return _s5_out(y_t, h, w_glu.astype(BF16), w_out.astype(BF16), gains[1], rc=rc)


def _attn_layer(h, bsz, seq, gains, w_qkv, q_gain, k_gain, w_out):
    q, k, v = _qkv(h, gains[0], w_qkv, q_gain, k_gain, bsz, seq, tm=512)
    o = _attention(q, k, v, tq=2048, tk=512, rc=256)
    return _proj_res(o, h, w_out.astype(BF16), gains[1], tm=512)


def kernel(x, norm_gains, ssm_w_in, ssm_lambda_re, ssm_lambda_im, ssm_log_step, ssm_b_re,
           ssm_b_im, ssm_c_re, ssm_c_im, ssm_d, ssm_w_glu, ssm_w_out, ffn_w_gate, ffn_w_up,
           ffn_w_down, attn_w_qkv, attn_q_gain, attn_k_gain, attn_w_out, moe_w_router,
           moe_w_gate, moe_w_up, moe_w_down):
    bsz, seq, d = x.shape
    depth = norm_gains.shape[0]
    h = x.reshape(bsz * seq, d)
    for i in range(depth):
        j = i // 2
        g = norm_gains[i]
        if i % 2 == 0:
            h = _s5_layer(h, bsz, seq, g, ssm_w_in[j], ssm_lambda_re[j], ssm_lambda_im[j],
                          ssm_log_step[j], ssm_b_re[j], ssm_b_im[j], ssm_c_re[j], ssm_c_im[j],
                          ssm_d[j], ssm_w_glu[j], ssm_w_out[j])
            h = _ffn(h, g[2], g[3], ffn_w_gate[j].astype(BF16), ffn_w_up[j].astype(BF16),
                     ffn_w_down[j].astype(BF16), tm=512, tf=1408)
        else:
            h = _attn_layer(h, bsz, seq, g, attn_w_qkv[j], attn_q_gain[j], attn_k_gain[j],
                            attn_w_out[j])
            h = _moe(h, g[2], g[3], moe_w_router[j], moe_w_gate[j].astype(BF16),
                     moe_w_up[j].astype(BF16), moe_w_down[j].astype(BF16),
                     tb=2048, tf=896, rt=256)
    return h.reshape(bsz, seq, d)
```

```python
import functools
import math

import jax
import jax.numpy as jnp
from jax import lax
from jax.experimental import pallas as pl
from jax.experimental.pallas import tpu as pltpu

F32 = jnp.float32
BF16 = jnp.bfloat16
NORM_EPS = 1e-6
ROPE_THETA = 10000.0
GRID_W = 64
N_HEADS = 16
N_KV_HEADS = 4
HEAD_DIM = 64
Q_PER_KV = N_HEADS // N_KV_HEADS
SSM_GROUP = 16
SSM_STATE = 64
S5_CHUNK = 16
S5_SEG_CHUNKS = 64
TOP_K = 2
LANES = 128
VMEM_LIMIT_BYTES = 56 * 1024 * 1024
MOE_VMEM_LIMIT_BYTES = 60 * 1024 * 1024

_NT = (((1,), (1,)), ((), ()))


def _cparams(*sem):
    return pltpu.CompilerParams(dimension_semantics=sem, vmem_limit_bytes=VMEM_LIMIT_BYTES)


def _rms(x, gain):
    return x * lax.rsqrt(jnp.mean(x * x, axis=-1, keepdims=True) + NORM_EPS) * gain


def _sigmoid(x):
    return 1.0 / (1.0 + jnp.exp(-x))


def _gelu_tanh(x):
    return x * (0.5 * (1.0 + jnp.tanh(math.sqrt(2.0 / math.pi) * (x + 0.044715 * (x * x * x)))))


def _norm_matmul_kernel(x_ref, g_ref, w_ref, o_ref):
    xn = _rms(x_ref[...], g_ref[...]).astype(BF16)
    o_ref[0] = jnp.dot(xn, w_ref[...], preferred_element_type=F32).astype(o_ref.dtype)


def _norm_matmul(x, gain, w, rc):
    n, d = x.shape
    m = w.shape[1]
    chunks = n // S5_CHUNK
    return pl.pallas_call(
        _norm_matmul_kernel,
        grid=(chunks // rc, S5_CHUNK),
        in_specs=[pl.BlockSpec((rc, d), lambda i, t: (i, t)),
                  pl.BlockSpec((1, d), lambda i, t: (0, 0)),
                  pl.BlockSpec((d, m), lambda i, t: (0, 0))],
        out_specs=pl.BlockSpec((1, rc, m), lambda i, t: (t, i, 0)),
        out_shape=jax.ShapeDtypeStruct((S5_CHUNK, chunks, m), BF16),
        compiler_params=_cparams("parallel", "parallel"),
        name="norm_matmul",
    )(x.reshape(chunks, S5_CHUNK * d), gain.reshape(1, d), w)


def _s5_prep_kernel(lr_ref, li_ref, ls_ref, bt_re_ref, bt_im_ref, c_re_ref, c_im_ref, dt_ref,
                    w1_ref, w2t_ref, pw_ref, aa_ref):
    t_n, s_n, p_n = S5_CHUNK, SSM_GROUP, SSM_STATE
    pair = 2 * s_n
    rows = t_n * pair
    hi = lax.Precision.HIGHEST

    rr = lax.broadcasted_iota(jnp.int32, (rows, rows), 0)
    cc = lax.broadcasted_iota(jnp.int32, (rows, rows), 1)
    same_group = ((rr // s_n) % 2) == ((cc // s_n) % 2)
    tvec = lax.broadcasted_iota(jnp.int32, (t_n, 1), 0).astype(F32)
    jvec = lax.broadcasted_iota(jnp.int32, (S5_SEG_CHUNKS, 1), 0).astype(F32)
    row8 = lax.broadcasted_iota(jnp.int32, (8, LANES), 0)
    own_lanes = ((lax.broadcasted_iota(jnp.int32, (rows, LANES), 1) // p_n)
                 == ((lax.broadcasted_iota(jnp.int32, (rows, LANES), 0) // s_n) % 2))

    def lanes_by_group(tab0, tab1):
        lane = lax.broadcasted_iota(jnp.int32, tab0.shape, 1)
        return jnp.where(lane < p_n, tab0, tab1)

    m_total = None
    for d in range(2):
        lsr, lsi, q_re, q_im = [], [], [], []
        for g in range(2):
            lr = lr_ref[0, d, g:g + 1, :]
            li = li_ref[0, d, g:g + 1, :]
            step = jnp.exp(ls_ref[0, d, g:g + 1, :])
            lsr.append(lr * step)
            lsi.append(li * step)
            mag = jnp.exp(lsr[g])
            a_re, a_im = mag * jnp.cos(lsi[g]), mag * jnp.sin(lsi[g])
            nr, ni = a_re - 1.0, a_im
            den = lr * lr + li * li
            q_re.append((nr * lr + ni * li) / den)
            q_im.append((ni * lr - nr * li) / den)

        def cpow(k, g):
            mag = jnp.exp(lsr[g] * k)
            ang = lsi[g] * k
            return mag * jnp.cos(ang), mag * jnp.sin(ang)

        def table(k):
            t0, t1 = cpow(k, 0), cpow(k, 1)
            return tuple(
                jnp.concatenate([jnp.broadcast_to(tg[part][t:t + 1, :], (s_n, LANES))
                                 for t in range(t_n) for tg in (t0, t1)], axis=0)
                for part in range(2))

        br, bi = bt_re_ref[0, d], bt_im_ref[0, d]
        qr = jnp.concatenate([jnp.broadcast_to(q_re[g], (s_n, LANES)) for g in range(2)], axis=0)
        qi = jnp.concatenate([jnp.broadcast_to(q_im[g], (s_n, LANES)) for g in range(2)], axis=0)
        bb_re = qr * br - qi * bi
        bb_im = qr * bi + qi * br
        cr, ci = c_re_ref[0, d], c_im_ref[0, d]

        def outer(x_re, x_im, k):
            pe_re, pe_im = table(k)
            xe_re = jnp.concatenate([x_re] * t_n, axis=0)
            xe_im = jnp.concatenate([x_im] * t_n, axis=0)
            return xe_re * pe_re - xe_im * pe_im, xe_re * pe_im + xe_im * pe_re

        if d == 0:
            l_re, l_im = outer(bb_re, bb_im, -tvec)
            rt_re, rt_im = outer(cr, ci, tvec)
            ws_re, ws_im = outer(bb_re, bb_im, (t_n - 1.0) - tvec)
            wy_re, wy_im = outer(cr, ci, tvec + 1.0)
            mask = (rr // pair) <= (cc // pair)
            kseg = t_n * jvec
        else:
            l_re, l_im = outer(bb_re, bb_im, tvec)
            rt_re, rt_im = outer(cr, ci, -tvec)
            ws_re, ws_im = l_re, l_im
            wy_re, wy_im = outer(cr, ci, t_n - tvec)
            mask = (rr // pair) >= (cc // pair)
            kseg = t_n * ((S5_SEG_CHUNKS - 1.0) - jvec)
        kern = (lax.dot_general(l_re[:, 0:p_n], rt_re[:, 0:p_n], _NT, precision=hi,
                                preferred_element_type=F32)
                - lax.dot_general(l_im[:, 0:p_n], rt_im[:, 0:p_n], _NT, precision=hi,
                                  preferred_element_type=F32))
        kern = jnp.where(mask & same_group, kern, 0.0)
        m_total = kern if m_total is None else m_total + kern

        ycols = rows
        for k2, val in ((2 * d, ws_re), (2 * d + 1, ws_im)):
            w1_ref[0, :, ycols + k2 * LANES:ycols + (k2 + 1) * LANES] = jnp.where(
                own_lanes, val, 0.0).astype(w1_ref.dtype)
        for k2, val in ((2 * d, wy_re), (2 * d + 1, -wy_im)):
            w2t_ref[0, :, k2 * LANES:(k2 + 1) * LANES] = jnp.where(
                own_lanes, val, 0.0).astype(w2t_ref.dtype)
        p0, p1 = cpow(kseg, 0), cpow(kseg, 1)
        pw_ref[0, :, 2 * d * LANES:(2 * d + 1) * LANES] = lanes_by_group(p0[0], p1[0])
        pw_ref[0, :, (2 * d + 1) * LANES:(2 * d + 2) * LANES] = lanes_by_group(p0[1], p1[1])
        one = jnp.ones((1, 1), F32)
        c0, c1 = cpow(float(t_n) * one, 0), cpow(float(t_n) * one, 1)
        s0, s1 = (cpow(float(t_n * S5_SEG_CHUNKS) * one, 0),
                  cpow(float(t_n * S5_SEG_CHUNKS) * one, 1))
        for part in range(2):
            chunk_pow = lanes_by_group(c0[part], c1[part])
            seg_pow = lanes_by_group(s0[part], s1[part])
            aa_ref[0, :, (2 * d + part) * LANES:(2 * d + part + 1) * LANES] = jnp.where(
                row8 == 0, chunk_pow, jnp.where(row8 == 1, seg_pow, 0.0))

    w1_ref[0, :, 0:rows] = (m_total + jnp.where(rr == cc, dt_ref[0], 0.0)).astype(w1_ref.dtype)


def _s5_prep(lam_re, lam_im, log_step, b_re, b_im, c_re, c_im, d_skip):
    g_n = lam_re.shape[1]
    pairs = g_n // 2
    pair = 2 * SSM_GROUP
    rows = S5_CHUNK * pair

    def states(x):
        x = jnp.transpose(x.reshape(2, pairs, 2, SSM_STATE), (1, 0, 2, 3))
        return jnp.tile(x, (1, 1, 1, LANES // SSM_STATE))

    def per_channel(x):
        x = jnp.transpose(x.reshape(2, pairs, pair, SSM_STATE), (1, 0, 2, 3))
        return jnp.tile(x, (1, 1, 1, LANES // SSM_STATE))

    ls = jnp.transpose(log_step.reshape(2, pairs, 2), (1, 0, 2))[..., None]
    bt_re = per_channel(jnp.swapaxes(b_re, 2, 3))
    bt_im = per_channel(jnp.swapaxes(b_im, 2, 3))
    dt = jnp.tile(d_skip.reshape(pairs, 1, pair), (1, 1, S5_CHUNK))

    def spec(*blk):
        return pl.BlockSpec((1,) + blk, lambda g: (g,) + (0,) * len(blk))

    return pl.pallas_call(
        _s5_prep_kernel,
        grid=(pairs,),
        in_specs=[spec(2, 2, LANES), spec(2, 2, LANES), spec(2, 2, 1),
                  spec(2, pair, LANES), spec(2, pair, LANES),
                  spec(2, pair, LANES), spec(2, pair, LANES), spec(1, rows)],
        out_specs=[spec(rows, rows + 4 * LANES), spec(rows, 4 * LANES),
                   spec(S5_SEG_CHUNKS, 4 * LANES), spec(8, 4 * LANES)],
        out_shape=[jax.ShapeDtypeStruct((pairs, rows, rows + 4 * LANES), BF16),
                   jax.ShapeDtypeStruct((pairs, rows, 4 * LANES), BF16),
                   jax.ShapeDtypeStruct((pairs, S5_SEG_CHUNKS, 4 * LANES), F32),
                   jax.ShapeDtypeStruct((pairs, 8, 4 * LANES), F32)],
        compiler_params=_cparams("parallel"),
        name="s5_prep",
    )(states(lam_re), states(lam_im), ls, bt_re, bt_im, per_channel(c_re), per_channel(c_im), dt)


S5_PAIR_LANES = 2 * SSM_GROUP
S5_PAIRS_PER_SLAB = LANES // S5_PAIR_LANES
S5_SEG_PITCH = S5_SEG_CHUNKS + 8
S5_SEG_BASE = 8


def _s5_core_kernel(x_ref, w1_ref, w2t_ref, pw_ref, aa_ref, y_ref, z_s, zs_s, x_s, u_s, xb_s,
                    *, nseq, nseg, strip):
    n_chunks = S5_SEG_CHUNKS
    rows = n_chunks * nseq
    ycols = u_s.shape[1]
    per_tile = LANES // S5_PAIR_LANES
    lane_grp = lax.broadcasted_iota(jnp.int32, (strip, LANES), 1) // S5_PAIR_LANES
    zero = jnp.zeros((nseq, LANES), F32)

    def seg_rows(j):
        return pl.ds(S5_SEG_BASE + j, nseq, stride=S5_SEG_PITCH)

    for q in range(S5_PAIRS_PER_SLAB):
        def gather(r, _, q=q):
            rws = pl.ds(pl.multiple_of(r * strip, strip), strip)
            for j in range(ycols // LANES):
                acc = None
                for i in range(per_tile):
                    xt = x_ref[per_tile * j + i, rws, :].astype(F32)
                    shift = (S5_PAIR_LANES * (i - q)) % LANES
                    if shift:
                        xt = pltpu.roll(xt, shift, axis=1)
                    acc = xt if acc is None else jnp.where(lane_grp == i, xt, acc)
                u_s[rws, j * LANES:(j + 1) * LANES] = acc.astype(BF16)
            return 0

        lax.fori_loop(0, rows // strip, gather, 0)
        z_s[...] = jnp.dot(u_s[...], w1_ref[q], preferred_element_type=F32)

        for b in range(nseq):
            dst = slice(S5_SEG_BASE + b * S5_SEG_PITCH, S5_SEG_BASE + b * S5_SEG_PITCH + n_chunks)
            for k in range(4):
                zs_s[k, dst, :] = z_s[b * n_chunks:(b + 1) * n_chunks,
                                      ycols + k * LANES:ycols + (k + 1) * LANES]
        x_s[0, seg_rows(0), :] = zero
        x_s[1, seg_rows(0), :] = zero
        x_s[2, seg_rows(n_chunks - 1), :] = zero
        x_s[3, seg_rows(n_chunks - 1), :] = zero

        aa = aa_ref[q]
        a_fr, a_fi = aa[0:1, 0:LANES], aa[0:1, LANES:2 * LANES]
        a_br, a_bi = aa[0:1, 2 * LANES:3 * LANES], aa[0:1, 3 * LANES:4 * LANES]
        g_fr, g_fi = aa[1:2, 0:LANES], aa[1:2, LANES:2 * LANES]
        g_br, g_bi = aa[1:2, 2 * LANES:3 * LANES], aa[1:2, 3 * LANES:4 * LANES]

        def fwd(j, carry):
            xr, xi = carry
            nr = a_fr * xr - a_fi * xi + zs_s[0, seg_rows(j), :]
            ni = a_fr * xi + a_fi * xr + zs_s[1, seg_rows(j), :]
            x_s[0, seg_rows(j + 1), :] = nr
            x_s[1, seg_rows(j + 1), :] = ni
            return nr, ni

        def bwd(i, carry):
            xr, xi = carry
            j = n_chunks - 1 - i
            nr = a_br * xr - a_bi * xi + zs_s[2, seg_rows(j), :]
            ni = a_br * xi + a_bi * xr + zs_s[3, seg_rows(j), :]
            x_s[2, seg_rows(j - 1), :] = nr
            x_s[3, seg_rows(j - 1), :] = ni
            return nr, ni

        ef_r, ef_i = lax.fori_loop(0, n_chunks, fwd, (zero, zero))
        eb_r, eb_i = lax.fori_loop(0, n_chunks, bwd, (zero, zero))

        def seg_carries(e_r, e_i, g_r, g_i, reverse):
            out_r = [None] * nseq
            out_i = [None] * nseq
            for b in range(nseq // nseg):
                c_r = jnp.zeros((1, LANES), F32)
                c_i = jnp.zeros((1, LANES), F32)
                order = range(nseg - 1, -1, -1) if reverse else range(nseg)
                for s in order:
                    r = b * nseg + s
                    out_r[r], out_i[r] = c_r, c_i
                    n_r = g_r * c_r - g_i * c_i + e_r[r:r + 1, :]
                    n_i = g_r * c_i + g_i * c_r + e_i[r:r + 1, :]
                    c_r, c_i = n_r, n_i
            return out_r, out_i

        cf_r, cf_i = seg_carries(ef_r, ef_i, g_fr, g_fi, False)
        cb_r, cb_i = seg_carries(eb_r, eb_i, g_br, g_bi, True)

        p = pw_ref[q]
        p_fr, p_fi = p[:, 0:LANES], p[:, LANES:2 * LANES]
        p_br, p_bi = p[:, 2 * LANES:3 * LANES], p[:, 3 * LANES:4 * LANES]
        for b in range(nseq):
            src = slice(S5_SEG_BASE + b * S5_SEG_PITCH, S5_SEG_BASE + b * S5_SEG_PITCH + n_chunks)
            dst = slice(b * n_chunks, (b + 1) * n_chunks)
            xb_s[dst, 0:LANES] = (x_s[0, src, :] + (p_fr * cf_r[b] - p_fi * cf_i[b])).astype(BF16)
            xb_s[dst, LANES:2 * LANES] = (
                x_s[1, src, :] + (p_fr * cf_i[b] + p_fi * cf_r[b])).astype(BF16)
            xb_s[dst, 2 * LANES:3 * LANES] = (
                x_s[2, src, :] + (p_br * cb_r[b] - p_bi * cb_i[b])).astype(BF16)
            xb_s[dst, 3 * LANES:4 * LANES] = (
                x_s[3, src, :] + (p_br * cb_i[b] + p_bi * cb_r[b])).astype(BF16)

        z_s[:, 0:ycols] += lax.dot_general(xb_s[...], w2t_ref[q], _NT,
                                           preferred_element_type=F32)

        def scatter(r, _, q=q):
            rws = pl.ds(pl.multiple_of(r * strip, strip), strip)
            for j in range(ycols // LANES):
                yq = z_s[rws, j * LANES:(j + 1) * LANES]
                for i in range(per_tile):
                    shift = (S5_PAIR_LANES * (q - i)) % LANES
                    yt = pltpu.roll(yq, shift, axis=1) if shift else yq
                    lanes_q = slice(q * S5_PAIR_LANES, (q + 1) * S5_PAIR_LANES)
                    y_ref[per_tile * j + i, rws, lanes_q] = yt[:, lanes_q].astype(y_ref.dtype)
            return 0

        lax.fori_loop(0, rows // strip, scatter, 0)


def _s5_core(u_t, w1, w2t, pw, aa, nseq, nseg):
    t_n, rows, d = u_t.shape
    width = w1.shape[1]
    pps = S5_PAIRS_PER_SLAB
    slab_rows = S5_SEG_BASE + nseq * S5_SEG_PITCH
    return pl.pallas_call(
        functools.partial(_s5_core_kernel, nseq=nseq, nseg=nseg, strip=128),
        grid=(d // LANES,),
        in_specs=[pl.BlockSpec((t_n, rows, LANES), lambda o: (0, 0, o)),
                  pl.BlockSpec((pps, width, w1.shape[2]), lambda o: (o, 0, 0)),
                  pl.BlockSpec((pps, width, w2t.shape[2]), lambda o: (o, 0, 0)),
                  pl.BlockSpec((pps, S5_SEG_CHUNKS, pw.shape[2]), lambda o: (o, 0, 0)),
                  pl.BlockSpec((pps, 8, aa.shape[2]), lambda o: (o, 0, 0))],
        out_specs=pl.BlockSpec((t_n, rows, LANES), lambda o: (0, 0, o)),
        out_shape=jax.ShapeDtypeStruct((t_n, rows, d), BF16),
        scratch_shapes=[pltpu.VMEM((rows, w1.shape[2]), F32),
                        pltpu.VMEM((4, slab_rows, LANES), F32),
                        pltpu.VMEM((4, slab_rows, LANES), F32),
                        pltpu.VMEM((rows, width), BF16),
                        pltpu.VMEM((rows, 4 * LANES), BF16)],
        compiler_params=_cparams("parallel"),
        name="s5_core",
    )(u_t, w1, w2t, pw, aa)


def _s5_out_kernel(y_ref, h_ref, wglu_ref, wout_ref, g_ref, o_ref):
    g = _gelu_tanh(y_ref[0].astype(F32))
    z = jnp.dot(g.astype(BF16), wglu_ref[...], preferred_element_type=F32)
    g2 = g * _sigmoid(z)
    mix = jnp.dot(g2.astype(BF16), wout_ref[...], preferred_element_type=F32)
    o_ref[...] = h_ref[...] + _rms(mix, g_ref[...])


def _s5_out(y_t, h, w_glu, w_out, gain, rc):
    n, d = h.shape
    chunks = n // S5_CHUNK
    out = pl.pallas_call(
        _s5_out_kernel,
        grid=(chunks // rc, S5_CHUNK),
        in_specs=[pl.BlockSpec((1, rc, d), lambda i, t: (t, i, 0)),
                  pl.BlockSpec((rc, d), lambda i, t: (i, t)),
                  pl.BlockSpec((d, d), lambda i, t: (0, 0)),
                  pl.BlockSpec((d, d), lambda i, t: (0, 0)),
                  pl.BlockSpec((1, d), lambda i, t: (0, 0))],
        out_specs=pl.BlockSpec((rc, d), lambda i, t: (i, t)),
        out_shape=jax.ShapeDtypeStruct((chunks, S5_CHUNK * d), F32),
        compiler_params=_cparams("parallel", "parallel"),
        name="s5_out",
    )(y_t, h.reshape(chunks, S5_CHUNK * d), w_glu, w_out, gain.reshape(1, d))
    return out.reshape(n, d)


def _ffn_kernel(h_ref, gpre_ref, gpost_ref, wg_ref, wu_ref, wd_ref, o_ref, hn_s, acc_s):
    f = pl.program_id(1)

    @pl.when(f == 0)
    def _():
        hn_s[...] = _rms(h_ref[...], gpre_ref[...]).astype(BF16)
        acc_s[...] = jnp.zeros_like(acc_s)

    hn = hn_s[...]
    a = jnp.dot(hn, wg_ref[...], preferred_element_type=F32)
    u = jnp.dot(hn, wu_ref[...], preferred_element_type=F32)
    act = a * _sigmoid(a) * u
    acc_s[...] += jnp.dot(act.astype(BF16), wd_ref[...], preferred_element_type=F32)

    @pl.when(f == pl.num_programs(1) - 1)
    def _():
        o_ref[...] = h_ref[...] + _rms(acc_s[...], gpost_ref[...])


def _ffn(h, gain_pre, gain_post, w_gate, w_up, w_down, tm, tf):
    n, d = h.shape
    d_ff = w_gate.shape[1]
    return pl.pallas_call(
        _ffn_kernel,
        grid=(n // tm, d_ff // tf),
        in_specs=[pl.BlockSpec((tm, d), lambda i, f: (i, 0)),
                  pl.BlockSpec((1, d), lambda i, f: (0, 0)),
                  pl.BlockSpec((1, d), lambda i, f: (0, 0)),
                  pl.BlockSpec((d, tf), lambda i, f: (0, f)),
                  pl.BlockSpec((d, tf), lambda i, f: (0, f)),
                  pl.BlockSpec((tf, d), lambda i, f: (f, 0))],
        out_specs=pl.BlockSpec((tm, d), lambda i, f: (i, 0)),
        out_shape=jax.ShapeDtypeStruct((n, d), F32),
        scratch_shapes=[pltpu.VMEM((tm, d), BF16), pltpu.VMEM((tm, d), F32)],
        compiler_params=_cparams("parallel", "arbitrary"),
        name="ffn",
    )(h, gain_pre.reshape(1, d), gain_post.reshape(1, d), w_gate, w_up, w_down)


def _moe_kernel(h_ref, gpre_ref, gpost_ref, wr_ref, wg_ref, wu_ref, wd_ref, o_ref,
                hn_s, gate_s, sel_s, pos_s, post_s, xc_s, yc_s, xg_s, cnt_s,
                *, n_exp, rt, strip):
    e = pl.program_id(1)
    f = pl.program_id(2)
    tb, d = hn_s.shape
    n_f = pl.num_programs(2)
    n_strips = tb // strip

    def put_counts(r, total):
        lane1 = lax.broadcasted_iota(jnp.int32, total.shape, 1)
        for x in range(n_exp):
            cnt_s[r * n_exp + x] = jnp.sum(jnp.where(lane1 == x, total, 0.0)).astype(jnp.int32)

    @pl.when((e == 0) & (f == 0))
    def _():
        def route(r, total):
            put_counts(r, total)
            rows = pl.ds(pl.multiple_of(r * strip, strip), strip)
            xn = _rms(h_ref[rows, :], gpre_ref[...])
            hn_s[rows, :] = xn.astype(BF16)
            o_ref[rows, :] = jnp.zeros((strip, d), F32)
            logits = jnp.dot(xn, wr_ref[...], precision=lax.Precision.HIGHEST,
                             preferred_element_type=F32)
            lane = lax.broadcasted_iota(jnp.int32, logits.shape, 1)
            neg = jnp.float32(-jnp.inf)
            logits = jnp.where(lane < n_exp, logits, neg)
            m1 = jnp.max(logits, axis=-1, keepdims=True)
            i1 = jnp.min(jnp.where(logits == m1, lane, LANES), axis=-1, keepdims=True)
            rest = jnp.where(lane == i1, neg, logits)
            m2 = jnp.max(rest, axis=-1, keepdims=True)
            i2 = jnp.min(jnp.where(rest == m2, lane, LANES), axis=-1, keepdims=True)
            e2 = jnp.exp(m2 - m1)
            gate_s[rows, :] = (jnp.where(lane == i1, 1.0 / (1.0 + e2), 0.0)
                               + jnp.where(lane == i2, e2 / (1.0 + e2), 0.0))
            sel = ((lane == i1) | (lane == i2)).astype(F32)
            sel_s[rows, :] = sel.astype(BF16)
            return total + jnp.sum(sel, axis=0, keepdims=True)

        total = lax.fori_loop(0, n_strips, route, jnp.zeros((1, LANES), F32))
        put_counts(n_strips, total)

        def rank(r, _):
            r0 = pl.multiple_of(r * strip, strip)
            rows = lax.broadcasted_iota(jnp.int32, (strip, tb), 0) + r0
            cols = lax.broadcasted_iota(jnp.int32, (strip, tb), 1)
            before = (cols < rows).astype(BF16)
            cnt = jnp.dot(before, sel_s[...], preferred_element_type=F32)
            chosen = sel_s[pl.ds(r0, strip), :] > 0
            pos_s[pl.ds(r0, strip), :] = jnp.where(chosen, cnt, -1.0)
            return 0

        lax.fori_loop(0, tb // strip, rank, 0)
        for r in range(tb // strip):
            post_s[:, r * strip:(r + 1) * strip] = pos_s[r * strip:(r + 1) * strip, :].T

    n_tiles = (cnt_s[n_strips * n_exp + e] + rt - 1) // rt
    lane = lax.broadcasted_iota(jnp.int32, (tb, LANES), 1)
    pos_col = jnp.sum(jnp.where(lane == e, pos_s[...], 0.0), axis=-1, keepdims=True)
    gate_col = jnp.sum(jnp.where(lane == e, gate_s[...], 0.0), axis=-1, keepdims=True)

    def tile(i, _):
        r0 = pl.multiple_of(i * rt, rt)

        def holds(r):
            return ((cnt_s[r * n_exp + e] < r0 + rt) & (cnt_s[(r + 1) * n_exp + e] > r0))

        @pl.when(f == 0)
        def _():
            xg_s[...] = jnp.zeros_like(xg_s)
            slot = (lax.broadcasted_iota(jnp.int32, (rt, strip), 0) + r0).astype(F32)
            for r in range(n_strips):
                @pl.when(holds(r))
                def _():
                    cols = slice(r * strip, (r + 1) * strip)
                    pick = (post_s[pl.ds(e, 1), cols] == slot).astype(BF16)
                    xg_s[...] += jnp.dot(pick, hn_s[cols, :], preferred_element_type=F32)
            xc_s[pl.ds(r0, rt), :] = xg_s[...].astype(BF16)

        xc = xc_s[pl.ds(r0, rt), :]
        a = jnp.dot(xc, wg_ref[0], preferred_element_type=F32)
        u = jnp.dot(xc, wu_ref[0], preferred_element_type=F32)
        act = (a * _sigmoid(a) * u).astype(BF16)
        y = jnp.dot(act, wd_ref[0], preferred_element_type=F32)
        prev = jnp.where(f == 0, 0.0, yc_s[pl.ds(r0, rt), :])
        yc_s[pl.ds(r0, rt), :] = prev + y

        @pl.when(f == n_f - 1)
        def _():
            yc = yc_s[pl.ds(r0, rt), :].astype(BF16)
            slot = (lax.broadcasted_iota(jnp.int32, (strip, rt), 1) + r0).astype(F32)
            for r in range(n_strips):
                @pl.when(holds(r))
                def _():
                    rows = slice(r * strip, (r + 1) * strip)
                    put = (pos_col[rows] == slot).astype(BF16)
                    back = jnp.dot(put, yc, preferred_element_type=F32)
                    o_ref[rows, :] += gate_col[rows] * back
        return 0

    lax.fori_loop(0, n_tiles, tile, 0)

    @pl.when((e == n_exp - 1) & (f == n_f - 1))
    def _():
        def finish(r, _):
            rows = pl.ds(pl.multiple_of(r * strip, strip), strip)
            o_ref[rows, :] = h_ref[rows, :] + _rms(o_ref[rows, :], gpost_ref[...])
            return 0

        lax.fori_loop(0, tb // strip, finish, 0)


def _moe(h, gain_pre, gain_post, w_router, w_gate, w_up, w_down, tb, tf, rt):
    n, d = h.shape
    n_exp, _, d_ff = w_gate.shape
    w_r = jnp.pad(w_router, ((0, 0), (0, LANES - n_exp)))
    once = pl.Buffered(1)
    strip = 256
    return pl.pallas_call(
        functools.partial(_moe_kernel, n_exp=n_exp, rt=rt, strip=strip),
        grid=(n // tb, n_exp, d_ff // tf),
        in_specs=[pl.BlockSpec((tb, d), lambda i, e, f: (i, 0), pipeline_mode=once),
                  pl.BlockSpec((1, d), lambda i, e, f: (0, 0)),
                  pl.BlockSpec((1, d), lambda i, e, f: (0, 0)),
                  pl.BlockSpec((d, LANES), lambda i, e, f: (0, 0)),
                  pl.BlockSpec((1, d, tf), lambda i, e, f: (e, 0, f)),
                  pl.BlockSpec((1, d, tf), lambda i, e, f: (e, 0, f)),
                  pl.BlockSpec((1, tf, d), lambda i, e, f: (e, f, 0))],
        out_specs=pl.BlockSpec((tb, d), lambda i, e, f: (i, 0), pipeline_mode=once),
        out_shape=jax.ShapeDtypeStruct((n, d), F32),
        scratch_shapes=[pltpu.VMEM((tb, d), BF16),
                        pltpu.VMEM((tb, LANES), F32),
                        pltpu.VMEM((tb, LANES), BF16),
                        pltpu.VMEM((tb, LANES), F32),
                        pltpu.VMEM((LANES, tb), F32),
                        pltpu.VMEM((tb, d), BF16),
                        pltpu.VMEM((tb, d), F32),
                        pltpu.VMEM((rt, d), F32),
                        pltpu.SMEM(((tb // strip + 1) * n_exp,), jnp.int32)],
        compiler_params=pltpu.CompilerParams(
            dimension_semantics=("parallel", "arbitrary", "arbitrary"),
            vmem_limit_bytes=MOE_VMEM_LIMIT_BYTES),
        name="moe_ffn",
    )(h, gain_pre.reshape(1, d), gain_post.reshape(1, d), w_r, w_gate, w_up, w_down)


def _qkv_kernel(h_ref, g_ref, w_ref, qg_ref, kg_ref, cs_ref, sn_ref, bd_ref,
                q_ref, k_ref, v_ref):
    xn = _rms(h_ref[...], g_ref[...]).astype(BF16)
    qkv = jnp.dot(xn, w_ref[...], preferred_element_type=F32)
    cs = cs_ref[...]
    sn = sn_ref[...]
    bd = bd_ref[...]
    lane = lax.broadcasted_iota(jnp.int32, cs.shape, 1)
    first_half = (lane % HEAD_DIM) < (HEAD_DIM // 2)
    scale = math.log2(math.e) / math.sqrt(HEAD_DIM)

    def norm_rope(x, gain):
        ms = jnp.dot(x * x, bd, precision=lax.Precision.HIGHEST, preferred_element_type=F32)
        y = x * lax.rsqrt(ms + NORM_EPS) * gain
        partner = jnp.where(first_half,
                            pltpu.roll(y, LANES - HEAD_DIM // 2, axis=1),
                            pltpu.roll(y, HEAD_DIM // 2, axis=1))
        return y * cs + partner * sn

    n_q_tiles = N_HEADS * HEAD_DIM // LANES
    for t in range(n_q_tiles):
        y = norm_rope(qkv[:, t * LANES:(t + 1) * LANES], qg_ref[...]) * scale
        q_ref[0, 2 * t] = y[:, 0:HEAD_DIM].astype(BF16)
        q_ref[0, 2 * t + 1] = y[:, HEAD_DIM:LANES].astype(BF16)
    k0 = N_HEADS * HEAD_DIM
    for t in range(N_KV_HEADS * HEAD_DIM // LANES):
        y = norm_rope(qkv[:, k0 + t * LANES:k0 + (t + 1) * LANES], kg_ref[...])
        k_ref[0, 2 * t] = y[:, 0:HEAD_DIM].astype(BF16)
        k_ref[0, 2 * t + 1] = y[:, HEAD_DIM:LANES].astype(BF16)
    v0 = (N_HEADS + N_KV_HEADS) * HEAD_DIM
    ones = jnp.ones((qkv.shape[0], LANES - HEAD_DIM), BF16)
    for j in range(N_KV_HEADS):
        vj = qkv[:, v0 + j * HEAD_DIM:v0 + (j + 1) * HEAD_DIM].astype(BF16)
        v_ref[0, j] = jnp.concatenate([vj, ones], axis=-1)


def _rope_tables(seq):
    axis_dim = HEAD_DIM // 2
    freqs = ROPE_THETA ** (-jnp.arange(0, axis_dim, 2, dtype=F32) / axis_dim)
    rows = seq // GRID_W
    row_ang = jnp.arange(rows, dtype=F32)[:, None] * freqs
    col_ang = jnp.arange(GRID_W, dtype=F32)[:, None] * freqs
    ang = jnp.concatenate([
        jnp.broadcast_to(row_ang[:, None, :], (rows, GRID_W, freqs.shape[0])),
        jnp.broadcast_to(col_ang[None, :, :], (rows, GRID_W, freqs.shape[0]))], axis=-1)
    ang = ang.reshape(seq, HEAD_DIM // 2)
    cos, sin = jnp.cos(ang), jnp.sin(ang)
    cs = jnp.tile(jnp.concatenate([cos, cos], axis=-1), (1, LANES // HEAD_DIM))
    sn = jnp.tile(jnp.concatenate([-sin, sin], axis=-1), (1, LANES // HEAD_DIM))
    return cs, sn


def _qkv(h, gain, w_qkv, q_gain, k_gain, bsz, seq, tm):
    n, d = h.shape
    width = w_qkv.shape[1]
    perm = jnp.concatenate([jnp.arange(0, HEAD_DIM, 2), jnp.arange(1, HEAD_DIM, 2)])
    n_rot = N_HEADS + N_KV_HEADS
    cols = (jnp.arange(n_rot)[:, None] * HEAD_DIM + perm[None, :]).reshape(-1)
    cols = jnp.concatenate([cols, jnp.arange(n_rot * HEAD_DIM, width)])
    w = w_qkv[:, cols].astype(BF16)
    qg = jnp.tile(q_gain[perm], LANES // HEAD_DIM).reshape(1, LANES)
    kg = jnp.tile(k_gain[perm], LANES // HEAD_DIM).reshape(1, LANES)
    cs, sn = _rope_tables(seq)
    blk = jnp.arange(LANES) // HEAD_DIM
    bd = (blk[:, None] == blk[None, :]).astype(F32) / HEAD_DIM
    per_seq = seq // tm
    return pl.pallas_call(
        _qkv_kernel,
        grid=(n // tm,),
        in_specs=[pl.BlockSpec((tm, d), lambda i: (i, 0)),
                  pl.BlockSpec((1, d), lambda i: (0, 0)),
                  pl.BlockSpec((d, width), lambda i: (0, 0)),
                  pl.BlockSpec((1, LANES), lambda i: (0, 0)),
                  pl.BlockSpec((1, LANES), lambda i: (0, 0)),
                  pl.BlockSpec((tm, LANES), lambda i: (i % per_seq, 0)),
                  pl.BlockSpec((tm, LANES), lambda i: (i % per_seq, 0)),
                  pl.BlockSpec((LANES, LANES), lambda i: (0, 0))],
        out_specs=[pl.BlockSpec((1, N_HEADS, tm, HEAD_DIM),
                                lambda i: (i // per_seq, 0, i % per_seq, 0)),
                   pl.BlockSpec((1, N_KV_HEADS, tm, HEAD_DIM),
                                lambda i: (i // per_seq, 0, i % per_seq, 0)),
                   pl.BlockSpec((1, N_KV_HEADS, tm, LANES),
                                lambda i: (i // per_seq, 0, i % per_seq, 0))],
        out_shape=[jax.ShapeDtypeStruct((bsz, N_HEADS, seq, HEAD_DIM), BF16),
                   jax.ShapeDtypeStruct((bsz, N_KV_HEADS, seq, HEAD_DIM), BF16),
                   jax.ShapeDtypeStruct((bsz, N_KV_HEADS, seq, LANES), BF16)],
        compiler_params=_cparams("parallel"),
        name="qkv_rope",
    )(h, gain.reshape(1, d), w, qg, kg, cs, sn, bd)


def _attn_kernel(q_ref, k_ref, v_ref, o_ref, m_s, acc_s, s_buf, p_buf, a_buf,
                 *, tq, tk, rc, unroll):
    seq = k_ref.shape[2]
    chunks_per_head = tq // rc
    n_chunks = Q_PER_KV * chunks_per_head
    n_steps = (seq // tk) * n_chunks

    m_s[...] = jnp.full_like(m_s, -jnp.inf)
    acc_s[...] = jnp.zeros_like(acc_s)

    def where(n):
        c = n % n_chunks
        return (pl.multiple_of((n // n_chunks) * tk, tk), c // chunks_per_head,
                pl.multiple_of((c % chunks_per_head) * rc, rc))

    def scores(n):
        k0, g, r0 = where(n)
        q = q_ref[0, g, pl.ds(r0, rc), :]
        s_buf[...] = lax.dot_general(q, k_ref[0, 0, pl.ds(k0, tk), :], _NT,
                                     preferred_element_type=F32)

    def softmax(n):
        _, g, r0 = where(n)
        s = s_buf[...]
        m_prev = m_s[g, pl.ds(r0, rc), :]
        m_new = jnp.maximum(m_prev, jnp.max(s, axis=-1, keepdims=True))
        a_buf[...] = jnp.exp2(m_prev - m_new)
        for t in range(tk // LANES):
            p_buf[:, t * LANES:(t + 1) * LANES] = jnp.exp2(
                s[:, t * LANES:(t + 1) * LANES] - m_new).astype(BF16)
        m_s[g, pl.ds(r0, rc), :] = m_new

    def values(n):
        k0, g, r0 = where(n)
        pv = jnp.dot(p_buf[...], v_ref[0, 0, pl.ds(k0, tk), :],
                     preferred_element_type=F32)
        acc_s[g, pl.ds(r0, rc), :] = a_buf[...] * acc_s[g, pl.ds(r0, rc), :] + pv

    scores(0)
    softmax(0)
    scores(1)

    def body(n, _):
        values(n)
        softmax(n + 1)
        scores(n + 2)
        return 0

    lax.fori_loop(0, n_steps - 2, body, 0, unroll=unroll)
    values(n_steps - 2)
    softmax(n_steps - 1)
    values(n_steps - 1)

    outs = []
    for g in range(Q_PER_KV):
        acc = acc_s[g]
        o = acc / pltpu.roll(acc, HEAD_DIM, axis=1)
        outs.append(o[:, 0:HEAD_DIM])
    o_ref[...] = jnp.concatenate(outs, axis=-1).astype(o_ref.dtype)


def _attention(q, k, v, tq, tk, rc):
    bsz, _, seq, _ = q.shape
    n_q = seq // tq
    n_steps = (seq // tk) * Q_PER_KV * (tq // rc)
    unroll = min(4, n_steps - 2)
    return pl.pallas_call(
        functools.partial(_attn_kernel, tq=tq, tk=tk, rc=rc, unroll=unroll),
        grid=(bsz, N_KV_HEADS, n_q),
        in_specs=[pl.BlockSpec((1, Q_PER_KV, tq, HEAD_DIM), lambda b, j, i: (b, j, i, 0)),
                  pl.BlockSpec((1, 1, seq, HEAD_DIM), lambda b, j, i: (b, j, 0, 0)),
                  pl.BlockSpec((1, 1, seq, LANES), lambda b, j, i: (b, j, 0, 0))],
        out_specs=pl.BlockSpec((tq, Q_PER_KV * HEAD_DIM), lambda b, j, i: (b * n_q + i, j)),
        out_shape=jax.ShapeDtypeStruct((bsz * seq, N_HEADS * HEAD_DIM), BF16),
        scratch_shapes=[pltpu.VMEM((Q_PER_KV, tq, LANES), F32),
                        pltpu.VMEM((Q_PER_KV, tq, LANES), F32),
                        pltpu.VMEM((rc, tk), F32),
                        pltpu.VMEM((rc, tk), BF16),
                        pltpu.VMEM((rc, LANES), F32)],
        compiler_params=_cparams("parallel", "parallel", "parallel"),
        name="flash_attn",
    )(q, k, v)


def _proj_res_kernel(x_ref, h_ref, w_ref, g_ref, o_ref):
    mix = jnp.dot(x_ref[...], w_ref[...], preferred_element_type=F32)
    o_ref[...] = h_ref[...] + _rms(mix, g_ref[...])


def _proj_res(x, h, w, gain, tm):
    n, d = h.shape
    k = x.shape[1]
    return pl.pallas_call(
        _proj_res_kernel,
        grid=(n // tm,),
        in_specs=[pl.BlockSpec((tm, k), lambda i: (i, 0)),
                  pl.BlockSpec((tm, d), lambda i: (i, 0)),
                  pl.BlockSpec((k, d), lambda i: (0, 0)),
                  pl.BlockSpec((1, d), lambda i: (0, 0))],
        out_specs=pl.BlockSpec((tm, d), lambda i: (i, 0)),
        out_shape=jax.ShapeDtypeStruct((n, d), F32),
        compiler_params=_cparams("parallel"),
        name="proj_res",
    )(x, h, w, gain.reshape(1, d))


def _s5_layer(h, bsz, seq, gains, w_in, lam_re, lam_im, log_step, b_re, b_im, c_re, c_im,
              d_skip, w_glu, w_out):
    seg_tokens = S5_CHUNK * S5_SEG_CHUNKS
    nseg = seq // seg_tokens
    nseq = bsz * nseg
    rc = min(512, h.shape[0] // S5_CHUNK)
    u_t = _norm_matmul(h, gains[0], w_in.astype(BF16), rc=rc)
    w1, w2t, pw_p, aa_p = _s5_prep(lam_re, lam_im, log_step, b_re, b_im, c_re, c_im, d_skip)
    y_t = _s5_core(u_t, w1, w2t, pw_p, aa_p, nseq, nseg)
    return _s5_out(y_t, h, w_glu.astype(BF16), w_out.astype(BF16), gains[1], rc=rc)


def _attn_layer(h, bsz, seq, gains, w_qkv, q_gain, k_gain, w_out):
    q, k, v = _qkv(h, gains[0], w_qkv, q_gain, k_gain, bsz, seq, tm=512)
    o = _attention(q, k, v, tq=2048, tk=512, rc=512)
    return _proj_res(o, h, w_out.astype(BF16), gains[1], tm=512)


def kernel(x, norm_gains, ssm_w_in, ssm_lambda_re, ssm_lambda_im, ssm_log_step, ssm_b_re,
           ssm_b_im, ssm_c_re, ssm_c_im, ssm_d, ssm_w_glu, ssm_w_out, ffn_w_gate, ffn_w_up,
           ffn_w_down, attn_w_qkv, attn_q_gain, attn_k_gain, attn_w_out, moe_w_router,
           moe_w_gate, moe_w_up, moe_w_down):
    bsz, seq, d = x.shape
    depth = norm_gains.shape[0]
    h = x.reshape(bsz * seq, d)
    for i in range(depth):
        j = i // 2
        g = norm_gains[i]
        if i % 2 == 0:
            h = _s5_layer(h, bsz, seq, g, ssm_w_in[j], ssm_lambda_re[j], ssm_lambda_im[j],
                          ssm_log_step[j], ssm_b_re[j], ssm_b_im[j], ssm_c_re[j], ssm_c_im[j],
                          ssm_d[j], ssm_w_glu[j], ssm_w_out[j])
            h = _ffn(h, g[2], g[3], ffn_w_gate[j].astype(BF16), ffn_w_up[j].astype(BF16),
                     ffn_w_down[j].astype(BF16), tm=512, tf=1408)
        else:
            h = _attn_layer(h, bsz, seq, g, attn_w_qkv[j], attn_q_gain[j], attn_k_gain[j],
                            attn_w_out[j])
            h = _moe(h, g[2], g[3], moe_w_router[j], moe_w_gate[j].astype(BF16),
                     moe_w_up[j].astype(BF16), moe_w_down[j].astype(BF16),
                     tb=2048, tf=896, rt=256)
    return h.reshape(bsz, seq, d)
```

```python
import functools
import math

import jax
import jax.numpy as jnp
from jax import lax
from jax.experimental import pallas as pl
from jax.experimental.pallas import tpu as pltpu

F32 = jnp.float32
BF16 = jnp.bfloat16
NORM_EPS = 1e-6
ROPE_THETA = 10000.0
GRID_W = 64
N_HEADS = 16
N_KV_HEADS = 4
HEAD_DIM = 64
Q_PER_KV = N_HEADS // N_KV_HEADS
SSM_GROUP = 16
SSM_STATE = 64
S5_CHUNK = 16
S5_SEG_CHUNKS = 64
TOP_K = 2
LANES = 128
VMEM_LIMIT_BYTES = 56 * 1024 * 1024
MOE_VMEM_LIMIT_BYTES = 60 * 1024 * 1024

_NT = (((1,), (1,)), ((), ()))


def _cparams(*sem):
    return pltpu.CompilerParams(dimension_semantics=sem, vmem_limit_bytes=VMEM_LIMIT_BYTES)


def _rms(x, gain):
    return x * lax.rsqrt(jnp.mean(x * x, axis=-1, keepdims=True) + NORM_EPS) * gain


def _sigmoid(x):
    return 1.0 / (1.0 + jnp.exp(-x))


def _gelu_tanh(x):
    return x * (0.5 * (1.0 + jnp.tanh(math.sqrt(2.0 / math.pi) * (x + 0.044715 * (x * x * x)))))


def _norm_matmul_kernel(x_ref, g_ref, w_ref, o_ref):
    xn = _rms(x_ref[...], g_ref[...]).astype(BF16)
    o_ref[0] = jnp.dot(xn, w_ref[...], preferred_element_type=F32).astype(o_ref.dtype)


def _norm_matmul(x, gain, w, rc):
    n, d = x.shape
    m = w.shape[1]
    chunks = n // S5_CHUNK
    return pl.pallas_call(
        _norm_matmul_kernel,
        grid=(chunks // rc, S5_CHUNK),
        in_specs=[pl.BlockSpec((rc, d), lambda i, t: (i, t)),
                  pl.BlockSpec((1, d), lambda i, t: (0, 0)),
                  pl.BlockSpec((d, m), lambda i, t: (0, 0))],
        out_specs=pl.BlockSpec((1, rc, m), lambda i, t: (t, i, 0)),
        out_shape=jax.ShapeDtypeStruct((S5_CHUNK, chunks, m), BF16),
        compiler_params=_cparams("parallel", "parallel"),
        name="norm_matmul",
    )(x.reshape(chunks, S5_CHUNK * d), gain.reshape(1, d), w)


def _s5_prep_kernel(lr_ref, li_ref, ls_ref, bt_re_ref, bt_im_ref, c_re_ref, c_im_ref, dt_ref,
                    w1_ref, w2t_ref, pw_ref, aa_ref):
    t_n, s_n, p_n = S5_CHUNK, SSM_GROUP, SSM_STATE
    pair = 2 * s_n
    rows = t_n * pair
    hi = lax.Precision.HIGHEST

    rr = lax.broadcasted_iota(jnp.int32, (rows, rows), 0)
    cc = lax.broadcasted_iota(jnp.int32, (rows, rows), 1)
    same_group = ((rr // s_n) % 2) == ((cc // s_n) % 2)
    tvec = lax.broadcasted_iota(jnp.int32, (t_n, 1), 0).astype(F32)
    jvec = lax.broadcasted_iota(jnp.int32, (S5_SEG_CHUNKS, 1), 0).astype(F32)
    row8 = lax.broadcasted_iota(jnp.int32, (8, LANES), 0)
    own_lanes = ((lax.broadcasted_iota(jnp.int32, (rows, LANES), 1) // p_n)
                 == ((lax.broadcasted_iota(jnp.int32, (rows, LANES), 0) // s_n) % 2))

    def lanes_by_group(tab0, tab1):
        lane = lax.broadcasted_iota(jnp.int32, tab0.shape, 1)
        return jnp.where(lane < p_n, tab0, tab1)

    m_total = None
    for d in range(2):
        lsr, lsi, q_re, q_im = [], [], [], []
        for g in range(2):
            lr = lr_ref[0, d, g:g + 1, :]
            li = li_ref[0, d, g:g + 1, :]
            step = jnp.exp(ls_ref[0, d, g:g + 1, :])
            lsr.append(lr * step)
            lsi.append(li * step)
            mag = jnp.exp(lsr[g])
            a_re, a_im = mag * jnp.cos(lsi[g]), mag * jnp.sin(lsi[g])
            nr, ni = a_re - 1.0, a_im
            den = lr * lr + li * li
            q_re.append((nr * lr + ni * li) / den)
            q_im.append((ni * lr - nr * li) / den)

        def cpow(k, g):
            mag = jnp.exp(lsr[g] * k)
            ang = lsi[g] * k
            return mag * jnp.cos(ang), mag * jnp.sin(ang)

        def table(k):
            t0, t1 = cpow(k, 0), cpow(k, 1)
            return tuple(
                jnp.concatenate([jnp.broadcast_to(tg[part][t:t + 1, :], (s_n, LANES))
                                 for t in range(t_n) for tg in (t0, t1)], axis=0)
                for part in range(2))

        br, bi = bt_re_ref[0, d], bt_im_ref[0, d]
        qr = jnp.concatenate([jnp.broadcast_to(q_re[g], (s_n, LANES)) for g in range(2)], axis=0)
        qi = jnp.concatenate([jnp.broadcast_to(q_im[g], (s_n, LANES)) for g in range(2)], axis=0)
        bb_re = qr * br - qi * bi
        bb_im = qr * bi + qi * br
        cr, ci = c_re_ref[0, d], c_im_ref[0, d]

        def outer(x_re, x_im, k):
            pe_re, pe_im = table(k)
            xe_re = jnp.concatenate([x_re] * t_n, axis=0)
            xe_im = jnp.concatenate([x_im] * t_n, axis=0)
            return xe_re * pe_re - xe_im * pe_im, xe_re * pe_im + xe_im * pe_re

        if d == 0:
            l_re, l_im = outer(bb_re, bb_im, -tvec)
            rt_re, rt_im = outer(cr, ci, tvec)
            ws_re, ws_im = outer(bb_re, bb_im, (t_n - 1.0) - tvec)
            wy_re, wy_im = outer(cr, ci, tvec + 1.0)
            mask = (rr // pair) <= (cc // pair)
            kseg = t_n * jvec
        else:
            l_re, l_im = outer(bb_re, bb_im, tvec)
            rt_re, rt_im = outer(cr, ci, -tvec)
            ws_re, ws_im = l_re, l_im
            wy_re, wy_im = outer(cr, ci, t_n - tvec)
            mask = (rr // pair) >= (cc // pair)
            kseg = t_n * ((S5_SEG_CHUNKS - 1.0) - jvec)
        kern = (lax.dot_general(l_re[:, 0:p_n], rt_re[:, 0:p_n], _NT, precision=hi,
                                preferred_element_type=F32)
                - lax.dot_general(l_im[:, 0:p_n], rt_im[:, 0:p_n], _NT, precision=hi,
                                  preferred_element_type=F32))
        kern = jnp.where(mask & same_group, kern, 0.0)
        m_total = kern if m_total is None else m_total + kern

        ycols = rows
        for k2, val in ((2 * d, ws_re), (2 * d + 1, ws_im)):
            w1_ref[0, :, ycols + k2 * LANES:ycols + (k2 + 1) * LANES] = jnp.where(
                own_lanes, val, 0.0).astype(w1_ref.dtype)
        for k2, val in ((2 * d, wy_re), (2 * d + 1, -wy_im)):
            w2t_ref[0, :, k2 * LANES:(k2 + 1) * LANES] = jnp.where(
                own_lanes, val, 0.0).astype(w2t_ref.dtype)
        p0, p1 = cpow(kseg, 0), cpow(kseg, 1)
        pw_ref[0, :, 2 * d * LANES:(2 * d + 1) * LANES] = lanes_by_group(p0[0], p1[0])
        pw_ref[0, :, (2 * d + 1) * LANES:(2 * d + 2) * LANES] = lanes_by_group(p0[1], p1[1])
        one = jnp.ones((1, 1), F32)
        c0, c1 = cpow(float(t_n) * one, 0), cpow(float(t_n) * one, 1)
        s0, s1 = (cpow(float(t_n * S5_SEG_CHUNKS) * one, 0),
                  cpow(float(t_n * S5_SEG_CHUNKS) * one, 1))
        for part in range(2):
            chunk_pow = lanes_by_group(c0[part], c1[part])
            seg_pow = lanes_by_group(s0[part], s1[part])
            aa_ref[0, :, (2 * d + part) * LANES:(2 * d + part + 1) * LANES] = jnp.where(
                row8 == 0, chunk_pow, jnp.where(row8 == 1, seg_pow, 0.0))

    w1_ref[0, :, 0:rows] = (m_total + jnp.where(rr == cc, dt_ref[0], 0.0)).astype(w1_ref.dtype)


def _s5_prep(lam_re, lam_im, log_step, b_re, b_im, c_re, c_im, d_skip):
    g_n = lam_re.shape[1]
    pairs = g_n // 2
    pair = 2 * SSM_GROUP
    rows = S5_CHUNK * pair

    def states(x):
        x = jnp.transpose(x.reshape(2, pairs, 2, SSM_STATE), (1, 0, 2, 3))
        return jnp.tile(x, (1, 1, 1, LANES // SSM_STATE))

    def per_channel(x):
        x = jnp.transpose(x.reshape(2, pairs, pair, SSM_STATE), (1, 0, 2, 3))
        return jnp.tile(x, (1, 1, 1, LANES // SSM_STATE))

    ls = jnp.transpose(log_step.reshape(2, pairs, 2), (1, 0, 2))[..., None]
    bt_re = per_channel(jnp.swapaxes(b_re, 2, 3))
    bt_im = per_channel(jnp.swapaxes(b_im, 2, 3))
    dt = jnp.tile(d_skip.reshape(pairs, 1, pair), (1, 1, S5_CHUNK))

    def spec(*blk):
        return pl.BlockSpec((1,) + blk, lambda g: (g,) + (0,) * len(blk))

    return pl.pallas_call(
        _s5_prep_kernel,
        grid=(pairs,),
        in_specs=[spec(2, 2, LANES), spec(2, 2, LANES), spec(2, 2, 1),
                  spec(2, pair, LANES), spec(2, pair, LANES),
                  spec(2, pair, LANES), spec(2, pair, LANES), spec(1, rows)],
        out_specs=[spec(rows, rows + 4 * LANES), spec(rows, 4 * LANES),
                   spec(S5_SEG_CHUNKS, 4 * LANES), spec(8, 4 * LANES)],
        out_shape=[jax.ShapeDtypeStruct((pairs, rows, rows + 4 * LANES), BF16),
                   jax.ShapeDtypeStruct((pairs, rows, 4 * LANES), BF16),
                   jax.ShapeDtypeStruct((pairs, S5_SEG_CHUNKS, 4 * LANES), F32),
                   jax.ShapeDtypeStruct((pairs, 8, 4 * LANES), F32)],
        compiler_params=_cparams("parallel"),
        name="s5_prep",
    )(states(lam_re), states(lam_im), ls, bt_re, bt_im, per_channel(c_re), per_channel(c_im), dt)


S5_PAIR_LANES = 2 * SSM_GROUP
S5_PAIRS_PER_SLAB = LANES // S5_PAIR_LANES
S5_SEG_PITCH = S5_SEG_CHUNKS + 8
S5_SEG_BASE = 8


def _s5_core_kernel(x_ref, w1_ref, w2t_ref, pw_ref, aa_ref, y_ref, z_s, zs_s, x_s, u_s, xb_s,
                    *, nseq, nseg, strip):
    n_chunks = S5_SEG_CHUNKS
    rows = n_chunks * nseq
    ycols = u_s.shape[1]
    per_tile = LANES // S5_PAIR_LANES
    lane_grp = lax.broadcasted_iota(jnp.int32, (strip, LANES), 1) // S5_PAIR_LANES
    zero = jnp.zeros((nseq, LANES), F32)

    def seg_rows(j):
        return pl.ds(S5_SEG_BASE + j, nseq, stride=S5_SEG_PITCH)

    for q in range(S5_PAIRS_PER_SLAB):
        def gather(r, _, q=q):
            rws = pl.ds(pl.multiple_of(r * strip, strip), strip)
            for j in range(ycols // LANES):
                acc = None
                for i in range(per_tile):
                    xt = x_ref[per_tile * j + i, rws, :].astype(F32)
                    shift = (S5_PAIR_LANES * (i - q)) % LANES
                    if shift:
                        xt = pltpu.roll(xt, shift, axis=1)
                    acc = xt if acc is None else jnp.where(lane_grp == i, xt, acc)
                u_s[rws, j * LANES:(j + 1) * LANES] = acc.astype(BF16)
            return 0

        lax.fori_loop(0, rows // strip, gather, 0)
        z_s[...] = jnp.dot(u_s[...], w1_ref[q], preferred_element_type=F32)

        for b in range(nseq):
            dst = slice(S5_SEG_BASE + b * S5_SEG_PITCH, S5_SEG_BASE + b * S5_SEG_PITCH + n_chunks)
            for k in range(4):
                zs_s[k, dst, :] = z_s[b * n_chunks:(b + 1) * n_chunks,
                                      ycols + k * LANES:ycols + (k + 1) * LANES]
        x_s[0, seg_rows(0), :] = zero
        x_s[1, seg_rows(0), :] = zero
        x_s[2, seg_rows(n_chunks - 1), :] = zero
        x_s[3, seg_rows(n_chunks - 1), :] = zero

        aa = aa_ref[q]
        a_fr, a_fi = aa[0:1, 0:LANES], aa[0:1, LANES:2 * LANES]
        a_br, a_bi = aa[0:1, 2 * LANES:3 * LANES], aa[0:1, 3 * LANES:4 * LANES]
        g_fr, g_fi = aa[1:2, 0:LANES], aa[1:2, LANES:2 * LANES]
        g_br, g_bi = aa[1:2, 2 * LANES:3 * LANES], aa[1:2, 3 * LANES:4 * LANES]

        def fwd(j, carry):
            xr, xi = carry
            nr = a_fr * xr - a_fi * xi + zs_s[0, seg_rows(j), :]
            ni = a_fr * xi + a_fi * xr + zs_s[1, seg_rows(j), :]
            x_s[0, seg_rows(j + 1), :] = nr
            x_s[1, seg_rows(j + 1), :] = ni
            return nr, ni

        def bwd(i, carry):
            xr, xi = carry
            j = n_chunks - 1 - i
            nr = a_br * xr - a_bi * xi + zs_s[2, seg_rows(j), :]
            ni = a_br * xi + a_bi * xr + zs_s[3, seg_rows(j), :]
            x_s[2, seg_rows(j - 1), :] = nr
            x_s[3, seg_rows(j - 1), :] = ni
            return nr, ni

        ef_r, ef_i = lax.fori_loop(0, n_chunks, fwd, (zero, zero))
        eb_r, eb_i = lax.fori_loop(0, n_chunks, bwd, (zero, zero))

        def seg_carries(e_r, e_i, g_r, g_i, reverse):
            out_r = [None] * nseq
            out_i = [None] * nseq
            for b in range(nseq // nseg):
                c_r = jnp.zeros((1, LANES), F32)
                c_i = jnp.zeros((1, LANES), F32)
                order = range(nseg - 1, -1, -1) if reverse else range(nseg)
                for s in order:
                    r = b * nseg + s
                    out_r[r], out_i[r] = c_r, c_i
                    n_r = g_r * c_r - g_i * c_i + e_r[r:r + 1, :]
                    n_i = g_r * c_i + g_i * c_r + e_i[r:r + 1, :]
                    c_r, c_i = n_r, n_i
            return out_r, out_i

        cf_r, cf_i = seg_carries(ef_r, ef_i, g_fr, g_fi, False)
        cb_r, cb_i = seg_carries(eb_r, eb_i, g_br, g_bi, True)

        p = pw_ref[q]
        p_fr, p_fi = p[:, 0:LANES], p[:, LANES:2 * LANES]
        p_br, p_bi = p[:, 2 * LANES:3 * LANES], p[:, 3 * LANES:4 * LANES]
        for b in range(nseq):
            src = slice(S5_SEG_BASE + b * S5_SEG_PITCH, S5_SEG_BASE + b * S5_SEG_PITCH + n_chunks)
            dst = slice(b * n_chunks, (b + 1) * n_chunks)
            xb_s[dst, 0:LANES] = (x_s[0, src, :] + (p_fr * cf_r[b] - p_fi * cf_i[b])).astype(BF16)
            xb_s[dst, LANES:2 * LANES] = (
                x_s[1, src, :] + (p_fr * cf_i[b] + p_fi * cf_r[b])).astype(BF16)
            xb_s[dst, 2 * LANES:3 * LANES] = (
                x_s[2, src, :] + (p_br * cb_r[b] - p_bi * cb_i[b])).astype(BF16)
            xb_s[dst, 3 * LANES:4 * LANES] = (
                x_s[3, src, :] + (p_br * cb_i[b] + p_bi * cb_r[b])).astype(BF16)

        z_s[:, 0:ycols] += lax.dot_general(xb_s[...], w2t_ref[q], _NT,
                                           preferred_element_type=F32)

        def scatter(r, _, q=q):
            rws = pl.ds(pl.multiple_of(r * strip, strip), strip)
            for j in range(ycols // LANES):
                yq = z_s[rws, j * LANES:(j + 1) * LANES]
                for i in range(per_tile):
                    shift = (S5_PAIR_LANES * (q - i)) % LANES
                    yt = pltpu.roll(yq, shift, axis=1) if shift else yq
                    lanes_q = slice(q * S5_PAIR_LANES, (q + 1) * S5_PAIR_LANES)
                    y_ref[per_tile * j + i, rws, lanes_q] = yt[:, lanes_q].astype(y_ref.dtype)
            return 0

        lax.fori_loop(0, rows // strip, scatter, 0)


def _s5_core(u_t, w1, w2t, pw, aa, nseq, nseg):
    t_n, rows, d = u_t.shape
    width = w1.shape[1]
    pps = S5_PAIRS_PER_SLAB
    slab_rows = S5_SEG_BASE + nseq * S5_SEG_PITCH
    return pl.pallas_call(
        functools.partial(_s5_core_kernel, nseq=nseq, nseg=nseg, strip=128),
        grid=(d // LANES,),
        in_specs=[pl.BlockSpec((t_n, rows, LANES), lambda o: (0, 0, o)),
                  pl.BlockSpec((pps, width, w1.shape[2]), lambda o: (o, 0, 0)),
                  pl.BlockSpec((pps, width, w2t.shape[2]), lambda o: (o, 0, 0)),
                  pl.BlockSpec((pps, S5_SEG_CHUNKS, pw.shape[2]), lambda o: (o, 0, 0)),
                  pl.BlockSpec((pps, 8, aa.shape[2]), lambda o: (o, 0, 0))],
        out_specs=pl.BlockSpec((t_n, rows, LANES), lambda o: (0, 0, o)),
        out_shape=jax.ShapeDtypeStruct((t_n, rows, d), BF16),
        scratch_shapes=[pltpu.VMEM((rows, w1.shape[2]), F32),
                        pltpu.VMEM((4, slab_rows, LANES), F32),
                        pltpu.VMEM((4, slab_rows, LANES), F32),
                        pltpu.VMEM((rows, width), BF16),
                        pltpu.VMEM((rows, 4 * LANES), BF16)],
        compiler_params=_cparams("parallel"),
        name="s5_core",
    )(u_t, w1, w2t, pw, aa)


def _s5_out_kernel(y_ref, h_ref, wglu_ref, wout_ref, g_ref, o_ref):
    g = _gelu_tanh(y_ref[0].astype(F32))
    z = jnp.dot(g.astype(BF16), wglu_ref[...], preferred_element_type=F32)
    g2 = g * _sigmoid(z)
    mix = jnp.dot(g2.astype(BF16), wout_ref[...], preferred_element_type=F32)
    o_ref[...] = h_ref[...] + _rms(mix, g_ref[...])


def _s5_out(y_t, h, w_glu, w_out, gain, rc):
    n, d = h.shape
    chunks = n // S5_CHUNK
    out = pl.pallas_call(
        _s5_out_kernel,
        grid=(chunks // rc, S5_CHUNK),
        in_specs=[pl.BlockSpec((1, rc, d), lambda i, t: (t, i, 0)),
                  pl.BlockSpec((rc, d), lambda i, t: (i, t)),
                  pl.BlockSpec((d, d), lambda i, t: (0, 0)),
                  pl.BlockSpec((d, d), lambda i, t: (0, 0)),
                  pl.BlockSpec((1, d), lambda i, t: (0, 0))],
        out_specs=pl.BlockSpec((rc, d), lambda i, t: (i, t)),
        out_shape=jax.ShapeDtypeStruct((chunks, S5_CHUNK * d), F32),
        compiler_params=_cparams("parallel", "parallel"),
        name="s5_out",
    )(y_t, h.reshape(chunks, S5_CHUNK * d), w_glu, w_out, gain.reshape(1, d))
    return out.reshape(n, d)


def _ffn_kernel(h_ref, gpre_ref, gpost_ref, wg_ref, wu_ref, wd_ref, o_ref, hn_s, acc_s):
    f = pl.program_id(1)

    @pl.when(f == 0)
    def _():
        hn_s[...] = _rms(h_ref[...], gpre_ref[...]).astype(BF16)
        acc_s[...] = jnp.zeros_like(acc_s)

    hn = hn_s[...]
    a = jnp.dot(hn, wg_ref[...], preferred_element_type=F32)
    u = jnp.dot(hn, wu_ref[...], preferred_element_type=F32)
    act = a * _sigmoid(a) * u
    acc_s[...] += jnp.dot(act.astype(BF16), wd_ref[...], preferred_element_type=F32)

    @pl.when(f == pl.num_programs(1) - 1)
    def _():
        o_ref[...] = h_ref[...] + _rms(acc_s[...], gpost_ref[...])


def _ffn(h, gain_pre, gain_post, w_gate, w_up, w_down, tm, tf):
    n, d = h.shape
    d_ff = w_gate.shape[1]
    return pl.pallas_call(
        _ffn_kernel,
        grid=(n // tm, d_ff // tf),
        in_specs=[pl.BlockSpec((tm, d), lambda i, f: (i, 0)),
                  pl.BlockSpec((1, d), lambda i, f: (0, 0)),
                  pl.BlockSpec((1, d), lambda i, f: (0, 0)),
                  pl.BlockSpec((d, tf), lambda i, f: (0, f)),
                  pl.BlockSpec((d, tf), lambda i, f: (0, f)),
                  pl.BlockSpec((tf, d), lambda i, f: (f, 0))],
        out_specs=pl.BlockSpec((tm, d), lambda i, f: (i, 0)),
        out_shape=jax.ShapeDtypeStruct((n, d), F32),
        scratch_shapes=[pltpu.VMEM((tm, d), BF16), pltpu.VMEM((tm, d), F32)],
        compiler_params=_cparams("parallel", "arbitrary"),
        name="ffn",
    )(h, gain_pre.reshape(1, d), gain_post.reshape(1, d), w_gate, w_up, w_down)


def _moe_kernel(h_ref, gpre_ref, gpost_ref, wr_ref, wg_ref, wu_ref, wd_ref, o_ref,
                hn_s, gate_s, sel_s, pos_s, post_s, xc_s, yc_s, xg_s, col_s, cnt_s,
                *, n_exp, rt, strip):
    e = pl.program_id(1)
    f = pl.program_id(2)
    tb, d = hn_s.shape
    n_f = pl.num_programs(2)
    n_strips = tb // strip

    def put_counts(r, total):
        lane1 = lax.broadcasted_iota(jnp.int32, total.shape, 1)
        for x in range(n_exp):
            cnt_s[r * n_exp + x] = jnp.sum(jnp.where(lane1 == x, total, 0.0)).astype(jnp.int32)

    @pl.when((e == 0) & (f == 0))
    def _():
        def route(r, total):
            put_counts(r, total)
            rows = pl.ds(pl.multiple_of(r * strip, strip), strip)
            xn = _rms(h_ref[rows, :], gpre_ref[...])
            hn_s[rows, :] = xn.astype(BF16)
            o_ref[rows, :] = jnp.zeros((strip, d), F32)
            logits = jnp.dot(xn, wr_ref[...], precision=lax.Precision.HIGHEST,
                             preferred_element_type=F32)
            lane = lax.broadcasted_iota(jnp.int32, logits.shape, 1)
            neg = jnp.float32(-jnp.inf)
            logits = jnp.where(lane < n_exp, logits, neg)
            m1 = jnp.max(logits, axis=-1, keepdims=True)
            i1 = jnp.min(jnp.where(logits == m1, lane, LANES), axis=-1, keepdims=True)
            rest = jnp.where(lane == i1, neg, logits)
            m2 = jnp.max(rest, axis=-1, keepdims=True)
            i2 = jnp.min(jnp.where(rest == m2, lane, LANES), axis=-1, keepdims=True)
            e2 = jnp.exp(m2 - m1)
            gate_s[rows, :] = (jnp.where(lane == i1, 1.0 / (1.0 + e2), 0.0)
                               + jnp.where(lane == i2, e2 / (1.0 + e2), 0.0))
            sel = ((lane == i1) | (lane == i2)).astype(F32)
            sel_s[rows, :] = sel.astype(BF16)
            return total + jnp.sum(sel, axis=0, keepdims=True)

        total = lax.fori_loop(0, n_strips, route, jnp.zeros((1, LANES), F32))
        put_counts(n_strips, total)

        def rank(r, _):
            r0 = pl.multiple_of(r * strip, strip)
            rows = lax.broadcasted_iota(jnp.int32, (strip, tb), 0) + r0
            cols = lax.broadcasted_iota(jnp.int32, (strip, tb), 1)
            before = (cols < rows).astype(BF16)
            cnt = jnp.dot(before, sel_s[...], preferred_element_type=F32)
            chosen = sel_s[pl.ds(r0, strip), :] > 0
            pos_s[pl.ds(r0, strip), :] = jnp.where(chosen, cnt, -1.0)
            return 0

        lax.fori_loop(0, tb // strip, rank, 0)
        for r in range(tb // strip):
            post_s[:, r * strip:(r + 1) * strip] = pos_s[r * strip:(r + 1) * strip, :].T

    n_tiles = (cnt_s[n_strips * n_exp + e] + rt - 1) // rt

    @pl.when(f == n_f - 1)
    def _():
        lane = lax.broadcasted_iota(jnp.int32, (tb, LANES), 1)
        col_s[0] = jnp.sum(jnp.where(lane == e, pos_s[...], 0.0), axis=-1, keepdims=True)
        col_s[1] = jnp.sum(jnp.where(lane == e, gate_s[...], 0.0), axis=-1, keepdims=True)

    def tile(i, _):
        r0 = pl.multiple_of(i * rt, rt)

        def holds(r):
            return ((cnt_s[r * n_exp + e] < r0 + rt) & (cnt_s[(r + 1) * n_exp + e] > r0))

        @pl.when(f == 0)
        def _():
            xg_s[...] = jnp.zeros_like(xg_s)
            slot = (lax.broadcasted_iota(jnp.int32, (rt, strip), 0) + r0).astype(F32)
            for r in range(n_strips):
                @pl.when(holds(r))
                def _():
                    cols = slice(r * strip, (r + 1) * strip)
                    pick = (post_s[pl.ds(e, 1), cols] == slot).astype(BF16)
                    xg_s[...] += jnp.dot(pick, hn_s[cols, :], preferred_element_type=F32)
            xc_s[pl.ds(r0, rt), :] = xg_s[...].astype(BF16)

        xc = xc_s[pl.ds(r0, rt), :]
        a = jnp.dot(xc, wg_ref[0], preferred_element_type=F32)
        u = jnp.dot(xc, wu_ref[0], preferred_element_type=F32)
        act = (a * _sigmoid(a) * u).astype(BF16)
        y = jnp.dot(act, wd_ref[0], preferred_element_type=F32)
        prev = jnp.where(f == 0, 0.0, yc_s[pl.ds(r0, rt), :])
        yc_s[pl.ds(r0, rt), :] = prev + y

        @pl.when(f == n_f - 1)
        def _():
            yc = yc_s[pl.ds(r0, rt), :].astype(BF16)
            slot = (lax.broadcasted_iota(jnp.int32, (strip, rt), 1) + r0).astype(F32)
            for r in range(n_strips):
                @pl.when(holds(r))
                def _():
                    rows = slice(r * strip, (r + 1) * strip)
                    put = (col_s[0, rows, :] == slot).astype(BF16)
                    back = jnp.dot(put, yc, preferred_element_type=F32)
                    o_ref[rows, :] += col_s[1, rows, :] * back
        return 0

    lax.fori_loop(0, n_tiles, tile, 0)

    @pl.when((e == n_exp - 1) & (f == n_f - 1))
    def _():
        def finish(r, _):
            rows = pl.ds(pl.multiple_of(r * strip, strip), strip)
            o_ref[rows, :] = h_ref[rows, :] + _rms(o_ref[rows, :], gpost_ref[...])
            return 0

        lax.fori_loop(0, tb // strip, finish, 0)


def _moe(h, gain_pre, gain_post, w_router, w_gate, w_up, w_down, tb, tf, rt):
    n, d = h.shape
    n_exp, _, d_ff = w_gate.shape
    w_r = jnp.pad(w_router, ((0, 0), (0, LANES - n_exp)))
    once = pl.Buffered(1)
    strip = 256
    return pl.pallas_call(
        functools.partial(_moe_kernel, n_exp=n_exp, rt=rt, strip=strip),
        grid=(n // tb, n_exp, d_ff // tf),
        in_specs=[pl.BlockSpec((tb, d), lambda i, e, f: (i, 0), pipeline_mode=once),
                  pl.BlockSpec((1, d), lambda i, e, f: (0, 0)),
                  pl.BlockSpec((1, d), lambda i, e, f: (0, 0)),
                  pl.BlockSpec((d, LANES), lambda i, e, f: (0, 0)),
                  pl.BlockSpec((1, d, tf), lambda i, e, f: (e, 0, f)),
                  pl.BlockSpec((1, d, tf), lambda i, e, f: (e, 0, f)),
                  pl.BlockSpec((1, tf, d), lambda i, e, f: (e, f, 0))],
        out_specs=pl.BlockSpec((tb, d), lambda i, e, f: (i, 0), pipeline_mode=once),
        out_shape=jax.ShapeDtypeStruct((n, d), F32),
        scratch_shapes=[pltpu.VMEM((tb, d), BF16),
                        pltpu.VMEM((tb, LANES), F32),
                        pltpu.VMEM((tb, LANES), BF16),
                        pltpu.VMEM((tb, LANES), F32),
                        pltpu.VMEM((LANES, tb), F32),
                        pltpu.VMEM((tb, d), BF16),
                        pltpu.VMEM((tb, d), F32),
                        pltpu.VMEM((rt, d), F32),
                        pltpu.VMEM((2, tb, 1), F32),
                        pltpu.SMEM(((tb // strip + 1) * n_exp,), jnp.int32)],
        compiler_params=pltpu.CompilerParams(
            dimension_semantics=("parallel", "arbitrary", "arbitrary"),
            vmem_limit_bytes=MOE_VMEM_LIMIT_BYTES),
        name="moe_ffn",
    )(h, gain_pre.reshape(1, d), gain_post.reshape(1, d), w_r, w_gate, w_up, w_down)


def _qkv_kernel(h_ref, g_ref, w_ref, qg_ref, kg_ref, cs_ref, sn_ref, bd_ref,
                q_ref, k_ref, v_ref):
    xn = _rms(h_ref[...], g_ref[...]).astype(BF16)
    qkv = jnp.dot(xn, w_ref[...], preferred_element_type=F32)
    cs = cs_ref[...]
    sn = sn_ref[...]
    bd = bd_ref[...]
    lane = lax.broadcasted_iota(jnp.int32, cs.shape, 1)
    first_half = (lane % HEAD_DIM) < (HEAD_DIM // 2)
    scale = math.log2(math.e) / math.sqrt(HEAD_DIM)

    def norm_rope(x, gain):
        ms = jnp.dot(x * x, bd, precision=lax.Precision.HIGHEST, preferred_element_type=F32)
        y = x * lax.rsqrt(ms + NORM_EPS) * gain
        partner = jnp.where(first_half,
                            pltpu.roll(y, LANES - HEAD_DIM // 2, axis=1),
                            pltpu.roll(y, HEAD_DIM // 2, axis=1))
        return y * cs + partner * sn

    n_q_tiles = N_HEADS * HEAD_DIM // LANES
    for t in range(n_q_tiles):
        y = norm_rope(qkv[:, t * LANES:(t + 1) * LANES], qg_ref[...]) * scale
        q_ref[0, 2 * t] = y[:, 0:HEAD_DIM].astype(BF16)
        q_ref[0, 2 * t + 1] = y[:, HEAD_DIM:LANES].astype(BF16)
    k0 = N_HEADS * HEAD_DIM
    for t in range(N_KV_HEADS * HEAD_DIM // LANES):
        y = norm_rope(qkv[:, k0 + t * LANES:k0 + (t + 1) * LANES], kg_ref[...])
        k_ref[0, 2 * t] = y[:, 0:HEAD_DIM].astype(BF16)
        k_ref[0, 2 * t + 1] = y[:, HEAD_DIM:LANES].astype(BF16)
    v0 = (N_HEADS + N_KV_HEADS) * HEAD_DIM
    ones = jnp.ones((qkv.shape[0], LANES - HEAD_DIM), BF16)
    for j in range(N_KV_HEADS):
        vj = qkv[:, v0 + j * HEAD_DIM:v0 + (j + 1) * HEAD_DIM].astype(BF16)
        v_ref[0, j] = jnp.concatenate([vj, ones], axis=-1)


def _rope_tables(seq):
    axis_dim = HEAD_DIM // 2
    freqs = ROPE_THETA ** (-jnp.arange(0, axis_dim, 2, dtype=F32) / axis_dim)
    rows = seq // GRID_W
    row_ang = jnp.arange(rows, dtype=F32)[:, None] * freqs
    col_ang = jnp.arange(GRID_W, dtype=F32)[:, None] * freqs
    ang = jnp.concatenate([
        jnp.broadcast_to(row_ang[:, None, :], (rows, GRID_W, freqs.shape[0])),
        jnp.broadcast_to(col_ang[None, :, :], (rows, GRID_W, freqs.shape[0]))], axis=-1)
    ang = ang.reshape(seq, HEAD_DIM // 2)
    cos, sin = jnp.cos(ang), jnp.sin(ang)
    cs = jnp.tile(jnp.concatenate([cos, cos], axis=-1), (1, LANES // HEAD_DIM))
    sn = jnp.tile(jnp.concatenate([-sin, sin], axis=-1), (1, LANES // HEAD_DIM))
    return cs, sn


def _qkv(h, gain, w_qkv, q_gain, k_gain, bsz, seq, tm):
    n, d = h.shape
    width = w_qkv.shape[1]
    perm = jnp.concatenate([jnp.arange(0, HEAD_DIM, 2), jnp.arange(1, HEAD_DIM, 2)])
    n_rot = N_HEADS + N_KV_HEADS
    cols = (jnp.arange(n_rot)[:, None] * HEAD_DIM + perm[None, :]).reshape(-1)
    cols = jnp.concatenate([cols, jnp.arange(n_rot * HEAD_DIM, width)])
    w = w_qkv[:, cols].astype(BF16)
    qg = jnp.tile(q_gain[perm], LANES // HEAD_DIM).reshape(1, LANES)
    kg = jnp.tile(k_gain[perm], LANES // HEAD_DIM).reshape(1, LANES)
    cs, sn = _rope_tables(seq)
    blk = jnp.arange(LANES) // HEAD_DIM
    bd = (blk[:, None] == blk[None, :]).astype(F32) / HEAD_DIM
    per_seq = seq // tm
    return pl.pallas_call(
        _qkv_kernel,
        grid=(n // tm,),
        in_specs=[pl.BlockSpec((tm, d), lambda i: (i, 0)),
                  pl.BlockSpec((1, d), lambda i: (0, 0)),
                  pl.BlockSpec((d, width), lambda i: (0, 0)),
                  pl.BlockSpec((1, LANES), lambda i: (0, 0)),
                  pl.BlockSpec((1, LANES), lambda i: (0, 0)),
                  pl.BlockSpec((tm, LANES), lambda i: (i % per_seq, 0)),
                  pl.BlockSpec((tm, LANES), lambda i: (i % per_seq, 0)),
                  pl.BlockSpec((LANES, LANES), lambda i: (0, 0))],
        out_specs=[pl.BlockSpec((1, N_HEADS, tm, HEAD_DIM),
                                lambda i: (i // per_seq, 0, i % per_seq, 0)),
                   pl.BlockSpec((1, N_KV_HEADS, tm, HEAD_DIM),
                                lambda i: (i // per_seq, 0, i % per_seq, 0)),
                   pl.BlockSpec((1, N_KV_HEADS, tm, LANES),
                                lambda i: (i // per_seq, 0, i % per_seq, 0))],
        out_shape=[jax.ShapeDtypeStruct((bsz, N_HEADS, seq, HEAD_DIM), BF16),
                   jax.ShapeDtypeStruct((bsz, N_KV_HEADS, seq, HEAD_DIM), BF16),
                   jax.ShapeDtypeStruct((bsz, N_KV_HEADS, seq, LANES), BF16)],
        compiler_params=_cparams("parallel"),
        name="qkv_rope",
    )(h, gain.reshape(1, d), w, qg, kg, cs, sn, bd)


def _attn_kernel(q_ref, k_ref, v_ref, o_ref, m_s, acc_s, s_buf, p_buf, a_buf,
                 *, tq, tk, rc, unroll):
    seq = k_ref.shape[2]
    chunks_per_head = tq // rc
    n_chunks = Q_PER_KV * chunks_per_head
    n_steps = (seq // tk) * n_chunks

    m_s[...] = jnp.full_like(m_s, -jnp.inf)
    acc_s[...] = jnp.zeros_like(acc_s)

    def where(n):
        c = n % n_chunks
        return (pl.multiple_of((n // n_chunks) * tk, tk), c // chunks_per_head,
                pl.multiple_of((c % chunks_per_head) * rc, rc))

    def scores(n):
        k0, g, r0 = where(n)
        q = q_ref[0, g, pl.ds(r0, rc), :]
        s_buf[...] = lax.dot_general(q, k_ref[0, 0, pl.ds(k0, tk), :], _NT,
                                     preferred_element_type=F32)

    def softmax(n):
        _, g, r0 = where(n)
        s = s_buf[...]
        m_prev = m_s[g, pl.ds(r0, rc), :]
        m_new = jnp.maximum(m_prev, jnp.max(s, axis=-1, keepdims=True))
        a_buf[...] = jnp.exp2(m_prev - m_new)
        for t in range(tk // LANES):
            p_buf[:, t * LANES:(t + 1) * LANES] = jnp.exp2(
                s[:, t * LANES:(t + 1) * LANES] - m_new).astype(BF16)
        m_s[g, pl.ds(r0, rc), :] = m_new

    def values(n):
        k0, g, r0 = where(n)
        pv = jnp.dot(p_buf[...], v_ref[0, 0, pl.ds(k0, tk), :],
                     preferred_element_type=F32)
        acc_s[g, pl.ds(r0, rc), :] = a_buf[...] * acc_s[g, pl.ds(r0, rc), :] + pv

    scores(0)
    softmax(0)
    scores(1)

    def body(n, _):
        values(n)
        softmax(n + 1)
        scores(n + 2)
        return 0

    lax.fori_loop(0, n_steps - 2, body, 0, unroll=unroll)
    values(n_steps - 2)
    softmax(n_steps - 1)
    values(n_steps - 1)

    outs = []
    for g in range(Q_PER_KV):
        acc = acc_s[g]
        o = acc / pltpu.roll(acc, HEAD_DIM, axis=1)
        outs.append(o[:, 0:HEAD_DIM])
    o_ref[...] = jnp.concatenate(outs, axis=-1).astype(o_ref.dtype)


def _attention(q, k, v, tq, tk, rc):
    bsz, _, seq, _ = q.shape
    n_q = seq // tq
    n_steps = (seq // tk) * Q_PER_KV * (tq // rc)
    unroll = min(4, n_steps - 2)
    return pl.pallas_call(
        functools.partial(_attn_kernel, tq=tq, tk=tk, rc=rc, unroll=unroll),
        grid=(bsz, N_KV_HEADS, n_q),
        in_specs=[pl.BlockSpec((1, Q_PER_KV, tq, HEAD_DIM), lambda b, j, i: (b, j, i, 0)),
                  pl.BlockSpec((1, 1, seq, HEAD_DIM), lambda b, j, i: (b, j, 0, 0)),
                  pl.BlockSpec((1, 1, seq, LANES), lambda b, j, i: (b, j, 0, 0))],
        out_specs=pl.BlockSpec((tq, Q_PER_KV * HEAD_DIM), lambda b, j, i: (b * n_q + i, j)),
        out_shape=jax.ShapeDtypeStruct((bsz * seq, N_HEADS * HEAD_DIM), BF16),
        scratch_shapes=[pltpu.VMEM((Q_PER_KV, tq, LANES), F32),
                        pltpu.VMEM((Q_PER_KV, tq, LANES), F32),
                        pltpu.VMEM((rc, tk), F32),
                        pltpu.VMEM((rc, tk), BF16),
                        pltpu.VMEM((rc, LANES), F32)],
        compiler_params=_cparams("parallel", "parallel", "parallel"),
        name="flash_attn",
    )(q, k, v)


def _proj_res_kernel(x_ref, h_ref, w_ref, g_ref, o_ref):
    mix = jnp.dot(x_ref[...], w_ref[...], preferred_element_type=F32)
    o_ref[...] = h_ref[...] + _rms(mix, g_ref[...])


def _proj_res(x, h, w, gain, tm):
    n, d = h.shape
    k = x.shape[1]
    return pl.pallas_call(
        _proj_res_kernel,
        grid=(n // tm,),
        in_specs=[pl.BlockSpec((tm, k), lambda i: (i, 0)),
                  pl.BlockSpec((tm, d), lambda i: (i, 0)),
                  pl.BlockSpec((k, d), lambda i: (0, 0)),
                  pl.BlockSpec((1, d), lambda i: (0, 0))],
        out_specs=pl.BlockSpec((tm, d), lambda i: (i, 0)),
        out_shape=jax.ShapeDtypeStruct((n, d), F32),
        compiler_params=_cparams("parallel"),
        name="proj_res",
    )(x, h, w, gain.reshape(1, d))


def _s5_layer(h, bsz, seq, gains, w_in, lam_re, lam_im, log_step, b_re, b_im, c_re, c_im,
              d_skip, w_glu, w_out):
    seg_tokens = S5_CHUNK * S5_SEG_CHUNKS
    nseg = seq // seg_tokens
    nseq = bsz * nseg
    rc = min(512, h.shape[0] // S5_CHUNK)
    u_t = _norm_matmul(h, gains[0], w_in.astype(BF16), rc=rc)
    w1, w2t, pw_p, aa_p = _s5_prep(lam_re, lam_im, log_step, b_re, b_im, c_re, c_im, d_skip)
    y_t = _s5_core(u_t, w1, w2t, pw_p, aa_p, nseq, nseg)
    return _s5_out(y_t, h, w_glu.astype(BF16), w_out.astype(BF16), gains[1], rc=rc)


def _attn_layer(h, bsz, seq, gains, w_qkv, q_gain, k_gain, w_out):
    q, k, v = _qkv(h, gains[0], w_qkv, q_gain, k_gain, bsz, seq, tm=512)
    o = _attention(q, k, v, tq=2048, tk=512, rc=512)
    return _proj_res(o, h, w_out.astype(BF16), gains[1], tm=512)


def kernel(x, norm_gains, ssm_w_in, ssm_lambda_re, ssm_lambda_im, ssm_log_step, ssm_b_re,
           ssm_b_im, ssm_c_re, ssm_c_im, ssm_d, ssm_w_glu, ssm_w_out, ffn_w_gate, ffn_w_up,
           ffn_w_down, attn_w_qkv, attn_q_gain, attn_k_gain, attn_w_out, moe_w_router,
           moe_w_gate, moe_w_up, moe_w_down):
    bsz, seq, d = x.shape
    depth = norm_gains.shape[0]
    h = x.reshape(bsz * seq, d)
    for i in range(depth):
        j = i // 2
        g = norm_gains[i]
        if i % 2 == 0:
            h = _s5_layer(h, bsz, seq, g, ssm_w_in[j], ssm_lambda_re[j], ssm_lambda_im[j],
                          ssm_log_step[j], ssm_b_re[j], ssm_b_im[j], ssm_c_re[j], ssm_c_im[j],
                          ssm_d[j], ssm_w_glu[j], ssm_w_out[j])
            h = _ffn(h, g[2], g[3], ffn_w_gate[j].astype(BF16), ffn_w_up[j].astype(BF16),
                     ffn_w_down[j].astype(BF16), tm=512, tf=1408)
        else:
            h = _attn_layer(h, bsz, seq, g, attn_w_qkv[j], attn_q_gain[j], attn_k_gain[j],
                            attn_w_out[j])
            h = _moe(h, g[2], g[3], moe_w_router[j], moe_w_gate[j].astype(BF16),
                     moe_w_up[j].astype(BF16), moe_w_down[j].astype(BF16),
                     tb=2048, tf=896, rt=256)
    return h.reshape(bsz, seq, d)
```

```python
import functools
import math

import jax
import jax.numpy as jnp
from jax import lax
from jax.experimental import pallas as pl
from jax.experimental.pallas import tpu as pltpu

F32 = jnp.float32
BF16 = jnp.bfloat16
NORM_EPS = 1e-6
ROPE_THETA = 10000.0
GRID_W = 64
N_HEADS = 16
N_KV_HEADS = 4
HEAD_DIM = 64
Q_PER_KV = N_HEADS // N_KV_HEADS
SSM_GROUP = 16
SSM_STATE = 64
S5_CHUNK = 16
S5_SEG_CHUNKS = 64
TOP_K = 2
LANES = 128
VMEM_LIMIT_BYTES = 56 * 1024 * 1024
MOE_VMEM_LIMIT_BYTES = 60 * 1024 * 1024

_NT = (((1,), (1,)), ((), ()))


def _cparams(*sem):
    return pltpu.CompilerParams(dimension_semantics=sem, vmem_limit_bytes=VMEM_LIMIT_BYTES)


def _rms(x, gain):
    return x * lax.rsqrt(jnp.mean(x * x, axis=-1, keepdims=True) + NORM_EPS) * gain


def _sigmoid(x):
    return 1.0 / (1.0 + jnp.exp(-x))


def _gelu_tanh(x):
    return x * (0.5 * (1.0 + jnp.tanh(math.sqrt(2.0 / math.pi) * (x + 0.044715 * (x * x * x)))))


def _norm_matmul_kernel(x_ref, g_ref, w_ref, o_ref):
    xn = _rms(x_ref[...], g_ref[...]).astype(BF16)
    o_ref[0] = jnp.dot(xn, w_ref[...], preferred_element_type=F32).astype(o_ref.dtype)


def _norm_matmul(x, gain, w, rc):
    n, d = x.shape
    m = w.shape[1]
    chunks = n // S5_CHUNK
    return pl.pallas_call(
        _norm_matmul_kernel,
        grid=(chunks // rc, S5_CHUNK),
        in_specs=[pl.BlockSpec((rc, d), lambda i, t: (i, t)),
                  pl.BlockSpec((1, d), lambda i, t: (0, 0)),
                  pl.BlockSpec((d, m), lambda i, t: (0, 0))],
        out_specs=pl.BlockSpec((1, rc, m), lambda i, t: (t, i, 0)),
        out_shape=jax.ShapeDtypeStruct((S5_CHUNK, chunks, m), BF16),
        compiler_params=_cparams("parallel", "parallel"),
        name="norm_matmul",
    )(x.reshape(chunks, S5_CHUNK * d), gain.reshape(1, d), w)


def _s5_prep_kernel(lr_ref, li_ref, ls_ref, bt_re_ref, bt_im_ref, c_re_ref, c_im_ref, dt_ref,
                    w1_ref, w2t_ref, pw_ref, aa_ref):
    t_n, s_n, p_n = S5_CHUNK, SSM_GROUP, SSM_STATE
    pair = 2 * s_n
    rows = t_n * pair
    hi = lax.Precision.HIGHEST

    rr = lax.broadcasted_iota(jnp.int32, (rows, rows), 0)
    cc = lax.broadcasted_iota(jnp.int32, (rows, rows), 1)
    same_group = ((rr // s_n) % 2) == ((cc // s_n) % 2)
    tvec = lax.broadcasted_iota(jnp.int32, (t_n, 1), 0).astype(F32)
    jvec = lax.broadcasted_iota(jnp.int32, (S5_SEG_CHUNKS, 1), 0).astype(F32)
    row8 = lax.broadcasted_iota(jnp.int32, (8, LANES), 0)
    own_lanes = ((lax.broadcasted_iota(jnp.int32, (rows, LANES), 1) // p_n)
                 == ((lax.broadcasted_iota(jnp.int32, (rows, LANES), 0) // s_n) % 2))

    def lanes_by_group(tab0, tab1):
        lane = lax.broadcasted_iota(jnp.int32, tab0.shape, 1)
        return jnp.where(lane < p_n, tab0, tab1)

    m_total = None
    for d in range(2):
        lsr, lsi, q_re, q_im = [], [], [], []
        for g in range(2):
            lr = lr_ref[0, d, g:g + 1, :]
            li = li_ref[0, d, g:g + 1, :]
            step = jnp.exp(ls_ref[0, d, g:g + 1, :])
            lsr.append(lr * step)
            lsi.append(li * step)
            mag = jnp.exp(lsr[g])
            a_re, a_im = mag * jnp.cos(lsi[g]), mag * jnp.sin(lsi[g])
            nr, ni = a_re - 1.0, a_im
            den = lr * lr + li * li
            q_re.append((nr * lr + ni * li) / den)
            q_im.append((ni * lr - nr * li) / den)

        def cpow(k, g):
            mag = jnp.exp(lsr[g] * k)
            ang = lsi[g] * k
            return mag * jnp.cos(ang), mag * jnp.sin(ang)

        def table(k):
            t0, t1 = cpow(k, 0), cpow(k, 1)
            return tuple(
                jnp.concatenate([jnp.broadcast_to(tg[part][t:t + 1, :], (s_n, LANES))
                                 for t in range(t_n) for tg in (t0, t1)], axis=0)
                for part in range(2))

        br, bi = bt_re_ref[0, d], bt_im_ref[0, d]
        qr = jnp.concatenate([jnp.broadcast_to(q_re[g], (s_n, LANES)) for g in range(2)], axis=0)
        qi = jnp.concatenate([jnp.broadcast_to(q_im[g], (s_n, LANES)) for g in range(2)], axis=0)
        bb_re = qr * br - qi * bi
        bb_im = qr * bi + qi * br
        cr, ci = c_re_ref[0, d], c_im_ref[0, d]

        def outer(x_re, x_im, k):
            pe_re, pe_im = table(k)
            xe_re = jnp.concatenate([x_re] * t_n, axis=0)
            xe_im = jnp.concatenate([x_im] * t_n, axis=0)
            return xe_re * pe_re - xe_im * pe_im, xe_re * pe_im + xe_im * pe_re

        if d == 0:
            l_re, l_im = outer(bb_re, bb_im, -tvec)
            rt_re, rt_im = outer(cr, ci, tvec)
            ws_re, ws_im = outer(bb_re, bb_im, (t_n - 1.0) - tvec)
            wy_re, wy_im = outer(cr, ci, tvec + 1.0)
            mask = (rr // pair) <= (cc // pair)
            kseg = t_n * jvec
        else:
            l_re, l_im = outer(bb_re, bb_im, tvec)
            rt_re, rt_im = outer(cr, ci, -tvec)
            ws_re, ws_im = l_re, l_im
            wy_re, wy_im = outer(cr, ci, t_n - tvec)
            mask = (rr // pair) >= (cc // pair)
            kseg = t_n * ((S5_SEG_CHUNKS - 1.0) - jvec)
        kern = (lax.dot_general(l_re[:, 0:p_n], rt_re[:, 0:p_n], _NT, precision=hi,
                                preferred_element_type=F32)
                - lax.dot_general(l_im[:, 0:p_n], rt_im[:, 0:p_n], _NT, precision=hi,
                                  preferred_element_type=F32))
        kern = jnp.where(mask & same_group, kern, 0.0)
        m_total = kern if m_total is None else m_total + kern

        ycols = rows
        for k2, val in ((2 * d, ws_re), (2 * d + 1, ws_im)):
            w1_ref[0, :, ycols + k2 * LANES:ycols + (k2 + 1) * LANES] = jnp.where(
                own_lanes, val, 0.0).astype(w1_ref.dtype)
        for k2, val in ((2 * d, wy_re), (2 * d + 1, -wy_im)):
            w2t_ref[0, :, k2 * LANES:(k2 + 1) * LANES] = jnp.where(
                own_lanes, val, 0.0).astype(w2t_ref.dtype)
        p0, p1 = cpow(kseg, 0), cpow(kseg, 1)
        pw_ref[0, :, 2 * d * LANES:(2 * d + 1) * LANES] = lanes_by_group(p0[0], p1[0])
        pw_ref[0, :, (2 * d + 1) * LANES:(2 * d + 2) * LANES] = lanes_by_group(p0[1], p1[1])
        one = jnp.ones((1, 1), F32)
        c0, c1 = cpow(float(t_n) * one, 0), cpow(float(t_n) * one, 1)
        s0, s1 = (cpow(float(t_n * S5_SEG_CHUNKS) * one, 0),
                  cpow(float(t_n * S5_SEG_CHUNKS) * one, 1))
        for part in range(2):
            chunk_pow = lanes_by_group(c0[part], c1[part])
            seg_pow = lanes_by_group(s0[part], s1[part])
            aa_ref[0, :, (2 * d + part) * LANES:(2 * d + part + 1) * LANES] = jnp.where(
                row8 == 0, chunk_pow, jnp.where(row8 == 1, seg_pow, 0.0))

    w1_ref[0, :, 0:rows] = (m_total + jnp.where(rr == cc, dt_ref[0], 0.0)).astype(w1_ref.dtype)


def _s5_prep(lam_re, lam_im, log_step, b_re, b_im, c_re, c_im, d_skip):
    g_n = lam_re.shape[1]
    pairs = g_n // 2
    pair = 2 * SSM_GROUP
    rows = S5_CHUNK * pair

    def states(x):
        x = jnp.transpose(x.reshape(2, pairs, 2, SSM_STATE), (1, 0, 2, 3))
        return jnp.tile(x, (1, 1, 1, LANES // SSM_STATE))

    def per_channel(x):
        x = jnp.transpose(x.reshape(2, pairs, pair, SSM_STATE), (1, 0, 2, 3))
        return jnp.tile(x, (1, 1, 1, LANES // SSM_STATE))

    ls = jnp.transpose(log_step.reshape(2, pairs, 2), (1, 0, 2))[..., None]
    bt_re = per_channel(jnp.swapaxes(b_re, 2, 3))
    bt_im = per_channel(jnp.swapaxes(b_im, 2, 3))
    dt = jnp.tile(d_skip.reshape(pairs, 1, pair), (1, 1, S5_CHUNK))

    def spec(*blk):
        return pl.BlockSpec((1,) + blk, lambda g: (g,) + (0,) * len(blk))

    return pl.pallas_call(
        _s5_prep_kernel,
        grid=(pairs,),
        in_specs=[spec(2, 2, LANES), spec(2, 2, LANES), spec(2, 2, 1),
                  spec(2, pair, LANES), spec(2, pair, LANES),
                  spec(2, pair, LANES), spec(2, pair, LANES), spec(1, rows)],
        out_specs=[spec(rows, rows + 4 * LANES), spec(rows, 4 * LANES),
                   spec(S5_SEG_CHUNKS, 4 * LANES), spec(8, 4 * LANES)],
        out_shape=[jax.ShapeDtypeStruct((pairs, rows, rows + 4 * LANES), BF16),
                   jax.ShapeDtypeStruct((pairs, rows, 4 * LANES), BF16),
                   jax.ShapeDtypeStruct((pairs, S5_SEG_CHUNKS, 4 * LANES), F32),
                   jax.ShapeDtypeStruct((pairs, 8, 4 * LANES), F32)],
        compiler_params=_cparams("parallel"),
        name="s5_prep",
    )(states(lam_re), states(lam_im), ls, bt_re, bt_im, per_channel(c_re), per_channel(c_im), dt)


S5_PAIR_LANES = 2 * SSM_GROUP
S5_PAIRS_PER_SLAB = LANES // S5_PAIR_LANES
S5_SEG_PITCH = S5_SEG_CHUNKS + 8
S5_SEG_BASE = 8


def _s5_core_kernel(x_ref, w1_ref, w2t_ref, pw_ref, aa_ref, y_ref, z_s, zs_s, x_s, u_s, xb_s,
                    *, nseq, nseg, strip):
    n_chunks = S5_SEG_CHUNKS
    rows = n_chunks * nseq
    ycols = u_s.shape[1]
    per_tile = LANES // S5_PAIR_LANES
    lane_grp = lax.broadcasted_iota(jnp.int32, (strip, LANES), 1) // S5_PAIR_LANES
    zero = jnp.zeros((nseq, LANES), F32)

    def seg_rows(j):
        return pl.ds(S5_SEG_BASE + j, nseq, stride=S5_SEG_PITCH)

    for q in range(S5_PAIRS_PER_SLAB):
        def gather(r, _, q=q):
            rws = pl.ds(pl.multiple_of(r * strip, strip), strip)
            for j in range(ycols // LANES):
                acc = None
                for i in range(per_tile):
                    xt = x_ref[per_tile * j + i, rws, :].astype(F32)
                    shift = (S5_PAIR_LANES * (i - q)) % LANES
                    if shift:
                        xt = pltpu.roll(xt, shift, axis=1)
                    acc = xt if acc is None else jnp.where(lane_grp == i, xt, acc)
                u_s[rws, j * LANES:(j + 1) * LANES] = acc.astype(BF16)
            return 0

        lax.fori_loop(0, rows // strip, gather, 0)
        z_s[...] = jnp.dot(u_s[...], w1_ref[q], preferred_element_type=F32)

        for b in range(nseq):
            dst = slice(S5_SEG_BASE + b * S5_SEG_PITCH, S5_SEG_BASE + b * S5_SEG_PITCH + n_chunks)
            for k in range(4):
                zs_s[k, dst, :] = z_s[b * n_chunks:(b + 1) * n_chunks,
                                      ycols + k * LANES:ycols + (k + 1) * LANES]
        x_s[0, seg_rows(0), :] = zero
        x_s[1, seg_rows(0), :] = zero
        x_s[2, seg_rows(n_chunks - 1), :] = zero
        x_s[3, seg_rows(n_chunks - 1), :] = zero

        aa = aa_ref[q]
        a_fr, a_fi = aa[0:1, 0:LANES], aa[0:1, LANES:2 * LANES]
        a_br, a_bi = aa[0:1, 2 * LANES:3 * LANES], aa[0:1, 3 * LANES:4 * LANES]
        g_fr, g_fi = aa[1:2, 0:LANES], aa[1:2, LANES:2 * LANES]
        g_br, g_bi = aa[1:2, 2 * LANES:3 * LANES], aa[1:2, 3 * LANES:4 * LANES]

        def fwd(j, carry):
            xr, xi = carry
            nr = a_fr * xr - a_fi * xi + zs_s[0, seg_rows(j), :]
            ni = a_fr * xi + a_fi * xr + zs_s[1, seg_rows(j), :]
            x_s[0, seg_rows(j + 1), :] = nr
            x_s[1, seg_rows(j + 1), :] = ni
            return nr, ni

        def bwd(i, carry):
            xr, xi = carry
            j = n_chunks - 1 - i
            nr = a_br * xr - a_bi * xi + zs_s[2, seg_rows(j), :]
            ni = a_br * xi + a_bi * xr + zs_s[3, seg_rows(j), :]
            x_s[2, seg_rows(j - 1), :] = nr
            x_s[3, seg_rows(j - 1), :] = ni
            return nr, ni

        ef_r, ef_i = lax.fori_loop(0, n_chunks, fwd, (zero, zero))
        eb_r, eb_i = lax.fori_loop(0, n_chunks, bwd, (zero, zero))

        def seg_carries(e_r, e_i, g_r, g_i, reverse):
            out_r = [None] * nseq
            out_i = [None] * nseq
            for b in range(nseq // nseg):
                c_r = jnp.zeros((1, LANES), F32)
                c_i = jnp.zeros((1, LANES), F32)
                order = range(nseg - 1, -1, -1) if reverse else range(nseg)
                for s in order:
                    r = b * nseg + s
                    out_r[r], out_i[r] = c_r, c_i
                    n_r = g_r * c_r - g_i * c_i + e_r[r:r + 1, :]
                    n_i = g_r * c_i + g_i * c_r + e_i[r:r + 1, :]
                    c_r, c_i = n_r, n_i
            return out_r, out_i

        cf_r, cf_i = seg_carries(ef_r, ef_i, g_fr, g_fi, False)
        cb_r, cb_i = seg_carries(eb_r, eb_i, g_br, g_bi, True)

        p = pw_ref[q]
        p_fr, p_fi = p[:, 0:LANES], p[:, LANES:2 * LANES]
        p_br, p_bi = p[:, 2 * LANES:3 * LANES], p[:, 3 * LANES:4 * LANES]
        for b in range(nseq):
            src = slice(S5_SEG_BASE + b * S5_SEG_PITCH, S5_SEG_BASE + b * S5_SEG_PITCH + n_chunks)
            dst = slice(b * n_chunks, (b + 1) * n_chunks)
            xb_s[dst, 0:LANES] = (x_s[0, src, :] + (p_fr * cf_r[b] - p_fi * cf_i[b])).astype(BF16)
            xb_s[dst, LANES:2 * LANES] = (
                x_s[1, src, :] + (p_fr * cf_i[b] + p_fi * cf_r[b])).astype(BF16)
            xb_s[dst, 2 * LANES:3 * LANES] = (
                x_s[2, src, :] + (p_br * cb_r[b] - p_bi * cb_i[b])).astype(BF16)
            xb_s[dst, 3 * LANES:4 * LANES] = (
                x_s[3, src, :] + (p_br * cb_i[b] + p_bi * cb_r[b])).astype(BF16)

        z_s[:, 0:ycols] += lax.dot_general(xb_s[...], w2t_ref[q], _NT,
                                           preferred_element_type=F32)

        def scatter(r, _, q=q):
            rws = pl.ds(pl.multiple_of(r * strip, strip), strip)
            for j in range(ycols // LANES):
                yq = z_s[rws, j * LANES:(j + 1) * LANES]
                for i in range(per_tile):
                    shift = (S5_PAIR_LANES * (q - i)) % LANES
                    yt = pltpu.roll(yq, shift, axis=1) if shift else yq
                    lanes_q = slice(q * S5_PAIR_LANES, (q + 1) * S5_PAIR_LANES)
                    y_ref[per_tile * j + i, rws, lanes_q] = yt[:, lanes_q].astype(y_ref.dtype)
            return 0

        lax.fori_loop(0, rows // strip, scatter, 0)


def _s5_core(u_t, w1, w2t, pw, aa, nseq, nseg):
    t_n, rows, d = u_t.shape
    width = w1.shape[1]
    pps = S5_PAIRS_PER_SLAB
    slab_rows = S5_SEG_BASE + nseq * S5_SEG_PITCH
    return pl.pallas_call(
        functools.partial(_s5_core_kernel, nseq=nseq, nseg=nseg, strip=128),
        grid=(d // LANES,),
        in_specs=[pl.BlockSpec((t_n, rows, LANES), lambda o: (0, 0, o)),
                  pl.BlockSpec((pps, width, w1.shape[2]), lambda o: (o, 0, 0)),
                  pl.BlockSpec((pps, width, w2t.shape[2]), lambda o: (o, 0, 0)),
                  pl.BlockSpec((pps, S5_SEG_CHUNKS, pw.shape[2]), lambda o: (o, 0, 0)),
                  pl.BlockSpec((pps, 8, aa.shape[2]), lambda o: (o, 0, 0))],
        out_specs=pl.BlockSpec((t_n, rows, LANES), lambda o: (0, 0, o)),
        out_shape=jax.ShapeDtypeStruct((t_n, rows, d), BF16),
        scratch_shapes=[pltpu.VMEM((rows, w1.shape[2]), F32),
                        pltpu.VMEM((4, slab_rows, LANES), F32),
                        pltpu.VMEM((4, slab_rows, LANES), F32),
                        pltpu.VMEM((rows, width), BF16),
                        pltpu.VMEM((rows, 4 * LANES), BF16)],
        compiler_params=_cparams("parallel"),
        name="s5_core",
    )(u_t, w1, w2t, pw, aa)


def _s5_out_kernel(y_ref, h_ref, wglu_ref, wout_ref, g_ref, o_ref):
    g = _gelu_tanh(y_ref[0].astype(F32))
    z = jnp.dot(g.astype(BF16), wglu_ref[...], preferred_element_type=F32)
    g2 = g * _sigmoid(z)
    mix = jnp.dot(g2.astype(BF16), wout_ref[...], preferred_element_type=F32)
    o_ref[...] = h_ref[...] + _rms(mix, g_ref[...])


def _s5_out(y_t, h, w_glu, w_out, gain, rc):
    n, d = h.shape
    chunks = n // S5_CHUNK
    out = pl.pallas_call(
        _s5_out_kernel,
        grid=(chunks // rc, S5_CHUNK),
        in_specs=[pl.BlockSpec((1, rc, d), lambda i, t: (t, i, 0)),
                  pl.BlockSpec((rc, d), lambda i, t: (i, t)),
                  pl.BlockSpec((d, d), lambda i, t: (0, 0)),
                  pl.BlockSpec((d, d), lambda i, t: (0, 0)),
                  pl.BlockSpec((1, d), lambda i, t: (0, 0))],
        out_specs=pl.BlockSpec((rc, d), lambda i, t: (i, t)),
        out_shape=jax.ShapeDtypeStruct((chunks, S5_CHUNK * d), F32),
        compiler_params=_cparams("parallel", "parallel"),
        name="s5_out",
    )(y_t, h.reshape(chunks, S5_CHUNK * d), w_glu, w_out, gain.reshape(1, d))
    return out.reshape(n, d)


def _ffn_kernel(h_ref, gpre_ref, gpost_ref, wg_ref, wu_ref, wd_ref, o_ref, hn_s, acc_s):
    f = pl.program_id(1)

    @pl.when(f == 0)
    def _():
        hn_s[...] = _rms(h_ref[...], gpre_ref[...]).astype(BF16)
        acc_s[...] = jnp.zeros_like(acc_s)

    hn = hn_s[...]
    a = jnp.dot(hn, wg_ref[...], preferred_element_type=F32)
    u = jnp.dot(hn, wu_ref[...], preferred_element_type=F32)
    act = a * _sigmoid(a) * u
    acc_s[...] += jnp.dot(act.astype(BF16), wd_ref[...], preferred_element_type=F32)

    @pl.when(f == pl.num_programs(1) - 1)
    def _():
        o_ref[...] = h_ref[...] + _rms(acc_s[...], gpost_ref[...])


def _ffn(h, gain_pre, gain_post, w_gate, w_up, w_down, tm, tf):
    n, d = h.shape
    d_ff = w_gate.shape[1]
    return pl.pallas_call(
        _ffn_kernel,
        grid=(n // tm, d_ff // tf),
        in_specs=[pl.BlockSpec((tm, d), lambda i, f: (i, 0)),
                  pl.BlockSpec((1, d), lambda i, f: (0, 0)),
                  pl.BlockSpec((1, d), lambda i, f: (0, 0)),
                  pl.BlockSpec((d, tf), lambda i, f: (0, f)),
                  pl.BlockSpec((d, tf), lambda i, f: (0, f)),
                  pl.BlockSpec((tf, d), lambda i, f: (f, 0))],
        out_specs=pl.BlockSpec((tm, d), lambda i, f: (i, 0)),
        out_shape=jax.ShapeDtypeStruct((n, d), F32),
        scratch_shapes=[pltpu.VMEM((tm, d), BF16), pltpu.VMEM((tm, d), F32)],
        compiler_params=_cparams("parallel", "arbitrary"),
        name="ffn",
    )(h, gain_pre.reshape(1, d), gain_post.reshape(1, d), w_gate, w_up, w_down)


def _moe_kernel(h_ref, gpre_ref, gpost_ref, wr_ref, wg_ref, wu_ref, wd_ref, o_ref,
                hn_s, gate_s, sel_s, pos_s, post_s, xc_s, yc_s, xg_s, col_s, cnt_s,
                *, n_exp, rt, strip):
    e = pl.program_id(1)
    f = pl.program_id(2)
    tb, d = hn_s.shape
    n_f = pl.num_programs(2)
    n_strips = tb // strip

    def put_counts(r, total):
        lane1 = lax.broadcasted_iota(jnp.int32, total.shape, 1)
        for x in range(n_exp):
            cnt_s[r * n_exp + x] = jnp.sum(jnp.where(lane1 == x, total, 0.0)).astype(jnp.int32)

    @pl.when((e == 0) & (f == 0))
    def _():
        def route(r, total):
            put_counts(r, total)
            rows = pl.ds(pl.multiple_of(r * strip, strip), strip)
            xn = _rms(h_ref[rows, :], gpre_ref[...])
            hn_s[rows, :] = xn.astype(BF16)
            o_ref[rows, :] = jnp.zeros((strip, d), F32)
            logits = jnp.dot(xn, wr_ref[...], precision=lax.Precision.HIGHEST,
                             preferred_element_type=F32)
            lane = lax.broadcasted_iota(jnp.int32, logits.shape, 1)
            neg = jnp.float32(-jnp.inf)
            logits = jnp.where(lane < n_exp, logits, neg)
            m1 = jnp.max(logits, axis=-1, keepdims=True)
            i1 = jnp.min(jnp.where(logits == m1, lane, LANES), axis=-1, keepdims=True)
            rest = jnp.where(lane == i1, neg, logits)
            m2 = jnp.max(rest, axis=-1, keepdims=True)
            i2 = jnp.min(jnp.where(rest == m2, lane, LANES), axis=-1, keepdims=True)
            e2 = jnp.exp(m2 - m1)
            gate_s[rows, :] = (jnp.where(lane == i1, 1.0 / (1.0 + e2), 0.0)
                               + jnp.where(lane == i2, e2 / (1.0 + e2), 0.0))
            sel = ((lane == i1) | (lane == i2)).astype(F32)
            sel_s[rows, :] = sel.astype(BF16)
            return total + jnp.sum(sel, axis=0, keepdims=True)

        total = lax.fori_loop(0, n_strips, route, jnp.zeros((1, LANES), F32))
        put_counts(n_strips, total)

        def rank(r, _):
            r0 = pl.multiple_of(r * strip, strip)
            rows = lax.broadcasted_iota(jnp.int32, (strip, tb), 0) + r0
            cols = lax.broadcasted_iota(jnp.int32, (strip, tb), 1)
            before = (cols < rows).astype(BF16)
            cnt = jnp.dot(before, sel_s[...], preferred_element_type=F32)
            chosen = sel_s[pl.ds(r0, strip), :] > 0
            pos_s[pl.ds(r0, strip), :] = jnp.where(chosen, cnt, -1.0)
            return 0

        lax.fori_loop(0, tb // strip, rank, 0)
        for r in range(tb // strip):
            post_s[:, r * strip:(r + 1) * strip] = pos_s[r * strip:(r + 1) * strip, :].T

    @pl.when(f == n_f - 1)
    def _():
        lane = lax.broadcasted_iota(jnp.int32, (tb, LANES), 1)
        col_s[0] = jnp.sum(jnp.where(lane == e, pos_s[...], 0.0), axis=-1, keepdims=True)
        col_s[1] = jnp.sum(jnp.where(lane == e, gate_s[...], 0.0), axis=-1, keepdims=True)

    def expert_rows(r0, m, srows):
        def holds(lo, hi):
            return (cnt_s[lo * n_exp + e] < r0 + m) & (cnt_s[hi * n_exp + e] > r0)

        @pl.when(f == 0)
        def _():
            xg_s[0:m, :] = jnp.zeros((m, d), F32)
            slot = (lax.broadcasted_iota(jnp.int32, (m, strip), 0) + r0).astype(F32)
            for r in range(n_strips):
                @pl.when(holds(r, r + 1))
                def _():
                    cols = slice(r * strip, (r + 1) * strip)
                    pick = (post_s[pl.ds(e, 1), cols] == slot).astype(BF16)
                    xg_s[0:m, :] += jnp.dot(pick, hn_s[cols, :], preferred_element_type=F32)
            xc_s[pl.ds(r0, m), :] = xg_s[0:m, :].astype(BF16)

        xc = xc_s[pl.ds(r0, m), :]
        a = jnp.dot(xc, wg_ref[0], preferred_element_type=F32)
        u = jnp.dot(xc, wu_ref[0], preferred_element_type=F32)
        act = (a * _sigmoid(a) * u).astype(BF16)
        y = jnp.dot(act, wd_ref[0], preferred_element_type=F32)
        prev = jnp.where(f == 0, 0.0, yc_s[pl.ds(r0, m), :])
        yc_s[pl.ds(r0, m), :] = prev + y

        @pl.when(f == n_f - 1)
        def _():
            yc = yc_s[pl.ds(r0, m), :].astype(BF16)
            slot = (lax.broadcasted_iota(jnp.int32, (srows, m), 1) + r0).astype(F32)
            per = srows // strip
            for r in range(tb // srows):
                @pl.when(holds(r * per, (r + 1) * per))
                def _():
                    rows = slice(r * srows, (r + 1) * srows)
                    put = (col_s[0, rows, :] == slot).astype(BF16)
                    back = jnp.dot(put, yc, preferred_element_type=F32)
                    o_ref[rows, :] += col_s[1, rows, :] * back

    count = cnt_s[n_strips * n_exp + e]
    wide = xg_s.shape[0]

    @pl.when(count <= wide)
    def _():
        expert_rows(0, wide, 2 * strip)

    @pl.when(count > wide)
    def _():
        def tile(i, _):
            expert_rows(pl.multiple_of(i * rt, rt), rt, strip)
            return 0

        lax.fori_loop(0, (count + rt - 1) // rt, tile, 0)

    @pl.when((e == n_exp - 1) & (f == n_f - 1))
    def _():
        def finish(r, _):
            rows = pl.ds(pl.multiple_of(r * strip, strip), strip)
            o_ref[rows, :] = h_ref[rows, :] + _rms(o_ref[rows, :], gpost_ref[...])
            return 0

        lax.fori_loop(0, tb // strip, finish, 0)


def _moe(h, gain_pre, gain_post, w_router, w_gate, w_up, w_down, tb, tf, rt, wide):
    n, d = h.shape
    n_exp, _, d_ff = w_gate.shape
    w_r = jnp.pad(w_router, ((0, 0), (0, LANES - n_exp)))
    once = pl.Buffered(1)
    strip = 256
    return pl.pallas_call(
        functools.partial(_moe_kernel, n_exp=n_exp, rt=rt, strip=strip),
        grid=(n // tb, n_exp, d_ff // tf),
        in_specs=[pl.BlockSpec((tb, d), lambda i, e, f: (i, 0), pipeline_mode=once),
                  pl.BlockSpec((1, d), lambda i, e, f: (0, 0)),
                  pl.BlockSpec((1, d), lambda i, e, f: (0, 0)),
                  pl.BlockSpec((d, LANES), lambda i, e, f: (0, 0)),
                  pl.BlockSpec((1, d, tf), lambda i, e, f: (e, 0, f)),
                  pl.BlockSpec((1, d, tf), lambda i, e, f: (e, 0, f)),
                  pl.BlockSpec((1, tf, d), lambda i, e, f: (e, f, 0))],
        out_specs=pl.BlockSpec((tb, d), lambda i, e, f: (i, 0), pipeline_mode=once),
        out_shape=jax.ShapeDtypeStruct((n, d), F32),
        scratch_shapes=[pltpu.VMEM((tb, d), BF16),
                        pltpu.VMEM((tb, LANES), F32),
                        pltpu.VMEM((tb, LANES), BF16),
                        pltpu.VMEM((tb, LANES), F32),
                        pltpu.VMEM((LANES, tb), F32),
                        pltpu.VMEM((tb, d), BF16),
                        pltpu.VMEM((tb, d), F32),
                        pltpu.VMEM((wide, d), F32),
                        pltpu.VMEM((2, tb, 1), F32),
                        pltpu.SMEM(((tb // strip + 1) * n_exp,), jnp.int32)],
        compiler_params=pltpu.CompilerParams(
            dimension_semantics=("parallel", "arbitrary", "arbitrary"),
            vmem_limit_bytes=MOE_VMEM_LIMIT_BYTES),
        name="moe_ffn",
    )(h, gain_pre.reshape(1, d), gain_post.reshape(1, d), w_r, w_gate, w_up, w_down)


def _qkv_kernel(h_ref, g_ref, w_ref, qg_ref, kg_ref, cs_ref, sn_ref, bd_ref,
                q_ref, k_ref, v_ref):
    xn = _rms(h_ref[...], g_ref[...]).astype(BF16)
    qkv = jnp.dot(xn, w_ref[...], preferred_element_type=F32)
    cs = cs_ref[...]
    sn = sn_ref[...]
    bd = bd_ref[...]
    lane = lax.broadcasted_iota(jnp.int32, cs.shape, 1)
    first_half = (lane % HEAD_DIM) < (HEAD_DIM // 2)
    scale = math.log2(math.e) / math.sqrt(HEAD_DIM)

    def norm_rope(x, gain):
        ms = jnp.dot(x * x, bd, precision=lax.Precision.HIGHEST, preferred_element_type=F32)
        y = x * lax.rsqrt(ms + NORM_EPS) * gain
        partner = jnp.where(first_half,
                            pltpu.roll(y, LANES - HEAD_DIM // 2, axis=1),
                            pltpu.roll(y, HEAD_DIM // 2, axis=1))
        return y * cs + partner * sn

    n_q_tiles = N_HEADS * HEAD_DIM // LANES
    for t in range(n_q_tiles):
        y = norm_rope(qkv[:, t * LANES:(t + 1) * LANES], qg_ref[...]) * scale
        q_ref[0, 2 * t] = y[:, 0:HEAD_DIM].astype(BF16)
        q_ref[0, 2 * t + 1] = y[:, HEAD_DIM:LANES].astype(BF16)
    k0 = N_HEADS * HEAD_DIM
    for t in range(N_KV_HEADS * HEAD_DIM // LANES):
        y = norm_rope(qkv[:, k0 + t * LANES:k0 + (t + 1) * LANES], kg_ref[...])
        k_ref[0, 2 * t] = y[:, 0:HEAD_DIM].astype(BF16)
        k_ref[0, 2 * t + 1] = y[:, HEAD_DIM:LANES].astype(BF16)
    v0 = (N_HEADS + N_KV_HEADS) * HEAD_DIM
    ones = jnp.ones((qkv.shape[0], LANES - HEAD_DIM), BF16)
    for j in range(N_KV_HEADS):
        vj = qkv[:, v0 + j * HEAD_DIM:v0 + (j + 1) * HEAD_DIM].astype(BF16)
        v_ref[0, j] = jnp.concatenate([vj, ones], axis=-1)


def _rope_tables(seq):
    axis_dim = HEAD_DIM // 2
    freqs = ROPE_THETA ** (-jnp.arange(0, axis_dim, 2, dtype=F32) / axis_dim)
    rows = seq // GRID_W
    row_ang = jnp.arange(rows, dtype=F32)[:, None] * freqs
    col_ang = jnp.arange(GRID_W, dtype=F32)[:, None] * freqs
    ang = jnp.concatenate([
        jnp.broadcast_to(row_ang[:, None, :], (rows, GRID_W, freqs.shape[0])),
        jnp.broadcast_to(col_ang[None, :, :], (rows, GRID_W, freqs.shape[0]))], axis=-1)
    ang = ang.reshape(seq, HEAD_DIM // 2)
    cos, sin = jnp.cos(ang), jnp.sin(ang)
    cs = jnp.tile(jnp.concatenate([cos, cos], axis=-1), (1, LANES // HEAD_DIM))
    sn = jnp.tile(jnp.concatenate([-sin, sin], axis=-1), (1, LANES // HEAD_DIM))
    return cs, sn


def _qkv(h, gain, w_qkv, q_gain, k_gain, bsz, seq, tm):
    n, d = h.shape
    width = w_qkv.shape[1]
    perm = jnp.concatenate([jnp.arange(0, HEAD_DIM, 2), jnp.arange(1, HEAD_DIM, 2)])
    n_rot = N_HEADS + N_KV_HEADS
    cols = (jnp.arange(n_rot)[:, None] * HEAD_DIM + perm[None, :]).reshape(-1)
    cols = jnp.concatenate([cols, jnp.arange(n_rot * HEAD_DIM, width)])
    w = w_qkv[:, cols].astype(BF16)
    qg = jnp.tile(q_gain[perm], LANES // HEAD_DIM).reshape(1, LANES)
    kg = jnp.tile(k_gain[perm], LANES // HEAD_DIM).reshape(1, LANES)
    cs, sn = _rope_tables(seq)
    blk = jnp.arange(LANES) // HEAD_DIM
    bd = (blk[:, None] == blk[None, :]).astype(F32) / HEAD_DIM
    per_seq = seq // tm
    return pl.pallas_call(
        _qkv_kernel,
        grid=(n // tm,),
        in_specs=[pl.BlockSpec((tm, d), lambda i: (i, 0)),
                  pl.BlockSpec((1, d), lambda i: (0, 0)),
                  pl.BlockSpec((d, width), lambda i: (0, 0)),
                  pl.BlockSpec((1, LANES), lambda i: (0, 0)),
                  pl.BlockSpec((1, LANES), lambda i: (0, 0)),
                  pl.BlockSpec((tm, LANES), lambda i: (i % per_seq, 0)),
                  pl.BlockSpec((tm, LANES), lambda i: (i % per_seq, 0)),
                  pl.BlockSpec((LANES, LANES), lambda i: (0, 0))],
        out_specs=[pl.BlockSpec((1, N_HEADS, tm, HEAD_DIM),
                                lambda i: (i // per_seq, 0, i % per_seq, 0)),
                   pl.BlockSpec((1, N_KV_HEADS, tm, HEAD_DIM),
                                lambda i: (i // per_seq, 0, i % per_seq, 0)),
                   pl.BlockSpec((1, N_KV_HEADS, tm, LANES),
                                lambda i: (i // per_seq, 0, i % per_seq, 0))],
        out_shape=[jax.ShapeDtypeStruct((bsz, N_HEADS, seq, HEAD_DIM), BF16),
                   jax.ShapeDtypeStruct((bsz, N_KV_HEADS, seq, HEAD_DIM), BF16),
                   jax.ShapeDtypeStruct((bsz, N_KV_HEADS, seq, LANES), BF16)],
        compiler_params=_cparams("parallel"),
        name="qkv_rope",
    )(h, gain.reshape(1, d), w, qg, kg, cs, sn, bd)


def _attn_kernel(q_ref, k_ref, v_ref, o_ref, m_s, acc_s, s_buf, p_buf, a_buf,
                 *, tq, tk, rc, unroll):
    seq = k_ref.shape[2]
    chunks_per_head = tq // rc
    n_chunks = Q_PER_KV * chunks_per_head
    n_steps = (seq // tk) * n_chunks

    m_s[...] = jnp.full_like(m_s, -jnp.inf)
    acc_s[...] = jnp.zeros_like(acc_s)

    def where(n):
        c = n % n_chunks
        return (pl.multiple_of((n // n_chunks) * tk, tk), c // chunks_per_head,
                pl.multiple_of((c % chunks_per_head) * rc, rc))

    def scores(n):
        k0, g, r0 = where(n)
        q = q_ref[0, g, pl.ds(r0, rc), :]
        s_buf[...] = lax.dot_general(q, k_ref[0, 0, pl.ds(k0, tk), :], _NT,
                                     preferred_element_type=F32)

    def softmax(n):
        _, g, r0 = where(n)
        s = s_buf[...]
        m_prev = m_s[g, pl.ds(r0, rc), :]
        m_new = jnp.maximum(m_prev, jnp.max(s, axis=-1, keepdims=True))
        a_buf[...] = jnp.exp2(m_prev - m_new)
        for t in range(tk // LANES):
            p_buf[:, t * LANES:(t + 1) * LANES] = jnp.exp2(
                s[:, t * LANES:(t + 1) * LANES] - m_new).astype(BF16)
        m_s[g, pl.ds(r0, rc), :] = m_new

    def values(n):
        k0, g, r0 = where(n)
        pv = jnp.dot(p_buf[...], v_ref[0, 0, pl.ds(k0, tk), :],
                     preferred_element_type=F32)
        acc_s[g, pl.ds(r0, rc), :] = a_buf[...] * acc_s[g, pl.ds(r0, rc), :] + pv

    scores(0)
    softmax(0)
    scores(1)

    def body(n, _):
        values(n)
        softmax(n + 1)
        scores(n + 2)
        return 0

    lax.fori_loop(0, n_steps - 2, body, 0, unroll=unroll)
    values(n_steps - 2)
    softmax(n_steps - 1)
    values(n_steps - 1)

    outs = []
    for g in range(Q_PER_KV):
        acc = acc_s[g]
        o = acc / pltpu.roll(acc, HEAD_DIM, axis=1)
        outs.append(o[:, 0:HEAD_DIM])
    o_ref[...] = jnp.concatenate(outs, axis=-1).astype(o_ref.dtype)


def _attention(q, k, v, tq, tk, rc):
    bsz, _, seq, _ = q.shape
    n_q = seq // tq
    n_steps = (seq // tk) * Q_PER_KV * (tq // rc)
    unroll = min(4, n_steps - 2)
    return pl.pallas_call(
        functools.partial(_attn_kernel, tq=tq, tk=tk, rc=rc, unroll=unroll),
        grid=(bsz, N_KV_HEADS, n_q),
        in_specs=[pl.BlockSpec((1, Q_PER_KV, tq, HEAD_DIM), lambda b, j, i: (b, j, i, 0)),
                  pl.BlockSpec((1, 1, seq, HEAD_DIM), lambda b, j, i: (b, j, 0, 0)),
                  pl.BlockSpec((1, 1, seq, LANES), lambda b, j, i: (b, j, 0, 0))],
        out_specs=pl.BlockSpec((tq, Q_PER_KV * HEAD_DIM), lambda b, j, i: (b * n_q + i, j)),
        out_shape=jax.ShapeDtypeStruct((bsz * seq, N_HEADS * HEAD_DIM), BF16),
        scratch_shapes=[pltpu.VMEM((Q_PER_KV, tq, LANES), F32),
                        pltpu.VMEM((Q_PER_KV, tq, LANES), F32),
                        pltpu.VMEM((rc, tk), F32),
                        pltpu.VMEM((rc, tk), BF16),
                        pltpu.VMEM((rc, LANES), F32)],
        compiler_params=_cparams("parallel", "parallel", "parallel"),
        name="flash_attn",
    )(q, k, v)


def _proj_res_kernel(x_ref, h_ref, w_ref, g_ref, o_ref):
    mix = jnp.dot(x_ref[...], w_ref[...], preferred_element_type=F32)
    o_ref[...] = h_ref[...] + _rms(mix, g_ref[...])


def _proj_res(x, h, w, gain, tm):
    n, d = h.shape
    k = x.shape[1]
    return pl.pallas_call(
        _proj_res_kernel,
        grid=(n // tm,),
        in_specs=[pl.BlockSpec((tm, k), lambda i: (i, 0)),
                  pl.BlockSpec((tm, d), lambda i: (i, 0)),
                  pl.BlockSpec((k, d), lambda i: (0, 0)),
                  pl.BlockSpec((1, d), lambda i: (0, 0))],
        out_specs=pl.BlockSpec((tm, d), lambda i: (i, 0)),
        out_shape=jax.ShapeDtypeStruct((n, d), F32),
        compiler_params=_cparams("parallel"),
        name="proj_res",
    )(x, h, w, gain.reshape(1, d))


def _s5_layer(h, bsz, seq, gains, w_in, lam_re, lam_im, log_step, b_re, b_im, c_re, c_im,
              d_skip, w_glu, w_out):
    seg_tokens = S5_CHUNK * S5_SEG_CHUNKS
    nseg = seq // seg_tokens
    nseq = bsz * nseg
    rc = min(512, h.shape[0] // S5_CHUNK)
    u_t = _norm_matmul(h, gains[0], w_in.astype(BF16), rc=rc)
    w1, w2t, pw_p, aa_p = _s5_prep(lam_re, lam_im, log_step, b_re, b_im, c_re, c_im, d_skip)
    y_t = _s5_core(u_t, w1, w2t, pw_p, aa_p, nseq, nseg)
    return _s5_out(y_t, h, w_glu.astype(BF16), w_out.astype(BF16), gains[1], rc=rc)


def _attn_layer(h, bsz, seq, gains, w_qkv, q_gain, k_gain, w_out):
    q, k, v = _qkv(h, gains[0], w_qkv, q_gain, k_gain, bsz, seq, tm=512)
    o = _attention(q, k, v, tq=2048, tk=512, rc=512)
    return _proj_res(o, h, w_out.astype(BF16), gains[1], tm=512)


def kernel(x, norm_gains, ssm_w_in, ssm_lambda_re, ssm_lambda_im, ssm_log_step, ssm_b_re,
           ssm_b_im, ssm_c_re, ssm_c_im, ssm_d, ssm_w_glu, ssm_w_out, ffn_w_gate, ffn_w_up,
           ffn_w_down, attn_w_qkv, attn_q_gain, attn_k_gain, attn_w_out, moe_w_router,
           moe_w_gate, moe_w_up, moe_w_down):
    bsz, seq, d = x.shape
    depth = norm_gains.shape[0]
    h = x.reshape(bsz * seq, d)
    for i in range(depth):
        j = i // 2
        g = norm_gains[i]
        if i % 2 == 0:
            h = _s5_layer(h, bsz, seq, g, ssm_w_in[j], ssm_lambda_re[j], ssm_lambda_im[j],
                          ssm_log_step[j], ssm_b_re[j], ssm_b_im[j], ssm_c_re[j], ssm_c_im[j],
                          ssm_d[j], ssm_w_glu[j], ssm_w_out[j])
            h = _ffn(h, g[2], g[3], ffn_w_gate[j].astype(BF16), ffn_w_up[j].astype(BF16),
                     ffn_w_down[j].astype(BF16), tm=512, tf=1408)
        else:
            h = _attn_layer(h, bsz, seq, g, attn_w_qkv[j], attn_q_gain[j], attn_k_gain[j],
                            attn_w_out[j])
            h = _moe(h, g[2], g[3], moe_w_router[j], moe_w_gate[j].astype(BF16),
                     moe_w_up[j].astype(BF16), moe_w_down[j].astype(BF16),
                     tb=2048, tf=512, rt=256, wide=640)
    return h.reshape(bsz, seq, d)
```

```python
import functools
import math

import jax
import jax.numpy as jnp
from jax import lax
from jax.experimental import pallas as pl
from jax.experimental.pallas import tpu as pltpu

F32 = jnp.float32
BF16 = jnp.bfloat16
NORM_EPS = 1e-6
ROPE_THETA = 10000.0
GRID_W = 64
N_HEADS = 16
N_KV_HEADS = 4
HEAD_DIM = 64
Q_PER_KV = N_HEADS // N_KV_HEADS
SSM_GROUP = 16
SSM_STATE = 64
S5_CHUNK = 16
S5_SEG_CHUNKS = 64
TOP_K = 2
LANES = 128
VMEM_LIMIT_BYTES = 56 * 1024 * 1024
MOE_VMEM_LIMIT_BYTES = 60 * 1024 * 1024

_NT = (((1,), (1,)), ((), ()))


def _cparams(*sem):
    return pltpu.CompilerParams(dimension_semantics=sem, vmem_limit_bytes=VMEM_LIMIT_BYTES)


def _rms(x, gain):
    return x * lax.rsqrt(jnp.mean(x * x, axis=-1, keepdims=True) + NORM_EPS) * gain


def _sigmoid(x):
    return 1.0 / (1.0 + jnp.exp(-x))


def _gelu_tanh(x):
    return x * (0.5 * (1.0 + jnp.tanh(math.sqrt(2.0 / math.pi) * (x + 0.044715 * (x * x * x)))))


def _norm_matmul_kernel(x_ref, g_ref, w_ref, o_ref):
    xn = _rms(x_ref[...], g_ref[...]).astype(BF16)
    o_ref[0] = jnp.dot(xn, w_ref[...], preferred_element_type=F32).astype(o_ref.dtype)


def _norm_matmul(x, gain, w, rc):
    n, d = x.shape
    m = w.shape[1]
    chunks = n // S5_CHUNK
    return pl.pallas_call(
        _norm_matmul_kernel,
        grid=(chunks // rc, S5_CHUNK),
        in_specs=[pl.BlockSpec((rc, d), lambda i, t: (i, t)),
                  pl.BlockSpec((1, d), lambda i, t: (0, 0)),
                  pl.BlockSpec((d, m), lambda i, t: (0, 0))],
        out_specs=pl.BlockSpec((1, rc, m), lambda i, t: (t, i, 0)),
        out_shape=jax.ShapeDtypeStruct((S5_CHUNK, chunks, m), BF16),
        compiler_params=_cparams("parallel", "parallel"),
        name="norm_matmul",
    )(x.reshape(chunks, S5_CHUNK * d), gain.reshape(1, d), w)


def _s5_prep_kernel(lr_ref, li_ref, ls_ref, bt_re_ref, bt_im_ref, c_re_ref, c_im_ref, dt_ref,
                    w1_ref, w2t_ref, pw_ref, aa_ref):
    t_n, s_n, p_n = S5_CHUNK, SSM_GROUP, SSM_STATE
    pair = 2 * s_n
    rows = t_n * pair
    hi = lax.Precision.HIGHEST

    rr = lax.broadcasted_iota(jnp.int32, (rows, rows), 0)
    cc = lax.broadcasted_iota(jnp.int32, (rows, rows), 1)
    same_group = ((rr // s_n) % 2) == ((cc // s_n) % 2)
    tvec = lax.broadcasted_iota(jnp.int32, (t_n, 1), 0).astype(F32)
    jvec = lax.broadcasted_iota(jnp.int32, (S5_SEG_CHUNKS, 1), 0).astype(F32)
    row8 = lax.broadcasted_iota(jnp.int32, (8, LANES), 0)
    own_lanes = ((lax.broadcasted_iota(jnp.int32, (rows, LANES), 1) // p_n)
                 == ((lax.broadcasted_iota(jnp.int32, (rows, LANES), 0) // s_n) % 2))

    def lanes_by_group(tab0, tab1):
        lane = lax.broadcasted_iota(jnp.int32, tab0.shape, 1)
        return jnp.where(lane < p_n, tab0, tab1)

    m_total = None
    for d in range(2):
        lsr, lsi, q_re, q_im = [], [], [], []
        for g in range(2):
            lr = lr_ref[0, d, g:g + 1, :]
            li = li_ref[0, d, g:g + 1, :]
            step = jnp.exp(ls_ref[0, d, g:g + 1, :])
            lsr.append(lr * step)
            lsi.append(li * step)
            mag = jnp.exp(lsr[g])
            a_re, a_im = mag * jnp.cos(lsi[g]), mag * jnp.sin(lsi[g])
            nr, ni = a_re - 1.0, a_im
            den = lr * lr + li * li
            q_re.append((nr * lr + ni * li) / den)
            q_im.append((ni * lr - nr * li) / den)

        def cpow(k, g):
            mag = jnp.exp(lsr[g] * k)
            ang = lsi[g] * k
            return mag * jnp.cos(ang), mag * jnp.sin(ang)

        def table(k):
            t0, t1 = cpow(k, 0), cpow(k, 1)
            return tuple(
                jnp.concatenate([jnp.broadcast_to(tg[part][t:t + 1, :], (s_n, LANES))
                                 for t in range(t_n) for tg in (t0, t1)], axis=0)
                for part in range(2))

        br, bi = bt_re_ref[0, d], bt_im_ref[0, d]
        qr = jnp.concatenate([jnp.broadcast_to(q_re[g], (s_n, LANES)) for g in range(2)], axis=0)
        qi = jnp.concatenate([jnp.broadcast_to(q_im[g], (s_n, LANES)) for g in range(2)], axis=0)
        bb_re = qr * br - qi * bi
        bb_im = qr * bi + qi * br
        cr, ci = c_re_ref[0, d], c_im_ref[0, d]

        def outer(x_re, x_im, k):
            pe_re, pe_im = table(k)
            xe_re = jnp.concatenate([x_re] * t_n, axis=0)
            xe_im = jnp.concatenate([x_im] * t_n, axis=0)
            return xe_re * pe_re - xe_im * pe_im, xe_re * pe_im + xe_im * pe_re

        if d == 0:
            l_re, l_im = outer(bb_re, bb_im, -tvec)
            rt_re, rt_im = outer(cr, ci, tvec)
            ws_re, ws_im = outer(bb_re, bb_im, (t_n - 1.0) - tvec)
            wy_re, wy_im = outer(cr, ci, tvec + 1.0)
            mask = (rr // pair) <= (cc // pair)
            kseg = t_n * jvec
        else:
            l_re, l_im = outer(bb_re, bb_im, tvec)
            rt_re, rt_im = outer(cr, ci, -tvec)
            ws_re, ws_im = l_re, l_im
            wy_re, wy_im = outer(cr, ci, t_n - tvec)
            mask = (rr // pair) >= (cc // pair)
            kseg = t_n * ((S5_SEG_CHUNKS - 1.0) - jvec)
        kern = (lax.dot_general(l_re[:, 0:p_n], rt_re[:, 0:p_n], _NT, precision=hi,
                                preferred_element_type=F32)
                - lax.dot_general(l_im[:, 0:p_n], rt_im[:, 0:p_n], _NT, precision=hi,
                                  preferred_element_type=F32))
        kern = jnp.where(mask & same_group, kern, 0.0)
        m_total = kern if m_total is None else m_total + kern

        ycols = rows
        for k2, val in ((2 * d, ws_re), (2 * d + 1, ws_im)):
            w1_ref[0, :, ycols + k2 * LANES:ycols + (k2 + 1) * LANES] = jnp.where(
                own_lanes, val, 0.0).astype(w1_ref.dtype)
        for k2, val in ((2 * d, wy_re), (2 * d + 1, -wy_im)):
            w2t_ref[0, :, k2 * LANES:(k2 + 1) * LANES] = jnp.where(
                own_lanes, val, 0.0).astype(w2t_ref.dtype)
        p0, p1 = cpow(kseg, 0), cpow(kseg, 1)
        pw_ref[0, :, 2 * d * LANES:(2 * d + 1) * LANES] = lanes_by_group(p0[0], p1[0])
        pw_ref[0, :, (2 * d + 1) * LANES:(2 * d + 2) * LANES] = lanes_by_group(p0[1], p1[1])
        one = jnp.ones((1, 1), F32)
        c0, c1 = cpow(float(t_n) * one, 0), cpow(float(t_n) * one, 1)
        s0, s1 = (cpow(float(t_n * S5_SEG_CHUNKS) * one, 0),
                  cpow(float(t_n * S5_SEG_CHUNKS) * one, 1))
        for part in range(2):
            chunk_pow = lanes_by_group(c0[part], c1[part])
            seg_pow = lanes_by_group(s0[part], s1[part])
            aa_ref[0, :, (2 * d + part) * LANES:(2 * d + part + 1) * LANES] = jnp.where(
                row8 == 0, chunk_pow, jnp.where(row8 == 1, seg_pow, 0.0))

    w1_ref[0, :, 0:rows] = (m_total + jnp.where(rr == cc, dt_ref[0], 0.0)).astype(w1_ref.dtype)


def _s5_prep(lam_re, lam_im, log_step, b_re, b_im, c_re, c_im, d_skip):
    g_n = lam_re.shape[1]
    pairs = g_n // 2
    pair = 2 * SSM_GROUP
    rows = S5_CHUNK * pair

    def states(x):
        x = jnp.transpose(x.reshape(2, pairs, 2, SSM_STATE), (1, 0, 2, 3))
        return jnp.tile(x, (1, 1, 1, LANES // SSM_STATE))

    def per_channel(x):
        x = jnp.transpose(x.reshape(2, pairs, pair, SSM_STATE), (1, 0, 2, 3))
        return jnp.tile(x, (1, 1, 1, LANES // SSM_STATE))

    ls = jnp.transpose(log_step.reshape(2, pairs, 2), (1, 0, 2))[..., None]
    bt_re = per_channel(jnp.swapaxes(b_re, 2, 3))
    bt_im = per_channel(jnp.swapaxes(b_im, 2, 3))
    dt = jnp.tile(d_skip.reshape(pairs, 1, pair), (1, 1, S5_CHUNK))

    def spec(*blk):
        return pl.BlockSpec((1,) + blk, lambda g: (g,) + (0,) * len(blk))

    return pl.pallas_call(
        _s5_prep_kernel,
        grid=(pairs,),
        in_specs=[spec(2, 2, LANES), spec(2, 2, LANES), spec(2, 2, 1),
                  spec(2, pair, LANES), spec(2, pair, LANES),
                  spec(2, pair, LANES), spec(2, pair, LANES), spec(1, rows)],
        out_specs=[spec(rows, rows + 4 * LANES), spec(rows, 4 * LANES),
                   spec(S5_SEG_CHUNKS, 4 * LANES), spec(8, 4 * LANES)],
        out_shape=[jax.ShapeDtypeStruct((pairs, rows, rows + 4 * LANES), BF16),
                   jax.ShapeDtypeStruct((pairs, rows, 4 * LANES), BF16),
                   jax.ShapeDtypeStruct((pairs, S5_SEG_CHUNKS, 4 * LANES), F32),
                   jax.ShapeDtypeStruct((pairs, 8, 4 * LANES), F32)],
        compiler_params=_cparams("parallel"),
        name="s5_prep",
    )(states(lam_re), states(lam_im), ls, bt_re, bt_im, per_channel(c_re), per_channel(c_im), dt)


S5_PAIR_LANES = 2 * SSM_GROUP
S5_PAIRS_PER_SLAB = LANES // S5_PAIR_LANES
S5_SEG_PITCH = S5_SEG_CHUNKS + 8
S5_SEG_BASE = 8


def _s5_core_kernel(x_ref, w1_ref, w2t_ref, pw_ref, aa_ref, y_ref, z_s, zs_s, x_s, u_s, xb_s,
                    *, nseq, nseg, strip):
    n_chunks = S5_SEG_CHUNKS
    rows = n_chunks * nseq
    ycols = u_s.shape[1]
    per_tile = LANES // S5_PAIR_LANES
    lane_grp = lax.broadcasted_iota(jnp.int32, (strip, LANES), 1) // S5_PAIR_LANES
    zero = jnp.zeros((nseq, LANES), F32)

    def seg_rows(j):
        return pl.ds(S5_SEG_BASE + j, nseq, stride=S5_SEG_PITCH)

    for q in range(S5_PAIRS_PER_SLAB):
        def gather(r, _, q=q):
            rws = pl.ds(pl.multiple_of(r * strip, strip), strip)
            for j in range(ycols // LANES):
                acc = None
                for i in range(per_tile):
                    xt = x_ref[per_tile * j + i, rws, :].astype(F32)
                    shift = (S5_PAIR_LANES * (i - q)) % LANES
                    if shift:
                        xt = pltpu.roll(xt, shift, axis=1)
                    acc = xt if acc is None else jnp.where(lane_grp == i, xt, acc)
                u_s[rws, j * LANES:(j + 1) * LANES] = acc.astype(BF16)
            return 0

        lax.fori_loop(0, rows // strip, gather, 0)
        z_s[...] = jnp.dot(u_s[...], w1_ref[q], preferred_element_type=F32)

        for b in range(nseq):
            dst = slice(S5_SEG_BASE + b * S5_SEG_PITCH, S5_SEG_BASE + b * S5_SEG_PITCH + n_chunks)
            for k in range(4):
                zs_s[k, dst, :] = z_s[b * n_chunks:(b + 1) * n_chunks,
                                      ycols + k * LANES:ycols + (k + 1) * LANES]
        x_s[0, seg_rows(0), :] = zero
        x_s[1, seg_rows(0), :] = zero
        x_s[2, seg_rows(n_chunks - 1), :] = zero
        x_s[3, seg_rows(n_chunks - 1), :] = zero

        aa = aa_ref[q]
        a_fr, a_fi = aa[0:1, 0:LANES], aa[0:1, LANES:2 * LANES]
        a_br, a_bi = aa[0:1, 2 * LANES:3 * LANES], aa[0:1, 3 * LANES:4 * LANES]
        g_fr, g_fi = aa[1:2, 0:LANES], aa[1:2, LANES:2 * LANES]
        g_br, g_bi = aa[1:2, 2 * LANES:3 * LANES], aa[1:2, 3 * LANES:4 * LANES]

        def fwd(j, carry):
            xr, xi = carry
            nr = a_fr * xr - a_fi * xi + zs_s[0, seg_rows(j), :]
            ni = a_fr * xi + a_fi * xr + zs_s[1, seg_rows(j), :]
            x_s[0, seg_rows(j + 1), :] = nr
            x_s[1, seg_rows(j + 1), :] = ni
            return nr, ni

        def bwd(i, carry):
            xr, xi = carry
            j = n_chunks - 1 - i
            nr = a_br * xr - a_bi * xi + zs_s[2, seg_rows(j), :]
            ni = a_br * xi + a_bi * xr + zs_s[3, seg_rows(j), :]
            x_s[2, seg_rows(j - 1), :] = nr
            x_s[3, seg_rows(j - 1), :] = ni
            return nr, ni

        ef_r, ef_i = lax.fori_loop(0, n_chunks, fwd, (zero, zero))
        eb_r, eb_i = lax.fori_loop(0, n_chunks, bwd, (zero, zero))

        def seg_carries(e_r, e_i, g_r, g_i, reverse):
            out_r = [None] * nseq
            out_i = [None] * nseq
            for b in range(nseq // nseg):
                c_r = jnp.zeros((1, LANES), F32)
                c_i = jnp.zeros((1, LANES), F32)
                order = range(nseg - 1, -1, -1) if reverse else range(nseg)
                for s in order:
                    r = b * nseg + s
                    out_r[r], out_i[r] = c_r, c_i
                    n_r = g_r * c_r - g_i * c_i + e_r[r:r + 1, :]
                    n_i = g_r * c_i + g_i * c_r + e_i[r:r + 1, :]
                    c_r, c_i = n_r, n_i
            return out_r, out_i

        cf_r, cf_i = seg_carries(ef_r, ef_i, g_fr, g_fi, False)
        cb_r, cb_i = seg_carries(eb_r, eb_i, g_br, g_bi, True)

        p = pw_ref[q]
        p_fr, p_fi = p[:, 0:LANES], p[:, LANES:2 * LANES]
        p_br, p_bi = p[:, 2 * LANES:3 * LANES], p[:, 3 * LANES:4 * LANES]
        for b in range(nseq):
            src = slice(S5_SEG_BASE + b * S5_SEG_PITCH, S5_SEG_BASE + b * S5_SEG_PITCH + n_chunks)
            dst = slice(b * n_chunks, (b + 1) * n_chunks)
            xb_s[dst, 0:LANES] = (x_s[0, src, :] + (p_fr * cf_r[b] - p_fi * cf_i[b])).astype(BF16)
            xb_s[dst, LANES:2 * LANES] = (
                x_s[1, src, :] + (p_fr * cf_i[b] + p_fi * cf_r[b])).astype(BF16)
            xb_s[dst, 2 * LANES:3 * LANES] = (
                x_s[2, src, :] + (p_br * cb_r[b] - p_bi * cb_i[b])).astype(BF16)
            xb_s[dst, 3 * LANES:4 * LANES] = (
                x_s[3, src, :] + (p_br * cb_i[b] + p_bi * cb_r[b])).astype(BF16)

        z_s[:, 0:ycols] += lax.dot_general(xb_s[...], w2t_ref[q], _NT,
                                           preferred_element_type=F32)

        def scatter(r, _, q=q):
            rws = pl.ds(pl.multiple_of(r * strip, strip), strip)
            for j in range(ycols // LANES):
                yq = z_s[rws, j * LANES:(j + 1) * LANES]
                for i in range(per_tile):
                    shift = (S5_PAIR_LANES * (q - i)) % LANES
                    yt = pltpu.roll(yq, shift, axis=1) if shift else yq
                    lanes_q = slice(q * S5_PAIR_LANES, (q + 1) * S5_PAIR_LANES)
                    y_ref[per_tile * j + i, rws, lanes_q] = yt[:, lanes_q].astype(y_ref.dtype)
            return 0

        lax.fori_loop(0, rows // strip, scatter, 0)


def _s5_core(u_t, w1, w2t, pw, aa, nseq, nseg):
    t_n, rows, d = u_t.shape
    width = w1.shape[1]
    pps = S5_PAIRS_PER_SLAB
    slab_rows = S5_SEG_BASE + nseq * S5_SEG_PITCH
    return pl.pallas_call(
        functools.partial(_s5_core_kernel, nseq=nseq, nseg=nseg, strip=128),
        grid=(d // LANES,),
        in_specs=[pl.BlockSpec((t_n, rows, LANES), lambda o: (0, 0, o)),
                  pl.BlockSpec((pps, width, w1.shape[2]), lambda o: (o, 0, 0)),
                  pl.BlockSpec((pps, width, w2t.shape[2]), lambda o: (o, 0, 0)),
                  pl.BlockSpec((pps, S5_SEG_CHUNKS, pw.shape[2]), lambda o: (o, 0, 0)),
                  pl.BlockSpec((pps, 8, aa.shape[2]), lambda o: (o, 0, 0))],
        out_specs=pl.BlockSpec((t_n, rows, LANES), lambda o: (0, 0, o)),
        out_shape=jax.ShapeDtypeStruct((t_n, rows, d), BF16),
        scratch_shapes=[pltpu.VMEM((rows, w1.shape[2]), F32),
                        pltpu.VMEM((4, slab_rows, LANES), F32),
                        pltpu.VMEM((4, slab_rows, LANES), F32),
                        pltpu.VMEM((rows, width), BF16),
                        pltpu.VMEM((rows, 4 * LANES), BF16)],
        compiler_params=_cparams("parallel"),
        name="s5_core",
    )(u_t, w1, w2t, pw, aa)


def _s5_out_kernel(y_ref, h_ref, wglu_ref, wout_ref, g_ref, o_ref):
    g = _gelu_tanh(y_ref[0].astype(F32))
    z = jnp.dot(g.astype(BF16), wglu_ref[...], preferred_element_type=F32)
    g2 = g * _sigmoid(z)
    mix = jnp.dot(g2.astype(BF16), wout_ref[...], preferred_element_type=F32)
    o_ref[...] = h_ref[...] + _rms(mix, g_ref[...])


def _s5_out(y_t, h, w_glu, w_out, gain, rc):
    n, d = h.shape
    chunks = n // S5_CHUNK
    out = pl.pallas_call(
        _s5_out_kernel,
        grid=(chunks // rc, S5_CHUNK),
        in_specs=[pl.BlockSpec((1, rc, d), lambda i, t: (t, i, 0)),
                  pl.BlockSpec((rc, d), lambda i, t: (i, t)),
                  pl.BlockSpec((d, d), lambda i, t: (0, 0)),
                  pl.BlockSpec((d, d), lambda i, t: (0, 0)),
                  pl.BlockSpec((1, d), lambda i, t: (0, 0))],
        out_specs=pl.BlockSpec((rc, d), lambda i, t: (i, t)),
        out_shape=jax.ShapeDtypeStruct((chunks, S5_CHUNK * d), F32),
        compiler_params=_cparams("parallel", "parallel"),
        name="s5_out",
    )(y_t, h.reshape(chunks, S5_CHUNK * d), w_glu, w_out, gain.reshape(1, d))
    return out.reshape(n, d)


def _ffn_kernel(h_ref, gpre_ref, gpost_ref, wg_ref, wu_ref, wd_ref, o_ref, hn_s, acc_s):
    f = pl.program_id(1)

    @pl.when(f == 0)
    def _():
        hn_s[...] = _rms(h_ref[...], gpre_ref[...]).astype(BF16)
        acc_s[...] = jnp.zeros_like(acc_s)

    hn = hn_s[...]
    a = jnp.dot(hn, wg_ref[...], preferred_element_type=F32)
    u = jnp.dot(hn, wu_ref[...], preferred_element_type=F32)
    act = a * _sigmoid(a) * u
    acc_s[...] += jnp.dot(act.astype(BF16), wd_ref[...], preferred_element_type=F32)

    @pl.when(f == pl.num_programs(1) - 1)
    def _():
        o_ref[...] = h_ref[...] + _rms(acc_s[...], gpost_ref[...])


def _ffn(h, gain_pre, gain_post, w_gate, w_up, w_down, tm, tf):
    n, d = h.shape
    d_ff = w_gate.shape[1]
    return pl.pallas_call(
        _ffn_kernel,
        grid=(n // tm, d_ff // tf),
        in_specs=[pl.BlockSpec((tm, d), lambda i, f: (i, 0)),
                  pl.BlockSpec((1, d), lambda i, f: (0, 0)),
                  pl.BlockSpec((1, d), lambda i, f: (0, 0)),
                  pl.BlockSpec((d, tf), lambda i, f: (0, f)),
                  pl.BlockSpec((d, tf), lambda i, f: (0, f)),
                  pl.BlockSpec((tf, d), lambda i, f: (f, 0))],
        out_specs=pl.BlockSpec((tm, d), lambda i, f: (i, 0)),
        out_shape=jax.ShapeDtypeStruct((n, d), F32),
        scratch_shapes=[pltpu.VMEM((tm, d), BF16), pltpu.VMEM((tm, d), F32)],
        compiler_params=_cparams("parallel", "arbitrary"),
        name="ffn",
    )(h, gain_pre.reshape(1, d), gain_post.reshape(1, d), w_gate, w_up, w_down)


def _moe_kernel(h_ref, gpre_ref, gpost_ref, wr_ref, wg_ref, wu_ref, wd_ref, o_ref,
                hn_s, gate_s, sel_s, pos_s, post_s, xc_s, yc_s, xg_s, col_s, cnt_s,
                *, n_exp, rt, strip):
    e = pl.program_id(1)
    f = pl.program_id(2)
    tb, d = hn_s.shape
    n_f = pl.num_programs(2)
    n_strips = tb // strip

    def put_counts(r, total):
        lane1 = lax.broadcasted_iota(jnp.int32, total.shape, 1)
        for x in range(n_exp):
            cnt_s[r * n_exp + x] = jnp.sum(jnp.where(lane1 == x, total, 0.0)).astype(jnp.int32)

    @pl.when((e == 0) & (f == 0))
    def _():
        def route(r, total):
            put_counts(r, total)
            rows = pl.ds(pl.multiple_of(r * strip, strip), strip)
            xn = _rms(h_ref[rows, :], gpre_ref[...])
            hn_s[rows, :] = xn.astype(BF16)
            o_ref[rows, :] = jnp.zeros((strip, d), F32)
            logits = jnp.dot(xn, wr_ref[...], precision=lax.Precision.HIGHEST,
                             preferred_element_type=F32)
            lane = lax.broadcasted_iota(jnp.int32, logits.shape, 1)
            neg = jnp.float32(-jnp.inf)
            logits = jnp.where(lane < n_exp, logits, neg)
            m1 = jnp.max(logits, axis=-1, keepdims=True)
            i1 = jnp.min(jnp.where(logits == m1, lane, LANES), axis=-1, keepdims=True)
            rest = jnp.where(lane == i1, neg, logits)
            m2 = jnp.max(rest, axis=-1, keepdims=True)
            i2 = jnp.min(jnp.where(rest == m2, lane, LANES), axis=-1, keepdims=True)
            e2 = jnp.exp(m2 - m1)
            gate_s[rows, :] = (jnp.where(lane == i1, 1.0 / (1.0 + e2), 0.0)
                               + jnp.where(lane == i2, e2 / (1.0 + e2), 0.0))
            sel = ((lane == i1) | (lane == i2)).astype(F32)
            sel_s[rows, :] = sel.astype(BF16)
            return total + jnp.sum(sel, axis=0, keepdims=True)

        total = lax.fori_loop(0, n_strips, route, jnp.zeros((1, LANES), F32))
        put_counts(n_strips, total)

        def rank(r, _):
            r0 = pl.multiple_of(r * strip, strip)
            rows = lax.broadcasted_iota(jnp.int32, (strip, tb), 0) + r0
            cols = lax.broadcasted_iota(jnp.int32, (strip, tb), 1)
            before = (cols < rows).astype(BF16)
            cnt = jnp.dot(before, sel_s[...], preferred_element_type=F32)
            chosen = sel_s[pl.ds(r0, strip), :] > 0
            pos_s[pl.ds(r0, strip), :] = jnp.where(chosen, cnt, -1.0)
            return 0

        lax.fori_loop(0, tb // strip, rank, 0)
        for r in range(tb // strip):
            post_s[:, r * strip:(r + 1) * strip] = pos_s[r * strip:(r + 1) * strip, :].T

    @pl.when(f == n_f - 1)
    def _():
        lane = lax.broadcasted_iota(jnp.int32, (tb, LANES), 1)
        col_s[0] = jnp.sum(jnp.where(lane == e, pos_s[...], 0.0), axis=-1, keepdims=True)
        col_s[1] = jnp.sum(jnp.where(lane == e, gate_s[...], 0.0), axis=-1, keepdims=True)

    n_tiles = (cnt_s[n_strips * n_exp + e] + rt - 1) // rt

    def tile(i, _):
        r0 = pl.multiple_of(i * rt, 16)

        def holds(r):
            return ((cnt_s[r * n_exp + e] < r0 + rt) & (cnt_s[(r + 1) * n_exp + e] > r0))

        @pl.when(f == 0)
        def _():
            xg_s[...] = jnp.zeros_like(xg_s)
            slot = (lax.broadcasted_iota(jnp.int32, (rt, strip), 0) + r0).astype(F32)
            for r in range(n_strips):
                @pl.when(holds(r))
                def _():
                    cols = slice(r * strip, (r + 1) * strip)
                    pick = (post_s[pl.ds(e, 1), cols] == slot).astype(BF16)
                    xg_s[...] += jnp.dot(pick, hn_s[cols, :], preferred_element_type=F32)
            xc_s[pl.ds(r0, rt), :] = xg_s[...].astype(BF16)

        xc = xc_s[pl.ds(r0, rt), :]
        a = jnp.dot(xc, wg_ref[0], preferred_element_type=F32)
        u = jnp.dot(xc, wu_ref[0], preferred_element_type=F32)
        act = (a * _sigmoid(a) * u).astype(BF16)
        y = jnp.dot(act, wd_ref[0], preferred_element_type=F32)
        prev = jnp.where(f == 0, 0.0, yc_s[pl.ds(r0, rt), :])
        yc_s[pl.ds(r0, rt), :] = prev + y

        @pl.when(f == n_f - 1)
        def _():
            yc = yc_s[pl.ds(r0, rt), :].astype(BF16)
            slot = (lax.broadcasted_iota(jnp.int32, (strip, rt), 1) + r0).astype(F32)
            for r in range(n_strips):
                @pl.when(holds(r))
                def _():
                    rows = slice(r * strip, (r + 1) * strip)
                    put = (col_s[0, rows, :] == slot).astype(BF16)
                    back = jnp.dot(put, yc, preferred_element_type=F32)
                    o_ref[rows, :] += col_s[1, rows, :] * back
        return 0

    lax.fori_loop(0, n_tiles, tile, 0)

    @pl.when((e == n_exp - 1) & (f == n_f - 1))
    def _():
        def finish(r, _):
            rows = pl.ds(pl.multiple_of(r * strip, strip), strip)
            o_ref[rows, :] = h_ref[rows, :] + _rms(o_ref[rows, :], gpost_ref[...])
            return 0

        lax.fori_loop(0, tb // strip, finish, 0)


def _moe(h, gain_pre, gain_post, w_router, w_gate, w_up, w_down, tb, tf, rt):
    n, d = h.shape
    n_exp, _, d_ff = w_gate.shape
    w_r = jnp.pad(w_router, ((0, 0), (0, LANES - n_exp)))
    once = pl.Buffered(1)
    strip = 256
    cap = pl.cdiv(tb, rt) * rt
    return pl.pallas_call(
        functools.partial(_moe_kernel, n_exp=n_exp, rt=rt, strip=strip),
        grid=(n // tb, n_exp, d_ff // tf),
        in_specs=[pl.BlockSpec((tb, d), lambda i, e, f: (i, 0), pipeline_mode=once),
                  pl.BlockSpec((1, d), lambda i, e, f: (0, 0)),
                  pl.BlockSpec((1, d), lambda i, e, f: (0, 0)),
                  pl.BlockSpec((d, LANES), lambda i, e, f: (0, 0)),
                  pl.BlockSpec((1, d, tf), lambda i, e, f: (e, 0, f)),
                  pl.BlockSpec((1, d, tf), lambda i, e, f: (e, 0, f)),
                  pl.BlockSpec((1, tf, d), lambda i, e, f: (e, f, 0))],
        out_specs=pl.BlockSpec((tb, d), lambda i, e, f: (i, 0), pipeline_mode=once),
        out_shape=jax.ShapeDtypeStruct((n, d), F32),
        scratch_shapes=[pltpu.VMEM((tb, d), BF16),
                        pltpu.VMEM((tb, LANES), F32),
                        pltpu.VMEM((tb, LANES), BF16),
                        pltpu.VMEM((tb, LANES), F32),
                        pltpu.VMEM((LANES, tb), F32),
                        pltpu.VMEM((cap, d), BF16),
                        pltpu.VMEM((cap, d), F32),
                        pltpu.VMEM((rt, d), F32),
                        pltpu.VMEM((2, tb, 1), F32),
                        pltpu.SMEM(((tb // strip + 1) * n_exp,), jnp.int32)],
        compiler_params=pltpu.CompilerParams(
            dimension_semantics=("parallel", "arbitrary", "arbitrary"),
            vmem_limit_bytes=MOE_VMEM_LIMIT_BYTES),
        name="moe_ffn",
    )(h, gain_pre.reshape(1, d), gain_post.reshape(1, d), w_r, w_gate, w_up, w_down)


def _qkv_kernel(h_ref, g_ref, w_ref, qg_ref, kg_ref, cs_ref, sn_ref, bd_ref,
                q_ref, k_ref, v_ref):
    xn = _rms(h_ref[...], g_ref[...]).astype(BF16)
    qkv = jnp.dot(xn, w_ref[...], preferred_element_type=F32)
    cs = cs_ref[...]
    sn = sn_ref[...]
    bd = bd_ref[...]
    lane = lax.broadcasted_iota(jnp.int32, cs.shape, 1)
    first_half = (lane % HEAD_DIM) < (HEAD_DIM // 2)
    scale = math.log2(math.e) / math.sqrt(HEAD_DIM)

    def norm_rope(x, gain):
        ms = jnp.dot(x * x, bd, precision=lax.Precision.HIGHEST, preferred_element_type=F32)
        y = x * lax.rsqrt(ms + NORM_EPS) * gain
        partner = jnp.where(first_half,
                            pltpu.roll(y, LANES - HEAD_DIM // 2, axis=1),
                            pltpu.roll(y, HEAD_DIM // 2, axis=1))
        return y * cs + partner * sn

    n_q_tiles = N_HEADS * HEAD_DIM // LANES
    for t in range(n_q_tiles):
        y = norm_rope(qkv[:, t * LANES:(t + 1) * LANES], qg_ref[...]) * scale
        q_ref[0, 2 * t] = y[:, 0:HEAD_DIM].astype(BF16)
        q_ref[0, 2 * t + 1] = y[:, HEAD_DIM:LANES].astype(BF16)
    k0 = N_HEADS * HEAD_DIM
    for t in range(N_KV_HEADS * HEAD_DIM // LANES):
        y = norm_rope(qkv[:, k0 + t * LANES:k0 + (t + 1) * LANES], kg_ref[...])
        k_ref[0, 2 * t] = y[:, 0:HEAD_DIM].astype(BF16)
        k_ref[0, 2 * t + 1] = y[:, HEAD_DIM:LANES].astype(BF16)
    v0 = (N_HEADS + N_KV_HEADS) * HEAD_DIM
    ones = jnp.ones((qkv.shape[0], LANES - HEAD_DIM), BF16)
    for j in range(N_KV_HEADS):
        vj = qkv[:, v0 + j * HEAD_DIM:v0 + (j + 1) * HEAD_DIM].astype(BF16)
        v_ref[0, j] = jnp.concatenate([vj, ones], axis=-1)


def _rope_tables(seq):
    axis_dim = HEAD_DIM // 2
    freqs = ROPE_THETA ** (-jnp.arange(0, axis_dim, 2, dtype=F32) / axis_dim)
    rows = seq // GRID_W
    row_ang = jnp.arange(rows, dtype=F32)[:, None] * freqs
    col_ang = jnp.arange(GRID_W, dtype=F32)[:, None] * freqs
    ang = jnp.concatenate([
        jnp.broadcast_to(row_ang[:, None, :], (rows, GRID_W, freqs.shape[0])),
        jnp.broadcast_to(col_ang[None, :, :], (rows, GRID_W, freqs.shape[0]))], axis=-1)
    ang = ang.reshape(seq, HEAD_DIM // 2)
    cos, sin = jnp.cos(ang), jnp.sin(ang)
    cs = jnp.tile(jnp.concatenate([cos, cos], axis=-1), (1, LANES // HEAD_DIM))
    sn = jnp.tile(jnp.concatenate([-sin, sin], axis=-1), (1, LANES // HEAD_DIM))
    return cs, sn


def _qkv(h, gain, w_qkv, q_gain, k_gain, bsz, seq, tm):
    n, d = h.shape
    width = w_qkv.shape[1]
    perm = jnp.concatenate([jnp.arange(0, HEAD_DIM, 2), jnp.arange(1, HEAD_DIM, 2)])
    n_rot = N_HEADS + N_KV_HEADS
    cols = (jnp.arange(n_rot)[:, None] * HEAD_DIM + perm[None, :]).reshape(-1)
    cols = jnp.concatenate([cols, jnp.arange(n_rot * HEAD_DIM, width)])
    w = w_qkv[:, cols].astype(BF16)
    qg = jnp.tile(q_gain[perm], LANES // HEAD_DIM).reshape(1, LANES)
    kg = jnp.tile(k_gain[perm], LANES // HEAD_DIM).reshape(1, LANES)
    cs, sn = _rope_tables(seq)
    blk = jnp.arange(LANES) // HEAD_DIM
    bd = (blk[:, None] == blk[None, :]).astype(F32) / HEAD_DIM
    per_seq = seq // tm
    return pl.pallas_call(
        _qkv_kernel,
        grid=(n // tm,),
        in_specs=[pl.BlockSpec((tm, d), lambda i: (i, 0)),
                  pl.BlockSpec((1, d), lambda i: (0, 0)),
                  pl.BlockSpec((d, width), lambda i: (0, 0)),
                  pl.BlockSpec((1, LANES), lambda i: (0, 0)),
                  pl.BlockSpec((1, LANES), lambda i: (0, 0)),
                  pl.BlockSpec((tm, LANES), lambda i: (i % per_seq, 0)),
                  pl.BlockSpec((tm, LANES), lambda i: (i % per_seq, 0)),
                  pl.BlockSpec((LANES, LANES), lambda i: (0, 0))],
        out_specs=[pl.BlockSpec((1, N_HEADS, tm, HEAD_DIM),
                                lambda i: (i // per_seq, 0, i % per_seq, 0)),
                   pl.BlockSpec((1, N_KV_HEADS, tm, HEAD_DIM),
                                lambda i: (i // per_seq, 0, i % per_seq, 0)),
                   pl.BlockSpec((1, N_KV_HEADS, tm, LANES),
                                lambda i: (i // per_seq, 0, i % per_seq, 0))],
        out_shape=[jax.ShapeDtypeStruct((bsz, N_HEADS, seq, HEAD_DIM), BF16),
                   jax.ShapeDtypeStruct((bsz, N_KV_HEADS, seq, HEAD_DIM), BF16),
                   jax.ShapeDtypeStruct((bsz, N_KV_HEADS, seq, LANES), BF16)],
        compiler_params=_cparams("parallel"),
        name="qkv_rope",
    )(h, gain.reshape(1, d), w, qg, kg, cs, sn, bd)


def _attn_kernel(q_ref, k_ref, v_ref, o_ref, m_s, acc_s, s_buf, p_buf, a_buf,
                 *, tq, tk, rc, unroll):
    seq = k_ref.shape[2]
    chunks_per_head = tq // rc
    n_chunks = Q_PER_KV * chunks_per_head
    n_steps = (seq // tk) * n_chunks

    m_s[...] = jnp.full_like(m_s, -jnp.inf)
    acc_s[...] = jnp.zeros_like(acc_s)

    def where(n):
        c = n % n_chunks
        return (pl.multiple_of((n // n_chunks) * tk, tk), c // chunks_per_head,
                pl.multiple_of((c % chunks_per_head) * rc, rc))

    def scores(n):
        k0, g, r0 = where(n)
        q = q_ref[0, g, pl.ds(r0, rc), :]
        s_buf[...] = lax.dot_general(q, k_ref[0, 0, pl.ds(k0, tk), :], _NT,
                                     preferred_element_type=F32)

    def softmax(n):
        _, g, r0 = where(n)
        s = s_buf[...]
        m_prev = m_s[g, pl.ds(r0, rc), :]
        m_new = jnp.maximum(m_prev, jnp.max(s, axis=-1, keepdims=True))
        a_buf[...] = jnp.exp2(m_prev - m_new)
        for t in range(tk // LANES):
            p_buf[:, t * LANES:(t + 1) * LANES] = jnp.exp2(
                s[:, t * LANES:(t + 1) * LANES] - m_new).astype(BF16)
        m_s[g, pl.ds(r0, rc), :] = m_new

    def values(n):
        k0, g, r0 = where(n)
        pv = jnp.dot(p_buf[...], v_ref[0, 0, pl.ds(k0, tk), :],
                     preferred_element_type=F32)
        acc_s[g, pl.ds(r0, rc), :] = a_buf[...] * acc_s[g, pl.ds(r0, rc), :] + pv

    scores(0)
    softmax(0)
    scores(1)

    def body(n, _):
        values(n)
        softmax(n + 1)
        scores(n + 2)
        return 0

    lax.fori_loop(0, n_steps - 2, body, 0, unroll=unroll)
    values(n_steps - 2)
    softmax(n_steps - 1)
    values(n_steps - 1)

    outs = []
    for g in range(Q_PER_KV):
        acc = acc_s[g]
        o = acc / pltpu.roll(acc, HEAD_DIM, axis=1)
        outs.append(o[:, 0:HEAD_DIM])
    o_ref[...] = jnp.concatenate(outs, axis=-1).astype(o_ref.dtype)


def _attention(q, k, v, tq, tk, rc):
    bsz, _, seq, _ = q.shape
    n_q = seq // tq
    n_steps = (seq // tk) * Q_PER_KV * (tq // rc)
    unroll = min(8, n_steps - 2)
    return pl.pallas_call(
        functools.partial(_attn_kernel, tq=tq, tk=tk, rc=rc, unroll=unroll),
        grid=(bsz, N_KV_HEADS, n_q),
        in_specs=[pl.BlockSpec((1, Q_PER_KV, tq, HEAD_DIM), lambda b, j, i: (b, j, i, 0)),
                  pl.BlockSpec((1, 1, seq, HEAD_DIM), lambda b, j, i: (b, j, 0, 0)),
                  pl.BlockSpec((1, 1, seq, LANES), lambda b, j, i: (b, j, 0, 0))],
        out_specs=pl.BlockSpec((tq, Q_PER_KV * HEAD_DIM), lambda b, j, i: (b * n_q + i, j)),
        out_shape=jax.ShapeDtypeStruct((bsz * seq, N_HEADS * HEAD_DIM), BF16),
        scratch_shapes=[pltpu.VMEM((Q_PER_KV, tq, LANES), F32),
                        pltpu.VMEM((Q_PER_KV, tq, LANES), F32),
                        pltpu.VMEM((rc, tk), F32),
                        pltpu.VMEM((rc, tk), BF16),
                        pltpu.VMEM((rc, LANES), F32)],
        compiler_params=_cparams("parallel", "parallel", "parallel"),
        name="flash_attn",
    )(q, k, v)


def _proj_res_kernel(x_ref, h_ref, w_ref, g_ref, o_ref):
    mix = jnp.dot(x_ref[...], w_ref[...], preferred_element_type=F32)
    o_ref[...] = h_ref[...] + _rms(mix, g_ref[...])


def _proj_res(x, h, w, gain, tm):
    n, d = h.shape
    k = x.shape[1]
    return pl.pallas_call(
        _proj_res_kernel,
        grid=(n // tm,),
        in_specs=[pl.BlockSpec((tm, k), lambda i: (i, 0)),
                  pl.BlockSpec((tm, d), lambda i: (i, 0)),
                  pl.BlockSpec((k, d), lambda i: (0, 0)),
                  pl.BlockSpec((1, d), lambda i: (0, 0))],
        out_specs=pl.BlockSpec((tm, d), lambda i: (i, 0)),
        out_shape=jax.ShapeDtypeStruct((n, d), F32),
        compiler_params=_cparams("parallel"),
        name="proj_res",
    )(x, h, w, gain.reshape(1, d))


def _s5_layer(h, bsz, seq, gains, w_in, lam_re, lam_im, log_step, b_re, b_im, c_re, c_im,
              d_skip, w_glu, w_out):
    seg_tokens = S5_CHUNK * S5_SEG_CHUNKS
    nseg = seq // seg_tokens
    nseq = bsz * nseg
    rc = min(512, h.shape[0] // S5_CHUNK)
    u_t = _norm_matmul(h, gains[0], w_in.astype(BF16), rc=rc)
    w1, w2t, pw_p, aa_p = _s5_prep(lam_re, lam_im, log_step, b_re, b_im, c_re, c_im, d_skip)
    y_t = _s5_core(u_t, w1, w2t, pw_p, aa_p, nseq, nseg)
    return _s5_out(y_t, h, w_glu.astype(BF16), w_out.astype(BF16), gains[1], rc=rc)


def _attn_layer(h, bsz, seq, gains, w_qkv, q_gain, k_gain, w_out):
    q, k, v = _qkv(h, gains[0], w_qkv, q_gain, k_gain, bsz, seq, tm=512)
    o = _attention(q, k, v, tq=2048, tk=512, rc=512)
    return _proj_res(o, h, w_out.astype(BF16), gains[1], tm=512)


def kernel(x, norm_gains, ssm_w_in, ssm_lambda_re, ssm_lambda_im, ssm_log_step, ssm_b_re,
           ssm_b_im, ssm_c_re, ssm_c_im, ssm_d, ssm_w_glu, ssm_w_out, ffn_w_gate, ffn_w_up,
           ffn_w_down, attn_w_qkv, attn_q_gain, attn_k_gain, attn_w_out, moe_w_router,
           moe_w_gate, moe_w_up, moe_w_down):
    bsz, seq, d = x.shape
    depth = norm_gains.shape[0]
    h = x.reshape(bsz * seq, d)
    for i in range(depth):
        j = i // 2
        g = norm_gains[i]
        if i % 2 == 0:
            h = _s5_layer(h, bsz, seq, g, ssm_w_in[j], ssm_lambda_re[j], ssm_lambda_im[j],
                          ssm_log_step[j], ssm_b_re[j], ssm_b_im[j], ssm_c_re[j], ssm_c_im[j],
                          ssm_d[j], ssm_w_glu[j], ssm_w_out[j])
            h = _ffn(h, g[2], g[3], ffn_w_gate[j].astype(BF16), ffn_w_up[j].astype(BF16),
                     ffn_w_down[j].astype(BF16), tm=512, tf=1408)
        else:
            h = _attn_layer(h, bsz, seq, g, attn_w_qkv[j], attn_q_gain[j], attn_k_gain[j],
                            attn_w_out[j])
            h = _moe(h, g[2], g[3], moe_w_router[j], moe_w_gate[j].astype(BF16),
                     moe_w_up[j].astype(BF16), moe_w_down[j].astype(BF16),
                     tb=2048, tf=512, rt=272)
    return h.reshape(bsz, seq, d)
```

```python
import functools
import math

import jax
import jax.numpy as jnp
from jax import lax
from jax.experimental import pallas as pl
from jax.experimental.pallas import tpu as pltpu

F32 = jnp.float32
BF16 = jnp.bfloat16
NORM_EPS = 1e-6
ROPE_THETA = 10000.0
GRID_W = 64
N_HEADS = 16
N_KV_HEADS = 4
HEAD_DIM = 64
Q_PER_KV = N_HEADS // N_KV_HEADS
SSM_GROUP = 16
SSM_STATE = 64
S5_CHUNK = 16
S5_SEG_CHUNKS = 64
TOP_K = 2
LANES = 128
VMEM_LIMIT_BYTES = 56 * 1024 * 1024
MOE_VMEM_LIMIT_BYTES = 60 * 1024 * 1024

_NT = (((1,), (1,)), ((), ()))


def _cparams(*sem):
    return pltpu.CompilerParams(dimension_semantics=sem, vmem_limit_bytes=VMEM_LIMIT_BYTES)


def _rms(x, gain):
    return x * lax.rsqrt(jnp.mean(x * x, axis=-1, keepdims=True) + NORM_EPS) * gain


def _sigmoid(x):
    return 1.0 / (1.0 + jnp.exp(-x))


def _gelu_tanh(x):
    return x * (0.5 * (1.0 + jnp.tanh(math.sqrt(2.0 / math.pi) * (x + 0.044715 * (x * x * x)))))


def _norm_matmul_kernel(x_ref, g_ref, w_ref, o_ref):
    xn = _rms(x_ref[...], g_ref[...]).astype(BF16)
    o_ref[0] = jnp.dot(xn, w_ref[...], preferred_element_type=F32).astype(o_ref.dtype)


def _norm_matmul(x, gain, w, rc):
    n, d = x.shape
    m = w.shape[1]
    chunks = n // S5_CHUNK
    return pl.pallas_call(
        _norm_matmul_kernel,
        grid=(chunks // rc, S5_CHUNK),
        in_specs=[pl.BlockSpec((rc, d), lambda i, t: (i, t)),
                  pl.BlockSpec((1, d), lambda i, t: (0, 0)),
                  pl.BlockSpec((d, m), lambda i, t: (0, 0))],
        out_specs=pl.BlockSpec((1, rc, m), lambda i, t: (t, i, 0)),
        out_shape=jax.ShapeDtypeStruct((S5_CHUNK, chunks, m), BF16),
        compiler_params=_cparams("parallel", "parallel"),
        name="norm_matmul",
    )(x.reshape(chunks, S5_CHUNK * d), gain.reshape(1, d), w)


def _s5_prep_kernel(lr_ref, li_ref, ls_ref, bt_re_ref, bt_im_ref, c_re_ref, c_im_ref, dt_ref,
                    w1_ref, w2t_ref, pw_ref, aa_ref):
    t_n, s_n, p_n = S5_CHUNK, SSM_GROUP, SSM_STATE
    pair = 2 * s_n
    rows = t_n * pair
    hi = lax.Precision.HIGHEST

    rr = lax.broadcasted_iota(jnp.int32, (rows, rows), 0)
    cc = lax.broadcasted_iota(jnp.int32, (rows, rows), 1)
    same_group = ((rr // s_n) % 2) == ((cc // s_n) % 2)
    tvec = lax.broadcasted_iota(jnp.int32, (t_n, 1), 0).astype(F32)
    jvec = lax.broadcasted_iota(jnp.int32, (S5_SEG_CHUNKS, 1), 0).astype(F32)
    row8 = lax.broadcasted_iota(jnp.int32, (8, LANES), 0)
    own_lanes = ((lax.broadcasted_iota(jnp.int32, (rows, LANES), 1) // p_n)
                 == ((lax.broadcasted_iota(jnp.int32, (rows, LANES), 0) // s_n) % 2))

    def lanes_by_group(tab0, tab1):
        lane = lax.broadcasted_iota(jnp.int32, tab0.shape, 1)
        return jnp.where(lane < p_n, tab0, tab1)

    m_total = None
    for d in range(2):
        lsr, lsi, q_re, q_im = [], [], [], []
        for g in range(2):
            lr = lr_ref[0, d, g:g + 1, :]
            li = li_ref[0, d, g:g + 1, :]
            step = jnp.exp(ls_ref[0, d, g:g + 1, :])
            lsr.append(lr * step)
            lsi.append(li * step)
            mag = jnp.exp(lsr[g])
            a_re, a_im = mag * jnp.cos(lsi[g]), mag * jnp.sin(lsi[g])
            nr, ni = a_re - 1.0, a_im
            den = lr * lr + li * li
            q_re.append((nr * lr + ni * li) / den)
            q_im.append((ni * lr - nr * li) / den)

        def cpow(k, g):
            mag = jnp.exp(lsr[g] * k)
            ang = lsi[g] * k
            return mag * jnp.cos(ang), mag * jnp.sin(ang)

        def table(k):
            t0, t1 = cpow(k, 0), cpow(k, 1)
            return tuple(
                jnp.concatenate([jnp.broadcast_to(tg[part][t:t + 1, :], (s_n, LANES))
                                 for t in range(t_n) for tg in (t0, t1)], axis=0)
                for part in range(2))

        br, bi = bt_re_ref[0, d], bt_im_ref[0, d]
        qr = jnp.concatenate([jnp.broadcast_to(q_re[g], (s_n, LANES)) for g in range(2)], axis=0)
        qi = jnp.concatenate([jnp.broadcast_to(q_im[g], (s_n, LANES)) for g in range(2)], axis=0)
        bb_re = qr * br - qi * bi
        bb_im = qr * bi + qi * br
        cr, ci = c_re_ref[0, d], c_im_ref[0, d]

        def outer(x_re, x_im, k):
            pe_re, pe_im = table(k)
            xe_re = jnp.concatenate([x_re] * t_n, axis=0)
            xe_im = jnp.concatenate([x_im] * t_n, axis=0)
            return xe_re * pe_re - xe_im * pe_im, xe_re * pe_im + xe_im * pe_re

        if d == 0:
            l_re, l_im = outer(bb_re, bb_im, -tvec)
            rt_re, rt_im = outer(cr, ci, tvec)
            ws_re, ws_im = outer(bb_re, bb_im, (t_n - 1.0) - tvec)
            wy_re, wy_im = outer(cr, ci, tvec + 1.0)
            mask = (rr // pair) <= (cc // pair)
            kseg = t_n * jvec
        else:
            l_re, l_im = outer(bb_re, bb_im, tvec)
            rt_re, rt_im = outer(cr, ci, -tvec)
            ws_re, ws_im = l_re, l_im
            wy_re, wy_im = outer(cr, ci, t_n - tvec)
            mask = (rr // pair) >= (cc // pair)
            kseg = t_n * ((S5_SEG_CHUNKS - 1.0) - jvec)
        kern = (lax.dot_general(l_re[:, 0:p_n], rt_re[:, 0:p_n], _NT, precision=hi,
                                preferred_element_type=F32)
                - lax.dot_general(l_im[:, 0:p_n], rt_im[:, 0:p_n], _NT, precision=hi,
                                  preferred_element_type=F32))
        kern = jnp.where(mask & same_group, kern, 0.0)
        m_total = kern if m_total is None else m_total + kern

        ycols = rows
        for k2, val in ((2 * d, ws_re), (2 * d + 1, ws_im)):
            w1_ref[0, :, ycols + k2 * LANES:ycols + (k2 + 1) * LANES] = jnp.where(
                own_lanes, val, 0.0).astype(w1_ref.dtype)
        for k2, val in ((2 * d, wy_re), (2 * d + 1, -wy_im)):
            w2t_ref[0, :, k2 * LANES:(k2 + 1) * LANES] = jnp.where(
                own_lanes, val, 0.0).astype(w2t_ref.dtype)
        p0, p1 = cpow(kseg, 0), cpow(kseg, 1)
        pw_ref[0, :, 2 * d * LANES:(2 * d + 1) * LANES] = lanes_by_group(p0[0], p1[0])
        pw_ref[0, :, (2 * d + 1) * LANES:(2 * d + 2) * LANES] = lanes_by_group(p0[1], p1[1])
        one = jnp.ones((1, 1), F32)
        c0, c1 = cpow(float(t_n) * one, 0), cpow(float(t_n) * one, 1)
        s0, s1 = (cpow(float(t_n * S5_SEG_CHUNKS) * one, 0),
                  cpow(float(t_n * S5_SEG_CHUNKS) * one, 1))
        for part in range(2):
            chunk_pow = lanes_by_group(c0[part], c1[part])
            seg_pow = lanes_by_group(s0[part], s1[part])
            aa_ref[0, :, (2 * d + part) * LANES:(2 * d + part + 1) * LANES] = jnp.where(
                row8 == 0, chunk_pow, jnp.where(row8 == 1, seg_pow, 0.0))

    w1_ref[0, :, 0:rows] = (m_total + jnp.where(rr == cc, dt_ref[0], 0.0)).astype(w1_ref.dtype)


def _s5_prep(lam_re, lam_im, log_step, b_re, b_im, c_re, c_im, d_skip):
    g_n = lam_re.shape[1]
    pairs = g_n // 2
    pair = 2 * SSM_GROUP
    rows = S5_CHUNK * pair

    def states(x):
        x = jnp.transpose(x.reshape(2, pairs, 2, SSM_STATE), (1, 0, 2, 3))
        return jnp.tile(x, (1, 1, 1, LANES // SSM_STATE))

    def per_channel(x):
        x = jnp.transpose(x.reshape(2, pairs, pair, SSM_STATE), (1, 0, 2, 3))
        return jnp.tile(x, (1, 1, 1, LANES // SSM_STATE))

    ls = jnp.transpose(log_step.reshape(2, pairs, 2), (1, 0, 2))[..., None]
    bt_re = per_channel(jnp.swapaxes(b_re, 2, 3))
    bt_im = per_channel(jnp.swapaxes(b_im, 2, 3))
    dt = jnp.tile(d_skip.reshape(pairs, 1, pair), (1, 1, S5_CHUNK))

    def spec(*blk):
        return pl.BlockSpec((1,) + blk, lambda g: (g,) + (0,) * len(blk))

    return pl.pallas_call(
        _s5_prep_kernel,
        grid=(pairs,),
        in_specs=[spec(2, 2, LANES), spec(2, 2, LANES), spec(2, 2, 1),
                  spec(2, pair, LANES), spec(2, pair, LANES),
                  spec(2, pair, LANES), spec(2, pair, LANES), spec(1, rows)],
        out_specs=[spec(rows, rows + 4 * LANES), spec(rows, 4 * LANES),
                   spec(S5_SEG_CHUNKS, 4 * LANES), spec(8, 4 * LANES)],
        out_shape=[jax.ShapeDtypeStruct((pairs, rows, rows + 4 * LANES), BF16),
                   jax.ShapeDtypeStruct((pairs, rows, 4 * LANES), BF16),
                   jax.ShapeDtypeStruct((pairs, S5_SEG_CHUNKS, 4 * LANES), F32),
                   jax.ShapeDtypeStruct((pairs, 8, 4 * LANES), F32)],
        compiler_params=_cparams("parallel"),
        name="s5_prep",
    )(states(lam_re), states(lam_im), ls, bt_re, bt_im, per_channel(c_re), per_channel(c_im), dt)


S5_PAIR_LANES = 2 * SSM_GROUP
S5_PAIRS_PER_SLAB = LANES // S5_PAIR_LANES
S5_SEG_PITCH = S5_SEG_CHUNKS + 8
S5_SEG_BASE = 8


def _s5_core_kernel(x_ref, w1_ref, w2t_ref, pw_ref, aa_ref, y_ref, z_s, zs_s, x_s, u_s, xb_s,
                    *, nseq, nseg, strip):
    n_chunks = S5_SEG_CHUNKS
    rows = n_chunks * nseq
    ycols = u_s.shape[1]
    per_tile = LANES // S5_PAIR_LANES
    lane_grp = lax.broadcasted_iota(jnp.int32, (strip, LANES), 1) // S5_PAIR_LANES
    zero = jnp.zeros((nseq, LANES), F32)

    def seg_rows(j):
        return pl.ds(S5_SEG_BASE + j, nseq, stride=S5_SEG_PITCH)

    for q in range(S5_PAIRS_PER_SLAB):
        def gather(r, _, q=q):
            rws = pl.ds(pl.multiple_of(r * strip, strip), strip)
            for j in range(ycols // LANES):
                acc = None
                for i in range(per_tile):
                    xt = x_ref[per_tile * j + i, rws, :].astype(F32)
                    shift = (S5_PAIR_LANES * (i - q)) % LANES
                    if shift:
                        xt = pltpu.roll(xt, shift, axis=1)
                    acc = xt if acc is None else jnp.where(lane_grp == i, xt, acc)
                u_s[rws, j * LANES:(j + 1) * LANES] = acc.astype(BF16)
            return 0

        lax.fori_loop(0, rows // strip, gather, 0)
        z_s[...] = jnp.dot(u_s[...], w1_ref[q], preferred_element_type=F32)

        for b in range(nseq):
            dst = slice(S5_SEG_BASE + b * S5_SEG_PITCH, S5_SEG_BASE + b * S5_SEG_PITCH + n_chunks)
            for k in range(4):
                zs_s[k, dst, :] = z_s[b * n_chunks:(b + 1) * n_chunks,
                                      ycols + k * LANES:ycols + (k + 1) * LANES]
        x_s[0, seg_rows(0), :] = zero
        x_s[1, seg_rows(0), :] = zero
        x_s[2, seg_rows(n_chunks - 1), :] = zero
        x_s[3, seg_rows(n_chunks - 1), :] = zero

        aa = aa_ref[q]
        a_fr, a_fi = aa[0:1, 0:LANES], aa[0:1, LANES:2 * LANES]
        a_br, a_bi = aa[0:1, 2 * LANES:3 * LANES], aa[0:1, 3 * LANES:4 * LANES]
        g_fr, g_fi = aa[1:2, 0:LANES], aa[1:2, LANES:2 * LANES]
        g_br, g_bi = aa[1:2, 2 * LANES:3 * LANES], aa[1:2, 3 * LANES:4 * LANES]

        def fwd(j, carry):
            xr, xi = carry
            nr = a_fr * xr - a_fi * xi + zs_s[0, seg_rows(j), :]
            ni = a_fr * xi + a_fi * xr + zs_s[1, seg_rows(j), :]
            x_s[0, seg_rows(j + 1), :] = nr
            x_s[1, seg_rows(j + 1), :] = ni
            return nr, ni

        def bwd(i, carry):
            xr, xi = carry
            j = n_chunks - 1 - i
            nr = a_br * xr - a_bi * xi + zs_s[2, seg_rows(j), :]
            ni = a_br * xi + a_bi * xr + zs_s[3, seg_rows(j), :]
            x_s[2, seg_rows(j - 1), :] = nr
            x_s[3, seg_rows(j - 1), :] = ni
            return nr, ni

        ef_r, ef_i = lax.fori_loop(0, n_chunks, fwd, (zero, zero))
        eb_r, eb_i = lax.fori_loop(0, n_chunks, bwd, (zero, zero))

        def seg_carries(e_r, e_i, g_r, g_i, reverse):
            out_r = [None] * nseq
            out_i = [None] * nseq
            for b in range(nseq // nseg):
                c_r = jnp.zeros((1, LANES), F32)
                c_i = jnp.zeros((1, LANES), F32)
                order = range(nseg - 1, -1, -1) if reverse else range(nseg)
                for s in order:
                    r = b * nseg + s
                    out_r[r], out_i[r] = c_r, c_i
                    n_r = g_r * c_r - g_i * c_i + e_r[r:r + 1, :]
                    n_i = g_r * c_i + g_i * c_r + e_i[r:r + 1, :]
                    c_r, c_i = n_r, n_i
            return out_r, out_i

        cf_r, cf_i = seg_carries(ef_r, ef_i, g_fr, g_fi, False)
        cb_r, cb_i = seg_carries(eb_r, eb_i, g_br, g_bi, True)

        p = pw_ref[q]
        p_fr, p_fi = p[:, 0:LANES], p[:, LANES:2 * LANES]
        p_br, p_bi = p[:, 2 * LANES:3 * LANES], p[:, 3 * LANES:4 * LANES]
        for b in range(nseq):
            src = slice(S5_SEG_BASE + b * S5_SEG_PITCH, S5_SEG_BASE + b * S5_SEG_PITCH + n_chunks)
            dst = slice(b * n_chunks, (b + 1) * n_chunks)
            xb_s[dst, 0:LANES] = (x_s[0, src, :] + (p_fr * cf_r[b] - p_fi * cf_i[b])).astype(BF16)
            xb_s[dst, LANES:2 * LANES] = (
                x_s[1, src, :] + (p_fr * cf_i[b] + p_fi * cf_r[b])).astype(BF16)
            xb_s[dst, 2 * LANES:3 * LANES] = (
                x_s[2, src, :] + (p_br * cb_r[b] - p_bi * cb_i[b])).astype(BF16)
            xb_s[dst, 3 * LANES:4 * LANES] = (
                x_s[3, src, :] + (p_br * cb_i[b] + p_bi * cb_r[b])).astype(BF16)

        z_s[:, 0:ycols] += lax.dot_general(xb_s[...], w2t_ref[q], _NT,
                                           preferred_element_type=F32)

        def scatter(r, _, q=q):
            rws = pl.ds(pl.multiple_of(r * strip, strip), strip)
            for j in range(ycols // LANES):
                yq = z_s[rws, j * LANES:(j + 1) * LANES]
                for i in range(per_tile):
                    shift = (S5_PAIR_LANES * (q - i)) % LANES
                    yt = pltpu.roll(yq, shift, axis=1) if shift else yq
                    lanes_q = slice(q * S5_PAIR_LANES, (q + 1) * S5_PAIR_LANES)
                    y_ref[per_tile * j + i, rws, lanes_q] = yt[:, lanes_q].astype(y_ref.dtype)
            return 0

        lax.fori_loop(0, rows // strip, scatter, 0)


def _s5_core(u_t, w1, w2t, pw, aa, nseq, nseg):
    t_n, rows, d = u_t.shape
    width = w1.shape[1]
    pps = S5_PAIRS_PER_SLAB
    slab_rows = S5_SEG_BASE + nseq * S5_SEG_PITCH
    return pl.pallas_call(
        functools.partial(_s5_core_kernel, nseq=nseq, nseg=nseg, strip=128),
        grid=(d // LANES,),
        in_specs=[pl.BlockSpec((t_n, rows, LANES), lambda o: (0, 0, o)),
                  pl.BlockSpec((pps, width, w1.shape[2]), lambda o: (o, 0, 0)),
                  pl.BlockSpec((pps, width, w2t.shape[2]), lambda o: (o, 0, 0)),
                  pl.BlockSpec((pps, S5_SEG_CHUNKS, pw.shape[2]), lambda o: (o, 0, 0)),
                  pl.BlockSpec((pps, 8, aa.shape[2]), lambda o: (o, 0, 0))],
        out_specs=pl.BlockSpec((t_n, rows, LANES), lambda o: (0, 0, o)),
        out_shape=jax.ShapeDtypeStruct((t_n, rows, d), BF16),
        scratch_shapes=[pltpu.VMEM((rows, w1.shape[2]), F32),
                        pltpu.VMEM((4, slab_rows, LANES), F32),
                        pltpu.VMEM((4, slab_rows, LANES), F32),
                        pltpu.VMEM((rows, width), BF16),
                        pltpu.VMEM((rows, 4 * LANES), BF16)],
        compiler_params=_cparams("parallel"),
        name="s5_core",
    )(u_t, w1, w2t, pw, aa)


def _s5_out_kernel(y_ref, h_ref, wglu_ref, wout_ref, g_ref, o_ref):
    g = _gelu_tanh(y_ref[0].astype(F32))
    z = jnp.dot(g.astype(BF16), wglu_ref[...], preferred_element_type=F32)
    g2 = g * _sigmoid(z)
    mix = jnp.dot(g2.astype(BF16), wout_ref[...], preferred_element_type=F32)
    o_ref[...] = h_ref[...] + _rms(mix, g_ref[...])


def _s5_out(y_t, h, w_glu, w_out, gain, rc):
    n, d = h.shape
    chunks = n // S5_CHUNK
    out = pl.pallas_call(
        _s5_out_kernel,
        grid=(chunks // rc, S5_CHUNK),
        in_specs=[pl.BlockSpec((1, rc, d), lambda i, t: (t, i, 0)),
                  pl.BlockSpec((rc, d), lambda i, t: (i, t)),
                  pl.BlockSpec((d, d), lambda i, t: (0, 0)),
                  pl.BlockSpec((d, d), lambda i, t: (0, 0)),
                  pl.BlockSpec((1, d), lambda i, t: (0, 0))],
        out_specs=pl.BlockSpec((rc, d), lambda i, t: (i, t)),
        out_shape=jax.ShapeDtypeStruct((chunks, S5_CHUNK * d), F32),
        compiler_params=_cparams("parallel", "parallel"),
        name="s5_out",
    )(y_t, h.reshape(chunks, S5_CHUNK * d), w_glu, w_out, gain.reshape(1, d))
    return out.reshape(n, d)


def _ffn_kernel(h_ref, gpre_ref, gpost_ref, wg_ref, wu_ref, wd_ref, o_ref, hn_s, acc_s):
    f = pl.program_id(1)

    @pl.when(f == 0)
    def _():
        hn_s[...] = _rms(h_ref[...], gpre_ref[...]).astype(BF16)
        acc_s[...] = jnp.zeros_like(acc_s)

    hn = hn_s[...]
    a = jnp.dot(hn, wg_ref[...], preferred_element_type=F32)
    u = jnp.dot(hn, wu_ref[...], preferred_element_type=F32)
    act = a * _sigmoid(a) * u
    acc_s[...] += jnp.dot(act.astype(BF16), wd_ref[...], preferred_element_type=F32)

    @pl.when(f == pl.num_programs(1) - 1)
    def _():
        o_ref[...] = h_ref[...] + _rms(acc_s[...], gpost_ref[...])


def _ffn(h, gain_pre, gain_post, w_gate, w_up, w_down, tm, tf):
    n, d = h.shape
    d_ff = w_gate.shape[1]
    return pl.pallas_call(
        _ffn_kernel,
        grid=(n // tm, d_ff // tf),
        in_specs=[pl.BlockSpec((tm, d), lambda i, f: (i, 0)),
                  pl.BlockSpec((1, d), lambda i, f: (0, 0)),
                  pl.BlockSpec((1, d), lambda i, f: (0, 0)),
                  pl.BlockSpec((d, tf), lambda i, f: (0, f)),
                  pl.BlockSpec((d, tf), lambda i, f: (0, f)),
                  pl.BlockSpec((tf, d), lambda i, f: (f, 0))],
        out_specs=pl.BlockSpec((tm, d), lambda i, f: (i, 0)),
        out_shape=jax.ShapeDtypeStruct((n, d), F32),
        scratch_shapes=[pltpu.VMEM((tm, d), BF16), pltpu.VMEM((tm, d), F32)],
        compiler_params=_cparams("parallel", "arbitrary"),
        name="ffn",
    )(h, gain_pre.reshape(1, d), gain_post.reshape(1, d), w_gate, w_up, w_down)


def _moe_kernel(h_ref, gpre_ref, gpost_ref, wr_ref, wg_ref, wu_ref, wd_ref, o_ref,
                hn_s, gate_s, sel_s, pos_s, post_s, xc_s, yc_s, xg_s, col_s, cnt_s,
                *, n_exp, rt, strip):
    e = pl.program_id(1)
    f = pl.program_id(2)
    tb, d = hn_s.shape
    n_f = pl.num_programs(2)
    n_strips = tb // strip

    def put_counts(r, total):
        lane1 = lax.broadcasted_iota(jnp.int32, total.shape, 1)
        for x in range(n_exp):
            cnt_s[r * n_exp + x] = jnp.sum(jnp.where(lane1 == x, total, 0.0)).astype(jnp.int32)

    @pl.when((e == 0) & (f == 0))
    def _():
        def route(r, total):
            put_counts(r, total)
            rows = pl.ds(pl.multiple_of(r * strip, strip), strip)
            xn = _rms(h_ref[rows, :], gpre_ref[...])
            hn_s[rows, :] = xn.astype(BF16)
            o_ref[rows, :] = jnp.zeros((strip, d), F32)
            logits = jnp.dot(xn, wr_ref[...], precision=lax.Precision.HIGHEST,
                             preferred_element_type=F32)
            lane = lax.broadcasted_iota(jnp.int32, logits.shape, 1)
            neg = jnp.float32(-jnp.inf)
            logits = jnp.where(lane < n_exp, logits, neg)
            m1 = jnp.max(logits, axis=-1, keepdims=True)
            i1 = jnp.min(jnp.where(logits == m1, lane, LANES), axis=-1, keepdims=True)
            rest = jnp.where(lane == i1, neg, logits)
            m2 = jnp.max(rest, axis=-1, keepdims=True)
            i2 = jnp.min(jnp.where(rest == m2, lane, LANES), axis=-1, keepdims=True)
            e2 = jnp.exp(m2 - m1)
            gate_s[rows, :] = (jnp.where(lane == i1, 1.0 / (1.0 + e2), 0.0)
                               + jnp.where(lane == i2, e2 / (1.0 + e2), 0.0))
            sel = ((lane == i1) | (lane == i2)).astype(F32)
            sel_s[rows, :] = sel.astype(BF16)
            return total + jnp.sum(sel, axis=0, keepdims=True)

        total = lax.fori_loop(0, n_strips, route, jnp.zeros((1, LANES), F32))
        put_counts(n_strips, total)

        def rank(r, _):
            r0 = pl.multiple_of(r * strip, strip)
            rows = lax.broadcasted_iota(jnp.int32, (strip, tb), 0) + r0
            cols = lax.broadcasted_iota(jnp.int32, (strip, tb), 1)
            before = (cols < rows).astype(BF16)
            cnt = jnp.dot(before, sel_s[...], preferred_element_type=F32)
            chosen = sel_s[pl.ds(r0, strip), :] > 0
            pos_s[pl.ds(r0, strip), :] = jnp.where(chosen, cnt, -1.0)
            return 0

        lax.fori_loop(0, tb // strip, rank, 0)
        for r in range(tb // strip):
            post_s[:, r * strip:(r + 1) * strip] = pos_s[r * strip:(r + 1) * strip, :].T

    @pl.when(f == n_f - 1)
    def _():
        lane = lax.broadcasted_iota(jnp.int32, (tb, LANES), 1)
        col_s[0] = jnp.sum(jnp.where(lane == e, pos_s[...], 0.0), axis=-1, keepdims=True)
        col_s[1] = jnp.sum(jnp.where(lane == e, gate_s[...], 0.0), axis=-1, keepdims=True)

    n_tiles = (cnt_s[n_strips * n_exp + e] + rt - 1) // rt

    def tile(i, _):
        r0 = pl.multiple_of(i * rt, 16)

        def holds(r):
            return ((cnt_s[r * n_exp + e] < r0 + rt) & (cnt_s[(r + 1) * n_exp + e] > r0))

        @pl.when(f == 0)
        def _():
            xg_s[...] = jnp.zeros_like(xg_s)
            slot = (lax.broadcasted_iota(jnp.int32, (rt, strip), 0) + r0).astype(F32)
            for r in range(n_strips):
                @pl.when(holds(r))
                def _():
                    cols = slice(r * strip, (r + 1) * strip)
                    pick = (post_s[pl.ds(e, 1), cols] == slot).astype(BF16)
                    xg_s[...] += jnp.dot(pick, hn_s[cols, :], preferred_element_type=F32)
            xc_s[pl.ds(r0, rt), :] = xg_s[...].astype(BF16)

        xc = xc_s[pl.ds(r0, rt), :]
        a = jnp.dot(xc, wg_ref[0, 0], preferred_element_type=F32)
        u = jnp.dot(xc, wu_ref[0, 0], preferred_element_type=F32)
        act = (a * _sigmoid(a) * u).astype(BF16)
        y = jnp.dot(act, wd_ref[0], preferred_element_type=F32)
        prev = jnp.where(f == 0, 0.0, yc_s[pl.ds(r0, rt), :])
        yc_s[pl.ds(r0, rt), :] = prev + y

        @pl.when(f == n_f - 1)
        def _():
            yc = yc_s[pl.ds(r0, rt), :].astype(BF16)
            slot = (lax.broadcasted_iota(jnp.int32, (strip, rt), 1) + r0).astype(F32)
            for r in range(n_strips):
                @pl.when(holds(r))
                def _():
                    rows = slice(r * strip, (r + 1) * strip)
                    put = (col_s[0, rows, :] == slot).astype(BF16)
                    back = jnp.dot(put, yc, preferred_element_type=F32)
                    o_ref[rows, :] += col_s[1, rows, :] * back
        return 0

    lax.fori_loop(0, n_tiles, tile, 0)

    @pl.when((e == n_exp - 1) & (f == n_f - 1))
    def _():
        def finish(r, _):
            rows = pl.ds(pl.multiple_of(r * strip, strip), strip)
            o_ref[rows, :] = h_ref[rows, :] + _rms(o_ref[rows, :], gpost_ref[...])
            return 0

        lax.fori_loop(0, tb // strip, finish, 0)


def _moe(h, gain_pre, gain_post, w_router, w_gate, w_up, w_down, tb, tf, rt):
    n, d = h.shape
    n_exp, _, d_ff = w_gate.shape
    w_r = jnp.pad(w_router, ((0, 0), (0, LANES - n_exp)))
    once = pl.Buffered(1)
    strip = 256
    cap = pl.cdiv(tb, rt) * rt

    def tiles(w):
        return jnp.transpose(w.reshape(n_exp, d, d_ff // tf, tf), (0, 2, 1, 3))

    return pl.pallas_call(
        functools.partial(_moe_kernel, n_exp=n_exp, rt=rt, strip=strip),
        grid=(n // tb, n_exp, d_ff // tf),
        in_specs=[pl.BlockSpec((tb, d), lambda i, e, f: (i, 0), pipeline_mode=once),
                  pl.BlockSpec((1, d), lambda i, e, f: (0, 0)),
                  pl.BlockSpec((1, d), lambda i, e, f: (0, 0)),
                  pl.BlockSpec((d, LANES), lambda i, e, f: (0, 0)),
                  pl.BlockSpec((1, 1, d, tf), lambda i, e, f: (e, f, 0, 0)),
                  pl.BlockSpec((1, 1, d, tf), lambda i, e, f: (e, f, 0, 0)),
                  pl.BlockSpec((1, tf, d), lambda i, e, f: (e, f, 0))],
        out_specs=pl.BlockSpec((tb, d), lambda i, e, f: (i, 0), pipeline_mode=once),
        out_shape=jax.ShapeDtypeStruct((n, d), F32),
        scratch_shapes=[pltpu.VMEM((tb, d), BF16),
                        pltpu.VMEM((tb, LANES), F32),
                        pltpu.VMEM((tb, LANES), BF16),
                        pltpu.VMEM((tb, LANES), F32),
                        pltpu.VMEM((LANES, tb), F32),
                        pltpu.VMEM((cap, d), BF16),
                        pltpu.VMEM((cap, d), F32),
                        pltpu.VMEM((rt, d), F32),
                        pltpu.VMEM((2, tb, 1), F32),
                        pltpu.SMEM(((tb // strip + 1) * n_exp,), jnp.int32)],
        compiler_params=pltpu.CompilerParams(
            dimension_semantics=("parallel", "arbitrary", "arbitrary"),
            vmem_limit_bytes=MOE_VMEM_LIMIT_BYTES),
        name="moe_ffn",
    )(h, gain_pre.reshape(1, d), gain_post.reshape(1, d), w_r, tiles(w_gate), tiles(w_up), w_down)


def _qkv_kernel(h_ref, g_ref, w_ref, qg_ref, kg_ref, cs_ref, sn_ref, bd_ref,
                q_ref, k_ref, v_ref):
    xn = _rms(h_ref[...], g_ref[...]).astype(BF16)
    qkv = jnp.dot(xn, w_ref[...], preferred_element_type=F32)
    cs = cs_ref[...]
    sn = sn_ref[...]
    bd = bd_ref[...]
    lane = lax.broadcasted_iota(jnp.int32, cs.shape, 1)
    first_half = (lane % HEAD_DIM) < (HEAD_DIM // 2)
    scale = math.log2(math.e) / math.sqrt(HEAD_DIM)

    def norm_rope(x, gain):
        ms = jnp.dot(x * x, bd, precision=lax.Precision.HIGHEST, preferred_element_type=F32)
        y = x * lax.rsqrt(ms + NORM_EPS) * gain
        partner = jnp.where(first_half,
                            pltpu.roll(y, LANES - HEAD_DIM // 2, axis=1),
                            pltpu.roll(y, HEAD_DIM // 2, axis=1))
        return y * cs + partner * sn

    n_q_tiles = N_HEADS * HEAD_DIM // LANES
    for t in range(n_q_tiles):
        y = norm_rope(qkv[:, t * LANES:(t + 1) * LANES], qg_ref[...]) * scale
        q_ref[0, 2 * t] = y[:, 0:HEAD_DIM].astype(BF16)
        q_ref[0, 2 * t + 1] = y[:, HEAD_DIM:LANES].astype(BF16)
    k0 = N_HEADS * HEAD_DIM
    for t in range(N_KV_HEADS * HEAD_DIM // LANES):
        y = norm_rope(qkv[:, k0 + t * LANES:k0 + (t + 1) * LANES], kg_ref[...])
        k_ref[0, 2 * t] = y[:, 0:HEAD_DIM].astype(BF16)
        k_ref[0, 2 * t + 1] = y[:, HEAD_DIM:LANES].astype(BF16)
    v0 = (N_HEADS + N_KV_HEADS) * HEAD_DIM
    ones = jnp.ones((qkv.shape[0], LANES - HEAD_DIM), BF16)
    for j in range(N_KV_HEADS):
        vj = qkv[:, v0 + j * HEAD_DIM:v0 + (j + 1) * HEAD_DIM].astype(BF16)
        v_ref[0, j] = jnp.concatenate([vj, ones], axis=-1)


def _rope_tables(seq):
    axis_dim = HEAD_DIM // 2
    freqs = ROPE_THETA ** (-jnp.arange(0, axis_dim, 2, dtype=F32) / axis_dim)
    rows = seq // GRID_W
    row_ang = jnp.arange(rows, dtype=F32)[:, None] * freqs
    col_ang = jnp.arange(GRID_W, dtype=F32)[:, None] * freqs
    ang = jnp.concatenate([
        jnp.broadcast_to(row_ang[:, None, :], (rows, GRID_W, freqs.shape[0])),
        jnp.broadcast_to(col_ang[None, :, :], (rows, GRID_W, freqs.shape[0]))], axis=-1)
    ang = ang.reshape(seq, HEAD_DIM // 2)
    cos, sin = jnp.cos(ang), jnp.sin(ang)
    cs = jnp.tile(jnp.concatenate([cos, cos], axis=-1), (1, LANES // HEAD_DIM))
    sn = jnp.tile(jnp.concatenate([-sin, sin], axis=-1), (1, LANES // HEAD_DIM))
    return cs, sn


def _qkv(h, gain, w_qkv, q_gain, k_gain, bsz, seq, tm):
    n, d = h.shape
    width = w_qkv.shape[1]
    perm = jnp.concatenate([jnp.arange(0, HEAD_DIM, 2), jnp.arange(1, HEAD_DIM, 2)])
    n_rot = N_HEADS + N_KV_HEADS
    cols = (jnp.arange(n_rot)[:, None] * HEAD_DIM + perm[None, :]).reshape(-1)
    cols = jnp.concatenate([cols, jnp.arange(n_rot * HEAD_DIM, width)])
    w = w_qkv[:, cols].astype(BF16)
    qg = jnp.tile(q_gain[perm], LANES // HEAD_DIM).reshape(1, LANES)
    kg = jnp.tile(k_gain[perm], LANES // HEAD_DIM).reshape(1, LANES)
    cs, sn = _rope_tables(seq)
    blk = jnp.arange(LANES) // HEAD_DIM
    bd = (blk[:, None] == blk[None, :]).astype(F32) / HEAD_DIM
    per_seq = seq // tm
    return pl.pallas_call(
        _qkv_kernel,
        grid=(n // tm,),
        in_specs=[pl.BlockSpec((tm, d), lambda i: (i, 0)),
                  pl.BlockSpec((1, d), lambda i: (0, 0)),
                  pl.BlockSpec((d, width), lambda i: (0, 0)),
                  pl.BlockSpec((1, LANES), lambda i: (0, 0)),
                  pl.BlockSpec((1, LANES), lambda i: (0, 0)),
                  pl.BlockSpec((tm, LANES), lambda i: (i % per_seq, 0)),
                  pl.BlockSpec((tm, LANES), lambda i: (i % per_seq, 0)),
                  pl.BlockSpec((LANES, LANES), lambda i: (0, 0))],
        out_specs=[pl.BlockSpec((1, N_HEADS, tm, HEAD_DIM),
                                lambda i: (i // per_seq, 0, i % per_seq, 0)),
                   pl.BlockSpec((1, N_KV_HEADS, tm, HEAD_DIM),
                                lambda i: (i // per_seq, 0, i % per_seq, 0)),
                   pl.BlockSpec((1, N_KV_HEADS, tm, LANES),
                                lambda i: (i // per_seq, 0, i % per_seq, 0))],
        out_shape=[jax.ShapeDtypeStruct((bsz, N_HEADS, seq, HEAD_DIM), BF16),
                   jax.ShapeDtypeStruct((bsz, N_KV_HEADS, seq, HEAD_DIM), BF16),
                   jax.ShapeDtypeStruct((bsz, N_KV_HEADS, seq, LANES), BF16)],
        compiler_params=_cparams("parallel"),
        name="qkv_rope",
    )(h, gain.reshape(1, d), w, qg, kg, cs, sn, bd)


def _attn_kernel(q_ref, k_ref, v_ref, o_ref, m_s, acc_s, s_buf, p_buf, a_buf,
                 *, tq, tk, rc, unroll):
    seq = k_ref.shape[2]
    chunks_per_head = tq // rc
    n_chunks = Q_PER_KV * chunks_per_head
    n_steps = (seq // tk) * n_chunks

    m_s[...] = jnp.full_like(m_s, -jnp.inf)
    acc_s[...] = jnp.zeros_like(acc_s)

    def where(n):
        c = n % n_chunks
        return (pl.multiple_of((n // n_chunks) * tk, tk), c // chunks_per_head,
                pl.multiple_of((c % chunks_per_head) * rc, rc))

    def scores(n):
        k0, g, r0 = where(n)
        q = q_ref[0, g, pl.ds(r0, rc), :]
        s_buf[...] = lax.dot_general(q, k_ref[0, 0, pl.ds(k0, tk), :], _NT,
                                     preferred_element_type=F32)

    def softmax(n):
        _, g, r0 = where(n)
        s = s_buf[...]
        m_prev = m_s[g, pl.ds(r0, rc), :]
        m_new = jnp.maximum(m_prev, jnp.max(s, axis=-1, keepdims=True))
        a_buf[...] = jnp.exp2(m_prev - m_new)
        for t in range(tk // LANES):
            p_buf[:, t * LANES:(t + 1) * LANES] = jnp.exp2(
                s[:, t * LANES:(t + 1) * LANES] - m_new).astype(BF16)
        m_s[g, pl.ds(r0, rc), :] = m_new

    def values(n):
        k0, g, r0 = where(n)
        pv = jnp.dot(p_buf[...], v_ref[0, 0, pl.ds(k0, tk), :],
                     preferred_element_type=F32)
        acc_s[g, pl.ds(r0, rc), :] = a_buf[...] * acc_s[g, pl.ds(r0, rc), :] + pv

    scores(0)
    softmax(0)
    scores(1)

    def body(n, _):
        values(n)
        softmax(n + 1)
        scores(n + 2)
        return 0

    lax.fori_loop(0, n_steps - 2, body, 0, unroll=unroll)
    values(n_steps - 2)
    softmax(n_steps - 1)
    values(n_steps - 1)

    outs = []
    for g in range(Q_PER_KV):
        acc = acc_s[g]
        o = acc / pltpu.roll(acc, HEAD_DIM, axis=1)
        outs.append(o[:, 0:HEAD_DIM])
    o_ref[...] = jnp.concatenate(outs, axis=-1).astype(o_ref.dtype)


def _attention(q, k, v, tq, tk, rc):
    bsz, _, seq, _ = q.shape
    n_q = seq // tq
    n_steps = (seq // tk) * Q_PER_KV * (tq // rc)
    unroll = min(8, n_steps - 2)
    return pl.pallas_call(
        functools.partial(_attn_kernel, tq=tq, tk=tk, rc=rc, unroll=unroll),
        grid=(bsz, N_KV_HEADS, n_q),
        in_specs=[pl.BlockSpec((1, Q_PER_KV, tq, HEAD_DIM), lambda b, j, i: (b, j, i, 0)),
                  pl.BlockSpec((1, 1, seq, HEAD_DIM), lambda b, j, i: (b, j, 0, 0)),
                  pl.BlockSpec((1, 1, seq, LANES), lambda b, j, i: (b, j, 0, 0))],
        out_specs=pl.BlockSpec((tq, Q_PER_KV * HEAD_DIM), lambda b, j, i: (b * n_q + i, j)),
        out_shape=jax.ShapeDtypeStruct((bsz * seq, N_HEADS * HEAD_DIM), BF16),
        scratch_shapes=[pltpu.VMEM((Q_PER_KV, tq, LANES), F32),
                        pltpu.VMEM((Q_PER_KV, tq, LANES), F32),
                        pltpu.VMEM((rc, tk), F32),
                        pltpu.VMEM((rc, tk), BF16),
                        pltpu.VMEM((rc, LANES), F32)],
        compiler_params=_cparams("parallel", "parallel", "parallel"),
        name="flash_attn",
    )(q, k, v)


def _proj_res_kernel(x_ref, h_ref, w_ref, g_ref, o_ref):
    mix = jnp.dot(x_ref[...], w_ref[...], preferred_element_type=F32)
    o_ref[...] = h_ref[...] + _rms(mix, g_ref[...])


def _proj_res(x, h, w, gain, tm):
    n, d = h.shape
    k = x.shape[1]
    return pl.pallas_call(
        _proj_res_kernel,
        grid=(n // tm,),
        in_specs=[pl.BlockSpec((tm, k), lambda i: (i, 0)),
                  pl.BlockSpec((tm, d), lambda i: (i, 0)),
                  pl.BlockSpec((k, d), lambda i: (0, 0)),
                  pl.BlockSpec((1, d), lambda i: (0, 0))],
        out_specs=pl.BlockSpec((tm, d), lambda i: (i, 0)),
        out_shape=jax.ShapeDtypeStruct((n, d), F32),
        compiler_params=_cparams("parallel"),
        name="proj_res",
    )(x, h, w, gain.reshape(1, d))


def _s5_layer(h, bsz, seq, gains, w_in, lam_re, lam_im, log_step, b_re, b_im, c_re, c_im,
              d_skip, w_glu, w_out):
    seg_tokens = S5_CHUNK * S5_SEG_CHUNKS
    nseg = seq // seg_tokens
    nseq = bsz * nseg
    rc = min(512, h.shape[0] // S5_CHUNK)
    u_t = _norm_matmul(h, gains[0], w_in.astype(BF16), rc=rc)
    w1, w2t, pw_p, aa_p = _s5_prep(lam_re, lam_im, log_step, b_re, b_im, c_re, c_im, d_skip)
    y_t = _s5_core(u_t, w1, w2t, pw_p, aa_p, nseq, nseg)
    return _s5_out(y_t, h, w_glu.astype(BF16), w_out.astype(BF16), gains[1], rc=rc)


def _attn_layer(h, bsz, seq, gains, w_qkv, q_gain, k_gain, w_out):
    q, k, v = _qkv(h, gains[0], w_qkv, q_gain, k_gain, bsz, seq, tm=512)
    o = _attention(q, k, v, tq=2048, tk=512, rc=512)
    return _proj_res(o, h, w_out.astype(BF16), gains[1], tm=512)


def kernel(x, norm_gains, ssm_w_in, ssm_lambda_re, ssm_lambda_im, ssm_log_step, ssm_b_re,
           ssm_b_im, ssm_c_re, ssm_c_im, ssm_d, ssm_w_glu, ssm_w_out, ffn_w_gate, ffn_w_up,
           ffn_w_down, attn_w_qkv, attn_q_gain, attn_k_gain, attn_w_out, moe_w_router,
           moe_w_gate, moe_w_up, moe_w_down):
    bsz, seq, d = x.shape
    depth = norm_gains.shape[0]
    h = x.reshape(bsz * seq, d)
    for i in range(depth):
        j = i // 2
        g = norm_gains[i]
        if i % 2 == 0:
            h = _s5_layer(h, bsz, seq, g, ssm_w_in[j], ssm_lambda_re[j], ssm_lambda_im[j],
                          ssm_log_step[j], ssm_b_re[j], ssm_b_im[j], ssm_c_re[j], ssm_c_im[j],
                          ssm_d[j], ssm_w_glu[j], ssm_w_out[j])
            h = _ffn(h, g[2], g[3], ffn_w_gate[j].astype(BF16), ffn_w_up[j].astype(BF16),
                     ffn_w_down[j].astype(BF16), tm=512, tf=1408)
        else:
            h = _attn_layer(h, bsz, seq, g, attn_w_qkv[j], attn_q_gain[j], attn_k_gain[j],
                            attn_w_out[j])
            h = _moe(h, g[2], g[3], moe_w_router[j], moe_w_gate[j].astype(BF16),
                     moe_w_up[j].astype(BF16), moe_w_down[j].astype(BF16),
                     tb=2048, tf=896, rt=272)
    return h.reshape(bsz, seq, d)
```

```python
import functools
import math

import jax
import jax.numpy as jnp
from jax import lax
from jax.experimental import pallas as pl
from jax.experimental.pallas import tpu as pltpu

F32 = jnp.float32
BF16 = jnp.bfloat16
NORM_EPS = 1e-6
ROPE_THETA = 10000.0
GRID_W = 64
N_HEADS = 16
N_KV_HEADS = 4
HEAD_DIM = 64
Q_PER_KV = N_HEADS // N_KV_HEADS
SSM_GROUP = 16
SSM_STATE = 64
S5_CHUNK = 16
S5_SEG_CHUNKS = 64
TOP_K = 2
LANES = 128
VMEM_LIMIT_BYTES = 56 * 1024 * 1024
MOE_VMEM_LIMIT_BYTES = 60 * 1024 * 1024

_NT = (((1,), (1,)), ((), ()))


def _cparams(*sem):
    return pltpu.CompilerParams(dimension_semantics=sem, vmem_limit_bytes=VMEM_LIMIT_BYTES)


def _rms(x, gain):
    return x * lax.rsqrt(jnp.mean(x * x, axis=-1, keepdims=True) + NORM_EPS) * gain


def _sigmoid(x):
    return 1.0 / (1.0 + jnp.exp(-x))


def _gelu_tanh(x):
    return x * (0.5 * (1.0 + jnp.tanh(math.sqrt(2.0 / math.pi) * (x + 0.044715 * (x * x * x)))))


def _norm_matmul_kernel(x_ref, g_ref, w_ref, o_ref):
    xn = _rms(x_ref[...], g_ref[...]).astype(BF16)
    o_ref[0] = jnp.dot(xn, w_ref[...], preferred_element_type=F32).astype(o_ref.dtype)


def _norm_matmul(x, gain, w, rc):
    n, d = x.shape
    m = w.shape[1]
    chunks = n // S5_CHUNK
    return pl.pallas_call(
        _norm_matmul_kernel,
        grid=(chunks // rc, S5_CHUNK),
        in_specs=[pl.BlockSpec((rc, d), lambda i, t: (i, t)),
                  pl.BlockSpec((1, d), lambda i, t: (0, 0)),
                  pl.BlockSpec((d, m), lambda i, t: (0, 0))],
        out_specs=pl.BlockSpec((1, rc, m), lambda i, t: (t, i, 0)),
        out_shape=jax.ShapeDtypeStruct((S5_CHUNK, chunks, m), BF16),
        compiler_params=_cparams("parallel", "parallel"),
        name="norm_matmul",
    )(x.reshape(chunks, S5_CHUNK * d), gain.reshape(1, d), w)


def _s5_prep_kernel(lr_ref, li_ref, ls_ref, bt_re_ref, bt_im_ref, c_re_ref, c_im_ref, dt_ref,
                    w1_ref, w2t_ref, pw_ref, aa_ref):
    t_n, s_n, p_n = S5_CHUNK, SSM_GROUP, SSM_STATE
    pair = 2 * s_n
    rows = t_n * pair
    hi = lax.Precision.HIGHEST

    rr = lax.broadcasted_iota(jnp.int32, (rows, rows), 0)
    cc = lax.broadcasted_iota(jnp.int32, (rows, rows), 1)
    same_group = ((rr // s_n) % 2) == ((cc // s_n) % 2)
    tvec = lax.broadcasted_iota(jnp.int32, (t_n, 1), 0).astype(F32)
    jvec = lax.broadcasted_iota(jnp.int32, (S5_SEG_CHUNKS, 1), 0).astype(F32)
    row8 = lax.broadcasted_iota(jnp.int32, (8, LANES), 0)
    own_lanes = ((lax.broadcasted_iota(jnp.int32, (rows, LANES), 1) // p_n)
                 == ((lax.broadcasted_iota(jnp.int32, (rows, LANES), 0) // s_n) % 2))

    def lanes_by_group(tab0, tab1):
        lane = lax.broadcasted_iota(jnp.int32, tab0.shape, 1)
        return jnp.where(lane < p_n, tab0, tab1)

    m_total = None
    for d in range(2):
        lsr, lsi, q_re, q_im = [], [], [], []
        for g in range(2):
            lr = lr_ref[0, d, g:g + 1, :]
            li = li_ref[0, d, g:g + 1, :]
            step = jnp.exp(ls_ref[0, d, g:g + 1, :])
            lsr.append(lr * step)
            lsi.append(li * step)
            mag = jnp.exp(lsr[g])
            a_re, a_im = mag * jnp.cos(lsi[g]), mag * jnp.sin(lsi[g])
            nr, ni = a_re - 1.0, a_im
            den = lr * lr + li * li
            q_re.append((nr * lr + ni * li) / den)
            q_im.append((ni * lr - nr * li) / den)

        def cpow(k, g):
            mag = jnp.exp(lsr[g] * k)
            ang = lsi[g] * k
            return mag * jnp.cos(ang), mag * jnp.sin(ang)

        def table(k):
            t0, t1 = cpow(k, 0), cpow(k, 1)
            return tuple(
                jnp.concatenate([jnp.broadcast_to(tg[part][t:t + 1, :], (s_n, LANES))
                                 for t in range(t_n) for tg in (t0, t1)], axis=0)
                for part in range(2))

        br, bi = bt_re_ref[0, d], bt_im_ref[0, d]
        qr = jnp.concatenate([jnp.broadcast_to(q_re[g], (s_n, LANES)) for g in range(2)], axis=0)
        qi = jnp.concatenate([jnp.broadcast_to(q_im[g], (s_n, LANES)) for g in range(2)], axis=0)
        bb_re = qr * br - qi * bi
        bb_im = qr * bi + qi * br
        cr, ci = c_re_ref[0, d], c_im_ref[0, d]

        def outer(x_re, x_im, k):
            pe_re, pe_im = table(k)
            xe_re = jnp.concatenate([x_re] * t_n, axis=0)
            xe_im = jnp.concatenate([x_im] * t_n, axis=0)
            return xe_re * pe_re - xe_im * pe_im, xe_re * pe_im + xe_im * pe_re

        if d == 0:
            l_re, l_im = outer(bb_re, bb_im, -tvec)
            rt_re, rt_im = outer(cr, ci, tvec)
            ws_re, ws_im = outer(bb_re, bb_im, (t_n - 1.0) - tvec)
            wy_re, wy_im = outer(cr, ci, tvec + 1.0)
            mask = (rr // pair) <= (cc // pair)
            kseg = t_n * jvec
        else:
            l_re, l_im = outer(bb_re, bb_im, tvec)
            rt_re, rt_im = outer(cr, ci, -tvec)
            ws_re, ws_im = l_re, l_im
            wy_re, wy_im = outer(cr, ci, t_n - tvec)
            mask = (rr // pair) >= (cc // pair)
            kseg = t_n * ((S5_SEG_CHUNKS - 1.0) - jvec)
        kern = (lax.dot_general(l_re[:, 0:p_n], rt_re[:, 0:p_n], _NT, precision=hi,
                                preferred_element_type=F32)
                - lax.dot_general(l_im[:, 0:p_n], rt_im[:, 0:p_n], _NT, precision=hi,
                                  preferred_element_type=F32))
        kern = jnp.where(mask & same_group, kern, 0.0)
        m_total = kern if m_total is None else m_total + kern

        ycols = rows
        for k2, val in ((2 * d, ws_re), (2 * d + 1, ws_im)):
            w1_ref[0, :, ycols + k2 * LANES:ycols + (k2 + 1) * LANES] = jnp.where(
                own_lanes, val, 0.0).astype(w1_ref.dtype)
        for k2, val in ((2 * d, wy_re), (2 * d + 1, -wy_im)):
            w2t_ref[0, :, k2 * LANES:(k2 + 1) * LANES] = jnp.where(
                own_lanes, val, 0.0).astype(w2t_ref.dtype)
        p0, p1 = cpow(kseg, 0), cpow(kseg, 1)
        pw_ref[0, :, 2 * d * LANES:(2 * d + 1) * LANES] = lanes_by_group(p0[0], p1[0])
        pw_ref[0, :, (2 * d + 1) * LANES:(2 * d + 2) * LANES] = lanes_by_group(p0[1], p1[1])
        one = jnp.ones((1, 1), F32)
        c0, c1 = cpow(float(t_n) * one, 0), cpow(float(t_n) * one, 1)
        s0, s1 = (cpow(float(t_n * S5_SEG_CHUNKS) * one, 0),
                  cpow(float(t_n * S5_SEG_CHUNKS) * one, 1))
        for part in range(2):
            chunk_pow = lanes_by_group(c0[part], c1[part])
            seg_pow = lanes_by_group(s0[part], s1[part])
            aa_ref[0, :, (2 * d + part) * LANES:(2 * d + part + 1) * LANES] = jnp.where(
                row8 == 0, chunk_pow, jnp.where(row8 == 1, seg_pow, 0.0))

    w1_ref[0, :, 0:rows] = (m_total + jnp.where(rr == cc, dt_ref[0], 0.0)).astype(w1_ref.dtype)


def _s5_prep(lam_re, lam_im, log_step, b_re, b_im, c_re, c_im, d_skip):
    g_n = lam_re.shape[1]
    pairs = g_n // 2
    pair = 2 * SSM_GROUP
    rows = S5_CHUNK * pair

    def states(x):
        x = jnp.transpose(x.reshape(2, pairs, 2, SSM_STATE), (1, 0, 2, 3))
        return jnp.tile(x, (1, 1, 1, LANES // SSM_STATE))

    def per_channel(x):
        x = jnp.transpose(x.reshape(2, pairs, pair, SSM_STATE), (1, 0, 2, 3))
        return jnp.tile(x, (1, 1, 1, LANES // SSM_STATE))

    ls = jnp.transpose(log_step.reshape(2, pairs, 2), (1, 0, 2))[..., None]
    bt_re = per_channel(jnp.swapaxes(b_re, 2, 3))
    bt_im = per_channel(jnp.swapaxes(b_im, 2, 3))
    dt = jnp.tile(d_skip.reshape(pairs, 1, pair), (1, 1, S5_CHUNK))

    def spec(*blk):
        return pl.BlockSpec((1,) + blk, lambda g: (g,) + (0,) * len(blk))

    return pl.pallas_call(
        _s5_prep_kernel,
        grid=(pairs,),
        in_specs=[spec(2, 2, LANES), spec(2, 2, LANES), spec(2, 2, 1),
                  spec(2, pair, LANES), spec(2, pair, LANES),
                  spec(2, pair, LANES), spec(2, pair, LANES), spec(1, rows)],
        out_specs=[spec(rows, rows + 4 * LANES), spec(rows, 4 * LANES),
                   spec(S5_SEG_CHUNKS, 4 * LANES), spec(8, 4 * LANES)],
        out_shape=[jax.ShapeDtypeStruct((pairs, rows, rows + 4 * LANES), BF16),
                   jax.ShapeDtypeStruct((pairs, rows, 4 * LANES), BF16),
                   jax.ShapeDtypeStruct((pairs, S5_SEG_CHUNKS, 4 * LANES), F32),
                   jax.ShapeDtypeStruct((pairs, 8, 4 * LANES), F32)],
        compiler_params=_cparams("parallel"),
        name="s5_prep",
    )(states(lam_re), states(lam_im), ls, bt_re, bt_im, per_channel(c_re), per_channel(c_im), dt)


S5_PAIR_LANES = 2 * SSM_GROUP
S5_PAIRS_PER_SLAB = LANES // S5_PAIR_LANES
S5_SEG_PITCH = S5_SEG_CHUNKS + 8
S5_SEG_BASE = 8


def _s5_core_kernel(x_ref, w1_ref, w2t_ref, pw_ref, aa_ref, y_ref, z_s, zs_s, x_s, u_s, xb_s,
                    *, nseq, nseg, strip):
    n_chunks = S5_SEG_CHUNKS
    rows = n_chunks * nseq
    ycols = u_s.shape[1]
    per_tile = LANES // S5_PAIR_LANES
    lane_grp = lax.broadcasted_iota(jnp.int32, (strip, LANES), 1) // S5_PAIR_LANES
    zero = jnp.zeros((nseq, LANES), F32)

    def seg_rows(j):
        return pl.ds(S5_SEG_BASE + j, nseq, stride=S5_SEG_PITCH)

    for q in range(S5_PAIRS_PER_SLAB):
        def gather(r, _, q=q):
            rws = pl.ds(pl.multiple_of(r * strip, strip), strip)
            for j in range(ycols // LANES):
                acc = None
                for i in range(per_tile):
                    xt = x_ref[per_tile * j + i, rws, :].astype(F32)
                    shift = (S5_PAIR_LANES * (i - q)) % LANES
                    if shift:
                        xt = pltpu.roll(xt, shift, axis=1)
                    acc = xt if acc is None else jnp.where(lane_grp == i, xt, acc)
                u_s[rws, j * LANES:(j + 1) * LANES] = acc.astype(BF16)
            return 0

        lax.fori_loop(0, rows // strip, gather, 0)
        z_s[...] = jnp.dot(u_s[...], w1_ref[q], preferred_element_type=F32)

        for b in range(nseq):
            dst = slice(S5_SEG_BASE + b * S5_SEG_PITCH, S5_SEG_BASE + b * S5_SEG_PITCH + n_chunks)
            for k in range(4):
                zs_s[k, dst, :] = z_s[b * n_chunks:(b + 1) * n_chunks,
                                      ycols + k * LANES:ycols + (k + 1) * LANES]
        x_s[0, seg_rows(0), :] = zero
        x_s[1, seg_rows(0), :] = zero
        x_s[2, seg_rows(n_chunks - 1), :] = zero
        x_s[3, seg_rows(n_chunks - 1), :] = zero

        aa = aa_ref[q]
        a_fr, a_fi = aa[0:1, 0:LANES], aa[0:1, LANES:2 * LANES]
        a_br, a_bi = aa[0:1, 2 * LANES:3 * LANES], aa[0:1, 3 * LANES:4 * LANES]
        g_fr, g_fi = aa[1:2, 0:LANES], aa[1:2, LANES:2 * LANES]
        g_br, g_bi = aa[1:2, 2 * LANES:3 * LANES], aa[1:2, 3 * LANES:4 * LANES]

        def fwd(j, carry):
            xr, xi = carry
            nr = a_fr * xr - a_fi * xi + zs_s[0, seg_rows(j), :]
            ni = a_fr * xi + a_fi * xr + zs_s[1, seg_rows(j), :]
            x_s[0, seg_rows(j + 1), :] = nr
            x_s[1, seg_rows(j + 1), :] = ni
            return nr, ni

        def bwd(i, carry):
            xr, xi = carry
            j = n_chunks - 1 - i
            nr = a_br * xr - a_bi * xi + zs_s[2, seg_rows(j), :]
            ni = a_br * xi + a_bi * xr + zs_s[3, seg_rows(j), :]
            x_s[2, seg_rows(j - 1), :] = nr
            x_s[3, seg_rows(j - 1), :] = ni
            return nr, ni

        ef_r, ef_i = lax.fori_loop(0, n_chunks, fwd, (zero, zero))
        eb_r, eb_i = lax.fori_loop(0, n_chunks, bwd, (zero, zero))

        def seg_carries(e_r, e_i, g_r, g_i, reverse):
            out_r = [None] * nseq
            out_i = [None] * nseq
            for b in range(nseq // nseg):
                c_r = jnp.zeros((1, LANES), F32)
                c_i = jnp.zeros((1, LANES), F32)
                order = range(nseg - 1, -1, -1) if reverse else range(nseg)
                for s in order:
                    r = b * nseg + s
                    out_r[r], out_i[r] = c_r, c_i
                    n_r = g_r * c_r - g_i * c_i + e_r[r:r + 1, :]
                    n_i = g_r * c_i + g_i * c_r + e_i[r:r + 1, :]
                    c_r, c_i = n_r, n_i
            return out_r, out_i

        cf_r, cf_i = seg_carries(ef_r, ef_i, g_fr, g_fi, False)
        cb_r, cb_i = seg_carries(eb_r, eb_i, g_br, g_bi, True)

        p = pw_ref[q]
        p_fr, p_fi = p[:, 0:LANES], p[:, LANES:2 * LANES]
        p_br, p_bi = p[:, 2 * LANES:3 * LANES], p[:, 3 * LANES:4 * LANES]
        for b in range(nseq):
            src = slice(S5_SEG_BASE + b * S5_SEG_PITCH, S5_SEG_BASE + b * S5_SEG_PITCH + n_chunks)
            dst = slice(b * n_chunks, (b + 1) * n_chunks)
            xb_s[dst, 0:LANES] = (x_s[0, src, :] + (p_fr * cf_r[b] - p_fi * cf_i[b])).astype(BF16)
            xb_s[dst, LANES:2 * LANES] = (
                x_s[1, src, :] + (p_fr * cf_i[b] + p_fi * cf_r[b])).astype(BF16)
            xb_s[dst, 2 * LANES:3 * LANES] = (
                x_s[2, src, :] + (p_br * cb_r[b] - p_bi * cb_i[b])).astype(BF16)
            xb_s[dst, 3 * LANES:4 * LANES] = (
                x_s[3, src, :] + (p_br * cb_i[b] + p_bi * cb_r[b])).astype(BF16)

        z_s[:, 0:ycols] += lax.dot_general(xb_s[...], w2t_ref[q], _NT,
                                           preferred_element_type=F32)

        def scatter(r, _, q=q):
            rws = pl.ds(pl.multiple_of(r * strip, strip), strip)
            for j in range(ycols // LANES):
                yq = z_s[rws, j * LANES:(j + 1) * LANES]
                for i in range(per_tile):
                    shift = (S5_PAIR_LANES * (q - i)) % LANES
                    yt = pltpu.roll(yq, shift, axis=1) if shift else yq
                    lanes_q = slice(q * S5_PAIR_LANES, (q + 1) * S5_PAIR_LANES)
                    y_ref[per_tile * j + i, rws, lanes_q] = yt[:, lanes_q].astype(y_ref.dtype)
            return 0

        lax.fori_loop(0, rows // strip, scatter, 0)


def _s5_core(u_t, w1, w2t, pw, aa, nseq, nseg):
    t_n, rows, d = u_t.shape
    width = w1.shape[1]
    pps = S5_PAIRS_PER_SLAB
    slab_rows = S5_SEG_BASE + nseq * S5_SEG_PITCH
    return pl.pallas_call(
        functools.partial(_s5_core_kernel, nseq=nseq, nseg=nseg, strip=128),
        grid=(d // LANES,),
        in_specs=[pl.BlockSpec((t_n, rows, LANES), lambda o: (0, 0, o)),
                  pl.BlockSpec((pps, width, w1.shape[2]), lambda o: (o, 0, 0)),
                  pl.BlockSpec((pps, width, w2t.shape[2]), lambda o: (o, 0, 0)),
                  pl.BlockSpec((pps, S5_SEG_CHUNKS, pw.shape[2]), lambda o: (o, 0, 0)),
                  pl.BlockSpec((pps, 8, aa.shape[2]), lambda o: (o, 0, 0))],
        out_specs=pl.BlockSpec((t_n, rows, LANES), lambda o: (0, 0, o)),
        out_shape=jax.ShapeDtypeStruct((t_n, rows, d), BF16),
        scratch_shapes=[pltpu.VMEM((rows, w1.shape[2]), F32),
                        pltpu.VMEM((4, slab_rows, LANES), F32),
                        pltpu.VMEM((4, slab_rows, LANES), F32),
                        pltpu.VMEM((rows, width), BF16),
                        pltpu.VMEM((rows, 4 * LANES), BF16)],
        compiler_params=_cparams("parallel"),
        name="s5_core",
    )(u_t, w1, w2t, pw, aa)


def _s5_out_kernel(y_ref, h_ref, wglu_ref, wout_ref, g_ref, o_ref):
    g = _gelu_tanh(y_ref[0].astype(F32))
    z = jnp.dot(g.astype(BF16), wglu_ref[...], preferred_element_type=F32)
    g2 = g * _sigmoid(z)
    mix = jnp.dot(g2.astype(BF16), wout_ref[...], preferred_element_type=F32)
    o_ref[...] = h_ref[...] + _rms(mix, g_ref[...])


def _s5_out(y_t, h, w_glu, w_out, gain, rc):
    n, d = h.shape
    chunks = n // S5_CHUNK
    out = pl.pallas_call(
        _s5_out_kernel,
        grid=(chunks // rc, S5_CHUNK),
        in_specs=[pl.BlockSpec((1, rc, d), lambda i, t: (t, i, 0)),
                  pl.BlockSpec((rc, d), lambda i, t: (i, t)),
                  pl.BlockSpec((d, d), lambda i, t: (0, 0)),
                  pl.BlockSpec((d, d), lambda i, t: (0, 0)),
                  pl.BlockSpec((1, d), lambda i, t: (0, 0))],
        out_specs=pl.BlockSpec((rc, d), lambda i, t: (i, t)),
        out_shape=jax.ShapeDtypeStruct((chunks, S5_CHUNK * d), F32),
        compiler_params=_cparams("parallel", "parallel"),
        name="s5_out",
    )(y_t, h.reshape(chunks, S5_CHUNK * d), w_glu, w_out, gain.reshape(1, d))
    return out.reshape(n, d)


def _ffn_kernel(h_ref, gpre_ref, gpost_ref, wg_ref, wu_ref, wd_ref, o_ref, hn_s, acc_s):
    f = pl.program_id(1)

    @pl.when(f == 0)
    def _():
        hn_s[...] = _rms(h_ref[...], gpre_ref[...]).astype(BF16)
        acc_s[...] = jnp.zeros_like(acc_s)

    hn = hn_s[...]
    a = jnp.dot(hn, wg_ref[...], preferred_element_type=F32)
    u = jnp.dot(hn, wu_ref[...], preferred_element_type=F32)
    act = a * _sigmoid(a) * u
    acc_s[...] += jnp.dot(act.astype(BF16), wd_ref[...], preferred_element_type=F32)

    @pl.when(f == pl.num_programs(1) - 1)
    def _():
        o_ref[...] = h_ref[...] + _rms(acc_s[...], gpost_ref[...])


def _ffn(h, gain_pre, gain_post, w_gate, w_up, w_down, tm, tf):
    n, d = h.shape
    d_ff = w_gate.shape[1]
    return pl.pallas_call(
        _ffn_kernel,
        grid=(n // tm, d_ff // tf),
        in_specs=[pl.BlockSpec((tm, d), lambda i, f: (i, 0)),
                  pl.BlockSpec((1, d), lambda i, f: (0, 0)),
                  pl.BlockSpec((1, d), lambda i, f: (0, 0)),
                  pl.BlockSpec((d, tf), lambda i, f: (0, f)),
                  pl.BlockSpec((d, tf), lambda i, f: (0, f)),
                  pl.BlockSpec((tf, d), lambda i, f: (f, 0))],
        out_specs=pl.BlockSpec((tm, d), lambda i, f: (i, 0)),
        out_shape=jax.ShapeDtypeStruct((n, d), F32),
        scratch_shapes=[pltpu.VMEM((tm, d), BF16), pltpu.VMEM((tm, d), F32)],
        compiler_params=_cparams("parallel", "arbitrary"),
        name="ffn",
    )(h, gain_pre.reshape(1, d), gain_post.reshape(1, d), w_gate, w_up, w_down)


def _moe_kernel(h_ref, gpre_ref, gpost_ref, wr_ref, wg_ref, wu_ref, wd_ref, o_ref,
                hn_s, gate_s, sel_s, pos_s, post_s, xc_s, yc_s, xg_s, col_s, cnt_s,
                *, n_exp, rt, strip):
    e = pl.program_id(1)
    f = pl.program_id(2)
    tb, d = hn_s.shape
    n_f = pl.num_programs(2)
    n_strips = tb // strip

    def put_counts(r, total):
        lane1 = lax.broadcasted_iota(jnp.int32, total.shape, 1)
        for x in range(n_exp):
            cnt_s[r * n_exp + x] = jnp.sum(jnp.where(lane1 == x, total, 0.0)).astype(jnp.int32)

    @pl.when((e == 0) & (f == 0))
    def _():
        def route(r, total):
            put_counts(r, total)
            rows = pl.ds(pl.multiple_of(r * strip, strip), strip)
            xn = _rms(h_ref[rows, :], gpre_ref[...])
            hn_s[rows, :] = xn.astype(BF16)
            o_ref[rows, :] = jnp.zeros((strip, d), F32)
            logits = jnp.dot(xn, wr_ref[...], precision=lax.Precision.HIGHEST,
                             preferred_element_type=F32)
            lane = lax.broadcasted_iota(jnp.int32, logits.shape, 1)
            neg = jnp.float32(-jnp.inf)
            logits = jnp.where(lane < n_exp, logits, neg)
            m1 = jnp.max(logits, axis=-1, keepdims=True)
            i1 = jnp.min(jnp.where(logits == m1, lane, LANES), axis=-1, keepdims=True)
            rest = jnp.where(lane == i1, neg, logits)
            m2 = jnp.max(rest, axis=-1, keepdims=True)
            i2 = jnp.min(jnp.where(rest == m2, lane, LANES), axis=-1, keepdims=True)
            e2 = jnp.exp(m2 - m1)
            gate_s[rows, :] = (jnp.where(lane == i1, 1.0 / (1.0 + e2), 0.0)
                               + jnp.where(lane == i2, e2 / (1.0 + e2), 0.0))
            sel = ((lane == i1) | (lane == i2)).astype(F32)
            sel_s[rows, :] = sel.astype(BF16)
            return total + jnp.sum(sel, axis=0, keepdims=True)

        total = lax.fori_loop(0, n_strips, route, jnp.zeros((1, LANES), F32))
        put_counts(n_strips, total)

        def rank(r, _):
            r0 = pl.multiple_of(r * strip, strip)
            rows = lax.broadcasted_iota(jnp.int32, (strip, tb), 0) + r0
            cols = lax.broadcasted_iota(jnp.int32, (strip, tb), 1)
            before = (cols < rows).astype(BF16)
            cnt = jnp.dot(before, sel_s[...], preferred_element_type=F32)
            chosen = sel_s[pl.ds(r0, strip), :] > 0
            pos_s[pl.ds(r0, strip), :] = jnp.where(chosen, cnt, -1.0)
            return 0

        lax.fori_loop(0, tb // strip, rank, 0)
        for r in range(tb // strip):
            post_s[:, r * strip:(r + 1) * strip] = pos_s[r * strip:(r + 1) * strip, :].T

    @pl.when(f == n_f - 1)
    def _():
        lane = lax.broadcasted_iota(jnp.int32, (tb, LANES), 1)
        col_s[0] = jnp.sum(jnp.where(lane == e, pos_s[...], 0.0), axis=-1, keepdims=True)
        col_s[1] = jnp.sum(jnp.where(lane == e, gate_s[...], 0.0), axis=-1, keepdims=True)

    n_tiles = (cnt_s[n_strips * n_exp + e] + rt - 1) // rt

    def tile(i, _):
        r0 = pl.multiple_of(i * rt, 16)

        def holds(r):
            return ((cnt_s[r * n_exp + e] < r0 + rt) & (cnt_s[(r + 1) * n_exp + e] > r0))

        @pl.when(f == 0)
        def _():
            xg_s[...] = jnp.zeros_like(xg_s)
            slot = (lax.broadcasted_iota(jnp.int32, (rt, strip), 0) + r0).astype(F32)
            for r in range(n_strips):
                @pl.when(holds(r))
                def _():
                    cols = slice(r * strip, (r + 1) * strip)
                    pick = (post_s[pl.ds(e, 1), cols] == slot).astype(BF16)
                    xg_s[...] += jnp.dot(pick, hn_s[cols, :], preferred_element_type=F32)
            xc_s[pl.ds(r0, rt), :] = xg_s[...].astype(BF16)

        xc = xc_s[pl.ds(r0, rt), :]
        a = jnp.dot(xc, wg_ref[0], preferred_element_type=F32)
        u = jnp.dot(xc, wu_ref[0], preferred_element_type=F32)
        act = (a * _sigmoid(a) * u).astype(BF16)
        y = jnp.dot(act, wd_ref[0], preferred_element_type=F32)
        prev = jnp.where(f == 0, 0.0, yc_s[pl.ds(r0, rt), :])
        yc_s[pl.ds(r0, rt), :] = prev + y

        @pl.when(f == n_f - 1)
        def _():
            yc = yc_s[pl.ds(r0, rt), :].astype(BF16)
            slot = (lax.broadcasted_iota(jnp.int32, (strip, rt), 1) + r0).astype(F32)
            for r in range(n_strips):
                @pl.when(holds(r))
                def _():
                    rows = slice(r * strip, (r + 1) * strip)
                    put = (col_s[0, rows, :] == slot).astype(BF16)
                    back = jnp.dot(put, yc, preferred_element_type=F32)
                    o_ref[rows, :] += col_s[1, rows, :] * back
        return 0

    lax.fori_loop(0, n_tiles, tile, 0)

    @pl.when((e == n_exp - 1) & (f == n_f - 1))
    def _():
        def finish(r, _):
            rows = pl.ds(pl.multiple_of(r * strip, strip), strip)
            o_ref[rows, :] = h_ref[rows, :] + _rms(o_ref[rows, :], gpost_ref[...])
            return 0

        lax.fori_loop(0, tb // strip, finish, 0)


def _moe(h, gain_pre, gain_post, w_router, w_gate, w_up, w_down, tb, tf, rt):
    n, d = h.shape
    n_exp, _, d_ff = w_gate.shape
    w_r = jnp.pad(w_router, ((0, 0), (0, LANES - n_exp)))
    once = pl.Buffered(1)
    strip = 256
    cap = pl.cdiv(tb, rt) * rt
    return pl.pallas_call(
        functools.partial(_moe_kernel, n_exp=n_exp, rt=rt, strip=strip),
        grid=(n // tb, n_exp, d_ff // tf),
        in_specs=[pl.BlockSpec((tb, d), lambda i, e, f: (i, 0), pipeline_mode=once),
                  pl.BlockSpec((1, d), lambda i, e, f: (0, 0)),
                  pl.BlockSpec((1, d), lambda i, e, f: (0, 0)),
                  pl.BlockSpec((d, LANES), lambda i, e, f: (0, 0)),
                  pl.BlockSpec((1, d, tf), lambda i, e, f: (e, 0, f)),
                  pl.BlockSpec((1, d, tf), lambda i, e, f: (e, 0, f)),
                  pl.BlockSpec((1, tf, d), lambda i, e, f: (e, f, 0))],
        out_specs=pl.BlockSpec((tb, d), lambda i, e, f: (i, 0), pipeline_mode=once),
        out_shape=jax.ShapeDtypeStruct((n, d), F32),
        scratch_shapes=[pltpu.VMEM((tb, d), BF16),
                        pltpu.VMEM((tb, LANES), F32),
                        pltpu.VMEM((tb, LANES), BF16),
                        pltpu.VMEM((tb, LANES), F32),
                        pltpu.VMEM((LANES, tb), F32),
                        pltpu.VMEM((cap, d), BF16),
                        pltpu.VMEM((cap, d), F32),
                        pltpu.VMEM((rt, d), F32),
                        pltpu.VMEM((2, tb, 1), F32),
                        pltpu.SMEM(((tb // strip + 1) * n_exp,), jnp.int32)],
        compiler_params=pltpu.CompilerParams(
            dimension_semantics=("parallel", "arbitrary", "arbitrary"),
            vmem_limit_bytes=MOE_VMEM_LIMIT_BYTES),
        name="moe_ffn",
    )(h, gain_pre.reshape(1, d), gain_post.reshape(1, d), w_r, w_gate, w_up, w_down)


def _qkv_kernel(h_ref, g_ref, w_ref, qg_ref, kg_ref, cs_ref, sn_ref, bd_ref,
                q_ref, k_ref, v_ref):
    xn = _rms(h_ref[...], g_ref[...]).astype(BF16)
    qkv = jnp.dot(xn, w_ref[...], preferred_element_type=F32)
    cs = cs_ref[...]
    sn = sn_ref[...]
    bd = bd_ref[...]
    lane = lax.broadcasted_iota(jnp.int32, cs.shape, 1)
    first_half = (lane % HEAD_DIM) < (HEAD_DIM // 2)
    scale = math.log2(math.e) / math.sqrt(HEAD_DIM)

    def norm_rope(x, gain):
        ms = jnp.dot(x * x, bd, preferred_element_type=F32)
        y = x * lax.rsqrt(ms + NORM_EPS) * gain
        partner = jnp.where(first_half,
                            pltpu.roll(y, LANES - HEAD_DIM // 2, axis=1),
                            pltpu.roll(y, HEAD_DIM // 2, axis=1))
        return y * cs + partner * sn

    n_q_tiles = N_HEADS * HEAD_DIM // LANES
    for t in range(n_q_tiles):
        y = norm_rope(qkv[:, t * LANES:(t + 1) * LANES], qg_ref[...]) * scale
        q_ref[0, 2 * t] = y[:, 0:HEAD_DIM].astype(BF16)
        q_ref[0, 2 * t + 1] = y[:, HEAD_DIM:LANES].astype(BF16)
    k0 = N_HEADS * HEAD_DIM
    for t in range(N_KV_HEADS * HEAD_DIM // LANES):
        y = norm_rope(qkv[:, k0 + t * LANES:k0 + (t + 1) * LANES], kg_ref[...])
        k_ref[0, 2 * t] = y[:, 0:HEAD_DIM].astype(BF16)
        k_ref[0, 2 * t + 1] = y[:, HEAD_DIM:LANES].astype(BF16)
    v0 = (N_HEADS + N_KV_HEADS) * HEAD_DIM
    ones = jnp.ones((qkv.shape[0], LANES - HEAD_DIM), BF16)
    for j in range(N_KV_HEADS):
        vj = qkv[:, v0 + j * HEAD_DIM:v0 + (j + 1) * HEAD_DIM].astype(BF16)
        v_ref[0, j] = jnp.concatenate([vj, ones], axis=-1)


def _rope_tables(seq):
    axis_dim = HEAD_DIM // 2
    freqs = ROPE_THETA ** (-jnp.arange(0, axis_dim, 2, dtype=F32) / axis_dim)
    rows = seq // GRID_W
    row_ang = jnp.arange(rows, dtype=F32)[:, None] * freqs
    col_ang = jnp.arange(GRID_W, dtype=F32)[:, None] * freqs
    ang = jnp.concatenate([
        jnp.broadcast_to(row_ang[:, None, :], (rows, GRID_W, freqs.shape[0])),
        jnp.broadcast_to(col_ang[None, :, :], (rows, GRID_W, freqs.shape[0]))], axis=-1)
    ang = ang.reshape(seq, HEAD_DIM // 2)
    cos, sin = jnp.cos(ang), jnp.sin(ang)
    cs = jnp.tile(jnp.concatenate([cos, cos], axis=-1), (1, LANES // HEAD_DIM))
    sn = jnp.tile(jnp.concatenate([-sin, sin], axis=-1), (1, LANES // HEAD_DIM))
    return cs, sn


def _qkv(h, gain, w_qkv, q_gain, k_gain, bsz, seq, tm):
    n, d = h.shape
    width = w_qkv.shape[1]
    perm = jnp.concatenate([jnp.arange(0, HEAD_DIM, 2), jnp.arange(1, HEAD_DIM, 2)])
    n_rot = N_HEADS + N_KV_HEADS
    cols = (jnp.arange(n_rot)[:, None] * HEAD_DIM + perm[None, :]).reshape(-1)
    cols = jnp.concatenate([cols, jnp.arange(n_rot * HEAD_DIM, width)])
    w = w_qkv[:, cols].astype(BF16)
    qg = jnp.tile(q_gain[perm], LANES // HEAD_DIM).reshape(1, LANES)
    kg = jnp.tile(k_gain[perm], LANES // HEAD_DIM).reshape(1, LANES)
    cs, sn = _rope_tables(seq)
    blk = jnp.arange(LANES) // HEAD_DIM
    bd = (blk[:, None] == blk[None, :]).astype(F32) / HEAD_DIM
    per_seq = seq // tm
    return pl.pallas_call(
        _qkv_kernel,
        grid=(n // tm,),
        in_specs=[pl.BlockSpec((tm, d), lambda i: (i, 0)),
                  pl.BlockSpec((1, d), lambda i: (0, 0)),
                  pl.BlockSpec((d, width), lambda i: (0, 0)),
                  pl.BlockSpec((1, LANES), lambda i: (0, 0)),
                  pl.BlockSpec((1, LANES), lambda i: (0, 0)),
                  pl.BlockSpec((tm, LANES), lambda i: (i % per_seq, 0)),
                  pl.BlockSpec((tm, LANES), lambda i: (i % per_seq, 0)),
                  pl.BlockSpec((LANES, LANES), lambda i: (0, 0))],
        out_specs=[pl.BlockSpec((1, N_HEADS, tm, HEAD_DIM),
                                lambda i: (i // per_seq, 0, i % per_seq, 0)),
                   pl.BlockSpec((1, N_KV_HEADS, tm, HEAD_DIM),
                                lambda i: (i // per_seq, 0, i % per_seq, 0)),
                   pl.BlockSpec((1, N_KV_HEADS, tm, LANES),
                                lambda i: (i // per_seq, 0, i % per_seq, 0))],
        out_shape=[jax.ShapeDtypeStruct((bsz, N_HEADS, seq, HEAD_DIM), BF16),
                   jax.ShapeDtypeStruct((bsz, N_KV_HEADS, seq, HEAD_DIM), BF16),
                   jax.ShapeDtypeStruct((bsz, N_KV_HEADS, seq, LANES), BF16)],
        compiler_params=_cparams("parallel"),
        name="qkv_rope",
    )(h, gain.reshape(1, d), w, qg, kg, cs, sn, bd)


def _attn_kernel(q_ref, k_ref, v_ref, o_ref, m_s, acc_s, s_buf, p_buf, a_buf,
                 *, tq, tk, rc, unroll):
    seq = k_ref.shape[2]
    chunks_per_head = tq // rc
    n_chunks = Q_PER_KV * chunks_per_head
    n_steps = (seq // tk) * n_chunks

    m_s[...] = jnp.full_like(m_s, -jnp.inf)
    acc_s[...] = jnp.zeros_like(acc_s)

    def where(n):
        c = n % n_chunks
        return (pl.multiple_of((n // n_chunks) * tk, tk), c // chunks_per_head,
                pl.multiple_of((c % chunks_per_head) * rc, rc))

    def scores(n):
        k0, g, r0 = where(n)
        q = q_ref[0, g, pl.ds(r0, rc), :]
        s_buf[...] = lax.dot_general(q, k_ref[0, 0, pl.ds(k0, tk), :], _NT,
                                     preferred_element_type=F32)

    def softmax(n):
        _, g, r0 = where(n)
        s = s_buf[...]
        m_prev = m_s[g, pl.ds(r0, rc), :]
        m_new = jnp.maximum(m_prev, jnp.max(s, axis=-1, keepdims=True))
        a_buf[...] = jnp.exp2(m_prev - m_new)
        for t in range(tk // LANES):
            p_buf[:, t * LANES:(t + 1) * LANES] = jnp.exp2(
                s[:, t * LANES:(t + 1) * LANES] - m_new).astype(BF16)
        m_s[g, pl.ds(r0, rc), :] = m_new

    def values(n):
        k0, g, r0 = where(n)
        pv = jnp.dot(p_buf[...], v_ref[0, 0, pl.ds(k0, tk), :],
                     preferred_element_type=F32)
        acc_s[g, pl.ds(r0, rc), :] = a_buf[...] * acc_s[g, pl.ds(r0, rc), :] + pv

    scores(0)
    softmax(0)
    scores(1)

    def body(n, _):
        values(n)
        softmax(n + 1)
        scores(n + 2)
        return 0

    lax.fori_loop(0, n_steps - 2, body, 0, unroll=unroll)
    values(n_steps - 2)
    softmax(n_steps - 1)
    values(n_steps - 1)

    outs = []
    for g in range(Q_PER_KV):
        acc = acc_s[g]
        o = acc / pltpu.roll(acc, HEAD_DIM, axis=1)
        outs.append(o[:, 0:HEAD_DIM])
    o_ref[...] = jnp.concatenate(outs, axis=-1).astype(o_ref.dtype)


def _attention(q, k, v, tq, tk, rc):
    bsz, _, seq, _ = q.shape
    n_q = seq // tq
    n_steps = (seq // tk) * Q_PER_KV * (tq // rc)
    unroll = min(8, n_steps - 2)
    return pl.pallas_call(
        functools.partial(_attn_kernel, tq=tq, tk=tk, rc=rc, unroll=unroll),
        grid=(bsz, N_KV_HEADS, n_q),
        in_specs=[pl.BlockSpec((1, Q_PER_KV, tq, HEAD_DIM), lambda b, j, i: (b, j, i, 0)),
                  pl.BlockSpec((1, 1, seq, HEAD_DIM), lambda b, j, i: (b, j, 0, 0)),
                  pl.BlockSpec((1, 1, seq, LANES), lambda b, j, i: (b, j, 0, 0))],
        out_specs=pl.BlockSpec((tq, Q_PER_KV * HEAD_DIM), lambda b, j, i: (b * n_q + i, j)),
        out_shape=jax.ShapeDtypeStruct((bsz * seq, N_HEADS * HEAD_DIM), BF16),
        scratch_shapes=[pltpu.VMEM((Q_PER_KV, tq, LANES), F32),
                        pltpu.VMEM((Q_PER_KV, tq, LANES), F32),
                        pltpu.VMEM((rc, tk), F32),
                        pltpu.VMEM((rc, tk), BF16),
                        pltpu.VMEM((rc, LANES), F32)],
        compiler_params=_cparams("parallel", "parallel", "parallel"),
        name="flash_attn",
    )(q, k, v)


def _proj_res_kernel(x_ref, h_ref, w_ref, g_ref, o_ref):
    mix = jnp.dot(x_ref[...], w_ref[...], preferred_element_type=F32)
    o_ref[...] = h_ref[...] + _rms(mix, g_ref[...])


def _proj_res(x, h, w, gain, tm):
    n, d = h.shape
    k = x.shape[1]
    return pl.pallas_call(
        _proj_res_kernel,
        grid=(n // tm,),
        in_specs=[pl.BlockSpec((tm, k), lambda i: (i, 0)),
                  pl.BlockSpec((tm, d), lambda i: (i, 0)),
                  pl.BlockSpec((k, d), lambda i: (0, 0)),
                  pl.BlockSpec((1, d), lambda i: (0, 0))],
        out_specs=pl.BlockSpec((tm, d), lambda i: (i, 0)),
        out_shape=jax.ShapeDtypeStruct((n, d), F32),
        compiler_params=_cparams("parallel"),
        name="proj_res",
    )(x, h, w, gain.reshape(1, d))


def _s5_layer(h, bsz, seq, gains, w_in, lam_re, lam_im, log_step, b_re, b_im, c_re, c_im,
              d_skip, w_glu, w_out):
    seg_tokens = S5_CHUNK * S5_SEG_CHUNKS
    nseg = seq // seg_tokens
    nseq = bsz * nseg
    rc = min(512, h.shape[0] // S5_CHUNK)
    u_t = _norm_matmul(h, gains[0], w_in.astype(BF16), rc=rc)
    w1, w2t, pw_p, aa_p = _s5_prep(lam_re, lam_im, log_step, b_re, b_im, c_re, c_im, d_skip)
    y_t = _s5_core(u_t, w1, w2t, pw_p, aa_p, nseq, nseg)
    return _s5_out(y_t, h, w_glu.astype(BF16), w_out.astype(BF16), gains[1], rc=rc)


def _attn_layer(h, bsz, seq, gains, w_qkv, q_gain, k_gain, w_out):
    q, k, v = _qkv(h, gains[0], w_qkv, q_gain, k_gain, bsz, seq, tm=512)
    o = _attention(q, k, v, tq=2048, tk=512, rc=512)
    return _proj_res(o, h, w_out.astype(BF16), gains[1], tm=512)


def kernel(x, norm_gains, ssm_w_in, ssm_lambda_re, ssm_lambda_im, ssm_log_step, ssm_b_re,
           ssm_b_im, ssm_c_re, ssm_c_im, ssm_d, ssm_w_glu, ssm_w_out, ffn_w_gate, ffn_w_up,
           ffn_w_down, attn_w_qkv, attn_q_gain, attn_k_gain, attn_w_out, moe_w_router,
           moe_w_gate, moe_w_up, moe_w_down):
    bsz, seq, d = x.shape
    depth = norm_gains.shape[0]
    h = x.reshape(bsz * seq, d)
    for i in range(depth):
        j = i // 2
        g = norm_gains[i]
        if i % 2 == 0:
            h = _s5_layer(h, bsz, seq, g, ssm_w_in[j], ssm_lambda_re[j], ssm_lambda_im[j],
                          ssm_log_step[j], ssm_b_re[j], ssm_b_im[j], ssm_c_re[j], ssm_c_im[j],
                          ssm_d[j], ssm_w_glu[j], ssm_w_out[j])
            h = _ffn(h, g[2], g[3], ffn_w_gate[j].astype(BF16), ffn_w_up[j].astype(BF16),
                     ffn_w_down[j].astype(BF16), tm=512, tf=1408)
        else:
            h = _attn_layer(h, bsz, seq, g, attn_w_qkv[j], attn_q_gain[j], attn_k_gain[j],
                            attn_w_out[j])
            h = _moe(h, g[2], g[3], moe_w_router[j], moe_w_gate[j].astype(BF16),
                     moe_w_up[j].astype(BF16), moe_w_down[j].astype(BF16),
                     tb=2048, tf=896, rt=256)
    return h.reshape(bsz, seq, d)
```

```python
import functools
import math

import jax
import jax.numpy as jnp
from jax import lax
from jax.experimental import pallas as pl
from jax.experimental.pallas import tpu as pltpu

F32 = jnp.float32
BF16 = jnp.bfloat16
NORM_EPS = 1e-6
ROPE_THETA = 10000.0
GRID_W = 64
N_HEADS = 16
N_KV_HEADS = 4
HEAD_DIM = 64
Q_PER_KV = N_HEADS // N_KV_HEADS
SSM_GROUP = 16
SSM_STATE = 64
S5_CHUNK = 16
S5_SEG_CHUNKS = 64
TOP_K = 2
LANES = 128
VMEM_LIMIT_BYTES = 56 * 1024 * 1024
MOE_VMEM_LIMIT_BYTES = 60 * 1024 * 1024

_NT = (((1,), (1,)), ((), ()))


def _cparams(*sem):
    return pltpu.CompilerParams(dimension_semantics=sem, vmem_limit_bytes=VMEM_LIMIT_BYTES)


def _rms(x, gain):
    return x * lax.rsqrt(jnp.mean(x * x, axis=-1, keepdims=True) + NORM_EPS) * gain


def _sigmoid(x):
    return 1.0 / (1.0 + jnp.exp(-x))


def _gelu_tanh(x):
    return x * (0.5 * (1.0 + jnp.tanh(math.sqrt(2.0 / math.pi) * (x + 0.044715 * (x * x * x)))))


def _norm_matmul_kernel(x_ref, g_ref, w_ref, o_ref):
    xn = _rms(x_ref[...], g_ref[...]).astype(BF16)
    o_ref[0] = jnp.dot(xn, w_ref[...], preferred_element_type=F32).astype(o_ref.dtype)


def _norm_matmul(x, gain, w, rc):
    n, d = x.shape
    m = w.shape[1]
    chunks = n // S5_CHUNK
    return pl.pallas_call(
        _norm_matmul_kernel,
        grid=(chunks // rc, S5_CHUNK),
        in_specs=[pl.BlockSpec((rc, d), lambda i, t: (i, t)),
                  pl.BlockSpec((1, d), lambda i, t: (0, 0)),
                  pl.BlockSpec((d, m), lambda i, t: (0, 0))],
        out_specs=pl.BlockSpec((1, rc, m), lambda i, t: (t, i, 0)),
        out_shape=jax.ShapeDtypeStruct((S5_CHUNK, chunks, m), BF16),
        compiler_params=_cparams("parallel", "parallel"),
        name="norm_matmul",
    )(x.reshape(chunks, S5_CHUNK * d), gain.reshape(1, d), w)


def _s5_prep_kernel(lr_ref, li_ref, ls_ref, bt_re_ref, bt_im_ref, c_re_ref, c_im_ref, dt_ref,
                    w1_ref, w2t_ref, pw_ref, aa_ref):
    t_n, s_n, p_n = S5_CHUNK, SSM_GROUP, SSM_STATE
    pair = 2 * s_n
    rows = t_n * pair
    hi = lax.Precision.HIGHEST

    rr = lax.broadcasted_iota(jnp.int32, (rows, rows), 0)
    cc = lax.broadcasted_iota(jnp.int32, (rows, rows), 1)
    same_group = ((rr // s_n) % 2) == ((cc // s_n) % 2)
    tvec = lax.broadcasted_iota(jnp.int32, (t_n, 1), 0).astype(F32)
    jvec = lax.broadcasted_iota(jnp.int32, (S5_SEG_CHUNKS, 1), 0).astype(F32)
    row8 = lax.broadcasted_iota(jnp.int32, (8, LANES), 0)
    own_lanes = ((lax.broadcasted_iota(jnp.int32, (rows, LANES), 1) // p_n)
                 == ((lax.broadcasted_iota(jnp.int32, (rows, LANES), 0) // s_n) % 2))

    def lanes_by_group(tab0, tab1):
        lane = lax.broadcasted_iota(jnp.int32, tab0.shape, 1)
        return jnp.where(lane < p_n, tab0, tab1)

    m_total = None
    for d in range(2):
        lsr, lsi, q_re, q_im = [], [], [], []
        for g in range(2):
            lr = lr_ref[0, d, g:g + 1, :]
            li = li_ref[0, d, g:g + 1, :]
            step = jnp.exp(ls_ref[0, d, g:g + 1, :])
            lsr.append(lr * step)
            lsi.append(li * step)
            mag = jnp.exp(lsr[g])
            a_re, a_im = mag * jnp.cos(lsi[g]), mag * jnp.sin(lsi[g])
            nr, ni = a_re - 1.0, a_im
            den = lr * lr + li * li
            q_re.append((nr * lr + ni * li) / den)
            q_im.append((ni * lr - nr * li) / den)

        def cpow(k, g):
            mag = jnp.exp(lsr[g] * k)
            ang = lsi[g] * k
            return mag * jnp.cos(ang), mag * jnp.sin(ang)

        def table(k):
            t0, t1 = cpow(k, 0), cpow(k, 1)
            return tuple(
                jnp.concatenate([jnp.broadcast_to(tg[part][t:t + 1, :], (s_n, LANES))
                                 for t in range(t_n) for tg in (t0, t1)], axis=0)
                for part in range(2))

        br, bi = bt_re_ref[0, d], bt_im_ref[0, d]
        qr = jnp.concatenate([jnp.broadcast_to(q_re[g], (s_n, LANES)) for g in range(2)], axis=0)
        qi = jnp.concatenate([jnp.broadcast_to(q_im[g], (s_n, LANES)) for g in range(2)], axis=0)
        bb_re = qr * br - qi * bi
        bb_im = qr * bi + qi * br
        cr, ci = c_re_ref[0, d], c_im_ref[0, d]

        def outer(x_re, x_im, k):
            pe_re, pe_im = table(k)
            xe_re = jnp.concatenate([x_re] * t_n, axis=0)
            xe_im = jnp.concatenate([x_im] * t_n, axis=0)
            return xe_re * pe_re - xe_im * pe_im, xe_re * pe_im + xe_im * pe_re

        if d == 0:
            l_re, l_im = outer(bb_re, bb_im, -tvec)
            rt_re, rt_im = outer(cr, ci, tvec)
            ws_re, ws_im = outer(bb_re, bb_im, (t_n - 1.0) - tvec)
            wy_re, wy_im = outer(cr, ci, tvec + 1.0)
            mask = (rr // pair) <= (cc // pair)
            kseg = t_n * jvec
        else:
            l_re, l_im = outer(bb_re, bb_im, tvec)
            rt_re, rt_im = outer(cr, ci, -tvec)
            ws_re, ws_im = l_re, l_im
            wy_re, wy_im = outer(cr, ci, t_n - tvec)
            mask = (rr // pair) >= (cc // pair)
            kseg = t_n * ((S5_SEG_CHUNKS - 1.0) - jvec)
        kern = (lax.dot_general(l_re[:, 0:p_n], rt_re[:, 0:p_n], _NT, precision=hi,
                                preferred_element_type=F32)
                - lax.dot_general(l_im[:, 0:p_n], rt_im[:, 0:p_n], _NT, precision=hi,
                                  preferred_element_type=F32))
        kern = jnp.where(mask & same_group, kern, 0.0)
        m_total = kern if m_total is None else m_total + kern

        ycols = rows
        for k2, val in ((2 * d, ws_re), (2 * d + 1, ws_im)):
            w1_ref[0, :, ycols + k2 * LANES:ycols + (k2 + 1) * LANES] = jnp.where(
                own_lanes, val, 0.0).astype(w1_ref.dtype)
        for k2, val in ((2 * d, wy_re), (2 * d + 1, -wy_im)):
            w2t_ref[0, :, k2 * LANES:(k2 + 1) * LANES] = jnp.where(
                own_lanes, val, 0.0).astype(w2t_ref.dtype)
        p0, p1 = cpow(kseg, 0), cpow(kseg, 1)
        pw_ref[0, :, 2 * d * LANES:(2 * d + 1) * LANES] = lanes_by_group(p0[0], p1[0])
        pw_ref[0, :, (2 * d + 1) * LANES:(2 * d + 2) * LANES] = lanes_by_group(p0[1], p1[1])
        one = jnp.ones((1, 1), F32)
        c0, c1 = cpow(float(t_n) * one, 0), cpow(float(t_n) * one, 1)
        s0, s1 = (cpow(float(t_n * S5_SEG_CHUNKS) * one, 0),
                  cpow(float(t_n * S5_SEG_CHUNKS) * one, 1))
        for part in range(2):
            chunk_pow = lanes_by_group(c0[part], c1[part])
            seg_pow = lanes_by_group(s0[part], s1[part])
            aa_ref[0, :, (2 * d + part) * LANES:(2 * d + part + 1) * LANES] = jnp.where(
                row8 == 0, chunk_pow, jnp.where(row8 == 1, seg_pow, 0.0))

    w1_ref[0, :, 0:rows] = (m_total + jnp.where(rr == cc, dt_ref[0], 0.0)).astype(w1_ref.dtype)


def _s5_prep(lam_re, lam_im, log_step, b_re, b_im, c_re, c_im, d_skip):
    g_n = lam_re.shape[1]
    pairs = g_n // 2
    pair = 2 * SSM_GROUP
    rows = S5_CHUNK * pair

    def states(x):
        x = jnp.transpose(x.reshape(2, pairs, 2, SSM_STATE), (1, 0, 2, 3))
        return jnp.tile(x, (1, 1, 1, LANES // SSM_STATE))

    def per_channel(x):
        x = jnp.transpose(x.reshape(2, pairs, pair, SSM_STATE), (1, 0, 2, 3))
        return jnp.tile(x, (1, 1, 1, LANES // SSM_STATE))

    ls = jnp.transpose(log_step.reshape(2, pairs, 2), (1, 0, 2))[..., None]
    bt_re = per_channel(jnp.swapaxes(b_re, 2, 3))
    bt_im = per_channel(jnp.swapaxes(b_im, 2, 3))
    dt = jnp.tile(d_skip.reshape(pairs, 1, pair), (1, 1, S5_CHUNK))

    def spec(*blk):
        return pl.BlockSpec((1,) + blk, lambda g: (g,) + (0,) * len(blk))

    return pl.pallas_call(
        _s5_prep_kernel,
        grid=(pairs,),
        in_specs=[spec(2, 2, LANES), spec(2, 2, LANES), spec(2, 2, 1),
                  spec(2, pair, LANES), spec(2, pair, LANES),
                  spec(2, pair, LANES), spec(2, pair, LANES), spec(1, rows)],
        out_specs=[spec(rows, rows + 4 * LANES), spec(rows, 4 * LANES),
                   spec(S5_SEG_CHUNKS, 4 * LANES), spec(8, 4 * LANES)],
        out_shape=[jax.ShapeDtypeStruct((pairs, rows, rows + 4 * LANES), BF16),
                   jax.ShapeDtypeStruct((pairs, rows, 4 * LANES), BF16),
                   jax.ShapeDtypeStruct((pairs, S5_SEG_CHUNKS, 4 * LANES), F32),
                   jax.ShapeDtypeStruct((pairs, 8, 4 * LANES), F32)],
        compiler_params=_cparams("parallel"),
        name="s5_prep",
    )(states(lam_re), states(lam_im), ls, bt_re, bt_im, per_channel(c_re), per_channel(c_im), dt)


S5_PAIR_LANES = 2 * SSM_GROUP
S5_PAIRS_PER_SLAB = LANES // S5_PAIR_LANES
S5_SEG_PITCH = S5_SEG_CHUNKS + 8
S5_SEG_BASE = 8


def _s5_core_kernel(x_ref, w1_ref, w2t_ref, pw_ref, aa_ref, y_ref, z_s, zs_s, x_s, u_s, xb_s,
                    *, nseq, nseg, strip):
    n_chunks = S5_SEG_CHUNKS
    rows = n_chunks * nseq
    ycols = u_s.shape[1]
    per_tile = LANES // S5_PAIR_LANES
    lane_grp = lax.broadcasted_iota(jnp.int32, (strip, LANES), 1) // S5_PAIR_LANES
    zero = jnp.zeros((nseq, LANES), F32)

    def seg_rows(j):
        return pl.ds(S5_SEG_BASE + j, nseq, stride=S5_SEG_PITCH)

    for q in range(S5_PAIRS_PER_SLAB):
        def gather(r, _, q=q):
            rws = pl.ds(pl.multiple_of(r * strip, strip), strip)
            for j in range(ycols // LANES):
                acc = None
                for i in range(per_tile):
                    xt = x_ref[per_tile * j + i, rws, :].astype(F32)
                    shift = (S5_PAIR_LANES * (i - q)) % LANES
                    if shift:
                        xt = pltpu.roll(xt, shift, axis=1)
                    acc = xt if acc is None else jnp.where(lane_grp == i, xt, acc)
                u_s[rws, j * LANES:(j + 1) * LANES] = acc.astype(BF16)
            return 0

        lax.fori_loop(0, rows // strip, gather, 0)
        z_s[...] = jnp.dot(u_s[...], w1_ref[q], preferred_element_type=F32)

        for b in range(nseq):
            dst = slice(S5_SEG_BASE + b * S5_SEG_PITCH, S5_SEG_BASE + b * S5_SEG_PITCH + n_chunks)
            for k in range(4):
                zs_s[k, dst, :] = z_s[b * n_chunks:(b + 1) * n_chunks,
                                      ycols + k * LANES:ycols + (k + 1) * LANES]
        x_s[0, seg_rows(0), :] = zero
        x_s[1, seg_rows(0), :] = zero
        x_s[2, seg_rows(n_chunks - 1), :] = zero
        x_s[3, seg_rows(n_chunks - 1), :] = zero

        aa = aa_ref[q]
        a_fr, a_fi = aa[0:1, 0:LANES], aa[0:1, LANES:2 * LANES]
        a_br, a_bi = aa[0:1, 2 * LANES:3 * LANES], aa[0:1, 3 * LANES:4 * LANES]
        g_fr, g_fi = aa[1:2, 0:LANES], aa[1:2, LANES:2 * LANES]
        g_br, g_bi = aa[1:2, 2 * LANES:3 * LANES], aa[1:2, 3 * LANES:4 * LANES]

        def fwd(j, carry):
            xr, xi = carry
            nr = a_fr * xr - a_fi * xi + zs_s[0, seg_rows(j), :]
            ni = a_fr * xi + a_fi * xr + zs_s[1, seg_rows(j), :]
            x_s[0, seg_rows(j + 1), :] = nr
            x_s[1, seg_rows(j + 1), :] = ni
            return nr, ni

        def bwd(i, carry):
            xr, xi = carry
            j = n_chunks - 1 - i
            nr = a_br * xr - a_bi * xi + zs_s[2, seg_rows(j), :]
            ni = a_br * xi + a_bi * xr + zs_s[3, seg_rows(j), :]
            x_s[2, seg_rows(j - 1), :] = nr
            x_s[3, seg_rows(j - 1), :] = ni
            return nr, ni

        ef_r, ef_i = lax.fori_loop(0, n_chunks, fwd, (zero, zero))
        eb_r, eb_i = lax.fori_loop(0, n_chunks, bwd, (zero, zero))

        def seg_carries(e_r, e_i, g_r, g_i, reverse):
            out_r = [None] * nseq
            out_i = [None] * nseq
            for b in range(nseq // nseg):
                c_r = jnp.zeros((1, LANES), F32)
                c_i = jnp.zeros((1, LANES), F32)
                order = range(nseg - 1, -1, -1) if reverse else range(nseg)
                for s in order:
                    r = b * nseg + s
                    out_r[r], out_i[r] = c_r, c_i
                    n_r = g_r * c_r - g_i * c_i + e_r[r:r + 1, :]
                    n_i = g_r * c_i + g_i * c_r + e_i[r:r + 1, :]
                    c_r, c_i = n_r, n_i
            return out_r, out_i

        cf_r, cf_i = seg_carries(ef_r, ef_i, g_fr, g_fi, False)
        cb_r, cb_i = seg_carries(eb_r, eb_i, g_br, g_bi, True)

        p = pw_ref[q]
        p_fr, p_fi = p[:, 0:LANES], p[:, LANES:2 * LANES]
        p_br, p_bi = p[:, 2 * LANES:3 * LANES], p[:, 3 * LANES:4 * LANES]
        for b in range(nseq):
            src = slice(S5_SEG_BASE + b * S5_SEG_PITCH, S5_SEG_BASE + b * S5_SEG_PITCH + n_chunks)
            dst = slice(b * n_chunks, (b + 1) * n_chunks)
            xb_s[dst, 0:LANES] = (x_s[0, src, :] + (p_fr * cf_r[b] - p_fi * cf_i[b])).astype(BF16)
            xb_s[dst, LANES:2 * LANES] = (
                x_s[1, src, :] + (p_fr * cf_i[b] + p_fi * cf_r[b])).astype(BF16)
            xb_s[dst, 2 * LANES:3 * LANES] = (
                x_s[2, src, :] + (p_br * cb_r[b] - p_bi * cb_i[b])).astype(BF16)
            xb_s[dst, 3 * LANES:4 * LANES] = (
                x_s[3, src, :] + (p_br * cb_i[b] + p_bi * cb_r[b])).astype(BF16)

        z_s[:, 0:ycols] += lax.dot_general(xb_s[...], w2t_ref[q], _NT,
                                           preferred_element_type=F32)

        def scatter(r, _, q=q):
            rws = pl.ds(pl.multiple_of(r * strip, strip), strip)
            for j in range(ycols // LANES):
                yq = z_s[rws, j * LANES:(j + 1) * LANES]
                for i in range(per_tile):
                    shift = (S5_PAIR_LANES * (q - i)) % LANES
                    yt = pltpu.roll(yq, shift, axis=1) if shift else yq
                    lanes_q = slice(q * S5_PAIR_LANES, (q + 1) * S5_PAIR_LANES)
                    y_ref[per_tile * j + i, rws, lanes_q] = yt[:, lanes_q].astype(y_ref.dtype)
            return 0

        lax.fori_loop(0, rows // strip, scatter, 0)


def _s5_core(u_t, w1, w2t, pw, aa, nseq, nseg):
    t_n, rows, d = u_t.shape
    width = w1.shape[1]
    pps = S5_PAIRS_PER_SLAB
    slab_rows = S5_SEG_BASE + nseq * S5_SEG_PITCH
    return pl.pallas_call(
        functools.partial(_s5_core_kernel, nseq=nseq, nseg=nseg, strip=128),
        grid=(d // LANES,),
        in_specs=[pl.BlockSpec((t_n, rows, LANES), lambda o: (0, 0, o)),
                  pl.BlockSpec((pps, width, w1.shape[2]), lambda o: (o, 0, 0)),
                  pl.BlockSpec((pps, width, w2t.shape[2]), lambda o: (o, 0, 0)),
                  pl.BlockSpec((pps, S5_SEG_CHUNKS, pw.shape[2]), lambda o: (o, 0, 0)),
                  pl.BlockSpec((pps, 8, aa.shape[2]), lambda o: (o, 0, 0))],
        out_specs=pl.BlockSpec((t_n, rows, LANES), lambda o: (0, 0, o)),
        out_shape=jax.ShapeDtypeStruct((t_n, rows, d), BF16),
        scratch_shapes=[pltpu.VMEM((rows, w1.shape[2]), F32),
                        pltpu.VMEM((4, slab_rows, LANES), F32),
                        pltpu.VMEM((4, slab_rows, LANES), F32),
                        pltpu.VMEM((rows, width), BF16),
                        pltpu.VMEM((rows, 4 * LANES), BF16)],
        compiler_params=_cparams("parallel"),
        name="s5_core",
    )(u_t, w1, w2t, pw, aa)


def _s5_out_kernel(y_ref, h_ref, wglu_ref, wout_ref, g_ref, o_ref):
    g = _gelu_tanh(y_ref[0].astype(F32))
    z = jnp.dot(g.astype(BF16), wglu_ref[...], preferred_element_type=F32)
    g2 = g * _sigmoid(z)
    mix = jnp.dot(g2.astype(BF16), wout_ref[...], preferred_element_type=F32)
    o_ref[...] = h_ref[...] + _rms(mix, g_ref[...])


def _s5_out(y_t, h, w_glu, w_out, gain, rc):
    n, d = h.shape
    chunks = n // S5_CHUNK
    out = pl.pallas_call(
        _s5_out_kernel,
        grid=(chunks // rc, S5_CHUNK),
        in_specs=[pl.BlockSpec((1, rc, d), lambda i, t: (t, i, 0)),
                  pl.BlockSpec((rc, d), lambda i, t: (i, t)),
                  pl.BlockSpec((d, d), lambda i, t: (0, 0)),
                  pl.BlockSpec((d, d), lambda i, t: (0, 0)),
                  pl.BlockSpec((1, d), lambda i, t: (0, 0))],
        out_specs=pl.BlockSpec((rc, d), lambda i, t: (i, t)),
        out_shape=jax.ShapeDtypeStruct((chunks, S5_CHUNK * d), F32),
        compiler_params=_cparams("parallel", "parallel"),
        name="s5_out",
    )(y_t, h.reshape(chunks, S5_CHUNK * d), w_glu, w_out, gain.reshape(1, d))
    return out.reshape(n, d)


def _ffn_kernel(h_ref, gpre_ref, gpost_ref, wg_ref, wu_ref, wd_ref, o_ref, hn_s, acc_s):
    f = pl.program_id(1)

    @pl.when(f == 0)
    def _():
        hn_s[...] = _rms(h_ref[...], gpre_ref[...]).astype(BF16)
        acc_s[...] = jnp.zeros_like(acc_s)

    hn = hn_s[...]
    a = jnp.dot(hn, wg_ref[...], preferred_element_type=F32)
    u = jnp.dot(hn, wu_ref[...], preferred_element_type=F32)
    act = a * _sigmoid(a) * u
    acc_s[...] += jnp.dot(act.astype(BF16), wd_ref[...], preferred_element_type=F32)

    @pl.when(f == pl.num_programs(1) - 1)
    def _():
        o_ref[...] = h_ref[...] + _rms(acc_s[...], gpost_ref[...])


def _ffn(h, gain_pre, gain_post, w_gate, w_up, w_down, tm, tf):
    n, d = h.shape
    d_ff = w_gate.shape[1]
    return pl.pallas_call(
        _ffn_kernel,
        grid=(n // tm, d_ff // tf),
        in_specs=[pl.BlockSpec((tm, d), lambda i, f: (i, 0)),
                  pl.BlockSpec((1, d), lambda i, f: (0, 0)),
                  pl.BlockSpec((1, d), lambda i, f: (0, 0)),
                  pl.BlockSpec((d, tf), lambda i, f: (0, f)),
                  pl.BlockSpec((d, tf), lambda i, f: (0, f)),
                  pl.BlockSpec((tf, d), lambda i, f: (f, 0))],
        out_specs=pl.BlockSpec((tm, d), lambda i, f: (i, 0)),
        out_shape=jax.ShapeDtypeStruct((n, d), F32),
        scratch_shapes=[pltpu.VMEM((tm, d), BF16), pltpu.VMEM((tm, d), F32)],
        compiler_params=_cparams("parallel", "arbitrary"),
        name="ffn",
    )(h, gain_pre.reshape(1, d), gain_post.reshape(1, d), w_gate, w_up, w_down)


def _moe_kernel(h_ref, gpre_ref, gpost_ref, wr_ref, wg_ref, wu_ref, wd_ref, o_ref,
                hn_s, gate_s, pos_s, post_s, xc_s, yc_s, xg_s, col_s, cnt_s,
                *, n_exp, rt, strip):
    e = pl.program_id(1)
    f = pl.program_id(2)
    tb, d = hn_s.shape
    n_f = pl.num_programs(2)
    n_strips = tb // strip

    def put_counts(r, total):
        lane1 = lax.broadcasted_iota(jnp.int32, total.shape, 1)
        for x in range(n_exp):
            cnt_s[r * n_exp + x] = jnp.sum(jnp.where(lane1 == x, total, 0.0)).astype(jnp.int32)

    @pl.when((e == 0) & (f == 0))
    def _():
        before = (lax.broadcasted_iota(jnp.int32, (strip, strip), 1)
                  < lax.broadcasted_iota(jnp.int32, (strip, strip), 0)).astype(BF16)

        def route(r, total):
            put_counts(r, total)
            rows = pl.ds(pl.multiple_of(r * strip, strip), strip)
            xn = _rms(h_ref[rows, :], gpre_ref[...])
            hn_s[rows, :] = xn.astype(BF16)
            o_ref[rows, :] = jnp.zeros((strip, d), F32)
            logits = jnp.dot(xn, wr_ref[...], precision=lax.Precision.HIGHEST,
                             preferred_element_type=F32)
            lane = lax.broadcasted_iota(jnp.int32, logits.shape, 1)
            neg = jnp.float32(-jnp.inf)
            logits = jnp.where(lane < n_exp, logits, neg)
            m1 = jnp.max(logits, axis=-1, keepdims=True)
            i1 = jnp.min(jnp.where(logits == m1, lane, LANES), axis=-1, keepdims=True)
            rest = jnp.where(lane == i1, neg, logits)
            m2 = jnp.max(rest, axis=-1, keepdims=True)
            i2 = jnp.min(jnp.where(rest == m2, lane, LANES), axis=-1, keepdims=True)
            e2 = jnp.exp(m2 - m1)
            gate_s[rows, :] = (jnp.where(lane == i1, 1.0 / (1.0 + e2), 0.0)
                               + jnp.where(lane == i2, e2 / (1.0 + e2), 0.0))
            sel = ((lane == i1) | (lane == i2)).astype(F32)
            earlier = jnp.dot(before, sel.astype(BF16), preferred_element_type=F32) + total
            pos_s[rows, :] = jnp.where(sel > 0, earlier, -1.0)
            return total + jnp.sum(sel, axis=0, keepdims=True)

        total = lax.fori_loop(0, n_strips, route, jnp.zeros((1, LANES), F32))
        put_counts(n_strips, total)

        for r in range(tb // strip):
            post_s[:, r * strip:(r + 1) * strip] = pos_s[r * strip:(r + 1) * strip, :].T

    @pl.when(f == n_f - 1)
    def _():
        lane = lax.broadcasted_iota(jnp.int32, (tb, LANES), 1)
        col_s[0] = jnp.sum(jnp.where(lane == e, pos_s[...], 0.0), axis=-1, keepdims=True)
        col_s[1] = jnp.sum(jnp.where(lane == e, gate_s[...], 0.0), axis=-1, keepdims=True)

    n_tiles = (cnt_s[n_strips * n_exp + e] + rt - 1) // rt

    def tile(i, _):
        r0 = pl.multiple_of(i * rt, 16)

        def holds(r):
            return ((cnt_s[r * n_exp + e] < r0 + rt) & (cnt_s[(r + 1) * n_exp + e] > r0))

        @pl.when(f == 0)
        def _():
            xg_s[...] = jnp.zeros_like(xg_s)
            slot = (lax.broadcasted_iota(jnp.int32, (rt, strip), 0) + r0).astype(F32)
            for r in range(n_strips):
                @pl.when(holds(r))
                def _():
                    cols = slice(r * strip, (r + 1) * strip)
                    pick = (post_s[pl.ds(e, 1), cols] == slot).astype(BF16)
                    xg_s[...] += jnp.dot(pick, hn_s[cols, :], preferred_element_type=F32)
            xc_s[pl.ds(r0, rt), :] = xg_s[...].astype(BF16)

        xc = xc_s[pl.ds(r0, rt), :]
        a = jnp.dot(xc, wg_ref[0], preferred_element_type=F32)
        u = jnp.dot(xc, wu_ref[0], preferred_element_type=F32)
        act = (a * _sigmoid(a) * u).astype(BF16)
        y = jnp.dot(act, wd_ref[0], preferred_element_type=F32)
        prev = jnp.where(f == 0, 0.0, yc_s[pl.ds(r0, rt), :])
        yc_s[pl.ds(r0, rt), :] = prev + y

        @pl.when(f == n_f - 1)
        def _():
            yc = yc_s[pl.ds(r0, rt), :].astype(BF16)
            slot = (lax.broadcasted_iota(jnp.int32, (strip, rt), 1) + r0).astype(F32)
            for r in range(n_strips):
                @pl.when(holds(r))
                def _():
                    rows = slice(r * strip, (r + 1) * strip)
                    put = (col_s[0, rows, :] == slot).astype(BF16)
                    back = jnp.dot(put, yc, preferred_element_type=F32)
                    o_ref[rows, :] += col_s[1, rows, :] * back
        return 0

    lax.fori_loop(0, n_tiles, tile, 0)

    @pl.when((e == n_exp - 1) & (f == n_f - 1))
    def _():
        def finish(r, _):
            rows = pl.ds(pl.multiple_of(r * strip, strip), strip)
            o_ref[rows, :] = h_ref[rows, :] + _rms(o_ref[rows, :], gpost_ref[...])
            return 0

        lax.fori_loop(0, tb // strip, finish, 0)


def _moe(h, gain_pre, gain_post, w_router, w_gate, w_up, w_down, tb, tf, rt):
    n, d = h.shape
    n_exp, _, d_ff = w_gate.shape
    w_r = jnp.pad(w_router, ((0, 0), (0, LANES - n_exp)))
    once = pl.Buffered(1)
    strip = 256
    cap = pl.cdiv(tb, rt) * rt
    return pl.pallas_call(
        functools.partial(_moe_kernel, n_exp=n_exp, rt=rt, strip=strip),
        grid=(n // tb, n_exp, d_ff // tf),
        in_specs=[pl.BlockSpec((tb, d), lambda i, e, f: (i, 0), pipeline_mode=once),
                  pl.BlockSpec((1, d), lambda i, e, f: (0, 0)),
                  pl.BlockSpec((1, d), lambda i, e, f: (0, 0)),
                  pl.BlockSpec((d, LANES), lambda i, e, f: (0, 0)),
                  pl.BlockSpec((1, d, tf), lambda i, e, f: (e, 0, f)),
                  pl.BlockSpec((1, d, tf), lambda i, e, f: (e, 0, f)),
                  pl.BlockSpec((1, tf, d), lambda i, e, f: (e, f, 0))],
        out_specs=pl.BlockSpec((tb, d), lambda i, e, f: (i, 0), pipeline_mode=once),
        out_shape=jax.ShapeDtypeStruct((n, d), F32),
        scratch_shapes=[pltpu.VMEM((tb, d), BF16),
                        pltpu.VMEM((tb, LANES), F32),
                        pltpu.VMEM((tb, LANES), F32),
                        pltpu.VMEM((LANES, tb), F32),
                        pltpu.VMEM((cap, d), BF16),
                        pltpu.VMEM((cap, d), F32),
                        pltpu.VMEM((rt, d), F32),
                        pltpu.VMEM((2, tb, 1), F32),
                        pltpu.SMEM(((tb // strip + 1) * n_exp,), jnp.int32)],
        compiler_params=pltpu.CompilerParams(
            dimension_semantics=("parallel", "arbitrary", "arbitrary"),
            vmem_limit_bytes=MOE_VMEM_LIMIT_BYTES),
        name="moe_ffn",
    )(h, gain_pre.reshape(1, d), gain_post.reshape(1, d), w_r, w_gate, w_up, w_down)


def _qkv_kernel(h_ref, g_ref, w_ref, qg_ref, kg_ref, cs_ref, sn_ref, bd_ref,
                q_ref, k_ref, v_ref):
    xn = _rms(h_ref[...], g_ref[...]).astype(BF16)
    qkv = jnp.dot(xn, w_ref[...], preferred_element_type=F32)
    cs = cs_ref[...]
    sn = sn_ref[...]
    bd = bd_ref[...]
    lane = lax.broadcasted_iota(jnp.int32, cs.shape, 1)
    first_half = (lane % HEAD_DIM) < (HEAD_DIM // 2)
    scale = math.log2(math.e) / math.sqrt(HEAD_DIM)

    def norm_rope(x, gain):
        ms = jnp.dot(x * x, bd, preferred_element_type=F32)
        y = x * lax.rsqrt(ms + NORM_EPS) * gain
        partner = jnp.where(first_half,
                            pltpu.roll(y, LANES - HEAD_DIM // 2, axis=1),
                            pltpu.roll(y, HEAD_DIM // 2, axis=1))
        return y * cs + partner * sn

    n_q_tiles = N_HEADS * HEAD_DIM // LANES
    for t in range(n_q_tiles):
        y = norm_rope(qkv[:, t * LANES:(t + 1) * LANES], qg_ref[...]) * scale
        q_ref[0, 2 * t] = y[:, 0:HEAD_DIM].astype(BF16)
        q_ref[0, 2 * t + 1] = y[:, HEAD_DIM:LANES].astype(BF16)
    k0 = N_HEADS * HEAD_DIM
    for t in range(N_KV_HEADS * HEAD_DIM // LANES):
        y = norm_rope(qkv[:, k0 + t * LANES:k0 + (t + 1) * LANES], kg_ref[...])
        k_ref[0, 2 * t] = y[:, 0:HEAD_DIM].astype(BF16)
        k_ref[0, 2 * t + 1] = y[:, HEAD_DIM:LANES].astype(BF16)
    v0 = (N_HEADS + N_KV_HEADS) * HEAD_DIM
    ones = jnp.ones((qkv.shape[0], LANES - HEAD_DIM), BF16)
    for j in range(N_KV_HEADS):
        vj = qkv[:, v0 + j * HEAD_DIM:v0 + (j + 1) * HEAD_DIM].astype(BF16)
        v_ref[0, j] = jnp.concatenate([vj, ones], axis=-1)


def _rope_tables(seq):
    axis_dim = HEAD_DIM // 2
    freqs = ROPE_THETA ** (-jnp.arange(0, axis_dim, 2, dtype=F32) / axis_dim)
    rows = seq // GRID_W
    row_ang = jnp.arange(rows, dtype=F32)[:, None] * freqs
    col_ang = jnp.arange(GRID_W, dtype=F32)[:, None] * freqs
    ang = jnp.concatenate([
        jnp.broadcast_to(row_ang[:, None, :], (rows, GRID_W, freqs.shape[0])),
        jnp.broadcast_to(col_ang[None, :, :], (rows, GRID_W, freqs.shape[0]))], axis=-1)
    ang = ang.reshape(seq, HEAD_DIM // 2)
    cos, sin = jnp.cos(ang), jnp.sin(ang)
    cs = jnp.tile(jnp.concatenate([cos, cos], axis=-1), (1, LANES // HEAD_DIM))
    sn = jnp.tile(jnp.concatenate([-sin, sin], axis=-1), (1, LANES // HEAD_DIM))
    return cs, sn


def _qkv(h, gain, w_qkv, q_gain, k_gain, bsz, seq, tm):
    n, d = h.shape
    width = w_qkv.shape[1]
    perm = jnp.concatenate([jnp.arange(0, HEAD_DIM, 2), jnp.arange(1, HEAD_DIM, 2)])
    n_rot = N_HEADS + N_KV_HEADS
    cols = (jnp.arange(n_rot)[:, None] * HEAD_DIM + perm[None, :]).reshape(-1)
    cols = jnp.concatenate([cols, jnp.arange(n_rot * HEAD_DIM, width)])
    w = w_qkv[:, cols].astype(BF16)
    qg = jnp.tile(q_gain[perm], LANES // HEAD_DIM).reshape(1, LANES)
    kg = jnp.tile(k_gain[perm], LANES // HEAD_DIM).reshape(1, LANES)
    cs, sn = _rope_tables(seq)
    blk = jnp.arange(LANES) // HEAD_DIM
    bd = (blk[:, None] == blk[None, :]).astype(F32) / HEAD_DIM
    per_seq = seq // tm
    return pl.pallas_call(
        _qkv_kernel,
        grid=(n // tm,),
        in_specs=[pl.BlockSpec((tm, d), lambda i: (i, 0)),
                  pl.BlockSpec((1, d), lambda i: (0, 0)),
                  pl.BlockSpec((d, width), lambda i: (0, 0)),
                  pl.BlockSpec((1, LANES), lambda i: (0, 0)),
                  pl.BlockSpec((1, LANES), lambda i: (0, 0)),
                  pl.BlockSpec((tm, LANES), lambda i: (i % per_seq, 0)),
                  pl.BlockSpec((tm, LANES), lambda i: (i % per_seq, 0)),
                  pl.BlockSpec((LANES, LANES), lambda i: (0, 0))],
        out_specs=[pl.BlockSpec((1, N_HEADS, tm, HEAD_DIM),
                                lambda i: (i // per_seq, 0, i % per_seq, 0)),
                   pl.BlockSpec((1, N_KV_HEADS, tm, HEAD_DIM),
                                lambda i: (i // per_seq, 0, i % per_seq, 0)),
                   pl.BlockSpec((1, N_KV_HEADS, tm, LANES),
                                lambda i: (i // per_seq, 0, i % per_seq, 0))],
        out_shape=[jax.ShapeDtypeStruct((bsz, N_HEADS, seq, HEAD_DIM), BF16),
                   jax.ShapeDtypeStruct((bsz, N_KV_HEADS, seq, HEAD_DIM), BF16),
                   jax.ShapeDtypeStruct((bsz, N_KV_HEADS, seq, LANES), BF16)],
        compiler_params=_cparams("parallel"),
        name="qkv_rope",
    )(h, gain.reshape(1, d), w, qg, kg, cs, sn, bd)


def _attn_kernel(q_ref, k_ref, v_ref, o_ref, m_s, acc_s, s_buf, p_buf, a_buf,
                 *, tq, tk, rc, unroll):
    seq = k_ref.shape[2]
    chunks_per_head = tq // rc
    n_chunks = Q_PER_KV * chunks_per_head
    n_steps = (seq // tk) * n_chunks

    m_s[...] = jnp.full_like(m_s, -jnp.inf)
    acc_s[...] = jnp.zeros_like(acc_s)

    def where(n):
        c = n % n_chunks
        return (pl.multiple_of((n // n_chunks) * tk, tk), c // chunks_per_head,
                pl.multiple_of((c % chunks_per_head) * rc, rc))

    def scores(n):
        k0, g, r0 = where(n)
        q = q_ref[0, g, pl.ds(r0, rc), :]
        s_buf[...] = lax.dot_general(q, k_ref[0, 0, pl.ds(k0, tk), :], _NT,
                                     preferred_element_type=F32)

    def softmax(n):
        _, g, r0 = where(n)
        s = s_buf[...]
        m_prev = m_s[g, pl.ds(r0, rc), :]
        m_new = jnp.maximum(m_prev, jnp.max(s, axis=-1, keepdims=True))
        a_buf[...] = jnp.exp2(m_prev - m_new)
        for t in range(tk // LANES):
            p_buf[:, t * LANES:(t + 1) * LANES] = jnp.exp2(
                s[:, t * LANES:(t + 1) * LANES] - m_new).astype(BF16)
        m_s[g, pl.ds(r0, rc), :] = m_new

    def values(n):
        k0, g, r0 = where(n)
        pv = jnp.dot(p_buf[...], v_ref[0, 0, pl.ds(k0, tk), :],
                     preferred_element_type=F32)
        acc_s[g, pl.ds(r0, rc), :] = a_buf[...] * acc_s[g, pl.ds(r0, rc), :] + pv

    scores(0)
    softmax(0)
    scores(1)

    def body(n, _):
        values(n)
        softmax(n + 1)
        scores(n + 2)
        return 0

    lax.fori_loop(0, n_steps - 2, body, 0, unroll=unroll)
    values(n_steps - 2)
    softmax(n_steps - 1)
    values(n_steps - 1)

    outs = []
    for g in range(Q_PER_KV):
        acc = acc_s[g]
        o = acc / pltpu.roll(acc, HEAD_DIM, axis=1)
        outs.append(o[:, 0:HEAD_DIM])
    o_ref[...] = jnp.concatenate(outs, axis=-1).astype(o_ref.dtype)


def _attention(q, k, v, tq, tk, rc):
    bsz, _, seq, _ = q.shape
    n_q = seq // tq
    n_steps = (seq // tk) * Q_PER_KV * (tq // rc)
    unroll = min(8, n_steps - 2)
    return pl.pallas_call(
        functools.partial(_attn_kernel, tq=tq, tk=tk, rc=rc, unroll=unroll),
        grid=(bsz, N_KV_HEADS, n_q),
        in_specs=[pl.BlockSpec((1, Q_PER_KV, tq, HEAD_DIM), lambda b, j, i: (b, j, i, 0)),
                  pl.BlockSpec((1, 1, seq, HEAD_DIM), lambda b, j, i: (b, j, 0, 0)),
                  pl.BlockSpec((1, 1, seq, LANES), lambda b, j, i: (b, j, 0, 0))],
        out_specs=pl.BlockSpec((tq, Q_PER_KV * HEAD_DIM), lambda b, j, i: (b * n_q + i, j)),
        out_shape=jax.ShapeDtypeStruct((bsz * seq, N_HEADS * HEAD_DIM), BF16),
        scratch_shapes=[pltpu.VMEM((Q_PER_KV, tq, LANES), F32),
                        pltpu.VMEM((Q_PER_KV, tq, LANES), F32),
                        pltpu.VMEM((rc, tk), F32),
                        pltpu.VMEM((rc, tk), BF16),
                        pltpu.VMEM((rc, LANES), F32)],
        compiler_params=_cparams("parallel", "parallel", "parallel"),
        name="flash_attn",
    )(q, k, v)


def _proj_res_kernel(x_ref, h_ref, w_ref, g_ref, o_ref):
    mix = jnp.dot(x_ref[...], w_ref[...], preferred_element_type=F32)
    o_ref[...] = h_ref[...] + _rms(mix, g_ref[...])


def _proj_res(x, h, w, gain, tm):
    n, d = h.shape
    k = x.shape[1]
    return pl.pallas_call(
        _proj_res_kernel,
        grid=(n // tm,),
        in_specs=[pl.BlockSpec((tm, k), lambda i: (i, 0)),
                  pl.BlockSpec((tm, d), lambda i: (i, 0)),
                  pl.BlockSpec((k, d), lambda i: (0, 0)),
                  pl.BlockSpec((1, d), lambda i: (0, 0))],
        out_specs=pl.BlockSpec((tm, d), lambda i: (i, 0)),
        out_shape=jax.ShapeDtypeStruct((n, d), F32),
        compiler_params=_cparams("parallel"),
        name="proj_res",
    )(x, h, w, gain.reshape(1, d))


def _s5_layer(h, bsz, seq, gains, w_in, lam_re, lam_im, log_step, b_re, b_im, c_re, c_im,
              d_skip, w_glu, w_out):
    seg_tokens = S5_CHUNK * S5_SEG_CHUNKS
    nseg = seq // seg_tokens
    nseq = bsz * nseg
    rc = min(512, h.shape[0] // S5_CHUNK)
    u_t = _norm_matmul(h, gains[0], w_in.astype(BF16), rc=rc)
    w1, w2t, pw_p, aa_p = _s5_prep(lam_re, lam_im, log_step, b_re, b_im, c_re, c_im, d_skip)
    y_t = _s5_core(u_t, w1, w2t, pw_p, aa_p, nseq, nseg)
    return _s5_out(y_t, h, w_glu.astype(BF16), w_out.astype(BF16), gains[1], rc=rc)


def _attn_layer(h, bsz, seq, gains, w_qkv, q_gain, k_gain, w_out):
    q, k, v = _qkv(h, gains[0], w_qkv, q_gain, k_gain, bsz, seq, tm=512)
    o = _attention(q, k, v, tq=min(4096, seq), tk=512, rc=512)
    return _proj_res(o, h, w_out.astype(BF16), gains[1], tm=512)


def kernel(x, norm_gains, ssm_w_in, ssm_lambda_re, ssm_lambda_im, ssm_log_step, ssm_b_re,
           ssm_b_im, ssm_c_re, ssm_c_im, ssm_d, ssm_w_glu, ssm_w_out, ffn_w_gate, ffn_w_up,
           ffn_w_down, attn_w_qkv, attn_q_gain, attn_k_gain, attn_w_out, moe_w_router,
           moe_w_gate, moe_w_up, moe_w_down):
    bsz, seq, d = x.shape
    depth = norm_gains.shape[0]
    h = x.reshape(bsz * seq, d)
    for i in range(depth):
        j = i // 2
        g = norm_gains[i]
        if i % 2 == 0:
            h = _s5_layer(h, bsz, seq, g, ssm_w_in[j], ssm_lambda_re[j], ssm_lambda_im[j],
                          ssm_log_step[j], ssm_b_re[j], ssm_b_im[j], ssm_c_re[j], ssm_c_im[j],
                          ssm_d[j], ssm_w_glu[j], ssm_w_out[j])
            h = _ffn(h, g[2], g[3], ffn_w_gate[j].astype(BF16), ffn_w_up[j].astype(BF16),
                     ffn_w_down[j].astype(BF16), tm=512, tf=1408)
        else:
            h = _attn_layer(h, bsz, seq, g, attn_w_qkv[j], attn_q_gain[j], attn_k_gain[j],
                            attn_w_out[j])
            h = _moe(h, g[2], g[3], moe_w_router[j], moe_w_gate[j].astype(BF16),
                     moe_w_up[j].astype(BF16), moe_w_down[j].astype(BF16),
                     tb=2048, tf=896, rt=256)
    return h.reshape(bsz, seq, d)
```

```python
import functools
import math

import jax
import jax.numpy as jnp
from jax import lax
from jax.experimental import pallas as pl
from jax.experimental.pallas import tpu as pltpu

F32 = jnp.float32
BF16 = jnp.bfloat16
NORM_EPS = 1e-6
ROPE_THETA = 10000.0
GRID_W = 64
N_HEADS = 16
N_KV_HEADS = 4
HEAD_DIM = 64
Q_PER_KV = N_HEADS // N_KV_HEADS
SSM_GROUP = 16
SSM_STATE = 64
S5_CHUNK = 16
S5_SEG_CHUNKS = 64
TOP_K = 2
LANES = 128
VMEM_LIMIT_BYTES = 56 * 1024 * 1024
MOE_VMEM_LIMIT_BYTES = 60 * 1024 * 1024

_NT = (((1,), (1,)), ((), ()))


def _cparams(*sem):
    return pltpu.CompilerParams(dimension_semantics=sem, vmem_limit_bytes=VMEM_LIMIT_BYTES)


def _rms(x, gain):
    return x * lax.rsqrt(jnp.mean(x * x, axis=-1, keepdims=True) + NORM_EPS) * gain


def _sigmoid(x):
    return 1.0 / (1.0 + jnp.exp(-x))


def _gelu_tanh(x):
    return x * (0.5 * (1.0 + jnp.tanh(math.sqrt(2.0 / math.pi) * (x + 0.044715 * (x * x * x)))))


def _norm_matmul_kernel(x_ref, g_ref, w_ref, o_ref):
    xn = _rms(x_ref[...], g_ref[...]).astype(BF16)
    o_ref[0] = jnp.dot(xn, w_ref[...], preferred_element_type=F32).astype(o_ref.dtype)


def _norm_matmul(x, gain, w, rc):
    n, d = x.shape
    m = w.shape[1]
    chunks = n // S5_CHUNK
    return pl.pallas_call(
        _norm_matmul_kernel,
        grid=(chunks // rc, S5_CHUNK),
        in_specs=[pl.BlockSpec((rc, d), lambda i, t: (i, t)),
                  pl.BlockSpec((1, d), lambda i, t: (0, 0)),
                  pl.BlockSpec((d, m), lambda i, t: (0, 0))],
        out_specs=pl.BlockSpec((1, rc, m), lambda i, t: (t, i, 0)),
        out_shape=jax.ShapeDtypeStruct((S5_CHUNK, chunks, m), BF16),
        compiler_params=_cparams("parallel", "parallel"),
        name="norm_matmul",
    )(x.reshape(chunks, S5_CHUNK * d), gain.reshape(1, d), w)


def _s5_prep_kernel(lr_ref, li_ref, ls_ref, bt_re_ref, bt_im_ref, c_re_ref, c_im_ref, dt_ref,
                    w1_ref, w2t_ref, pw_ref, aa_ref):
    t_n, s_n, p_n = S5_CHUNK, SSM_GROUP, SSM_STATE
    pair = 2 * s_n
    rows = t_n * pair
    hi = lax.Precision.HIGHEST

    rr = lax.broadcasted_iota(jnp.int32, (rows, rows), 0)
    cc = lax.broadcasted_iota(jnp.int32, (rows, rows), 1)
    same_group = ((rr // s_n) % 2) == ((cc // s_n) % 2)
    tvec = lax.broadcasted_iota(jnp.int32, (t_n, 1), 0).astype(F32)
    jvec = lax.broadcasted_iota(jnp.int32, (S5_SEG_CHUNKS, 1), 0).astype(F32)
    row8 = lax.broadcasted_iota(jnp.int32, (8, LANES), 0)
    own_lanes = ((lax.broadcasted_iota(jnp.int32, (rows, LANES), 1) // p_n)
                 == ((lax.broadcasted_iota(jnp.int32, (rows, LANES), 0) // s_n) % 2))

    def lanes_by_group(tab0, tab1):
        lane = lax.broadcasted_iota(jnp.int32, tab0.shape, 1)
        return jnp.where(lane < p_n, tab0, tab1)

    m_total = None
    for d in range(2):
        lsr, lsi, q_re, q_im = [], [], [], []
        for g in range(2):
            lr = lr_ref[0, d, g:g + 1, :]
            li = li_ref[0, d, g:g + 1, :]
            step = jnp.exp(ls_ref[0, d, g:g + 1, :])
            lsr.append(lr * step)
            lsi.append(li * step)
            mag = jnp.exp(lsr[g])
            a_re, a_im = mag * jnp.cos(lsi[g]), mag * jnp.sin(lsi[g])
            nr, ni = a_re - 1.0, a_im
            den = lr * lr + li * li
            q_re.append((nr * lr + ni * li) / den)
            q_im.append((ni * lr - nr * li) / den)

        def cpow(k, g):
            mag = jnp.exp(lsr[g] * k)
            ang = lsi[g] * k
            return mag * jnp.cos(ang), mag * jnp.sin(ang)

        def table(k):
            t0, t1 = cpow(k, 0), cpow(k, 1)
            return tuple(
                jnp.concatenate([jnp.broadcast_to(tg[part][t:t + 1, :], (s_n, LANES))
                                 for t in range(t_n) for tg in (t0, t1)], axis=0)
                for part in range(2))

        br, bi = bt_re_ref[0, d], bt_im_ref[0, d]
        qr = jnp.concatenate([jnp.broadcast_to(q_re[g], (s_n, LANES)) for g in range(2)], axis=0)
        qi = jnp.concatenate([jnp.broadcast_to(q_im[g], (s_n, LANES)) for g in range(2)], axis=0)
        bb_re = qr * br - qi * bi
        bb_im = qr * bi + qi * br
        cr, ci = c_re_ref[0, d], c_im_ref[0, d]

        def outer(x_re, x_im, k):
            pe_re, pe_im = table(k)
            xe_re = jnp.concatenate([x_re] * t_n, axis=0)
            xe_im = jnp.concatenate([x_im] * t_n, axis=0)
            return xe_re * pe_re - xe_im * pe_im, xe_re * pe_im + xe_im * pe_re

        if d == 0:
            l_re, l_im = outer(bb_re, bb_im, -tvec)
            rt_re, rt_im = outer(cr, ci, tvec)
            ws_re, ws_im = outer(bb_re, bb_im, (t_n - 1.0) - tvec)
            wy_re, wy_im = outer(cr, ci, tvec + 1.0)
            mask = (rr // pair) <= (cc // pair)
            kseg = t_n * jvec
        else:
            l_re, l_im = outer(bb_re, bb_im, tvec)
            rt_re, rt_im = outer(cr, ci, -tvec)
            ws_re, ws_im = l_re, l_im
            wy_re, wy_im = outer(cr, ci, t_n - tvec)
            mask = (rr // pair) >= (cc // pair)
            kseg = t_n * ((S5_SEG_CHUNKS - 1.0) - jvec)
        kern = (lax.dot_general(l_re[:, 0:p_n], rt_re[:, 0:p_n], _NT, precision=hi,
                                preferred_element_type=F32)
                - lax.dot_general(l_im[:, 0:p_n], rt_im[:, 0:p_n], _NT, precision=hi,
                                  preferred_element_type=F32))
        kern = jnp.where(mask & same_group, kern, 0.0)
        m_total = kern if m_total is None else m_total + kern

        ycols = rows
        for k2, val in ((2 * d, ws_re), (2 * d + 1, ws_im)):
            w1_ref[0, :, ycols + k2 * LANES:ycols + (k2 + 1) * LANES] = jnp.where(
                own_lanes, val, 0.0).astype(w1_ref.dtype)
        for k2, val in ((2 * d, wy_re), (2 * d + 1, -wy_im)):
            w2t_ref[0, :, k2 * LANES:(k2 + 1) * LANES] = jnp.where(
                own_lanes, val, 0.0).astype(w2t_ref.dtype)
        p0, p1 = cpow(kseg, 0), cpow(kseg, 1)
        pw_ref[0, :, 2 * d * LANES:(2 * d + 1) * LANES] = lanes_by_group(p0[0], p1[0])
        pw_ref[0, :, (2 * d + 1) * LANES:(2 * d + 2) * LANES] = lanes_by_group(p0[1], p1[1])
        one = jnp.ones((1, 1), F32)
        c0, c1 = cpow(float(t_n) * one, 0), cpow(float(t_n) * one, 1)
        s0, s1 = (cpow(float(t_n * S5_SEG_CHUNKS) * one, 0),
                  cpow(float(t_n * S5_SEG_CHUNKS) * one, 1))
        for part in range(2):
            chunk_pow = lanes_by_group(c0[part], c1[part])
            seg_pow = lanes_by_group(s0[part], s1[part])
            aa_ref[0, :, (2 * d + part) * LANES:(2 * d + part + 1) * LANES] = jnp.where(
                row8 == 0, chunk_pow, jnp.where(row8 == 1, seg_pow, 0.0))

    w1_ref[0, :, 0:rows] = (m_total + jnp.where(rr == cc, dt_ref[0], 0.0)).astype(w1_ref.dtype)


def _s5_prep(lam_re, lam_im, log_step, b_re, b_im, c_re, c_im, d_skip):
    g_n = lam_re.shape[1]
    pairs = g_n // 2
    pair = 2 * SSM_GROUP
    rows = S5_CHUNK * pair

    def states(x):
        x = jnp.transpose(x.reshape(2, pairs, 2, SSM_STATE), (1, 0, 2, 3))
        return jnp.tile(x, (1, 1, 1, LANES // SSM_STATE))

    def per_channel(x):
        x = jnp.transpose(x.reshape(2, pairs, pair, SSM_STATE), (1, 0, 2, 3))
        return jnp.tile(x, (1, 1, 1, LANES // SSM_STATE))

    ls = jnp.transpose(log_step.reshape(2, pairs, 2), (1, 0, 2))[..., None]
    bt_re = per_channel(jnp.swapaxes(b_re, 2, 3))
    bt_im = per_channel(jnp.swapaxes(b_im, 2, 3))
    dt = jnp.tile(d_skip.reshape(pairs, 1, pair), (1, 1, S5_CHUNK))

    def spec(*blk):
        return pl.BlockSpec((1,) + blk, lambda g: (g,) + (0,) * len(blk))

    return pl.pallas_call(
        _s5_prep_kernel,
        grid=(pairs,),
        in_specs=[spec(2, 2, LANES), spec(2, 2, LANES), spec(2, 2, 1),
                  spec(2, pair, LANES), spec(2, pair, LANES),
                  spec(2, pair, LANES), spec(2, pair, LANES), spec(1, rows)],
        out_specs=[spec(rows, rows + 4 * LANES), spec(rows, 4 * LANES),
                   spec(S5_SEG_CHUNKS, 4 * LANES), spec(8, 4 * LANES)],
        out_shape=[jax.ShapeDtypeStruct((pairs, rows, rows + 4 * LANES), BF16),
                   jax.ShapeDtypeStruct((pairs, rows, 4 * LANES), BF16),
                   jax.ShapeDtypeStruct((pairs, S5_SEG_CHUNKS, 4 * LANES), F32),
                   jax.ShapeDtypeStruct((pairs, 8, 4 * LANES), F32)],
        compiler_params=_cparams("parallel"),
        name="s5_prep",
    )(states(lam_re), states(lam_im), ls, bt_re, bt_im, per_channel(c_re), per_channel(c_im), dt)


S5_PAIR_LANES = 2 * SSM_GROUP
S5_PAIRS_PER_SLAB = LANES // S5_PAIR_LANES
S5_SEG_PITCH = S5_SEG_CHUNKS + 8
S5_SEG_BASE = 8


def _s5_core_kernel(x_ref, w1_ref, w2t_ref, pw_ref, aa_ref, y_ref, z_s, zs_s, x_s, u_s, xb_s,
                    *, nseq, nseg, strip):
    n_chunks = S5_SEG_CHUNKS
    rows = n_chunks * nseq
    ycols = u_s.shape[1]
    per_tile = LANES // S5_PAIR_LANES
    lane_grp = lax.broadcasted_iota(jnp.int32, (strip, LANES), 1) // S5_PAIR_LANES
    zero = jnp.zeros((nseq, LANES), F32)

    def seg_rows(j):
        return pl.ds(S5_SEG_BASE + j, nseq, stride=S5_SEG_PITCH)

    for q in range(S5_PAIRS_PER_SLAB):
        def gather(r, _, q=q):
            rws = pl.ds(pl.multiple_of(r * strip, strip), strip)
            for j in range(ycols // LANES):
                acc = None
                for i in range(per_tile):
                    xt = x_ref[per_tile * j + i, rws, :].astype(F32)
                    shift = (S5_PAIR_LANES * (i - q)) % LANES
                    if shift:
                        xt = pltpu.roll(xt, shift, axis=1)
                    acc = xt if acc is None else jnp.where(lane_grp == i, xt, acc)
                u_s[rws, j * LANES:(j + 1) * LANES] = acc.astype(BF16)
            return 0

        lax.fori_loop(0, rows // strip, gather, 0)
        z_s[...] = jnp.dot(u_s[...], w1_ref[q], preferred_element_type=F32)

        for b in range(nseq):
            dst = slice(S5_SEG_BASE + b * S5_SEG_PITCH, S5_SEG_BASE + b * S5_SEG_PITCH + n_chunks)
            for k in range(4):
                zs_s[k, dst, :] = z_s[b * n_chunks:(b + 1) * n_chunks,
                                      ycols + k * LANES:ycols + (k + 1) * LANES]
        x_s[0, seg_rows(0), :] = zero
        x_s[1, seg_rows(0), :] = zero
        x_s[2, seg_rows(n_chunks - 1), :] = zero
        x_s[3, seg_rows(n_chunks - 1), :] = zero

        aa = aa_ref[q]
        a_fr, a_fi = aa[0:1, 0:LANES], aa[0:1, LANES:2 * LANES]
        a_br, a_bi = aa[0:1, 2 * LANES:3 * LANES], aa[0:1, 3 * LANES:4 * LANES]
        g_fr, g_fi = aa[1:2, 0:LANES], aa[1:2, LANES:2 * LANES]
        g_br, g_bi = aa[1:2, 2 * LANES:3 * LANES], aa[1:2, 3 * LANES:4 * LANES]

        def fwd(j, carry):
            xr, xi = carry
            nr = a_fr * xr - a_fi * xi + zs_s[0, seg_rows(j), :]
            ni = a_fr * xi + a_fi * xr + zs_s[1, seg_rows(j), :]
            x_s[0, seg_rows(j + 1), :] = nr
            x_s[1, seg_rows(j + 1), :] = ni
            return nr, ni

        def bwd(i, carry):
            xr, xi = carry
            j = n_chunks - 1 - i
            nr = a_br * xr - a_bi * xi + zs_s[2, seg_rows(j), :]
            ni = a_br * xi + a_bi * xr + zs_s[3, seg_rows(j), :]
            x_s[2, seg_rows(j - 1), :] = nr
            x_s[3, seg_rows(j - 1), :] = ni
            return nr, ni

        ef_r, ef_i = lax.fori_loop(0, n_chunks, fwd, (zero, zero))
        eb_r, eb_i = lax.fori_loop(0, n_chunks, bwd, (zero, zero))

        def seg_carries(e_r, e_i, g_r, g_i, reverse):
            out_r = [None] * nseq
            out_i = [None] * nseq
            for b in range(nseq // nseg):
                c_r = jnp.zeros((1, LANES), F32)
                c_i = jnp.zeros((1, LANES), F32)
                order = range(nseg - 1, -1, -1) if reverse else range(nseg)
                for s in order:
                    r = b * nseg + s
                    out_r[r], out_i[r] = c_r, c_i
                    n_r = g_r * c_r - g_i * c_i + e_r[r:r + 1, :]
                    n_i = g_r * c_i + g_i * c_r + e_i[r:r + 1, :]
                    c_r, c_i = n_r, n_i
            return out_r, out_i

        cf_r, cf_i = seg_carries(ef_r, ef_i, g_fr, g_fi, False)
        cb_r, cb_i = seg_carries(eb_r, eb_i, g_br, g_bi, True)

        p = pw_ref[q]
        p_fr, p_fi = p[:, 0:LANES], p[:, LANES:2 * LANES]
        p_br, p_bi = p[:, 2 * LANES:3 * LANES], p[:, 3 * LANES:4 * LANES]
        for b in range(nseq):
            src = slice(S5_SEG_BASE + b * S5_SEG_PITCH, S5_SEG_BASE + b * S5_SEG_PITCH + n_chunks)
            dst = slice(b * n_chunks, (b + 1) * n_chunks)
            xb_s[dst, 0:LANES] = (x_s[0, src, :] + (p_fr * cf_r[b] - p_fi * cf_i[b])).astype(BF16)
            xb_s[dst, LANES:2 * LANES] = (
                x_s[1, src, :] + (p_fr * cf_i[b] + p_fi * cf_r[b])).astype(BF16)
            xb_s[dst, 2 * LANES:3 * LANES] = (
                x_s[2, src, :] + (p_br * cb_r[b] - p_bi * cb_i[b])).astype(BF16)
            xb_s[dst, 3 * LANES:4 * LANES] = (
                x_s[3, src, :] + (p_br * cb_i[b] + p_bi * cb_r[b])).astype(BF16)

        z_s[:, 0:ycols] += lax.dot_general(xb_s[...], w2t_ref[q], _NT,
                                           preferred_element_type=F32)

        def scatter(r, _, q=q):
            rws = pl.ds(pl.multiple_of(r * strip, strip), strip)
            for j in range(ycols // LANES):
                yq = z_s[rws, j * LANES:(j + 1) * LANES]
                for i in range(per_tile):
                    shift = (S5_PAIR_LANES * (q - i)) % LANES
                    yt = pltpu.roll(yq, shift, axis=1) if shift else yq
                    lanes_q = slice(q * S5_PAIR_LANES, (q + 1) * S5_PAIR_LANES)
                    y_ref[per_tile * j + i, rws, lanes_q] = yt[:, lanes_q].astype(y_ref.dtype)
            return 0

        lax.fori_loop(0, rows // strip, scatter, 0)


def _s5_core(u_t, w1, w2t, pw, aa, nseq, nseg):
    t_n, rows, d = u_t.shape
    width = w1.shape[1]
    pps = S5_PAIRS_PER_SLAB
    slab_rows = S5_SEG_BASE + nseq * S5_SEG_PITCH
    return pl.pallas_call(
        functools.partial(_s5_core_kernel, nseq=nseq, nseg=nseg, strip=128),
        grid=(d // LANES,),
        in_specs=[pl.BlockSpec((t_n, rows, LANES), lambda o: (0, 0, o)),
                  pl.BlockSpec((pps, width, w1.shape[2]), lambda o: (o, 0, 0)),
                  pl.BlockSpec((pps, width, w2t.shape[2]), lambda o: (o, 0, 0)),
                  pl.BlockSpec((pps, S5_SEG_CHUNKS, pw.shape[2]), lambda o: (o, 0, 0)),
                  pl.BlockSpec((pps, 8, aa.shape[2]), lambda o: (o, 0, 0))],
        out_specs=pl.BlockSpec((t_n, rows, LANES), lambda o: (0, 0, o)),
        out_shape=jax.ShapeDtypeStruct((t_n, rows, d), BF16),
        scratch_shapes=[pltpu.VMEM((rows, w1.shape[2]), F32),
                        pltpu.VMEM((4, slab_rows, LANES), F32),
                        pltpu.VMEM((4, slab_rows, LANES), F32),
                        pltpu.VMEM((rows, width), BF16),
                        pltpu.VMEM((rows, 4 * LANES), BF16)],
        compiler_params=_cparams("parallel"),
        name="s5_core",
    )(u_t, w1, w2t, pw, aa)


def _s5_out_kernel(y_ref, h_ref, wglu_ref, wout_ref, g_ref, o_ref):
    g = _gelu_tanh(y_ref[0].astype(F32))
    z = jnp.dot(g.astype(BF16), wglu_ref[...], preferred_element_type=F32)
    g2 = g * _sigmoid(z)
    mix = jnp.dot(g2.astype(BF16), wout_ref[...], preferred_element_type=F32)
    o_ref[...] = h_ref[...] + _rms(mix, g_ref[...])


def _s5_out(y_t, h, w_glu, w_out, gain, rc):
    n, d = h.shape
    chunks = n // S5_CHUNK
    out = pl.pallas_call(
        _s5_out_kernel,
        grid=(chunks // rc, S5_CHUNK),
        in_specs=[pl.BlockSpec((1, rc, d), lambda i, t: (t, i, 0)),
                  pl.BlockSpec((rc, d), lambda i, t: (i, t)),
                  pl.BlockSpec((d, d), lambda i, t: (0, 0)),
                  pl.BlockSpec((d, d), lambda i, t: (0, 0)),
                  pl.BlockSpec((1, d), lambda i, t: (0, 0))],
        out_specs=pl.BlockSpec((rc, d), lambda i, t: (i, t)),
        out_shape=jax.ShapeDtypeStruct((chunks, S5_CHUNK * d), F32),
        compiler_params=_cparams("parallel", "parallel"),
        name="s5_out",
    )(y_t, h.reshape(chunks, S5_CHUNK * d), w_glu, w_out, gain.reshape(1, d))
    return out.reshape(n, d)


def _ffn_kernel(h_ref, gpre_ref, gpost_ref, wg_ref, wu_ref, wd_ref, o_ref, hn_s, acc_s):
    f = pl.program_id(1)

    @pl.when(f == 0)
    def _():
        hn_s[...] = _rms(h_ref[...], gpre_ref[...]).astype(BF16)
        acc_s[...] = jnp.zeros_like(acc_s)

    hn = hn_s[...]
    a = jnp.dot(hn, wg_ref[...], preferred_element_type=F32)
    u = jnp.dot(hn, wu_ref[...], preferred_element_type=F32)
    act = a * _sigmoid(a) * u
    acc_s[...] += jnp.dot(act.astype(BF16), wd_ref[...], preferred_element_type=F32)

    @pl.when(f == pl.num_programs(1) - 1)
    def _():
        o_ref[...] = h_ref[...] + _rms(acc_s[...], gpost_ref[...])


def _ffn(h, gain_pre, gain_post, w_gate, w_up, w_down, tm, tf):
    n, d = h.shape
    d_ff = w_gate.shape[1]
    return pl.pallas_call(
        _ffn_kernel,
        grid=(n // tm, d_ff // tf),
        in_specs=[pl.BlockSpec((tm, d), lambda i, f: (i, 0)),
                  pl.BlockSpec((1, d), lambda i, f: (0, 0)),
                  pl.BlockSpec((1, d), lambda i, f: (0, 0)),
                  pl.BlockSpec((d, tf), lambda i, f: (0, f)),
                  pl.BlockSpec((d, tf), lambda i, f: (0, f)),
                  pl.BlockSpec((tf, d), lambda i, f: (f, 0))],
        out_specs=pl.BlockSpec((tm, d), lambda i, f: (i, 0)),
        out_shape=jax.ShapeDtypeStruct((n, d), F32),
        scratch_shapes=[pltpu.VMEM((tm, d), BF16), pltpu.VMEM((tm, d), F32)],
        compiler_params=_cparams("parallel", "arbitrary"),
        name="ffn",
    )(h, gain_pre.reshape(1, d), gain_post.reshape(1, d), w_gate, w_up, w_down)


def _moe_kernel(h_ref, gpre_ref, gpost_ref, wr_ref, wg_ref, wu_ref, wd_ref, o_ref,
                hn_s, gate_s, pos_s, post_s, xc_s, yc_s, xg_s, col_s, cnt_s,
                *, n_exp, rt, strip):
    e = pl.program_id(1)
    f = pl.program_id(2)
    tb, d = hn_s.shape
    n_f = pl.num_programs(2)
    n_strips = tb // strip

    def put_counts(r, total):
        lane1 = lax.broadcasted_iota(jnp.int32, total.shape, 1)
        for x in range(n_exp):
            cnt_s[r * n_exp + x] = jnp.sum(jnp.where(lane1 == x, total, 0.0)).astype(jnp.int32)

    @pl.when((e == 0) & (f == 0))
    def _():
        before = (lax.broadcasted_iota(jnp.int32, (strip, strip), 1)
                  < lax.broadcasted_iota(jnp.int32, (strip, strip), 0)).astype(BF16)

        def route(r, total):
            put_counts(r, total)
            rows = pl.ds(pl.multiple_of(r * strip, strip), strip)
            xn = _rms(h_ref[rows, :], gpre_ref[...])
            hn_s[rows, :] = xn.astype(BF16)
            o_ref[rows, :] = jnp.zeros((strip, d), F32)
            x_hi = xn.astype(BF16)
            x_lo = (xn - x_hi.astype(F32)).astype(BF16)
            logits = (jnp.dot(x_hi, wr_ref[0], preferred_element_type=F32)
                      + jnp.dot(x_lo, wr_ref[0], preferred_element_type=F32)
                      + jnp.dot(x_hi, wr_ref[1], preferred_element_type=F32))
            lane = lax.broadcasted_iota(jnp.int32, logits.shape, 1)
            neg = jnp.float32(-jnp.inf)
            logits = jnp.where(lane < n_exp, logits, neg)
            m1 = jnp.max(logits, axis=-1, keepdims=True)
            i1 = jnp.min(jnp.where(logits == m1, lane, LANES), axis=-1, keepdims=True)
            rest = jnp.where(lane == i1, neg, logits)
            m2 = jnp.max(rest, axis=-1, keepdims=True)
            i2 = jnp.min(jnp.where(rest == m2, lane, LANES), axis=-1, keepdims=True)
            e2 = jnp.exp(m2 - m1)
            gate_s[rows, :] = (jnp.where(lane == i1, 1.0 / (1.0 + e2), 0.0)
                               + jnp.where(lane == i2, e2 / (1.0 + e2), 0.0))
            sel = ((lane == i1) | (lane == i2)).astype(F32)
            earlier = jnp.dot(before, sel.astype(BF16), preferred_element_type=F32) + total
            pos_s[rows, :] = jnp.where(sel > 0, earlier, -1.0)
            return total + jnp.sum(sel, axis=0, keepdims=True)

        total = lax.fori_loop(0, n_strips, route, jnp.zeros((1, LANES), F32))
        put_counts(n_strips, total)

        for r in range(tb // strip):
            post_s[:, r * strip:(r + 1) * strip] = pos_s[r * strip:(r + 1) * strip, :].T

    @pl.when(f == n_f - 1)
    def _():
        lane = lax.broadcasted_iota(jnp.int32, (tb, LANES), 1)
        col_s[0] = jnp.sum(jnp.where(lane == e, pos_s[...], 0.0), axis=-1, keepdims=True)
        col_s[1] = jnp.sum(jnp.where(lane == e, gate_s[...], 0.0), axis=-1, keepdims=True)

    n_tiles = (cnt_s[n_strips * n_exp + e] + rt - 1) // rt

    def tile(i, _):
        r0 = pl.multiple_of(i * rt, 16)

        def holds(r):
            return ((cnt_s[r * n_exp + e] < r0 + rt) & (cnt_s[(r + 1) * n_exp + e] > r0))

        @pl.when(f == 0)
        def _():
            xg_s[...] = jnp.zeros_like(xg_s)
            slot = (lax.broadcasted_iota(jnp.int32, (rt, strip), 0) + r0).astype(F32)
            for r in range(n_strips):
                @pl.when(holds(r))
                def _():
                    cols = slice(r * strip, (r + 1) * strip)
                    pick = (post_s[pl.ds(e, 1), cols] == slot).astype(BF16)
                    xg_s[...] += jnp.dot(pick, hn_s[cols, :], preferred_element_type=F32)
            xc_s[pl.ds(r0, rt), :] = xg_s[...].astype(BF16)

        xc = xc_s[pl.ds(r0, rt), :]
        a = jnp.dot(xc, wg_ref[0], preferred_element_type=F32)
        u = jnp.dot(xc, wu_ref[0], preferred_element_type=F32)
        act = (a * _sigmoid(a) * u).astype(BF16)
        y = jnp.dot(act, wd_ref[0], preferred_element_type=F32)
        prev = jnp.where(f == 0, 0.0, yc_s[pl.ds(r0, rt), :])
        yc_s[pl.ds(r0, rt), :] = prev + y

        @pl.when(f == n_f - 1)
        def _():
            yc = yc_s[pl.ds(r0, rt), :].astype(BF16)
            slot = (lax.broadcasted_iota(jnp.int32, (strip, rt), 1) + r0).astype(F32)
            for r in range(n_strips):
                @pl.when(holds(r))
                def _():
                    rows = slice(r * strip, (r + 1) * strip)
                    put = (col_s[0, rows, :] == slot).astype(BF16)
                    back = jnp.dot(put, yc, preferred_element_type=F32)
                    o_ref[rows, :] += col_s[1, rows, :] * back
        return 0

    lax.fori_loop(0, n_tiles, tile, 0)

    @pl.when((e == n_exp - 1) & (f == n_f - 1))
    def _():
        def finish(r, _):
            rows = pl.ds(pl.multiple_of(r * strip, strip), strip)
            o_ref[rows, :] = h_ref[rows, :] + _rms(o_ref[rows, :], gpost_ref[...])
            return 0

        lax.fori_loop(0, tb // strip, finish, 0)


def _moe(h, gain_pre, gain_post, w_router, w_gate, w_up, w_down, tb, tf, rt):
    n, d = h.shape
    n_exp, _, d_ff = w_gate.shape
    w_r = jnp.pad(w_router, ((0, 0), (0, LANES - n_exp)))
    w_hi = w_r.astype(BF16)
    w_r = jnp.stack([w_hi, (w_r - w_hi.astype(F32)).astype(BF16)])
    once = pl.Buffered(1)
    strip = 256
    cap = pl.cdiv(tb, rt) * rt
    return pl.pallas_call(
        functools.partial(_moe_kernel, n_exp=n_exp, rt=rt, strip=strip),
        grid=(n // tb, n_exp, d_ff // tf),
        in_specs=[pl.BlockSpec((tb, d), lambda i, e, f: (i, 0), pipeline_mode=once),
                  pl.BlockSpec((1, d), lambda i, e, f: (0, 0)),
                  pl.BlockSpec((1, d), lambda i, e, f: (0, 0)),
                  pl.BlockSpec((2, d, LANES), lambda i, e, f: (0, 0, 0)),
                  pl.BlockSpec((1, d, tf), lambda i, e, f: (e, 0, f)),
                  pl.BlockSpec((1, d, tf), lambda i, e, f: (e, 0, f)),
                  pl.BlockSpec((1, tf, d), lambda i, e, f: (e, f, 0))],
        out_specs=pl.BlockSpec((tb, d), lambda i, e, f: (i, 0), pipeline_mode=once),
        out_shape=jax.ShapeDtypeStruct((n, d), F32),
        scratch_shapes=[pltpu.VMEM((tb, d), BF16),
                        pltpu.VMEM((tb, LANES), F32),
                        pltpu.VMEM((tb, LANES), F32),
                        pltpu.VMEM((LANES, tb), F32),
                        pltpu.VMEM((cap, d), BF16),
                        pltpu.VMEM((cap, d), F32),
                        pltpu.VMEM((rt, d), F32),
                        pltpu.VMEM((2, tb, 1), F32),
                        pltpu.SMEM(((tb // strip + 1) * n_exp,), jnp.int32)],
        compiler_params=pltpu.CompilerParams(
            dimension_semantics=("parallel", "arbitrary", "arbitrary"),
            vmem_limit_bytes=MOE_VMEM_LIMIT_BYTES),
        name="moe_ffn",
    )(h, gain_pre.reshape(1, d), gain_post.reshape(1, d), w_r, w_gate, w_up, w_down)


def _qkv_kernel(h_ref, g_ref, w_ref, qg_ref, kg_ref, cs_ref, sn_ref, bd_ref,
                q_ref, k_ref, v_ref):
    xn = _rms(h_ref[...], g_ref[...]).astype(BF16)
    qkv = jnp.dot(xn, w_ref[...], preferred_element_type=F32)
    cs = cs_ref[...]
    sn = sn_ref[...]
    bd = bd_ref[...]
    lane = lax.broadcasted_iota(jnp.int32, cs.shape, 1)
    first_half = (lane % HEAD_DIM) < (HEAD_DIM // 2)
    scale = math.log2(math.e) / math.sqrt(HEAD_DIM)

    def norm_rope(x, gain):
        ms = jnp.dot(x * x, bd, preferred_element_type=F32)
        y = x * lax.rsqrt(ms + NORM_EPS) * gain
        partner = jnp.where(first_half,
                            pltpu.roll(y, LANES - HEAD_DIM // 2, axis=1),
                            pltpu.roll(y, HEAD_DIM // 2, axis=1))
        return y * cs + partner * sn

    n_q_tiles = N_HEADS * HEAD_DIM // LANES
    for t in range(n_q_tiles):
        y = norm_rope(qkv[:, t * LANES:(t + 1) * LANES], qg_ref[...]) * scale
        q_ref[0, 2 * t] = y[:, 0:HEAD_DIM].astype(BF16)
        q_ref[0, 2 * t + 1] = y[:, HEAD_DIM:LANES].astype(BF16)
    k0 = N_HEADS * HEAD_DIM
    for t in range(N_KV_HEADS * HEAD_DIM // LANES):
        y = norm_rope(qkv[:, k0 + t * LANES:k0 + (t + 1) * LANES], kg_ref[...])
        y_t = y.T
        k_ref[0, 2 * t, 0] = y_t[0:HEAD_DIM, :].astype(BF16)
        k_ref[0, 2 * t + 1, 0] = y_t[HEAD_DIM:LANES, :].astype(BF16)
    v0 = (N_HEADS + N_KV_HEADS) * HEAD_DIM
    ones = jnp.ones((qkv.shape[0], LANES - HEAD_DIM), BF16)
    for j in range(N_KV_HEADS):
        vj = qkv[:, v0 + j * HEAD_DIM:v0 + (j + 1) * HEAD_DIM].astype(BF16)
        v_ref[0, j] = jnp.concatenate([vj, ones], axis=-1)


def _rope_tables(seq):
    axis_dim = HEAD_DIM // 2
    freqs = ROPE_THETA ** (-jnp.arange(0, axis_dim, 2, dtype=F32) / axis_dim)
    rows = seq // GRID_W
    row_ang = jnp.arange(rows, dtype=F32)[:, None] * freqs
    col_ang = jnp.arange(GRID_W, dtype=F32)[:, None] * freqs
    ang = jnp.concatenate([
        jnp.broadcast_to(row_ang[:, None, :], (rows, GRID_W, freqs.shape[0])),
        jnp.broadcast_to(col_ang[None, :, :], (rows, GRID_W, freqs.shape[0]))], axis=-1)
    ang = ang.reshape(seq, HEAD_DIM // 2)
    cos, sin = jnp.cos(ang), jnp.sin(ang)
    cs = jnp.tile(jnp.concatenate([cos, cos], axis=-1), (1, LANES // HEAD_DIM))
    sn = jnp.tile(jnp.concatenate([-sin, sin], axis=-1), (1, LANES // HEAD_DIM))
    return cs, sn


def _qkv(h, gain, w_qkv, q_gain, k_gain, bsz, seq, tm):
    n, d = h.shape
    width = w_qkv.shape[1]
    perm = jnp.concatenate([jnp.arange(0, HEAD_DIM, 2), jnp.arange(1, HEAD_DIM, 2)])
    n_rot = N_HEADS + N_KV_HEADS
    cols = (jnp.arange(n_rot)[:, None] * HEAD_DIM + perm[None, :]).reshape(-1)
    cols = jnp.concatenate([cols, jnp.arange(n_rot * HEAD_DIM, width)])
    w = w_qkv[:, cols].astype(BF16)
    qg = jnp.tile(q_gain[perm], LANES // HEAD_DIM).reshape(1, LANES)
    kg = jnp.tile(k_gain[perm], LANES // HEAD_DIM).reshape(1, LANES)
    cs, sn = _rope_tables(seq)
    blk = jnp.arange(LANES) // HEAD_DIM
    bd = (blk[:, None] == blk[None, :]).astype(F32) / HEAD_DIM
    per_seq = seq // tm
    return pl.pallas_call(
        _qkv_kernel,
        grid=(n // tm,),
        in_specs=[pl.BlockSpec((tm, d), lambda i: (i, 0)),
                  pl.BlockSpec((1, d), lambda i: (0, 0)),
                  pl.BlockSpec((d, width), lambda i: (0, 0)),
                  pl.BlockSpec((1, LANES), lambda i: (0, 0)),
                  pl.BlockSpec((1, LANES), lambda i: (0, 0)),
                  pl.BlockSpec((tm, LANES), lambda i: (i % per_seq, 0)),
                  pl.BlockSpec((tm, LANES), lambda i: (i % per_seq, 0)),
                  pl.BlockSpec((LANES, LANES), lambda i: (0, 0))],
        out_specs=[pl.BlockSpec((1, N_HEADS, tm, HEAD_DIM),
                                lambda i: (i // per_seq, 0, i % per_seq, 0)),
                   pl.BlockSpec((1, N_KV_HEADS, 1, HEAD_DIM, tm),
                                lambda i: (i // per_seq, 0, i % per_seq, 0, 0)),
                   pl.BlockSpec((1, N_KV_HEADS, tm, LANES),
                                lambda i: (i // per_seq, 0, i % per_seq, 0))],
        out_shape=[jax.ShapeDtypeStruct((bsz, N_HEADS, seq, HEAD_DIM), BF16),
                   jax.ShapeDtypeStruct((bsz, N_KV_HEADS, per_seq, HEAD_DIM, tm), BF16),
                   jax.ShapeDtypeStruct((bsz, N_KV_HEADS, seq, LANES), BF16)],
        compiler_params=_cparams("parallel"),
        name="qkv_rope",
    )(h, gain.reshape(1, d), w, qg, kg, cs, sn, bd)


def _attn_kernel(q_ref, k_ref, v_ref, o_ref, m_s, acc_s, s_buf, p_buf, a_buf,
                 *, tq, rc, unroll):
    tk = k_ref.shape[4]
    seq = k_ref.shape[2] * tk
    chunks_per_head = tq // rc
    n_chunks = Q_PER_KV * chunks_per_head
    n_steps = (seq // tk) * n_chunks

    m_s[...] = jnp.full_like(m_s, -jnp.inf)
    acc_s[...] = jnp.zeros_like(acc_s)

    def where(n):
        c = n % n_chunks
        return (pl.multiple_of((n // n_chunks) * tk, tk), c // chunks_per_head,
                pl.multiple_of((c % chunks_per_head) * rc, rc))

    def scores(n):
        k0, g, r0 = where(n)
        q = q_ref[0, g, pl.ds(r0, rc), :]
        s_buf[...] = jnp.dot(q, k_ref[0, 0, k0 // tk], preferred_element_type=F32)

    def softmax(n):
        _, g, r0 = where(n)
        s = s_buf[...]
        m_prev = m_s[g, pl.ds(r0, rc), :]
        m_new = jnp.maximum(m_prev, jnp.max(s, axis=-1, keepdims=True))
        a_buf[...] = jnp.exp2(m_prev - m_new)
        for t in range(tk // LANES):
            p_buf[:, t * LANES:(t + 1) * LANES] = jnp.exp2(
                s[:, t * LANES:(t + 1) * LANES] - m_new).astype(BF16)
        m_s[g, pl.ds(r0, rc), :] = m_new

    def values(n):
        k0, g, r0 = where(n)
        pv = jnp.dot(p_buf[...], v_ref[0, 0, pl.ds(k0, tk), :],
                     preferred_element_type=F32)
        acc_s[g, pl.ds(r0, rc), :] = a_buf[...] * acc_s[g, pl.ds(r0, rc), :] + pv

    scores(0)
    softmax(0)
    scores(1)

    def body(n, _):
        values(n)
        softmax(n + 1)
        scores(n + 2)
        return 0

    lax.fori_loop(0, n_steps - 2, body, 0, unroll=unroll)
    values(n_steps - 2)
    softmax(n_steps - 1)
    values(n_steps - 1)

    outs = []
    for g in range(Q_PER_KV):
        acc = acc_s[g]
        o = acc / pltpu.roll(acc, HEAD_DIM, axis=1)
        outs.append(o[:, 0:HEAD_DIM])
    o_ref[...] = jnp.concatenate(outs, axis=-1).astype(o_ref.dtype)


def _attention(q, k, v, tq, rc):
    tk = k.shape[4]
    bsz, _, seq, _ = q.shape
    n_q = seq // tq
    n_steps = (seq // tk) * Q_PER_KV * (tq // rc)
    unroll = min(8, n_steps - 2)
    return pl.pallas_call(
        functools.partial(_attn_kernel, tq=tq, rc=rc, unroll=unroll),
        grid=(bsz, N_KV_HEADS, n_q),
        in_specs=[pl.BlockSpec((1, Q_PER_KV, tq, HEAD_DIM), lambda b, j, i: (b, j, i, 0)),
                  pl.BlockSpec((1, 1, seq // tk, HEAD_DIM, tk), lambda b, j, i: (b, j, 0, 0, 0)),
                  pl.BlockSpec((1, 1, seq, LANES), lambda b, j, i: (b, j, 0, 0))],
        out_specs=pl.BlockSpec((tq, Q_PER_KV * HEAD_DIM), lambda b, j, i: (b * n_q + i, j)),
        out_shape=jax.ShapeDtypeStruct((bsz * seq, N_HEADS * HEAD_DIM), BF16),
        scratch_shapes=[pltpu.VMEM((Q_PER_KV, tq, LANES), F32),
                        pltpu.VMEM((Q_PER_KV, tq, LANES), F32),
                        pltpu.VMEM((rc, tk), F32),
                        pltpu.VMEM((rc, tk), BF16),
                        pltpu.VMEM((rc, LANES), F32)],
        compiler_params=_cparams("parallel", "parallel", "parallel"),
        name="flash_attn",
    )(q, k, v)


def _proj_res_kernel(x_ref, h_ref, w_ref, g_ref, o_ref):
    mix = jnp.dot(x_ref[...], w_ref[...], preferred_element_type=F32)
    o_ref[...] = h_ref[...] + _rms(mix, g_ref[...])


def _proj_res(x, h, w, gain, tm):
    n, d = h.shape
    k = x.shape[1]
    return pl.pallas_call(
        _proj_res_kernel,
        grid=(n // tm,),
        in_specs=[pl.BlockSpec((tm, k), lambda i: (i, 0)),
                  pl.BlockSpec((tm, d), lambda i: (i, 0)),
                  pl.BlockSpec((k, d), lambda i: (0, 0)),
                  pl.BlockSpec((1, d), lambda i: (0, 0))],
        out_specs=pl.BlockSpec((tm, d), lambda i: (i, 0)),
        out_shape=jax.ShapeDtypeStruct((n, d), F32),
        compiler_params=_cparams("parallel"),
        name="proj_res",
    )(x, h, w, gain.reshape(1, d))


def _s5_layer(h, bsz, seq, gains, w_in, lam_re, lam_im, log_step, b_re, b_im, c_re, c_im,
              d_skip, w_glu, w_out):
    seg_tokens = S5_CHUNK * S5_SEG_CHUNKS
    nseg = seq // seg_tokens
    nseq = bsz * nseg
    rc = min(512, h.shape[0] // S5_CHUNK)
    u_t = _norm_matmul(h, gains[0], w_in.astype(BF16), rc=rc)
    w1, w2t, pw_p, aa_p = _s5_prep(lam_re, lam_im, log_step, b_re, b_im, c_re, c_im, d_skip)
    y_t = _s5_core(u_t, w1, w2t, pw_p, aa_p, nseq, nseg)
    return _s5_out(y_t, h, w_glu.astype(BF16), w_out.astype(BF16), gains[1], rc=rc)


def _attn_layer(h, bsz, seq, gains, w_qkv, q_gain, k_gain, w_out):
    q, k, v = _qkv(h, gains[0], w_qkv, q_gain, k_gain, bsz, seq, tm=512)
    o = _attention(q, k, v, tq=min(4096, seq), rc=512)
    return _proj_res(o, h, w_out.astype(BF16), gains[1], tm=512)


def kernel(x, norm_gains, ssm_w_in, ssm_lambda_re, ssm_lambda_im, ssm_log_step, ssm_b_re,
           ssm_b_im, ssm_c_re, ssm_c_im, ssm_d, ssm_w_glu, ssm_w_out, ffn_w_gate, ffn_w_up,
           ffn_w_down, attn_w_qkv, attn_q_gain, attn_k_gain, attn_w_out, moe_w_router,
           moe_w_gate, moe_w_up, moe_w_down):
    bsz, seq, d = x.shape
    depth = norm_gains.shape[0]
    h = x.reshape(bsz * seq, d)
    for i in range(depth):
        j = i // 2
        g = norm_gains[i]
        if i % 2 == 0:
            h = _s5_layer(h, bsz, seq, g, ssm_w_in[j], ssm_lambda_re[j], ssm_lambda_im[j],
                          ssm_log_step[j], ssm_b_re[j], ssm_b_im[j], ssm_c_re[j], ssm_c_im[j],
                          ssm_d[j], ssm_w_glu[j], ssm_w_out[j])
            h = _ffn(h, g[2], g[3], ffn_w_gate[j].astype(BF16), ffn_w_up[j].astype(BF16),
                     ffn_w_down[j].astype(BF16), tm=512, tf=1408)
        else:
            h = _attn_layer(h, bsz, seq, g, attn_w_qkv[j], attn_q_gain[j], attn_k_gain[j],
                            attn_w_out[j])
            h = _moe(h, g[2], g[3], moe_w_router[j], moe_w_gate[j].astype(BF16),
                     moe_w_up[j].astype(BF16), moe_w_down[j].astype(BF16),
                     tb=2048, tf=896, rt=256)
    return h.reshape(bsz, seq, d)
```

```python
import functools
import math

import jax
import jax.numpy as jnp
from jax import lax
from jax.experimental import pallas as pl
from jax.experimental.pallas import tpu as pltpu

F32 = jnp.float32
BF16 = jnp.bfloat16
NORM_EPS = 1e-6
ROPE_THETA = 10000.0
GRID_W = 64
N_HEADS = 16
N_KV_HEADS = 4
HEAD_DIM = 64
Q_PER_KV = N_HEADS // N_KV_HEADS
SSM_GROUP = 16
SSM_STATE = 64
S5_CHUNK = 16
S5_SEG_CHUNKS = 64
TOP_K = 2
LANES = 128
VMEM_LIMIT_BYTES = 56 * 1024 * 1024
MOE_VMEM_LIMIT_BYTES = 60 * 1024 * 1024

_NT = (((1,), (1,)), ((), ()))


def _cparams(*sem):
    return pltpu.CompilerParams(dimension_semantics=sem, vmem_limit_bytes=VMEM_LIMIT_BYTES)


def _rms(x, gain):
    return x * lax.rsqrt(jnp.mean(x * x, axis=-1, keepdims=True) + NORM_EPS) * gain


def _sigmoid(x):
    return 1.0 / (1.0 + jnp.exp(-x))


def _gelu_tanh(x):
    return x * (0.5 * (1.0 + jnp.tanh(math.sqrt(2.0 / math.pi) * (x + 0.044715 * (x * x * x)))))


def _norm_matmul_kernel(x_ref, g_ref, w_ref, o_ref):
    xn = _rms(x_ref[...], g_ref[...]).astype(BF16)
    o_ref[0] = jnp.dot(xn, w_ref[...], preferred_element_type=F32).astype(o_ref.dtype)


def _norm_matmul(x, gain, w, rc):
    n, d = x.shape
    m = w.shape[1]
    chunks = n // S5_CHUNK
    return pl.pallas_call(
        _norm_matmul_kernel,
        grid=(chunks // rc, S5_CHUNK),
        in_specs=[pl.BlockSpec((rc, d), lambda i, t: (i, t)),
                  pl.BlockSpec((1, d), lambda i, t: (0, 0)),
                  pl.BlockSpec((d, m), lambda i, t: (0, 0))],
        out_specs=pl.BlockSpec((1, rc, m), lambda i, t: (t, i, 0)),
        out_shape=jax.ShapeDtypeStruct((S5_CHUNK, chunks, m), BF16),
        compiler_params=_cparams("parallel", "parallel"),
        name="norm_matmul",
    )(x.reshape(chunks, S5_CHUNK * d), gain.reshape(1, d), w)


def _s5_prep_kernel(lr_ref, li_ref, ls_ref, bt_re_ref, bt_im_ref, c_re_ref, c_im_ref, dt_ref,
                    w1_ref, w2t_ref, pw_ref, aa_ref):
    t_n, s_n, p_n = S5_CHUNK, SSM_GROUP, SSM_STATE
    pair = 2 * s_n
    rows = t_n * pair
    hi = lax.Precision.HIGHEST

    rr = lax.broadcasted_iota(jnp.int32, (rows, rows), 0)
    cc = lax.broadcasted_iota(jnp.int32, (rows, rows), 1)
    same_group = ((rr // s_n) % 2) == ((cc // s_n) % 2)
    tvec = lax.broadcasted_iota(jnp.int32, (t_n, 1), 0).astype(F32)
    jvec = lax.broadcasted_iota(jnp.int32, (S5_SEG_CHUNKS, 1), 0).astype(F32)
    row8 = lax.broadcasted_iota(jnp.int32, (8, LANES), 0)
    own_lanes = ((lax.broadcasted_iota(jnp.int32, (rows, LANES), 1) // p_n)
                 == ((lax.broadcasted_iota(jnp.int32, (rows, LANES), 0) // s_n) % 2))

    def lanes_by_group(tab0, tab1):
        lane = lax.broadcasted_iota(jnp.int32, tab0.shape, 1)
        return jnp.where(lane < p_n, tab0, tab1)

    m_total = None
    for d in range(2):
        lsr, lsi, q_re, q_im = [], [], [], []
        for g in range(2):
            lr = lr_ref[0, d, g:g + 1, :]
            li = li_ref[0, d, g:g + 1, :]
            step = jnp.exp(ls_ref[0, d, g:g + 1, :])
            lsr.append(lr * step)
            lsi.append(li * step)
            mag = jnp.exp(lsr[g])
            a_re, a_im = mag * jnp.cos(lsi[g]), mag * jnp.sin(lsi[g])
            nr, ni = a_re - 1.0, a_im
            den = lr * lr + li * li
            q_re.append((nr * lr + ni * li) / den)
            q_im.append((ni * lr - nr * li) / den)

        def cpow(k, g):
            mag = jnp.exp(lsr[g] * k)
            ang = lsi[g] * k
            return mag * jnp.cos(ang), mag * jnp.sin(ang)

        def table(k):
            t0, t1 = cpow(k, 0), cpow(k, 1)
            return tuple(
                jnp.concatenate([jnp.broadcast_to(tg[part][t:t + 1, :], (s_n, LANES))
                                 for t in range(t_n) for tg in (t0, t1)], axis=0)
                for part in range(2))

        br, bi = bt_re_ref[0, d], bt_im_ref[0, d]
        qr = jnp.concatenate([jnp.broadcast_to(q_re[g], (s_n, LANES)) for g in range(2)], axis=0)
        qi = jnp.concatenate([jnp.broadcast_to(q_im[g], (s_n, LANES)) for g in range(2)], axis=0)
        bb_re = qr * br - qi * bi
        bb_im = qr * bi + qi * br
        cr, ci = c_re_ref[0, d], c_im_ref[0, d]

        def outer(x_re, x_im, k):
            pe_re, pe_im = table(k)
            xe_re = jnp.concatenate([x_re] * t_n, axis=0)
            xe_im = jnp.concatenate([x_im] * t_n, axis=0)
            return xe_re * pe_re - xe_im * pe_im, xe_re * pe_im + xe_im * pe_re

        if d == 0:
            l_re, l_im = outer(bb_re, bb_im, -tvec)
            rt_re, rt_im = outer(cr, ci, tvec)
            ws_re, ws_im = outer(bb_re, bb_im, (t_n - 1.0) - tvec)
            wy_re, wy_im = outer(cr, ci, tvec + 1.0)
            mask = (rr // pair) <= (cc // pair)
            kseg = t_n * jvec
        else:
            l_re, l_im = outer(bb_re, bb_im, tvec)
            rt_re, rt_im = outer(cr, ci, -tvec)
            ws_re, ws_im = l_re, l_im
            wy_re, wy_im = outer(cr, ci, t_n - tvec)
            mask = (rr // pair) >= (cc // pair)
            kseg = t_n * ((S5_SEG_CHUNKS - 1.0) - jvec)
        kern = (lax.dot_general(l_re[:, 0:p_n], rt_re[:, 0:p_n], _NT, precision=hi,
                                preferred_element_type=F32)
                - lax.dot_general(l_im[:, 0:p_n], rt_im[:, 0:p_n], _NT, precision=hi,
                                  preferred_element_type=F32))
        kern = jnp.where(mask & same_group, kern, 0.0)
        m_total = kern if m_total is None else m_total + kern

        ycols = rows
        for k2, val in ((2 * d, ws_re), (2 * d + 1, ws_im)):
            w1_ref[0, :, ycols + k2 * LANES:ycols + (k2 + 1) * LANES] = jnp.where(
                own_lanes, val, 0.0).astype(w1_ref.dtype)
        for k2, val in ((2 * d, wy_re), (2 * d + 1, -wy_im)):
            w2t_ref[0, :, k2 * LANES:(k2 + 1) * LANES] = jnp.where(
                own_lanes, val, 0.0).astype(w2t_ref.dtype)
        p0, p1 = cpow(kseg, 0), cpow(kseg, 1)
        pw_ref[0, :, 2 * d * LANES:(2 * d + 1) * LANES] = lanes_by_group(p0[0], p1[0])
        pw_ref[0, :, (2 * d + 1) * LANES:(2 * d + 2) * LANES] = lanes_by_group(p0[1], p1[1])
        one = jnp.ones((1, 1), F32)
        c0, c1 = cpow(float(t_n) * one, 0), cpow(float(t_n) * one, 1)
        s0, s1 = (cpow(float(t_n * S5_SEG_CHUNKS) * one, 0),
                  cpow(float(t_n * S5_SEG_CHUNKS) * one, 1))
        for part in range(2):
            chunk_pow = lanes_by_group(c0[part], c1[part])
            seg_pow = lanes_by_group(s0[part], s1[part])
            aa_ref[0, :, (2 * d + part) * LANES:(2 * d + part + 1) * LANES] = jnp.where(
                row8 == 0, chunk_pow, jnp.where(row8 == 1, seg_pow, 0.0))

    w1_ref[0, :, 0:rows] = (m_total + jnp.where(rr == cc, dt_ref[0], 0.0)).astype(w1_ref.dtype)


def _s5_prep(lam_re, lam_im, log_step, b_re, b_im, c_re, c_im, d_skip):
    g_n = lam_re.shape[1]
    pairs = g_n // 2
    pair = 2 * SSM_GROUP
    rows = S5_CHUNK * pair

    def states(x):
        x = jnp.transpose(x.reshape(2, pairs, 2, SSM_STATE), (1, 0, 2, 3))
        return jnp.tile(x, (1, 1, 1, LANES // SSM_STATE))

    def per_channel(x):
        x = jnp.transpose(x.reshape(2, pairs, pair, SSM_STATE), (1, 0, 2, 3))
        return jnp.tile(x, (1, 1, 1, LANES // SSM_STATE))

    ls = jnp.transpose(log_step.reshape(2, pairs, 2), (1, 0, 2))[..., None]
    bt_re = per_channel(jnp.swapaxes(b_re, 2, 3))
    bt_im = per_channel(jnp.swapaxes(b_im, 2, 3))
    dt = jnp.tile(d_skip.reshape(pairs, 1, pair), (1, 1, S5_CHUNK))

    def spec(*blk):
        return pl.BlockSpec((1,) + blk, lambda g: (g,) + (0,) * len(blk))

    return pl.pallas_call(
        _s5_prep_kernel,
        grid=(pairs,),
        in_specs=[spec(2, 2, LANES), spec(2, 2, LANES), spec(2, 2, 1),
                  spec(2, pair, LANES), spec(2, pair, LANES),
                  spec(2, pair, LANES), spec(2, pair, LANES), spec(1, rows)],
        out_specs=[spec(rows, rows + 4 * LANES), spec(rows, 4 * LANES),
                   spec(S5_SEG_CHUNKS, 4 * LANES), spec(8, 4 * LANES)],
        out_shape=[jax.ShapeDtypeStruct((pairs, rows, rows + 4 * LANES), BF16),
                   jax.ShapeDtypeStruct((pairs, rows, 4 * LANES), BF16),
                   jax.ShapeDtypeStruct((pairs, S5_SEG_CHUNKS, 4 * LANES), F32),
                   jax.ShapeDtypeStruct((pairs, 8, 4 * LANES), F32)],
        compiler_params=_cparams("parallel"),
        name="s5_prep",
    )(states(lam_re), states(lam_im), ls, bt_re, bt_im, per_channel(c_re), per_channel(c_im), dt)


S5_PAIR_LANES = 2 * SSM_GROUP
S5_PAIRS_PER_SLAB = LANES // S5_PAIR_LANES
S5_SEG_PITCH = S5_SEG_CHUNKS + 8
S5_SEG_BASE = 8


def _s5_core_kernel(x_ref, w1_ref, w2t_ref, pw_ref, aa_ref, y_ref, z_s, zs_s, x_s, u_s, xb_s,
                    *, nseq, nseg, strip):
    n_chunks = S5_SEG_CHUNKS
    rows = n_chunks * nseq
    ycols = u_s.shape[1]
    per_tile = LANES // S5_PAIR_LANES
    lane_grp = lax.broadcasted_iota(jnp.int32, (strip, LANES), 1) // S5_PAIR_LANES
    zero = jnp.zeros((nseq, LANES), F32)

    def seg_rows(j):
        return pl.ds(S5_SEG_BASE + j, nseq, stride=S5_SEG_PITCH)

    for q in range(S5_PAIRS_PER_SLAB):
        def gather(r, _, q=q):
            rws = pl.ds(pl.multiple_of(r * strip, strip), strip)
            for j in range(ycols // LANES):
                acc = None
                for i in range(per_tile):
                    xt = x_ref[per_tile * j + i, rws, :].astype(F32)
                    shift = (S5_PAIR_LANES * (i - q)) % LANES
                    if shift:
                        xt = pltpu.roll(xt, shift, axis=1)
                    acc = xt if acc is None else jnp.where(lane_grp == i, xt, acc)
                u_s[rws, j * LANES:(j + 1) * LANES] = acc.astype(BF16)
            return 0

        lax.fori_loop(0, rows // strip, gather, 0)
        z_s[...] = jnp.dot(u_s[...], w1_ref[q], preferred_element_type=F32)

        for b in range(nseq):
            dst = slice(S5_SEG_BASE + b * S5_SEG_PITCH, S5_SEG_BASE + b * S5_SEG_PITCH + n_chunks)
            for k in range(4):
                zs_s[k, dst, :] = z_s[b * n_chunks:(b + 1) * n_chunks,
                                      ycols + k * LANES:ycols + (k + 1) * LANES]
        x_s[0, seg_rows(0), :] = zero
        x_s[1, seg_rows(0), :] = zero
        x_s[2, seg_rows(n_chunks - 1), :] = zero
        x_s[3, seg_rows(n_chunks - 1), :] = zero

        aa = aa_ref[q]
        a_fr, a_fi = aa[0:1, 0:LANES], aa[0:1, LANES:2 * LANES]
        a_br, a_bi = aa[0:1, 2 * LANES:3 * LANES], aa[0:1, 3 * LANES:4 * LANES]
        g_fr, g_fi = aa[1:2, 0:LANES], aa[1:2, LANES:2 * LANES]
        g_br, g_bi = aa[1:2, 2 * LANES:3 * LANES], aa[1:2, 3 * LANES:4 * LANES]

        def fwd(j, carry):
            xr, xi = carry
            nr = a_fr * xr - a_fi * xi + zs_s[0, seg_rows(j), :]
            ni = a_fr * xi + a_fi * xr + zs_s[1, seg_rows(j), :]
            x_s[0, seg_rows(j + 1), :] = nr
            x_s[1, seg_rows(j + 1), :] = ni
            return nr, ni

        def bwd(i, carry):
            xr, xi = carry
            j = n_chunks - 1 - i
            nr = a_br * xr - a_bi * xi + zs_s[2, seg_rows(j), :]
            ni = a_br * xi + a_bi * xr + zs_s[3, seg_rows(j), :]
            x_s[2, seg_rows(j - 1), :] = nr
            x_s[3, seg_rows(j - 1), :] = ni
            return nr, ni

        ef_r, ef_i = lax.fori_loop(0, n_chunks, fwd, (zero, zero))
        eb_r, eb_i = lax.fori_loop(0, n_chunks, bwd, (zero, zero))

        def seg_carries(e_r, e_i, g_r, g_i, reverse):
            out_r = [None] * nseq
            out_i = [None] * nseq
            for b in range(nseq // nseg):
                c_r = jnp.zeros((1, LANES), F32)
                c_i = jnp.zeros((1, LANES), F32)
                order = range(nseg - 1, -1, -1) if reverse else range(nseg)
                for s in order:
                    r = b * nseg + s
                    out_r[r], out_i[r] = c_r, c_i
                    n_r = g_r * c_r - g_i * c_i + e_r[r:r + 1, :]
                    n_i = g_r * c_i + g_i * c_r + e_i[r:r + 1, :]
                    c_r, c_i = n_r, n_i
            return out_r, out_i

        cf_r, cf_i = seg_carries(ef_r, ef_i, g_fr, g_fi, False)
        cb_r, cb_i = seg_carries(eb_r, eb_i, g_br, g_bi, True)

        p = pw_ref[q]
        p_fr, p_fi = p[:, 0:LANES], p[:, LANES:2 * LANES]
        p_br, p_bi = p[:, 2 * LANES:3 * LANES], p[:, 3 * LANES:4 * LANES]
        for b in range(nseq):
            src = slice(S5_SEG_BASE + b * S5_SEG_PITCH, S5_SEG_BASE + b * S5_SEG_PITCH + n_chunks)
            dst = slice(b * n_chunks, (b + 1) * n_chunks)
            xb_s[dst, 0:LANES] = (x_s[0, src, :] + (p_fr * cf_r[b] - p_fi * cf_i[b])).astype(BF16)
            xb_s[dst, LANES:2 * LANES] = (
                x_s[1, src, :] + (p_fr * cf_i[b] + p_fi * cf_r[b])).astype(BF16)
            xb_s[dst, 2 * LANES:3 * LANES] = (
                x_s[2, src, :] + (p_br * cb_r[b] - p_bi * cb_i[b])).astype(BF16)
            xb_s[dst, 3 * LANES:4 * LANES] = (
                x_s[3, src, :] + (p_br * cb_i[b] + p_bi * cb_r[b])).astype(BF16)

        z_s[:, 0:ycols] += lax.dot_general(xb_s[...], w2t_ref[q], _NT,
                                           preferred_element_type=F32)

        def scatter(r, _, q=q):
            rws = pl.ds(pl.multiple_of(r * strip, strip), strip)
            for j in range(ycols // LANES):
                yq = z_s[rws, j * LANES:(j + 1) * LANES]
                for i in range(per_tile):
                    shift = (S5_PAIR_LANES * (q - i)) % LANES
                    yt = pltpu.roll(yq, shift, axis=1) if shift else yq
                    lanes_q = slice(q * S5_PAIR_LANES, (q + 1) * S5_PAIR_LANES)
                    y_ref[per_tile * j + i, rws, lanes_q] = yt[:, lanes_q].astype(y_ref.dtype)
            return 0

        lax.fori_loop(0, rows // strip, scatter, 0)


def _s5_core(u_t, w1, w2t, pw, aa, nseq, nseg):
    t_n, rows, d = u_t.shape
    width = w1.shape[1]
    pps = S5_PAIRS_PER_SLAB
    slab_rows = S5_SEG_BASE + nseq * S5_SEG_PITCH
    return pl.pallas_call(
        functools.partial(_s5_core_kernel, nseq=nseq, nseg=nseg, strip=128),
        grid=(d // LANES,),
        in_specs=[pl.BlockSpec((t_n, rows, LANES), lambda o: (0, 0, o)),
                  pl.BlockSpec((pps, width, w1.shape[2]), lambda o: (o, 0, 0)),
                  pl.BlockSpec((pps, width, w2t.shape[2]), lambda o: (o, 0, 0)),
                  pl.BlockSpec((pps, S5_SEG_CHUNKS, pw.shape[2]), lambda o: (o, 0, 0)),
                  pl.BlockSpec((pps, 8, aa.shape[2]), lambda o: (o, 0, 0))],
        out_specs=pl.BlockSpec((t_n, rows, LANES), lambda o: (0, 0, o)),
        out_shape=jax.ShapeDtypeStruct((t_n, rows, d), BF16),
        scratch_shapes=[pltpu.VMEM((rows, w1.shape[2]), F32),
                        pltpu.VMEM((4, slab_rows, LANES), F32),
                        pltpu.VMEM((4, slab_rows, LANES), F32),
                        pltpu.VMEM((rows, width), BF16),
                        pltpu.VMEM((rows, 4 * LANES), BF16)],
        compiler_params=_cparams("parallel"),
        name="s5_core",
    )(u_t, w1, w2t, pw, aa)


def _s5_out_kernel(y_ref, h_ref, wglu_ref, wout_ref, g_ref, o_ref):
    g = _gelu_tanh(y_ref[0].astype(F32))
    z = jnp.dot(g.astype(BF16), wglu_ref[...], preferred_element_type=F32)
    g2 = g * _sigmoid(z)
    mix = jnp.dot(g2.astype(BF16), wout_ref[...], preferred_element_type=F32)
    o_ref[...] = h_ref[...] + _rms(mix, g_ref[...])


def _s5_out(y_t, h, w_glu, w_out, gain, rc):
    n, d = h.shape
    chunks = n // S5_CHUNK
    out = pl.pallas_call(
        _s5_out_kernel,
        grid=(chunks // rc, S5_CHUNK),
        in_specs=[pl.BlockSpec((1, rc, d), lambda i, t: (t, i, 0)),
                  pl.BlockSpec((rc, d), lambda i, t: (i, t)),
                  pl.BlockSpec((d, d), lambda i, t: (0, 0)),
                  pl.BlockSpec((d, d), lambda i, t: (0, 0)),
                  pl.BlockSpec((1, d), lambda i, t: (0, 0))],
        out_specs=pl.BlockSpec((rc, d), lambda i, t: (i, t)),
        out_shape=jax.ShapeDtypeStruct((chunks, S5_CHUNK * d), F32),
        compiler_params=_cparams("parallel", "parallel"),
        name="s5_out",
    )(y_t, h.reshape(chunks, S5_CHUNK * d), w_glu, w_out, gain.reshape(1, d))
    return out.reshape(n, d)


def _swiglu(x, wg, wu, wd, chunk):
    n_chunks = wg.shape[-1] // chunk

    def gate_up(c):
        cols = slice(c * chunk, (c + 1) * chunk)
        return (jnp.dot(x, wg[:, cols], preferred_element_type=F32),
                jnp.dot(x, wu[:, cols], preferred_element_type=F32))

    y = None
    ahead = 2
    queue = [gate_up(c) for c in range(min(ahead, n_chunks))]
    for c in range(n_chunks):
        a, u = queue.pop(0)
        if c + ahead < n_chunks:
            queue.append(gate_up(c + ahead))
        act = (a * _sigmoid(a) * u).astype(BF16)
        part = jnp.dot(act, wd[c * chunk:(c + 1) * chunk, :], preferred_element_type=F32)
        y = part if y is None else y + part
    return y


def _ffn_kernel(h_ref, gpre_ref, gpost_ref, wg_ref, wu_ref, wd_ref, o_ref, hn_s, acc_s,
                *, chunk):
    f = pl.program_id(1)

    @pl.when(f == 0)
    def _():
        hn_s[...] = _rms(h_ref[...], gpre_ref[...]).astype(BF16)
        acc_s[...] = jnp.zeros_like(acc_s)

    acc_s[...] += _swiglu(hn_s[...], wg_ref, wu_ref, wd_ref, chunk)

    @pl.when(f == pl.num_programs(1) - 1)
    def _():
        o_ref[...] = h_ref[...] + _rms(acc_s[...], gpost_ref[...])


def _ffn(h, gain_pre, gain_post, w_gate, w_up, w_down, tm, tf):
    n, d = h.shape
    d_ff = w_gate.shape[1]
    mode = pl.Buffered(1) if tf == d_ff else pl.Buffered(2)
    return pl.pallas_call(
        functools.partial(_ffn_kernel, chunk=256),
        grid=(n // tm, d_ff // tf),
        in_specs=[pl.BlockSpec((tm, d), lambda i, f: (i, 0)),
                  pl.BlockSpec((1, d), lambda i, f: (0, 0)),
                  pl.BlockSpec((1, d), lambda i, f: (0, 0)),
                  pl.BlockSpec((d, tf), lambda i, f: (0, f), pipeline_mode=mode),
                  pl.BlockSpec((d, tf), lambda i, f: (0, f), pipeline_mode=mode),
                  pl.BlockSpec((tf, d), lambda i, f: (f, 0), pipeline_mode=mode)],
        out_specs=pl.BlockSpec((tm, d), lambda i, f: (i, 0)),
        out_shape=jax.ShapeDtypeStruct((n, d), F32),
        scratch_shapes=[pltpu.VMEM((tm, d), BF16), pltpu.VMEM((tm, d), F32)],
        compiler_params=_cparams("parallel", "arbitrary"),
        name="ffn",
    )(h, gain_pre.reshape(1, d), gain_post.reshape(1, d), w_gate, w_up, w_down)


def _moe_kernel(h_ref, gpre_ref, gpost_ref, wr_ref, wg_ref, wu_ref, wd_ref, o_ref,
                hn_s, gate_s, pos_s, post_s, xc_s, yc_s, xg_s, col_s, cnt_s,
                *, n_exp, rt, strip):
    e = pl.program_id(1)
    f = pl.program_id(2)
    tb, d = hn_s.shape
    n_f = pl.num_programs(2)
    n_strips = tb // strip

    def put_counts(r, total):
        lane1 = lax.broadcasted_iota(jnp.int32, total.shape, 1)
        for x in range(n_exp):
            cnt_s[r * n_exp + x] = jnp.sum(jnp.where(lane1 == x, total, 0.0)).astype(jnp.int32)

    @pl.when((e == 0) & (f == 0))
    def _():
        before = (lax.broadcasted_iota(jnp.int32, (strip, strip), 1)
                  < lax.broadcasted_iota(jnp.int32, (strip, strip), 0)).astype(BF16)

        def route(r, total):
            put_counts(r, total)
            rows = pl.ds(pl.multiple_of(r * strip, strip), strip)
            xn = _rms(h_ref[rows, :], gpre_ref[...])
            hn_s[rows, :] = xn.astype(BF16)
            o_ref[rows, :] = jnp.zeros((strip, d), F32)
            x_hi = xn.astype(BF16)
            x_lo = (xn - x_hi.astype(F32)).astype(BF16)
            logits = (jnp.dot(x_hi, wr_ref[0], preferred_element_type=F32)
                      + jnp.dot(x_lo, wr_ref[0], preferred_element_type=F32)
                      + jnp.dot(x_hi, wr_ref[1], preferred_element_type=F32))
            lane = lax.broadcasted_iota(jnp.int32, logits.shape, 1)
            neg = jnp.float32(-jnp.inf)
            logits = jnp.where(lane < n_exp, logits, neg)
            m1 = jnp.max(logits, axis=-1, keepdims=True)
            i1 = jnp.min(jnp.where(logits == m1, lane, LANES), axis=-1, keepdims=True)
            rest = jnp.where(lane == i1, neg, logits)
            m2 = jnp.max(rest, axis=-1, keepdims=True)
            i2 = jnp.min(jnp.where(rest == m2, lane, LANES), axis=-1, keepdims=True)
            e2 = jnp.exp(m2 - m1)
            gate_s[rows, :] = (jnp.where(lane == i1, 1.0 / (1.0 + e2), 0.0)
                               + jnp.where(lane == i2, e2 / (1.0 + e2), 0.0))
            sel = ((lane == i1) | (lane == i2)).astype(F32)
            earlier = jnp.dot(before, sel.astype(BF16), preferred_element_type=F32) + total
            pos_s[rows, :] = jnp.where(sel > 0, earlier, -1.0)
            return total + jnp.sum(sel, axis=0, keepdims=True)

        total = lax.fori_loop(0, n_strips, route, jnp.zeros((1, LANES), F32))
        put_counts(n_strips, total)

        for r in range(tb // strip):
            post_s[:, r * strip:(r + 1) * strip] = pos_s[r * strip:(r + 1) * strip, :].T

    @pl.when(f == n_f - 1)
    def _():
        lane = lax.broadcasted_iota(jnp.int32, (tb, LANES), 1)
        col_s[0] = jnp.sum(jnp.where(lane == e, pos_s[...], 0.0), axis=-1, keepdims=True)
        col_s[1] = jnp.sum(jnp.where(lane == e, gate_s[...], 0.0), axis=-1, keepdims=True)

    n_tiles = (cnt_s[n_strips * n_exp + e] + rt - 1) // rt

    def tile(i, _):
        r0 = pl.multiple_of(i * rt, 16)

        def holds(r):
            return ((cnt_s[r * n_exp + e] < r0 + rt) & (cnt_s[(r + 1) * n_exp + e] > r0))

        @pl.when(f == 0)
        def _():
            xg_s[...] = jnp.zeros_like(xg_s)
            slot = (lax.broadcasted_iota(jnp.int32, (rt, strip), 0) + r0).astype(F32)
            for r in range(n_strips):
                @pl.when(holds(r))
                def _():
                    cols = slice(r * strip, (r + 1) * strip)
                    pick = (post_s[pl.ds(e, 1), cols] == slot).astype(BF16)
                    xg_s[...] += jnp.dot(pick, hn_s[cols, :], preferred_element_type=F32)
            xc_s[pl.ds(r0, rt), :] = xg_s[...].astype(BF16)

        xc = xc_s[pl.ds(r0, rt), :]
        a = jnp.dot(xc, wg_ref[0], preferred_element_type=F32)
        u = jnp.dot(xc, wu_ref[0], preferred_element_type=F32)
        act = (a * _sigmoid(a) * u).astype(BF16)
        y = jnp.dot(act, wd_ref[0], preferred_element_type=F32)
        prev = jnp.where(f == 0, 0.0, yc_s[pl.ds(r0, rt), :])
        yc_s[pl.ds(r0, rt), :] = prev + y

        @pl.when(f == n_f - 1)
        def _():
            yc = yc_s[pl.ds(r0, rt), :].astype(BF16)
            slot = (lax.broadcasted_iota(jnp.int32, (strip, rt), 1) + r0).astype(F32)
            for r in range(n_strips):
                @pl.when(holds(r))
                def _():
                    rows = slice(r * strip, (r + 1) * strip)
                    put = (col_s[0, rows, :] == slot).astype(BF16)
                    back = jnp.dot(put, yc, preferred_element_type=F32)
                    o_ref[rows, :] += col_s[1, rows, :] * back
        return 0

    lax.fori_loop(0, n_tiles, tile, 0)

    @pl.when((e == n_exp - 1) & (f == n_f - 1))
    def _():
        def finish(r, _):
            rows = pl.ds(pl.multiple_of(r * strip, strip), strip)
            o_ref[rows, :] = h_ref[rows, :] + _rms(o_ref[rows, :], gpost_ref[...])
            return 0

        lax.fori_loop(0, tb // strip, finish, 0)


def _moe(h, gain_pre, gain_post, w_router, w_gate, w_up, w_down, tb, tf, rt):
    n, d = h.shape
    n_exp, _, d_ff = w_gate.shape
    w_r = jnp.pad(w_router, ((0, 0), (0, LANES - n_exp)))
    w_hi = w_r.astype(BF16)
    w_r = jnp.stack([w_hi, (w_r - w_hi.astype(F32)).astype(BF16)])
    once = pl.Buffered(1)
    strip = 256
    cap = pl.cdiv(tb, rt) * rt
    return pl.pallas_call(
        functools.partial(_moe_kernel, n_exp=n_exp, rt=rt, strip=strip),
        grid=(n // tb, n_exp, d_ff // tf),
        in_specs=[pl.BlockSpec((tb, d), lambda i, e, f: (i, 0), pipeline_mode=once),
                  pl.BlockSpec((1, d), lambda i, e, f: (0, 0)),
                  pl.BlockSpec((1, d), lambda i, e, f: (0, 0)),
                  pl.BlockSpec((2, d, LANES), lambda i, e, f: (0, 0, 0)),
                  pl.BlockSpec((1, d, tf), lambda i, e, f: (e, 0, f)),
                  pl.BlockSpec((1, d, tf), lambda i, e, f: (e, 0, f)),
                  pl.BlockSpec((1, tf, d), lambda i, e, f: (e, f, 0))],
        out_specs=pl.BlockSpec((tb, d), lambda i, e, f: (i, 0), pipeline_mode=once),
        out_shape=jax.ShapeDtypeStruct((n, d), F32),
        scratch_shapes=[pltpu.VMEM((tb, d), BF16),
                        pltpu.VMEM((tb, LANES), F32),
                        pltpu.VMEM((tb, LANES), F32),
                        pltpu.VMEM((LANES, tb), F32),
                        pltpu.VMEM((cap, d), BF16),
                        pltpu.VMEM((cap, d), F32),
                        pltpu.VMEM((rt, d), F32),
                        pltpu.VMEM((2, tb, 1), F32),
                        pltpu.SMEM(((tb // strip + 1) * n_exp,), jnp.int32)],
        compiler_params=pltpu.CompilerParams(
            dimension_semantics=("parallel", "arbitrary", "arbitrary"),
            vmem_limit_bytes=MOE_VMEM_LIMIT_BYTES),
        name="moe_ffn",
    )(h, gain_pre.reshape(1, d), gain_post.reshape(1, d), w_r, w_gate, w_up, w_down)


def _qkv_kernel(h_ref, g_ref, w_ref, qg_ref, kg_ref, cs_ref, sn_ref, bd_ref,
                q_ref, k_ref, v_ref):
    xn = _rms(h_ref[...], g_ref[...]).astype(BF16)
    qkv = jnp.dot(xn, w_ref[...], preferred_element_type=F32)
    cs = cs_ref[...]
    sn = sn_ref[...]
    bd = bd_ref[...]
    lane = lax.broadcasted_iota(jnp.int32, cs.shape, 1)
    first_half = (lane % HEAD_DIM) < (HEAD_DIM // 2)
    scale = math.log2(math.e) / math.sqrt(HEAD_DIM)

    def norm_rope(x, gain):
        ms = jnp.dot(x * x, bd, preferred_element_type=F32)
        y = x * lax.rsqrt(ms + NORM_EPS) * gain
        partner = jnp.where(first_half,
                            pltpu.roll(y, LANES - HEAD_DIM // 2, axis=1),
                            pltpu.roll(y, HEAD_DIM // 2, axis=1))
        return y * cs + partner * sn

    n_q_tiles = N_HEADS * HEAD_DIM // LANES
    for t in range(n_q_tiles):
        y = norm_rope(qkv[:, t * LANES:(t + 1) * LANES], qg_ref[...]) * scale
        q_ref[0, 2 * t] = y[:, 0:HEAD_DIM].astype(BF16)
        q_ref[0, 2 * t + 1] = y[:, HEAD_DIM:LANES].astype(BF16)
    k0 = N_HEADS * HEAD_DIM
    for t in range(N_KV_HEADS * HEAD_DIM // LANES):
        y = norm_rope(qkv[:, k0 + t * LANES:k0 + (t + 1) * LANES], kg_ref[...])
        k_ref[0, 2 * t] = y[:, 0:HEAD_DIM].astype(BF16)
        k_ref[0, 2 * t + 1] = y[:, HEAD_DIM:LANES].astype(BF16)
    v0 = (N_HEADS + N_KV_HEADS) * HEAD_DIM
    ones = jnp.ones((qkv.shape[0], LANES - HEAD_DIM), BF16)
    for j in range(N_KV_HEADS):
        vj = qkv[:, v0 + j * HEAD_DIM:v0 + (j + 1) * HEAD_DIM].astype(BF16)
        v_ref[0, j] = jnp.concatenate([vj, ones], axis=-1)


def _rope_tables(seq):
    axis_dim = HEAD_DIM // 2
    freqs = ROPE_THETA ** (-jnp.arange(0, axis_dim, 2, dtype=F32) / axis_dim)
    rows = seq // GRID_W
    row_ang = jnp.arange(rows, dtype=F32)[:, None] * freqs
    col_ang = jnp.arange(GRID_W, dtype=F32)[:, None] * freqs
    ang = jnp.concatenate([
        jnp.broadcast_to(row_ang[:, None, :], (rows, GRID_W, freqs.shape[0])),
        jnp.broadcast_to(col_ang[None, :, :], (rows, GRID_W, freqs.shape[0]))], axis=-1)
    ang = ang.reshape(seq, HEAD_DIM // 2)
    cos, sin = jnp.cos(ang), jnp.sin(ang)
    cs = jnp.tile(jnp.concatenate([cos, cos], axis=-1), (1, LANES // HEAD_DIM))
    sn = jnp.tile(jnp.concatenate([-sin, sin], axis=-1), (1, LANES // HEAD_DIM))
    return cs, sn


def _qkv(h, gain, w_qkv, q_gain, k_gain, bsz, seq, tm):
    n, d = h.shape
    width = w_qkv.shape[1]
    perm = jnp.concatenate([jnp.arange(0, HEAD_DIM, 2), jnp.arange(1, HEAD_DIM, 2)])
    n_rot = N_HEADS + N_KV_HEADS
    cols = (jnp.arange(n_rot)[:, None] * HEAD_DIM + perm[None, :]).reshape(-1)
    cols = jnp.concatenate([cols, jnp.arange(n_rot * HEAD_DIM, width)])
    w = w_qkv[:, cols].astype(BF16)
    qg = jnp.tile(q_gain[perm], LANES // HEAD_DIM).reshape(1, LANES)
    kg = jnp.tile(k_gain[perm], LANES // HEAD_DIM).reshape(1, LANES)
    cs, sn = _rope_tables(seq)
    blk = jnp.arange(LANES) // HEAD_DIM
    bd = (blk[:, None] == blk[None, :]).astype(F32) / HEAD_DIM
    per_seq = seq // tm
    return pl.pallas_call(
        _qkv_kernel,
        grid=(n // tm,),
        in_specs=[pl.BlockSpec((tm, d), lambda i: (i, 0)),
                  pl.BlockSpec((1, d), lambda i: (0, 0)),
                  pl.BlockSpec((d, width), lambda i: (0, 0)),
                  pl.BlockSpec((1, LANES), lambda i: (0, 0)),
                  pl.BlockSpec((1, LANES), lambda i: (0, 0)),
                  pl.BlockSpec((tm, LANES), lambda i: (i % per_seq, 0)),
                  pl.BlockSpec((tm, LANES), lambda i: (i % per_seq, 0)),
                  pl.BlockSpec((LANES, LANES), lambda i: (0, 0))],
        out_specs=[pl.BlockSpec((1, N_HEADS, tm, HEAD_DIM),
                                lambda i: (i // per_seq, 0, i % per_seq, 0)),
                   pl.BlockSpec((1, N_KV_HEADS, tm, HEAD_DIM),
                                lambda i: (i // per_seq, 0, i % per_seq, 0)),
                   pl.BlockSpec((1, N_KV_HEADS, tm, LANES),
                                lambda i: (i // per_seq, 0, i % per_seq, 0))],
        out_shape=[jax.ShapeDtypeStruct((bsz, N_HEADS, seq, HEAD_DIM), BF16),
                   jax.ShapeDtypeStruct((bsz, N_KV_HEADS, seq, HEAD_DIM), BF16),
                   jax.ShapeDtypeStruct((bsz, N_KV_HEADS, seq, LANES), BF16)],
        compiler_params=_cparams("parallel"),
        name="qkv_rope",
    )(h, gain.reshape(1, d), w, qg, kg, cs, sn, bd)


def _attn_kernel(q_ref, k_ref, v_ref, o_ref, m_s, acc_s, s_buf, p_buf, a_buf,
                 *, tq, tk, rc, unroll):
    seq = k_ref.shape[2]
    chunks_per_head = tq // rc
    n_chunks = Q_PER_KV * chunks_per_head
    n_steps = (seq // tk) * n_chunks

    m_s[...] = jnp.full_like(m_s, -jnp.inf)
    acc_s[...] = jnp.zeros_like(acc_s)

    def where(n):
        c = n % n_chunks
        return (pl.multiple_of((n // n_chunks) * tk, tk), c // chunks_per_head,
                pl.multiple_of((c % chunks_per_head) * rc, rc))

    def scores(n):
        k0, g, r0 = where(n)
        q = q_ref[0, g, pl.ds(r0, rc), :]
        s_buf[...] = lax.dot_general(q, k_ref[0, 0, pl.ds(k0, tk), :], _NT,
                                     preferred_element_type=F32)

    def softmax(n):
        _, g, r0 = where(n)
        s = s_buf[...]
        m_prev = m_s[g, pl.ds(r0, rc), :]
        m_new = jnp.maximum(m_prev, jnp.max(s, axis=-1, keepdims=True))
        a_buf[...] = jnp.exp2(m_prev - m_new)
        for t in range(tk // LANES):
            p_buf[:, t * LANES:(t + 1) * LANES] = jnp.exp2(
                s[:, t * LANES:(t + 1) * LANES] - m_new).astype(BF16)
        m_s[g, pl.ds(r0, rc), :] = m_new

    def values(n):
        k0, g, r0 = where(n)
        pv = jnp.dot(p_buf[...], v_ref[0, 0, pl.ds(k0, tk), :],
                     preferred_element_type=F32)
        acc_s[g, pl.ds(r0, rc), :] = a_buf[...] * acc_s[g, pl.ds(r0, rc), :] + pv

    scores(0)
    softmax(0)
    scores(1)

    def body(n, _):
        values(n)
        softmax(n + 1)
        scores(n + 2)
        return 0

    lax.fori_loop(0, n_steps - 2, body, 0, unroll=unroll)
    values(n_steps - 2)
    softmax(n_steps - 1)
    values(n_steps - 1)

    outs = []
    for g in range(Q_PER_KV):
        acc = acc_s[g]
        o = acc / pltpu.roll(acc, HEAD_DIM, axis=1)
        outs.append(o[:, 0:HEAD_DIM])
    o_ref[...] = jnp.concatenate(outs, axis=-1).astype(o_ref.dtype)


def _attention(q, k, v, tq, tk, rc):
    bsz, _, seq, _ = q.shape
    n_q = seq // tq
    n_steps = (seq // tk) * Q_PER_KV * (tq // rc)
    unroll = min(8, n_steps - 2)
    return pl.pallas_call(
        functools.partial(_attn_kernel, tq=tq, tk=tk, rc=rc, unroll=unroll),
        grid=(bsz, N_KV_HEADS, n_q),
        in_specs=[pl.BlockSpec((1, Q_PER_KV, tq, HEAD_DIM), lambda b, j, i: (b, j, i, 0)),
                  pl.BlockSpec((1, 1, seq, HEAD_DIM), lambda b, j, i: (b, j, 0, 0)),
                  pl.BlockSpec((1, 1, seq, LANES), lambda b, j, i: (b, j, 0, 0))],
        out_specs=pl.BlockSpec((tq, Q_PER_KV * HEAD_DIM), lambda b, j, i: (b * n_q + i, j)),
        out_shape=jax.ShapeDtypeStruct((bsz * seq, N_HEADS * HEAD_DIM), BF16),
        scratch_shapes=[pltpu.VMEM((Q_PER_KV, tq, LANES), F32),
                        pltpu.VMEM((Q_PER_KV, tq, LANES), F32),
                        pltpu.VMEM((rc, tk), F32),
                        pltpu.VMEM((rc, tk), BF16),
                        pltpu.VMEM((rc, LANES), F32)],
        compiler_params=_cparams("parallel", "parallel", "parallel"),
        name="flash_attn",
    )(q, k, v)


def _proj_res_kernel(x_ref, h_ref, w_ref, g_ref, o_ref):
    mix = jnp.dot(x_ref[...], w_ref[...], preferred_element_type=F32)
    o_ref[...] = h_ref[...] + _rms(mix, g_ref[...])


def _proj_res(x, h, w, gain, tm):
    n, d = h.shape
    k = x.shape[1]
    return pl.pallas_call(
        _proj_res_kernel,
        grid=(n // tm,),
        in_specs=[pl.BlockSpec((tm, k), lambda i: (i, 0)),
                  pl.BlockSpec((tm, d), lambda i: (i, 0)),
                  pl.BlockSpec((k, d), lambda i: (0, 0)),
                  pl.BlockSpec((1, d), lambda i: (0, 0))],
        out_specs=pl.BlockSpec((tm, d), lambda i: (i, 0)),
        out_shape=jax.ShapeDtypeStruct((n, d), F32),
        compiler_params=_cparams("parallel"),
        name="proj_res",
    )(x, h, w, gain.reshape(1, d))


def _s5_layer(h, bsz, seq, gains, w_in, lam_re, lam_im, log_step, b_re, b_im, c_re, c_im,
              d_skip, w_glu, w_out):
    seg_tokens = S5_CHUNK * S5_SEG_CHUNKS
    nseg = seq // seg_tokens
    nseq = bsz * nseg
    rc = min(512, h.shape[0] // S5_CHUNK)
    u_t = _norm_matmul(h, gains[0], w_in.astype(BF16), rc=rc)
    w1, w2t, pw_p, aa_p = _s5_prep(lam_re, lam_im, log_step, b_re, b_im, c_re, c_im, d_skip)
    y_t = _s5_core(u_t, w1, w2t, pw_p, aa_p, nseq, nseg)
    return _s5_out(y_t, h, w_glu.astype(BF16), w_out.astype(BF16), gains[1], rc=rc)


def _attn_layer(h, bsz, seq, gains, w_qkv, q_gain, k_gain, w_out):
    q, k, v = _qkv(h, gains[0], w_qkv, q_gain, k_gain, bsz, seq, tm=512)
    o = _attention(q, k, v, tq=min(4096, seq), tk=512, rc=512)
    return _proj_res(o, h, w_out.astype(BF16), gains[1], tm=512)


def kernel(x, norm_gains, ssm_w_in, ssm_lambda_re, ssm_lambda_im, ssm_log_step, ssm_b_re,
           ssm_b_im, ssm_c_re, ssm_c_im, ssm_d, ssm_w_glu, ssm_w_out, ffn_w_gate, ffn_w_up,
           ffn_w_down, attn_w_qkv, attn_q_gain, attn_k_gain, attn_w_out, moe_w_router,
           moe_w_gate, moe_w_up, moe_w_down):
    bsz, seq, d = x.shape
    depth = norm_gains.shape[0]
    h = x.reshape(bsz * seq, d)
    for i in range(depth):
        j = i // 2
        g = norm_gains[i]
        if i % 2 == 0:
            h = _s5_layer(h, bsz, seq, g, ssm_w_in[j], ssm_lambda_re[j], ssm_lambda_im[j],
                          ssm_log_step[j], ssm_b_re[j], ssm_b_im[j], ssm_c_re[j], ssm_c_im[j],
                          ssm_d[j], ssm_w_glu[j], ssm_w_out[j])
            h = _ffn(h, g[2], g[3], ffn_w_gate[j].astype(BF16), ffn_w_up[j].astype(BF16),
                     ffn_w_down[j].astype(BF16), tm=1024, tf=ffn_w_gate.shape[-1])
        else:
            h = _attn_layer(h, bsz, seq, g, attn_w_qkv[j], attn_q_gain[j], attn_k_gain[j],
                            attn_w_out[j])
            h = _moe(h, g[2], g[3], moe_w_router[j], moe_w_gate[j].astype(BF16),
                     moe_w_up[j].astype(BF16), moe_w_down[j].astype(BF16),
                     tb=1024, tf=1792, rt=288)
    return h.reshape(bsz, seq, d)
```

```python
import functools
import math

import jax
import jax.numpy as jnp
from jax import lax
from jax.experimental import pallas as pl
from jax.experimental.pallas import tpu as pltpu

F32 = jnp.float32
BF16 = jnp.bfloat16
NORM_EPS = 1e-6
ROPE_THETA = 10000.0
GRID_W = 64
N_HEADS = 16
N_KV_HEADS = 4
HEAD_DIM = 64
Q_PER_KV = N_HEADS // N_KV_HEADS
SSM_GROUP = 16
SSM_STATE = 64
S5_CHUNK = 16
S5_SEG_CHUNKS = 64
TOP_K = 2
LANES = 128
VMEM_LIMIT_BYTES = 56 * 1024 * 1024
MOE_VMEM_LIMIT_BYTES = 60 * 1024 * 1024

_NT = (((1,), (1,)), ((), ()))


def _cparams(*sem):
    return pltpu.CompilerParams(dimension_semantics=sem, vmem_limit_bytes=VMEM_LIMIT_BYTES)


def _rms(x, gain):
    return x * lax.rsqrt(jnp.mean(x * x, axis=-1, keepdims=True) + NORM_EPS) * gain


def _sigmoid(x):
    return 1.0 / (1.0 + jnp.exp(-x))


def _gelu_tanh(x):
    return x * (0.5 * (1.0 + jnp.tanh(math.sqrt(2.0 / math.pi) * (x + 0.044715 * (x * x * x)))))


def _norm_matmul_kernel(x_ref, g_ref, w_ref, o_ref):
    xn = _rms(x_ref[...], g_ref[...]).astype(BF16)
    o_ref[0] = jnp.dot(xn, w_ref[...], preferred_element_type=F32).astype(o_ref.dtype)


def _norm_matmul(x, gain, w, rc):
    n, d = x.shape
    m = w.shape[1]
    chunks = n // S5_CHUNK
    return pl.pallas_call(
        _norm_matmul_kernel,
        grid=(chunks // rc, S5_CHUNK),
        in_specs=[pl.BlockSpec((rc, d), lambda i, t: (i, t)),
                  pl.BlockSpec((1, d), lambda i, t: (0, 0)),
                  pl.BlockSpec((d, m), lambda i, t: (0, 0))],
        out_specs=pl.BlockSpec((1, rc, m), lambda i, t: (t, i, 0)),
        out_shape=jax.ShapeDtypeStruct((S5_CHUNK, chunks, m), BF16),
        compiler_params=_cparams("parallel", "parallel"),
        name="norm_matmul",
    )(x.reshape(chunks, S5_CHUNK * d), gain.reshape(1, d), w)


def _s5_prep_kernel(lr_ref, li_ref, ls_ref, bt_re_ref, bt_im_ref, c_re_ref, c_im_ref, dt_ref,
                    w1_ref, w2t_ref, pw_ref, aa_ref):
    t_n, s_n, p_n = S5_CHUNK, SSM_GROUP, SSM_STATE
    pair = 2 * s_n
    rows = t_n * pair
    hi = lax.Precision.HIGHEST

    rr = lax.broadcasted_iota(jnp.int32, (rows, rows), 0)
    cc = lax.broadcasted_iota(jnp.int32, (rows, rows), 1)
    same_group = ((rr // s_n) % 2) == ((cc // s_n) % 2)
    tvec = lax.broadcasted_iota(jnp.int32, (t_n, 1), 0).astype(F32)
    jvec = lax.broadcasted_iota(jnp.int32, (S5_SEG_CHUNKS, 1), 0).astype(F32)
    row8 = lax.broadcasted_iota(jnp.int32, (8, LANES), 0)
    own_lanes = ((lax.broadcasted_iota(jnp.int32, (rows, LANES), 1) // p_n)
                 == ((lax.broadcasted_iota(jnp.int32, (rows, LANES), 0) // s_n) % 2))

    def lanes_by_group(tab0, tab1):
        lane = lax.broadcasted_iota(jnp.int32, tab0.shape, 1)
        return jnp.where(lane < p_n, tab0, tab1)

    m_total = None
    for d in range(2):
        lsr, lsi, q_re, q_im = [], [], [], []
        for g in range(2):
            lr = lr_ref[0, d, g:g + 1, :]
            li = li_ref[0, d, g:g + 1, :]
            step = jnp.exp(ls_ref[0, d, g:g + 1, :])
            lsr.append(lr * step)
            lsi.append(li * step)
            mag = jnp.exp(lsr[g])
            a_re, a_im = mag * jnp.cos(lsi[g]), mag * jnp.sin(lsi[g])
            nr, ni = a_re - 1.0, a_im
            den = lr * lr + li * li
            q_re.append((nr * lr + ni * li) / den)
            q_im.append((ni * lr - nr * li) / den)

        def cpow(k, g):
            mag = jnp.exp(lsr[g] * k)
            ang = lsi[g] * k
            return mag * jnp.cos(ang), mag * jnp.sin(ang)

        def table(k):
            t0, t1 = cpow(k, 0), cpow(k, 1)
            return tuple(
                jnp.concatenate([jnp.broadcast_to(tg[part][t:t + 1, :], (s_n, LANES))
                                 for t in range(t_n) for tg in (t0, t1)], axis=0)
                for part in range(2))

        br, bi = bt_re_ref[0, d], bt_im_ref[0, d]
        qr = jnp.concatenate([jnp.broadcast_to(q_re[g], (s_n, LANES)) for g in range(2)], axis=0)
        qi = jnp.concatenate([jnp.broadcast_to(q_im[g], (s_n, LANES)) for g in range(2)], axis=0)
        bb_re = qr * br - qi * bi
        bb_im = qr * bi + qi * br
        cr, ci = c_re_ref[0, d], c_im_ref[0, d]

        def outer(x_re, x_im, k):
            pe_re, pe_im = table(k)
            xe_re = jnp.concatenate([x_re] * t_n, axis=0)
            xe_im = jnp.concatenate([x_im] * t_n, axis=0)
            return xe_re * pe_re - xe_im * pe_im, xe_re * pe_im + xe_im * pe_re

        if d == 0:
            l_re, l_im = outer(bb_re, bb_im, -tvec)
            rt_re, rt_im = outer(cr, ci, tvec)
            ws_re, ws_im = outer(bb_re, bb_im, (t_n - 1.0) - tvec)
            wy_re, wy_im = outer(cr, ci, tvec + 1.0)
            mask = (rr // pair) <= (cc // pair)
            kseg = t_n * jvec
        else:
            l_re, l_im = outer(bb_re, bb_im, tvec)
            rt_re, rt_im = outer(cr, ci, -tvec)
            ws_re, ws_im = l_re, l_im
            wy_re, wy_im = outer(cr, ci, t_n - tvec)
            mask = (rr // pair) >= (cc // pair)
            kseg = t_n * ((S5_SEG_CHUNKS - 1.0) - jvec)
        kern = (lax.dot_general(l_re[:, 0:p_n], rt_re[:, 0:p_n], _NT, precision=hi,
                                preferred_element_type=F32)
                - lax.dot_general(l_im[:, 0:p_n], rt_im[:, 0:p_n], _NT, precision=hi,
                                  preferred_element_type=F32))
        kern = jnp.where(mask & same_group, kern, 0.0)
        m_total = kern if m_total is None else m_total + kern

        ycols = rows
        for k2, val in ((2 * d, ws_re), (2 * d + 1, ws_im)):
            w1_ref[0, :, ycols + k2 * LANES:ycols + (k2 + 1) * LANES] = jnp.where(
                own_lanes, val, 0.0).astype(w1_ref.dtype)
        for k2, val in ((2 * d, wy_re), (2 * d + 1, -wy_im)):
            w2t_ref[0, :, k2 * LANES:(k2 + 1) * LANES] = jnp.where(
                own_lanes, val, 0.0).astype(w2t_ref.dtype)
        p0, p1 = cpow(kseg, 0), cpow(kseg, 1)
        pw_ref[0, :, 2 * d * LANES:(2 * d + 1) * LANES] = lanes_by_group(p0[0], p1[0])
        pw_ref[0, :, (2 * d + 1) * LANES:(2 * d + 2) * LANES] = lanes_by_group(p0[1], p1[1])
        one = jnp.ones((1, 1), F32)
        c0, c1 = cpow(float(t_n) * one, 0), cpow(float(t_n) * one, 1)
        s0, s1 = (cpow(float(t_n * S5_SEG_CHUNKS) * one, 0),
                  cpow(float(t_n * S5_SEG_CHUNKS) * one, 1))
        for part in range(2):
            chunk_pow = lanes_by_group(c0[part], c1[part])
            seg_pow = lanes_by_group(s0[part], s1[part])
            aa_ref[0, :, (2 * d + part) * LANES:(2 * d + part + 1) * LANES] = jnp.where(
                row8 == 0, chunk_pow, jnp.where(row8 == 1, seg_pow, 0.0))

    w1_ref[0, :, 0:rows] = (m_total + jnp.where(rr == cc, dt_ref[0], 0.0)).astype(w1_ref.dtype)


def _s5_prep(lam_re, lam_im, log_step, b_re, b_im, c_re, c_im, d_skip):
    g_n = lam_re.shape[1]
    pairs = g_n // 2
    pair = 2 * SSM_GROUP
    rows = S5_CHUNK * pair

    def states(x):
        x = jnp.transpose(x.reshape(2, pairs, 2, SSM_STATE), (1, 0, 2, 3))
        return jnp.tile(x, (1, 1, 1, LANES // SSM_STATE))

    def per_channel(x):
        x = jnp.transpose(x.reshape(2, pairs, pair, SSM_STATE), (1, 0, 2, 3))
        return jnp.tile(x, (1, 1, 1, LANES // SSM_STATE))

    ls = jnp.transpose(log_step.reshape(2, pairs, 2), (1, 0, 2))[..., None]
    bt_re = per_channel(jnp.swapaxes(b_re, 2, 3))
    bt_im = per_channel(jnp.swapaxes(b_im, 2, 3))
    dt = jnp.tile(d_skip.reshape(pairs, 1, pair), (1, 1, S5_CHUNK))

    def spec(*blk):
        return pl.BlockSpec((1,) + blk, lambda g: (g,) + (0,) * len(blk))

    return pl.pallas_call(
        _s5_prep_kernel,
        grid=(pairs,),
        in_specs=[spec(2, 2, LANES), spec(2, 2, LANES), spec(2, 2, 1),
                  spec(2, pair, LANES), spec(2, pair, LANES),
                  spec(2, pair, LANES), spec(2, pair, LANES), spec(1, rows)],
        out_specs=[spec(rows, rows + 4 * LANES), spec(rows, 4 * LANES),
                   spec(S5_SEG_CHUNKS, 4 * LANES), spec(8, 4 * LANES)],
        out_shape=[jax.ShapeDtypeStruct((pairs, rows, rows + 4 * LANES), BF16),
                   jax.ShapeDtypeStruct((pairs, rows, 4 * LANES), BF16),
                   jax.ShapeDtypeStruct((pairs, S5_SEG_CHUNKS, 4 * LANES), F32),
                   jax.ShapeDtypeStruct((pairs, 8, 4 * LANES), F32)],
        compiler_params=_cparams("parallel"),
        name="s5_prep",
    )(states(lam_re), states(lam_im), ls, bt_re, bt_im, per_channel(c_re), per_channel(c_im), dt)


S5_PAIR_LANES = 2 * SSM_GROUP
S5_PAIRS_PER_SLAB = LANES // S5_PAIR_LANES
S5_SEG_PITCH = S5_SEG_CHUNKS + 8
S5_SEG_BASE = 8


def _s5_core_kernel(x_ref, w1_ref, w2t_ref, pw_ref, aa_ref, y_ref, z_s, zs_s, x_s, u_s, xb_s,
                    *, nseq, nseg, strip):
    n_chunks = S5_SEG_CHUNKS
    rows = n_chunks * nseq
    ycols = u_s.shape[1]
    per_tile = LANES // S5_PAIR_LANES
    lane_grp = lax.broadcasted_iota(jnp.int32, (strip, LANES), 1) // S5_PAIR_LANES
    zero = jnp.zeros((nseq, LANES), F32)

    def seg_rows(j):
        return pl.ds(S5_SEG_BASE + j, nseq, stride=S5_SEG_PITCH)

    for q in range(S5_PAIRS_PER_SLAB):
        def gather(r, _, q=q):
            rws = pl.ds(pl.multiple_of(r * strip, strip), strip)
            for j in range(ycols // LANES):
                acc = None
                for i in range(per_tile):
                    xt = x_ref[per_tile * j + i, rws, :].astype(F32)
                    shift = (S5_PAIR_LANES * (i - q)) % LANES
                    if shift:
                        xt = pltpu.roll(xt, shift, axis=1)
                    acc = xt if acc is None else jnp.where(lane_grp == i, xt, acc)
                u_s[rws, j * LANES:(j + 1) * LANES] = acc.astype(BF16)
            return 0

        lax.fori_loop(0, rows // strip, gather, 0)
        z_s[...] = jnp.dot(u_s[...], w1_ref[q], preferred_element_type=F32)

        for b in range(nseq):
            dst = slice(S5_SEG_BASE + b * S5_SEG_PITCH, S5_SEG_BASE + b * S5_SEG_PITCH + n_chunks)
            for k in range(4):
                zs_s[k, dst, :] = z_s[b * n_chunks:(b + 1) * n_chunks,
                                      ycols + k * LANES:ycols + (k + 1) * LANES]
        x_s[0, seg_rows(0), :] = zero
        x_s[1, seg_rows(0), :] = zero
        x_s[2, seg_rows(n_chunks - 1), :] = zero
        x_s[3, seg_rows(n_chunks - 1), :] = zero

        aa = aa_ref[q]
        a_fr, a_fi = aa[0:1, 0:LANES], aa[0:1, LANES:2 * LANES]
        a_br, a_bi = aa[0:1, 2 * LANES:3 * LANES], aa[0:1, 3 * LANES:4 * LANES]
        g_fr, g_fi = aa[1:2, 0:LANES], aa[1:2, LANES:2 * LANES]
        g_br, g_bi = aa[1:2, 2 * LANES:3 * LANES], aa[1:2, 3 * LANES:4 * LANES]

        def fwd(j, carry):
            xr, xi = carry
            nr = a_fr * xr - a_fi * xi + zs_s[0, seg_rows(j), :]
            ni = a_fr * xi + a_fi * xr + zs_s[1, seg_rows(j), :]
            x_s[0, seg_rows(j + 1), :] = nr
            x_s[1, seg_rows(j + 1), :] = ni
            return nr, ni

        def bwd(i, carry):
            xr, xi = carry
            j = n_chunks - 1 - i
            nr = a_br * xr - a_bi * xi + zs_s[2, seg_rows(j), :]
            ni = a_br * xi + a_bi * xr + zs_s[3, seg_rows(j), :]
            x_s[2, seg_rows(j - 1), :] = nr
            x_s[3, seg_rows(j - 1), :] = ni
            return nr, ni

        ef_r, ef_i = lax.fori_loop(0, n_chunks, fwd, (zero, zero))
        eb_r, eb_i = lax.fori_loop(0, n_chunks, bwd, (zero, zero))

        def seg_carries(e_r, e_i, g_r, g_i, reverse):
            out_r = [None] * nseq
            out_i = [None] * nseq
            for b in range(nseq // nseg):
                c_r = jnp.zeros((1, LANES), F32)
                c_i = jnp.zeros((1, LANES), F32)
                order = range(nseg - 1, -1, -1) if reverse else range(nseg)
                for s in order:
                    r = b * nseg + s
                    out_r[r], out_i[r] = c_r, c_i
                    n_r = g_r * c_r - g_i * c_i + e_r[r:r + 1, :]
                    n_i = g_r * c_i + g_i * c_r + e_i[r:r + 1, :]
                    c_r, c_i = n_r, n_i
            return out_r, out_i

        cf_r, cf_i = seg_carries(ef_r, ef_i, g_fr, g_fi, False)
        cb_r, cb_i = seg_carries(eb_r, eb_i, g_br, g_bi, True)

        p = pw_ref[q]
        p_fr, p_fi = p[:, 0:LANES], p[:, LANES:2 * LANES]
        p_br, p_bi = p[:, 2 * LANES:3 * LANES], p[:, 3 * LANES:4 * LANES]
        for b in range(nseq):
            src = slice(S5_SEG_BASE + b * S5_SEG_PITCH, S5_SEG_BASE + b * S5_SEG_PITCH + n_chunks)
            dst = slice(b * n_chunks, (b + 1) * n_chunks)
            xb_s[dst, 0:LANES] = (x_s[0, src, :] + (p_fr * cf_r[b] - p_fi * cf_i[b])).astype(BF16)
            xb_s[dst, LANES:2 * LANES] = (
                x_s[1, src, :] + (p_fr * cf_i[b] + p_fi * cf_r[b])).astype(BF16)
            xb_s[dst, 2 * LANES:3 * LANES] = (
                x_s[2, src, :] + (p_br * cb_r[b] - p_bi * cb_i[b])).astype(BF16)
            xb_s[dst, 3 * LANES:4 * LANES] = (
                x_s[3, src, :] + (p_br * cb_i[b] + p_bi * cb_r[b])).astype(BF16)

        z_s[:, 0:ycols] += lax.dot_general(xb_s[...], w2t_ref[q], _NT,
                                           preferred_element_type=F32)

        def scatter(r, _, q=q):
            rws = pl.ds(pl.multiple_of(r * strip, strip), strip)
            for j in range(ycols // LANES):
                yq = z_s[rws, j * LANES:(j + 1) * LANES]
                for i in range(per_tile):
                    shift = (S5_PAIR_LANES * (q - i)) % LANES
                    yt = pltpu.roll(yq, shift, axis=1) if shift else yq
                    lanes_q = slice(q * S5_PAIR_LANES, (q + 1) * S5_PAIR_LANES)
                    y_ref[per_tile * j + i, rws, lanes_q] = yt[:, lanes_q].astype(y_ref.dtype)
            return 0

        lax.fori_loop(0, rows // strip, scatter, 0)


def _s5_core(u_t, w1, w2t, pw, aa, nseq, nseg):
    t_n, rows, d = u_t.shape
    width = w1.shape[1]
    pps = S5_PAIRS_PER_SLAB
    slab_rows = S5_SEG_BASE + nseq * S5_SEG_PITCH
    return pl.pallas_call(
        functools.partial(_s5_core_kernel, nseq=nseq, nseg=nseg, strip=128),
        grid=(d // LANES,),
        in_specs=[pl.BlockSpec((t_n, rows, LANES), lambda o: (0, 0, o)),
                  pl.BlockSpec((pps, width, w1.shape[2]), lambda o: (o, 0, 0)),
                  pl.BlockSpec((pps, width, w2t.shape[2]), lambda o: (o, 0, 0)),
                  pl.BlockSpec((pps, S5_SEG_CHUNKS, pw.shape[2]), lambda o: (o, 0, 0)),
                  pl.BlockSpec((pps, 8, aa.shape[2]), lambda o: (o, 0, 0))],
        out_specs=pl.BlockSpec((t_n, rows, LANES), lambda o: (0, 0, o)),
        out_shape=jax.ShapeDtypeStruct((t_n, rows, d), BF16),
        scratch_shapes=[pltpu.VMEM((rows, w1.shape[2]), F32),
                        pltpu.VMEM((4, slab_rows, LANES), F32),
                        pltpu.VMEM((4, slab_rows, LANES), F32),
                        pltpu.VMEM((rows, width), BF16),
                        pltpu.VMEM((rows, 4 * LANES), BF16)],
        compiler_params=_cparams("parallel"),
        name="s5_core",
    )(u_t, w1, w2t, pw, aa)


def _s5_out_kernel(y_ref, h_ref, wglu_ref, wout_ref, g_ref, o_ref):
    g = _gelu_tanh(y_ref[0].astype(F32))
    z = jnp.dot(g.astype(BF16), wglu_ref[...], preferred_element_type=F32)
    g2 = g * _sigmoid(z)
    mix = jnp.dot(g2.astype(BF16), wout_ref[...], preferred_element_type=F32)
    o_ref[...] = h_ref[...] + _rms(mix, g_ref[...])


def _s5_out(y_t, h, w_glu, w_out, gain, rc):
    n, d = h.shape
    chunks = n // S5_CHUNK
    out = pl.pallas_call(
        _s5_out_kernel,
        grid=(chunks // rc, S5_CHUNK),
        in_specs=[pl.BlockSpec((1, rc, d), lambda i, t: (t, i, 0)),
                  pl.BlockSpec((rc, d), lambda i, t: (i, t)),
                  pl.BlockSpec((d, d), lambda i, t: (0, 0)),
                  pl.BlockSpec((d, d), lambda i, t: (0, 0)),
                  pl.BlockSpec((1, d), lambda i, t: (0, 0))],
        out_specs=pl.BlockSpec((rc, d), lambda i, t: (i, t)),
        out_shape=jax.ShapeDtypeStruct((chunks, S5_CHUNK * d), F32),
        compiler_params=_cparams("parallel", "parallel"),
        name="s5_out",
    )(y_t, h.reshape(chunks, S5_CHUNK * d), w_glu, w_out, gain.reshape(1, d))
    return out.reshape(n, d)


def _swiglu(x, wg, wu, wd, chunk):
    n_chunks = wg.shape[-1] // chunk

    def gate_up(c):
        cols = slice(c * chunk, (c + 1) * chunk)
        return (jnp.dot(x, wg[:, cols], preferred_element_type=F32),
                jnp.dot(x, wu[:, cols], preferred_element_type=F32))

    y = None
    ahead = 2
    queue = [gate_up(c) for c in range(min(ahead, n_chunks))]
    for c in range(n_chunks):
        a, u = queue.pop(0)
        if c + ahead < n_chunks:
            queue.append(gate_up(c + ahead))
        act = (a * _sigmoid(a) * u).astype(BF16)
        part = jnp.dot(act, wd[c * chunk:(c + 1) * chunk, :], preferred_element_type=F32)
        y = part if y is None else y + part
    return y


def _ffn_kernel(h_ref, gpre_ref, gpost_ref, wg_ref, wu_ref, wd_ref, o_ref, hn_s, acc_s,
                *, chunk):
    f = pl.program_id(1)

    @pl.when(f == 0)
    def _():
        hn_s[...] = _rms(h_ref[...], gpre_ref[...]).astype(BF16)
        acc_s[...] = jnp.zeros_like(acc_s)

    acc_s[...] += _swiglu(hn_s[...], wg_ref, wu_ref, wd_ref, chunk)

    @pl.when(f == pl.num_programs(1) - 1)
    def _():
        o_ref[...] = h_ref[...] + _rms(acc_s[...], gpost_ref[...])


def _ffn(h, gain_pre, gain_post, w_gate, w_up, w_down, tm, tf):
    n, d = h.shape
    d_ff = w_gate.shape[1]
    mode = pl.Buffered(1) if tf == d_ff else pl.Buffered(2)
    return pl.pallas_call(
        functools.partial(_ffn_kernel, chunk=256),
        grid=(n // tm, d_ff // tf),
        in_specs=[pl.BlockSpec((tm, d), lambda i, f: (i, 0)),
                  pl.BlockSpec((1, d), lambda i, f: (0, 0)),
                  pl.BlockSpec((1, d), lambda i, f: (0, 0)),
                  pl.BlockSpec((d, tf), lambda i, f: (0, f), pipeline_mode=mode),
                  pl.BlockSpec((d, tf), lambda i, f: (0, f), pipeline_mode=mode),
                  pl.BlockSpec((tf, d), lambda i, f: (f, 0), pipeline_mode=mode)],
        out_specs=pl.BlockSpec((tm, d), lambda i, f: (i, 0)),
        out_shape=jax.ShapeDtypeStruct((n, d), F32),
        scratch_shapes=[pltpu.VMEM((tm, d), BF16), pltpu.VMEM((tm, d), F32)],
        compiler_params=_cparams("parallel", "arbitrary"),
        name="ffn",
    )(h, gain_pre.reshape(1, d), gain_post.reshape(1, d), w_gate, w_up, w_down)


def _moe_kernel(h_ref, gpre_ref, gpost_ref, wr_ref, wg_ref, wu_ref, wd_ref, o_ref,
                hn_s, gate_s, pos_s, post_s, xc_s, yc_s, xg_s, col_s, cnt_s,
                *, n_exp, rt, strip):
    e = pl.program_id(1)
    f = pl.program_id(2)
    tb, d = hn_s.shape
    n_f = pl.num_programs(2)
    n_strips = tb // strip

    def put_counts(r, total):
        lane1 = lax.broadcasted_iota(jnp.int32, total.shape, 1)
        for x in range(n_exp):
            cnt_s[r * n_exp + x] = jnp.sum(jnp.where(lane1 == x, total, 0.0)).astype(jnp.int32)

    @pl.when((e == 0) & (f == 0))
    def _():
        before = (lax.broadcasted_iota(jnp.int32, (strip, strip), 1)
                  < lax.broadcasted_iota(jnp.int32, (strip, strip), 0)).astype(BF16)

        def route(r, total):
            put_counts(r, total)
            rows = pl.ds(pl.multiple_of(r * strip, strip), strip)
            xn = _rms(h_ref[rows, :], gpre_ref[...])
            hn_s[rows, :] = xn.astype(BF16)
            o_ref[rows, :] = jnp.zeros((strip, d), F32)
            x_hi = xn.astype(BF16)
            x_lo = (xn - x_hi.astype(F32)).astype(BF16)
            logits = (jnp.dot(x_hi, wr_ref[0], preferred_element_type=F32)
                      + jnp.dot(x_lo, wr_ref[0], preferred_element_type=F32)
                      + jnp.dot(x_hi, wr_ref[1], preferred_element_type=F32))
            lane = lax.broadcasted_iota(jnp.int32, logits.shape, 1)
            neg = jnp.float32(-jnp.inf)
            logits = jnp.where(lane < n_exp, logits, neg)
            m1 = jnp.max(logits, axis=-1, keepdims=True)
            i1 = jnp.min(jnp.where(logits == m1, lane, LANES), axis=-1, keepdims=True)
            rest = jnp.where(lane == i1, neg, logits)
            m2 = jnp.max(rest, axis=-1, keepdims=True)
            i2 = jnp.min(jnp.where(rest == m2, lane, LANES), axis=-1, keepdims=True)
            e2 = jnp.exp(m2 - m1)
            gate_s[rows, :] = (jnp.where(lane == i1, 1.0 / (1.0 + e2), 0.0)
                               + jnp.where(lane == i2, e2 / (1.0 + e2), 0.0))
            sel = ((lane == i1) | (lane == i2)).astype(F32)
            earlier = jnp.dot(before, sel.astype(BF16), preferred_element_type=F32) + total
            pos_s[rows, :] = jnp.where(sel > 0, earlier, -1.0)
            return total + jnp.sum(sel, axis=0, keepdims=True)

        total = lax.fori_loop(0, n_strips, route, jnp.zeros((1, LANES), F32))
        put_counts(n_strips, total)

        for r in range(tb // strip):
            post_s[:, r * strip:(r + 1) * strip] = pos_s[r * strip:(r + 1) * strip, :].T

    @pl.when(f == n_f - 1)
    def _():
        lane = lax.broadcasted_iota(jnp.int32, (tb, LANES), 1)
        col_s[0] = jnp.sum(jnp.where(lane == e, pos_s[...], 0.0), axis=-1, keepdims=True)
        col_s[1] = jnp.sum(jnp.where(lane == e, gate_s[...], 0.0), axis=-1, keepdims=True)

    n_tiles = (cnt_s[n_strips * n_exp + e] + rt - 1) // rt

    def tile(i, _):
        r0 = pl.multiple_of(i * rt, 16)

        def holds(r):
            return ((cnt_s[r * n_exp + e] < r0 + rt) & (cnt_s[(r + 1) * n_exp + e] > r0))

        @pl.when(f == 0)
        def _():
            xg_s[...] = jnp.zeros_like(xg_s)
            slot = (lax.broadcasted_iota(jnp.int32, (rt, strip), 0) + r0).astype(F32)
            for r in range(n_strips):
                @pl.when(holds(r))
                def _():
                    cols = slice(r * strip, (r + 1) * strip)
                    pick = (post_s[pl.ds(e, 1), cols] == slot).astype(BF16)
                    xg_s[...] += jnp.dot(pick, hn_s[cols, :], preferred_element_type=F32)
            xc_s[pl.ds(r0, rt), :] = xg_s[...].astype(BF16)

        xc = xc_s[pl.ds(r0, rt), :]
        a = jnp.dot(xc, wg_ref[0], preferred_element_type=F32)
        u = jnp.dot(xc, wu_ref[0], preferred_element_type=F32)
        act = (a * _sigmoid(a) * u).astype(BF16)
        y = jnp.dot(act, wd_ref[0], preferred_element_type=F32)
        prev = jnp.where(f == 0, 0.0, yc_s[pl.ds(r0, rt), :])
        yc_s[pl.ds(r0, rt), :] = prev + y

        @pl.when(f == n_f - 1)
        def _():
            yc = yc_s[pl.ds(r0, rt), :].astype(BF16)
            slot = (lax.broadcasted_iota(jnp.int32, (strip, rt), 1) + r0).astype(F32)
            for r in range(n_strips):
                @pl.when(holds(r))
                def _():
                    rows = slice(r * strip, (r + 1) * strip)
                    put = (col_s[0, rows, :] == slot).astype(BF16)
                    back = jnp.dot(put, yc, preferred_element_type=F32)
                    o_ref[rows, :] += col_s[1, rows, :] * back
        return 0

    lax.fori_loop(0, n_tiles, tile, 0)

    @pl.when((e == n_exp - 1) & (f == n_f - 1))
    def _():
        def finish(r, _):
            rows = pl.ds(pl.multiple_of(r * strip, strip), strip)
            o_ref[rows, :] = h_ref[rows, :] + _rms(o_ref[rows, :], gpost_ref[...])
            return 0

        lax.fori_loop(0, tb // strip, finish, 0)


def _moe(h, gain_pre, gain_post, w_router, w_gate, w_up, w_down, tb, tf, rt):
    n, d = h.shape
    n_exp, _, d_ff = w_gate.shape
    w_r = jnp.pad(w_router, ((0, 0), (0, LANES - n_exp)))
    w_hi = w_r.astype(BF16)
    w_r = jnp.stack([w_hi, (w_r - w_hi.astype(F32)).astype(BF16)])
    once = pl.Buffered(1)
    strip = 256
    cap = pl.cdiv(tb, rt) * rt
    return pl.pallas_call(
        functools.partial(_moe_kernel, n_exp=n_exp, rt=rt, strip=strip),
        grid=(n // tb, n_exp, d_ff // tf),
        in_specs=[pl.BlockSpec((tb, d), lambda i, e, f: (i, 0), pipeline_mode=once),
                  pl.BlockSpec((1, d), lambda i, e, f: (0, 0)),
                  pl.BlockSpec((1, d), lambda i, e, f: (0, 0)),
                  pl.BlockSpec((2, d, LANES), lambda i, e, f: (0, 0, 0)),
                  pl.BlockSpec((1, d, tf), lambda i, e, f: (e, 0, f)),
                  pl.BlockSpec((1, d, tf), lambda i, e, f: (e, 0, f)),
                  pl.BlockSpec((1, tf, d), lambda i, e, f: (e, f, 0))],
        out_specs=pl.BlockSpec((tb, d), lambda i, e, f: (i, 0), pipeline_mode=once),
        out_shape=jax.ShapeDtypeStruct((n, d), F32),
        scratch_shapes=[pltpu.VMEM((tb, d), BF16),
                        pltpu.VMEM((tb, LANES), F32),
                        pltpu.VMEM((tb, LANES), F32),
                        pltpu.VMEM((LANES, tb), F32),
                        pltpu.VMEM((cap, d), BF16),
                        pltpu.VMEM((cap, d), F32),
                        pltpu.VMEM((rt, d), F32),
                        pltpu.VMEM((2, tb, 1), F32),
                        pltpu.SMEM(((tb // strip + 1) * n_exp,), jnp.int32)],
        compiler_params=pltpu.CompilerParams(
            dimension_semantics=("parallel", "arbitrary", "arbitrary"),
            vmem_limit_bytes=MOE_VMEM_LIMIT_BYTES),
        name="moe_ffn",
    )(h, gain_pre.reshape(1, d), gain_post.reshape(1, d), w_r, w_gate, w_up, w_down)


def _qkv_kernel(h_ref, g_ref, w_ref, qg_ref, kg_ref, cs_ref, sn_ref, bd_ref,
                q_ref, k_ref, v_ref):
    xn = _rms(h_ref[...], g_ref[...]).astype(BF16)
    qkv = jnp.dot(xn, w_ref[...], preferred_element_type=F32)
    cs = cs_ref[...]
    sn = sn_ref[...]
    bd = bd_ref[...]
    lane = lax.broadcasted_iota(jnp.int32, cs.shape, 1)
    first_half = (lane % HEAD_DIM) < (HEAD_DIM // 2)
    scale = math.log2(math.e) / math.sqrt(HEAD_DIM)

    def norm_rope(x, gain):
        ms = jnp.dot(x * x, bd, preferred_element_type=F32)
        y = x * lax.rsqrt(ms + NORM_EPS) * gain
        partner = jnp.where(first_half,
                            pltpu.roll(y, LANES - HEAD_DIM // 2, axis=1),
                            pltpu.roll(y, HEAD_DIM // 2, axis=1))
        return y * cs + partner * sn

    n_q_tiles = N_HEADS * HEAD_DIM // LANES
    for t in range(n_q_tiles):
        y = norm_rope(qkv[:, t * LANES:(t + 1) * LANES], qg_ref[...]) * scale
        q_ref[0, 2 * t] = y[:, 0:HEAD_DIM].astype(BF16)
        q_ref[0, 2 * t + 1] = y[:, HEAD_DIM:LANES].astype(BF16)
    k0 = N_HEADS * HEAD_DIM
    for t in range(N_KV_HEADS * HEAD_DIM // LANES):
        y = norm_rope(qkv[:, k0 + t * LANES:k0 + (t + 1) * LANES], kg_ref[...])
        k_ref[0, 2 * t] = y[:, 0:HEAD_DIM].astype(BF16)
        k_ref[0, 2 * t + 1] = y[:, HEAD_DIM:LANES].astype(BF16)
    v0 = (N_HEADS + N_KV_HEADS) * HEAD_DIM
    ones = jnp.ones((qkv.shape[0], LANES - HEAD_DIM), BF16)
    for j in range(N_KV_HEADS):
        vj = qkv[:, v0 + j * HEAD_DIM:v0 + (j + 1) * HEAD_DIM].astype(BF16)
        v_ref[0, j] = jnp.concatenate([vj, ones], axis=-1)


def _rope_tables(seq):
    axis_dim = HEAD_DIM // 2
    freqs = ROPE_THETA ** (-jnp.arange(0, axis_dim, 2, dtype=F32) / axis_dim)
    rows = seq // GRID_W
    row_ang = jnp.arange(rows, dtype=F32)[:, None] * freqs
    col_ang = jnp.arange(GRID_W, dtype=F32)[:, None] * freqs
    ang = jnp.concatenate([
        jnp.broadcast_to(row_ang[:, None, :], (rows, GRID_W, freqs.shape[0])),
        jnp.broadcast_to(col_ang[None, :, :], (rows, GRID_W, freqs.shape[0]))], axis=-1)
    ang = ang.reshape(seq, HEAD_DIM // 2)
    cos, sin = jnp.cos(ang), jnp.sin(ang)
    cs = jnp.tile(jnp.concatenate([cos, cos], axis=-1), (1, LANES // HEAD_DIM))
    sn = jnp.tile(jnp.concatenate([-sin, sin], axis=-1), (1, LANES // HEAD_DIM))
    return cs, sn


def _qkv(h, gain, w_qkv, q_gain, k_gain, bsz, seq, tm):
    n, d = h.shape
    width = w_qkv.shape[1]
    perm = jnp.concatenate([jnp.arange(0, HEAD_DIM, 2), jnp.arange(1, HEAD_DIM, 2)])
    n_rot = N_HEADS + N_KV_HEADS
    cols = (jnp.arange(n_rot)[:, None] * HEAD_DIM + perm[None, :]).reshape(-1)
    cols = jnp.concatenate([cols, jnp.arange(n_rot * HEAD_DIM, width)])
    w = w_qkv[:, cols].astype(BF16)
    qg = jnp.tile(q_gain[perm], LANES // HEAD_DIM).reshape(1, LANES)
    kg = jnp.tile(k_gain[perm], LANES // HEAD_DIM).reshape(1, LANES)
    cs, sn = _rope_tables(seq)
    blk = jnp.arange(LANES) // HEAD_DIM
    bd = (blk[:, None] == blk[None, :]).astype(F32) / HEAD_DIM
    per_seq = seq // tm
    return pl.pallas_call(
        _qkv_kernel,
        grid=(n // tm,),
        in_specs=[pl.BlockSpec((tm, d), lambda i: (i, 0)),
                  pl.BlockSpec((1, d), lambda i: (0, 0)),
                  pl.BlockSpec((d, width), lambda i: (0, 0)),
                  pl.BlockSpec((1, LANES), lambda i: (0, 0)),
                  pl.BlockSpec((1, LANES), lambda i: (0, 0)),
                  pl.BlockSpec((tm, LANES), lambda i: (i % per_seq, 0)),
                  pl.BlockSpec((tm, LANES), lambda i: (i % per_seq, 0)),
                  pl.BlockSpec((LANES, LANES), lambda i: (0, 0))],
        out_specs=[pl.BlockSpec((1, N_HEADS, tm, HEAD_DIM),
                                lambda i: (i // per_seq, 0, i % per_seq, 0)),
                   pl.BlockSpec((1, N_KV_HEADS, tm, HEAD_DIM),
                                lambda i: (i // per_seq, 0, i % per_seq, 0)),
                   pl.BlockSpec((1, N_KV_HEADS, tm, LANES),
                                lambda i: (i // per_seq, 0, i % per_seq, 0))],
        out_shape=[jax.ShapeDtypeStruct((bsz, N_HEADS, seq, HEAD_DIM), BF16),
                   jax.ShapeDtypeStruct((bsz, N_KV_HEADS, seq, HEAD_DIM), BF16),
                   jax.ShapeDtypeStruct((bsz, N_KV_HEADS, seq, LANES), BF16)],
        compiler_params=_cparams("parallel"),
        name="qkv_rope",
    )(h, gain.reshape(1, d), w, qg, kg, cs, sn, bd)


def _attn_kernel(q_ref, k_ref, v_ref, o_ref, m_s, acc_s, s_buf, p_buf, a_buf,
                 *, tq, tk, rc, unroll):
    seq = k_ref.shape[2]
    chunks_per_head = tq // rc
    n_chunks = Q_PER_KV * chunks_per_head
    n_steps = (seq // tk) * n_chunks

    m_s[...] = jnp.full_like(m_s, -jnp.inf)
    acc_s[...] = jnp.zeros_like(acc_s)

    def where(n):
        c = n % n_chunks
        return (pl.multiple_of((n // n_chunks) * tk, tk), c // chunks_per_head,
                pl.multiple_of((c % chunks_per_head) * rc, rc))

    def scores(n):
        k0, g, r0 = where(n)
        q = q_ref[0, g, pl.ds(r0, rc), :]
        s_buf[...] = lax.dot_general(q, k_ref[0, 0, pl.ds(k0, tk), :], _NT,
                                     preferred_element_type=F32)

    def softmax(n):
        _, g, r0 = where(n)
        s = s_buf[...]
        m_prev = m_s[g, pl.ds(r0, rc), :]
        m_new = jnp.maximum(m_prev, jnp.max(s, axis=-1, keepdims=True))
        a_buf[...] = jnp.exp2(m_prev - m_new)
        for t in range(tk // LANES):
            p_buf[:, t * LANES:(t + 1) * LANES] = jnp.exp2(
                s[:, t * LANES:(t + 1) * LANES] - m_new).astype(BF16)
        m_s[g, pl.ds(r0, rc), :] = m_new

    def values(n):
        k0, g, r0 = where(n)
        pv = jnp.dot(p_buf[...], v_ref[0, 0, pl.ds(k0, tk), :],
                     preferred_element_type=F32)
        acc_s[g, pl.ds(r0, rc), :] = a_buf[...] * acc_s[g, pl.ds(r0, rc), :] + pv

    scores(0)
    softmax(0)
    scores(1)

    def body(n, _):
        values(n)
        softmax(n + 1)
        scores(n + 2)
        return 0

    lax.fori_loop(0, n_steps - 2, body, 0, unroll=unroll)
    values(n_steps - 2)
    softmax(n_steps - 1)
    values(n_steps - 1)

    outs = []
    for g in range(Q_PER_KV):
        acc = acc_s[g]
        o = acc / pltpu.roll(acc, HEAD_DIM, axis=1)
        outs.append(o[:, 0:HEAD_DIM])
    o_ref[...] = jnp.concatenate(outs, axis=-1).astype(o_ref.dtype)


def _attention(q, k, v, tq, tk, rc):
    bsz, _, seq, _ = q.shape
    n_q = seq // tq
    n_steps = (seq // tk) * Q_PER_KV * (tq // rc)
    unroll = min(16, n_steps - 2)
    return pl.pallas_call(
        functools.partial(_attn_kernel, tq=tq, tk=tk, rc=rc, unroll=unroll),
        grid=(bsz, N_KV_HEADS, n_q),
        in_specs=[pl.BlockSpec((1, Q_PER_KV, tq, HEAD_DIM), lambda b, j, i: (b, j, i, 0)),
                  pl.BlockSpec((1, 1, seq, HEAD_DIM), lambda b, j, i: (b, j, 0, 0)),
                  pl.BlockSpec((1, 1, seq, LANES), lambda b, j, i: (b, j, 0, 0))],
        out_specs=pl.BlockSpec((tq, Q_PER_KV * HEAD_DIM), lambda b, j, i: (b * n_q + i, j)),
        out_shape=jax.ShapeDtypeStruct((bsz * seq, N_HEADS * HEAD_DIM), BF16),
        scratch_shapes=[pltpu.VMEM((Q_PER_KV, tq, LANES), F32),
                        pltpu.VMEM((Q_PER_KV, tq, LANES), F32),
                        pltpu.VMEM((rc, tk), F32),
                        pltpu.VMEM((rc, tk), BF16),
                        pltpu.VMEM((rc, LANES), F32)],
        compiler_params=_cparams("parallel", "parallel", "parallel"),
        name="flash_attn",
    )(q, k, v)


def _proj_res_kernel(x_ref, h_ref, w_ref, g_ref, o_ref):
    mix = jnp.dot(x_ref[...], w_ref[...], preferred_element_type=F32)
    o_ref[...] = h_ref[...] + _rms(mix, g_ref[...])


def _proj_res(x, h, w, gain, tm):
    n, d = h.shape
    k = x.shape[1]
    return pl.pallas_call(
        _proj_res_kernel,
        grid=(n // tm,),
        in_specs=[pl.BlockSpec((tm, k), lambda i: (i, 0)),
                  pl.BlockSpec((tm, d), lambda i: (i, 0)),
                  pl.BlockSpec((k, d), lambda i: (0, 0)),
                  pl.BlockSpec((1, d), lambda i: (0, 0))],
        out_specs=pl.BlockSpec((tm, d), lambda i: (i, 0)),
        out_shape=jax.ShapeDtypeStruct((n, d), F32),
        compiler_params=_cparams("parallel"),
        name="proj_res",
    )(x, h, w, gain.reshape(1, d))


def _s5_layer(h, bsz, seq, gains, w_in, lam_re, lam_im, log_step, b_re, b_im, c_re, c_im,
              d_skip, w_glu, w_out):
    seg_tokens = S5_CHUNK * S5_SEG_CHUNKS
    nseg = seq // seg_tokens
    nseq = bsz * nseg
    rc = min(512, h.shape[0] // S5_CHUNK)
    u_t = _norm_matmul(h, gains[0], w_in.astype(BF16), rc=rc)
    w1, w2t, pw_p, aa_p = _s5_prep(lam_re, lam_im, log_step, b_re, b_im, c_re, c_im, d_skip)
    y_t = _s5_core(u_t, w1, w2t, pw_p, aa_p, nseq, nseg)
    return _s5_out(y_t, h, w_glu.astype(BF16), w_out.astype(BF16), gains[1], rc=rc)


def _attn_layer(h, bsz, seq, gains, w_qkv, q_gain, k_gain, w_out):
    q, k, v = _qkv(h, gains[0], w_qkv, q_gain, k_gain, bsz, seq, tm=512)
    o = _attention(q, k, v, tq=min(4096, seq), tk=512, rc=512)
    return _proj_res(o, h, w_out.astype(BF16), gains[1], tm=512)


def kernel(x, norm_gains, ssm_w_in, ssm_lambda_re, ssm_lambda_im, ssm_log_step, ssm_b_re,
           ssm_b_im, ssm_c_re, ssm_c_im, ssm_d, ssm_w_glu, ssm_w_out, ffn_w_gate, ffn_w_up,
           ffn_w_down, attn_w_qkv, attn_q_gain, attn_k_gain, attn_w_out, moe_w_router,
           moe_w_gate, moe_w_up, moe_w_down):
    bsz, seq, d = x.shape
    depth = norm_gains.shape[0]
    h = x.reshape(bsz * seq, d)
    for i in range(depth):
        j = i // 2
        g = norm_gains[i]
        if i % 2 == 0:
            h = _s5_layer(h, bsz, seq, g, ssm_w_in[j], ssm_lambda_re[j], ssm_lambda_im[j],
                          ssm_log_step[j], ssm_b_re[j], ssm_b_im[j], ssm_c_re[j], ssm_c_im[j],
                          ssm_d[j], ssm_w_glu[j], ssm_w_out[j])
            h = _ffn(h, g[2], g[3], ffn_w_gate[j].astype(BF16), ffn_w_up[j].astype(BF16),
                     ffn_w_down[j].astype(BF16), tm=1024, tf=ffn_w_gate.shape[-1])
        else:
            h = _attn_layer(h, bsz, seq, g, attn_w_qkv[j], attn_q_gain[j], attn_k_gain[j],
                            attn_w_out[j])
            h = _moe(h, g[2], g[3], moe_w_router[j], moe_w_gate[j].astype(BF16),
                     moe_w_up[j].astype(BF16), moe_w_down[j].astype(BF16),
                     tb=2048, tf=896, rt=256)
    return h.reshape(bsz, seq, d)
```

```python
import functools
import math

import jax
import jax.numpy as jnp
from jax import lax
from jax.experimental import pallas as pl
from jax.experimental.pallas import tpu as pltpu

F32 = jnp.float32
BF16 = jnp.bfloat16
NORM_EPS = 1e-6
ROPE_THETA = 10000.0
GRID_W = 64
N_HEADS = 16
N_KV_HEADS = 4
HEAD_DIM = 64
Q_PER_KV = N_HEADS // N_KV_HEADS
SSM_GROUP = 16
SSM_STATE = 64
S5_CHUNK = 16
S5_SEG_CHUNKS = 64
TOP_K = 2
LANES = 128
VMEM_LIMIT_BYTES = 56 * 1024 * 1024
MOE_VMEM_LIMIT_BYTES = 60 * 1024 * 1024

_NT = (((1,), (1,)), ((), ()))


def _cparams(*sem):
    return pltpu.CompilerParams(dimension_semantics=sem, vmem_limit_bytes=VMEM_LIMIT_BYTES)


def _rms(x, gain):
    return x * lax.rsqrt(jnp.mean(x * x, axis=-1, keepdims=True) + NORM_EPS) * gain


def _sigmoid(x):
    return 1.0 / (1.0 + jnp.exp(-x))


def _gelu_tanh(x):
    return x * (0.5 * (1.0 + jnp.tanh(math.sqrt(2.0 / math.pi) * (x + 0.044715 * (x * x * x)))))


def _norm_matmul_kernel(x_ref, g_ref, w_ref, o_ref):
    xn = _rms(x_ref[...], g_ref[...]).astype(BF16)
    o_ref[0] = jnp.dot(xn, w_ref[...], preferred_element_type=F32).astype(o_ref.dtype)


def _norm_matmul(x, gain, w, rc):
    n, d = x.shape
    m = w.shape[1]
    chunks = n // S5_CHUNK
    return pl.pallas_call(
        _norm_matmul_kernel,
        grid=(chunks // rc, S5_CHUNK),
        in_specs=[pl.BlockSpec((rc, d), lambda i, t: (i, t)),
                  pl.BlockSpec((1, d), lambda i, t: (0, 0)),
                  pl.BlockSpec((d, m), lambda i, t: (0, 0))],
        out_specs=pl.BlockSpec((1, rc, m), lambda i, t: (t, i, 0)),
        out_shape=jax.ShapeDtypeStruct((S5_CHUNK, chunks, m), BF16),
        compiler_params=_cparams("parallel", "parallel"),
        name="norm_matmul",
    )(x.reshape(chunks, S5_CHUNK * d), gain.reshape(1, d), w)


def _s5_prep_kernel(lr_ref, li_ref, ls_ref, bt_re_ref, bt_im_ref, c_re_ref, c_im_ref, dt_ref,
                    w1_ref, w2t_ref, pw_ref, aa_ref):
    t_n, s_n, p_n = S5_CHUNK, SSM_GROUP, SSM_STATE
    pair = 2 * s_n
    rows = t_n * pair
    hi = lax.Precision.HIGHEST

    rr = lax.broadcasted_iota(jnp.int32, (rows, rows), 0)
    cc = lax.broadcasted_iota(jnp.int32, (rows, rows), 1)
    same_group = ((rr // s_n) % 2) == ((cc // s_n) % 2)
    tvec = lax.broadcasted_iota(jnp.int32, (t_n, 1), 0).astype(F32)
    jvec = lax.broadcasted_iota(jnp.int32, (S5_SEG_CHUNKS, 1), 0).astype(F32)
    row8 = lax.broadcasted_iota(jnp.int32, (8, LANES), 0)
    own_lanes = ((lax.broadcasted_iota(jnp.int32, (rows, LANES), 1) // p_n)
                 == ((lax.broadcasted_iota(jnp.int32, (rows, LANES), 0) // s_n) % 2))

    def lanes_by_group(tab0, tab1):
        lane = lax.broadcasted_iota(jnp.int32, tab0.shape, 1)
        return jnp.where(lane < p_n, tab0, tab1)

    m_total = None
    for d in range(2):
        lsr, lsi, q_re, q_im = [], [], [], []
        for g in range(2):
            lr = lr_ref[0, d, g:g + 1, :]
            li = li_ref[0, d, g:g + 1, :]
            step = jnp.exp(ls_ref[0, d, g:g + 1, :])
            lsr.append(lr * step)
            lsi.append(li * step)
            mag = jnp.exp(lsr[g])
            a_re, a_im = mag * jnp.cos(lsi[g]), mag * jnp.sin(lsi[g])
            nr, ni = a_re - 1.0, a_im
            den = lr * lr + li * li
            q_re.append((nr * lr + ni * li) / den)
            q_im.append((ni * lr - nr * li) / den)

        def cpow(k, g):
            mag = jnp.exp(lsr[g] * k)
            ang = lsi[g] * k
            return mag * jnp.cos(ang), mag * jnp.sin(ang)

        def table(k):
            t0, t1 = cpow(k, 0), cpow(k, 1)
            return tuple(
                jnp.concatenate([jnp.broadcast_to(tg[part][t:t + 1, :], (s_n, LANES))
                                 for t in range(t_n) for tg in (t0, t1)], axis=0)
                for part in range(2))

        br, bi = bt_re_ref[0, d], bt_im_ref[0, d]
        qr = jnp.concatenate([jnp.broadcast_to(q_re[g], (s_n, LANES)) for g in range(2)], axis=0)
        qi = jnp.concatenate([jnp.broadcast_to(q_im[g], (s_n, LANES)) for g in range(2)], axis=0)
        bb_re = qr * br - qi * bi
        bb_im = qr * bi + qi * br
        cr, ci = c_re_ref[0, d], c_im_ref[0, d]

        def outer(x_re, x_im, k):
            pe_re, pe_im = table(k)
            xe_re = jnp.concatenate([x_re] * t_n, axis=0)
            xe_im = jnp.concatenate([x_im] * t_n, axis=0)
            return xe_re * pe_re - xe_im * pe_im, xe_re * pe_im + xe_im * pe_re

        if d == 0:
            l_re, l_im = outer(bb_re, bb_im, -tvec)
            rt_re, rt_im = outer(cr, ci, tvec)
            ws_re, ws_im = outer(bb_re, bb_im, (t_n - 1.0) - tvec)
            wy_re, wy_im = outer(cr, ci, tvec + 1.0)
            mask = (rr // pair) <= (cc // pair)
            kseg = t_n * jvec
        else:
            l_re, l_im = outer(bb_re, bb_im, tvec)
            rt_re, rt_im = outer(cr, ci, -tvec)
            ws_re, ws_im = l_re, l_im
            wy_re, wy_im = outer(cr, ci, t_n - tvec)
            mask = (rr // pair) >= (cc // pair)
            kseg = t_n * ((S5_SEG_CHUNKS - 1.0) - jvec)
        kern = (lax.dot_general(l_re[:, 0:p_n], rt_re[:, 0:p_n], _NT, precision=hi,
                                preferred_element_type=F32)
                - lax.dot_general(l_im[:, 0:p_n], rt_im[:, 0:p_n], _NT, precision=hi,
                                  preferred_element_type=F32))
        kern = jnp.where(mask & same_group, kern, 0.0)
        m_total = kern if m_total is None else m_total + kern

        ycols = rows
        for k2, val in ((2 * d, ws_re), (2 * d + 1, ws_im)):
            w1_ref[0, :, ycols + k2 * LANES:ycols + (k2 + 1) * LANES] = jnp.where(
                own_lanes, val, 0.0).astype(w1_ref.dtype)
        for k2, val in ((2 * d, wy_re), (2 * d + 1, -wy_im)):
            w2t_ref[0, :, k2 * LANES:(k2 + 1) * LANES] = jnp.where(
                own_lanes, val, 0.0).astype(w2t_ref.dtype)
        p0, p1 = cpow(kseg, 0), cpow(kseg, 1)
        pw_ref[0, :, 2 * d * LANES:(2 * d + 1) * LANES] = lanes_by_group(p0[0], p1[0])
        pw_ref[0, :, (2 * d + 1) * LANES:(2 * d + 2) * LANES] = lanes_by_group(p0[1], p1[1])
        one = jnp.ones((1, 1), F32)
        c0, c1 = cpow(float(t_n) * one, 0), cpow(float(t_n) * one, 1)
        s0, s1 = (cpow(float(t_n * S5_SEG_CHUNKS) * one, 0),
                  cpow(float(t_n * S5_SEG_CHUNKS) * one, 1))
        for part in range(2):
            chunk_pow = lanes_by_group(c0[part], c1[part])
            seg_pow = lanes_by_group(s0[part], s1[part])
            aa_ref[0, :, (2 * d + part) * LANES:(2 * d + part + 1) * LANES] = jnp.where(
                row8 == 0, chunk_pow, jnp.where(row8 == 1, seg_pow, 0.0))

    w1_ref[0, :, 0:rows] = (m_total + jnp.where(rr == cc, dt_ref[0], 0.0)).astype(w1_ref.dtype)


def _s5_prep(lam_re, lam_im, log_step, b_re, b_im, c_re, c_im, d_skip):
    g_n = lam_re.shape[1]
    pairs = g_n // 2
    pair = 2 * SSM_GROUP
    rows = S5_CHUNK * pair

    def states(x):
        x = jnp.transpose(x.reshape(2, pairs, 2, SSM_STATE), (1, 0, 2, 3))
        return jnp.tile(x, (1, 1, 1, LANES // SSM_STATE))

    def per_channel(x):
        x = jnp.transpose(x.reshape(2, pairs, pair, SSM_STATE), (1, 0, 2, 3))
        return jnp.tile(x, (1, 1, 1, LANES // SSM_STATE))

    ls = jnp.transpose(log_step.reshape(2, pairs, 2), (1, 0, 2))[..., None]
    bt_re = per_channel(jnp.swapaxes(b_re, 2, 3))
    bt_im = per_channel(jnp.swapaxes(b_im, 2, 3))
    dt = jnp.tile(d_skip.reshape(pairs, 1, pair), (1, 1, S5_CHUNK))

    def spec(*blk):
        return pl.BlockSpec((1,) + blk, lambda g: (g,) + (0,) * len(blk))

    return pl.pallas_call(
        _s5_prep_kernel,
        grid=(pairs,),
        in_specs=[spec(2, 2, LANES), spec(2, 2, LANES), spec(2, 2, 1),
                  spec(2, pair, LANES), spec(2, pair, LANES),
                  spec(2, pair, LANES), spec(2, pair, LANES), spec(1, rows)],
        out_specs=[spec(rows, rows + 4 * LANES), spec(rows, 4 * LANES),
                   spec(S5_SEG_CHUNKS, 4 * LANES), spec(8, 4 * LANES)],
        out_shape=[jax.ShapeDtypeStruct((pairs, rows, rows + 4 * LANES), BF16),
                   jax.ShapeDtypeStruct((pairs, rows, 4 * LANES), BF16),
                   jax.ShapeDtypeStruct((pairs, S5_SEG_CHUNKS, 4 * LANES), F32),
                   jax.ShapeDtypeStruct((pairs, 8, 4 * LANES), F32)],
        compiler_params=_cparams("parallel"),
        name="s5_prep",
    )(states(lam_re), states(lam_im), ls, bt_re, bt_im, per_channel(c_re), per_channel(c_im), dt)


S5_PAIR_LANES = 2 * SSM_GROUP
S5_PAIRS_PER_SLAB = LANES // S5_PAIR_LANES
S5_SEG_PITCH = S5_SEG_CHUNKS + 8
S5_SEG_BASE = 8


def _s5_core_kernel(x_ref, w1_ref, w2t_ref, pw_ref, aa_ref, y_ref, z_s, zs_s, x_s, u_s, xb_s,
                    *, nseq, nseg, strip):
    n_chunks = S5_SEG_CHUNKS
    rows = n_chunks * nseq
    ycols = u_s.shape[1]
    per_tile = LANES // S5_PAIR_LANES
    lane_grp = lax.broadcasted_iota(jnp.int32, (strip, LANES), 1) // S5_PAIR_LANES
    zero = jnp.zeros((nseq, LANES), F32)

    def seg_rows(j):
        return pl.ds(S5_SEG_BASE + j, nseq, stride=S5_SEG_PITCH)

    for q in range(S5_PAIRS_PER_SLAB):
        def gather(r, _, q=q):
            rws = pl.ds(pl.multiple_of(r * strip, strip), strip)
            for j in range(ycols // LANES):
                acc = None
                for i in range(per_tile):
                    xt = x_ref[per_tile * j + i, rws, :].astype(F32)
                    shift = (S5_PAIR_LANES * (i - q)) % LANES
                    if shift:
                        xt = pltpu.roll(xt, shift, axis=1)
                    acc = xt if acc is None else jnp.where(lane_grp == i, xt, acc)
                u_s[rws, j * LANES:(j + 1) * LANES] = acc.astype(BF16)
            return 0

        lax.fori_loop(0, rows // strip, gather, 0)
        z_s[...] = jnp.dot(u_s[...], w1_ref[q], preferred_element_type=F32)

        for b in range(nseq):
            dst = slice(S5_SEG_BASE + b * S5_SEG_PITCH, S5_SEG_BASE + b * S5_SEG_PITCH + n_chunks)
            for k in range(4):
                zs_s[k, dst, :] = z_s[b * n_chunks:(b + 1) * n_chunks,
                                      ycols + k * LANES:ycols + (k + 1) * LANES]
        x_s[0, seg_rows(0), :] = zero
        x_s[1, seg_rows(0), :] = zero
        x_s[2, seg_rows(n_chunks - 1), :] = zero
        x_s[3, seg_rows(n_chunks - 1), :] = zero

        aa = aa_ref[q]
        a_fr, a_fi = aa[0:1, 0:LANES], aa[0:1, LANES:2 * LANES]
        a_br, a_bi = aa[0:1, 2 * LANES:3 * LANES], aa[0:1, 3 * LANES:4 * LANES]
        g_fr, g_fi = aa[1:2, 0:LANES], aa[1:2, LANES:2 * LANES]
        g_br, g_bi = aa[1:2, 2 * LANES:3 * LANES], aa[1:2, 3 * LANES:4 * LANES]

        def fwd(j, carry):
            xr, xi = carry
            nr = a_fr * xr - a_fi * xi + zs_s[0, seg_rows(j), :]
            ni = a_fr * xi + a_fi * xr + zs_s[1, seg_rows(j), :]
            x_s[0, seg_rows(j + 1), :] = nr
            x_s[1, seg_rows(j + 1), :] = ni
            return nr, ni

        def bwd(i, carry):
            xr, xi = carry
            j = n_chunks - 1 - i
            nr = a_br * xr - a_bi * xi + zs_s[2, seg_rows(j), :]
            ni = a_br * xi + a_bi * xr + zs_s[3, seg_rows(j), :]
            x_s[2, seg_rows(j - 1), :] = nr
            x_s[3, seg_rows(j - 1), :] = ni
            return nr, ni

        ef_r, ef_i = lax.fori_loop(0, n_chunks, fwd, (zero, zero))
        eb_r, eb_i = lax.fori_loop(0, n_chunks, bwd, (zero, zero))

        def seg_carries(e_r, e_i, g_r, g_i, reverse):
            out_r = [None] * nseq
            out_i = [None] * nseq
            for b in range(nseq // nseg):
                c_r = jnp.zeros((1, LANES), F32)
                c_i = jnp.zeros((1, LANES), F32)
                order = range(nseg - 1, -1, -1) if reverse else range(nseg)
                for s in order:
                    r = b * nseg + s
                    out_r[r], out_i[r] = c_r, c_i
                    n_r = g_r * c_r - g_i * c_i + e_r[r:r + 1, :]
                    n_i = g_r * c_i + g_i * c_r + e_i[r:r + 1, :]
                    c_r, c_i = n_r, n_i
            return out_r, out_i

        cf_r, cf_i = seg_carries(ef_r, ef_i, g_fr, g_fi, False)
        cb_r, cb_i = seg_carries(eb_r, eb_i, g_br, g_bi, True)

        p = pw_ref[q]
        p_fr, p_fi = p[:, 0:LANES], p[:, LANES:2 * LANES]
        p_br, p_bi = p[:, 2 * LANES:3 * LANES], p[:, 3 * LANES:4 * LANES]
        for b in range(nseq):
            src = slice(S5_SEG_BASE + b * S5_SEG_PITCH, S5_SEG_BASE + b * S5_SEG_PITCH + n_chunks)
            dst = slice(b * n_chunks, (b + 1) * n_chunks)
            xb_s[dst, 0:LANES] = (x_s[0, src, :] + (p_fr * cf_r[b] - p_fi * cf_i[b])).astype(BF16)
            xb_s[dst, LANES:2 * LANES] = (
                x_s[1, src, :] + (p_fr * cf_i[b] + p_fi * cf_r[b])).astype(BF16)
            xb_s[dst, 2 * LANES:3 * LANES] = (
                x_s[2, src, :] + (p_br * cb_r[b] - p_bi * cb_i[b])).astype(BF16)
            xb_s[dst, 3 * LANES:4 * LANES] = (
                x_s[3, src, :] + (p_br * cb_i[b] + p_bi * cb_r[b])).astype(BF16)

        z_s[:, 0:ycols] += lax.dot_general(xb_s[...], w2t_ref[q], _NT,
                                           preferred_element_type=F32)

        def scatter(r, _, q=q):
            rws = pl.ds(pl.multiple_of(r * strip, strip), strip)
            for j in range(ycols // LANES):
                yq = z_s[rws, j * LANES:(j + 1) * LANES]
                for i in range(per_tile):
                    shift = (S5_PAIR_LANES * (q - i)) % LANES
                    yt = pltpu.roll(yq, shift, axis=1) if shift else yq
                    lanes_q = slice(q * S5_PAIR_LANES, (q + 1) * S5_PAIR_LANES)
                    y_ref[per_tile * j + i, rws, lanes_q] = yt[:, lanes_q].astype(y_ref.dtype)
            return 0

        lax.fori_loop(0, rows // strip, scatter, 0)


def _s5_core(u_t, w1, w2t, pw, aa, nseq, nseg):
    t_n, rows, d = u_t.shape
    width = w1.shape[1]
    pps = S5_PAIRS_PER_SLAB
    slab_rows = S5_SEG_BASE + nseq * S5_SEG_PITCH
    return pl.pallas_call(
        functools.partial(_s5_core_kernel, nseq=nseq, nseg=nseg, strip=128),
        grid=(d // LANES,),
        in_specs=[pl.BlockSpec((t_n, rows, LANES), lambda o: (0, 0, o)),
                  pl.BlockSpec((pps, width, w1.shape[2]), lambda o: (o, 0, 0)),
                  pl.BlockSpec((pps, width, w2t.shape[2]), lambda o: (o, 0, 0)),
                  pl.BlockSpec((pps, S5_SEG_CHUNKS, pw.shape[2]), lambda o: (o, 0, 0)),
                  pl.BlockSpec((pps, 8, aa.shape[2]), lambda o: (o, 0, 0))],
        out_specs=pl.BlockSpec((t_n, rows, LANES), lambda o: (0, 0, o)),
        out_shape=jax.ShapeDtypeStruct((t_n, rows, d), BF16),
        scratch_shapes=[pltpu.VMEM((rows, w1.shape[2]), F32),
                        pltpu.VMEM((4, slab_rows, LANES), F32),
                        pltpu.VMEM((4, slab_rows, LANES), F32),
                        pltpu.VMEM((rows, width), BF16),
                        pltpu.VMEM((rows, 4 * LANES), BF16)],
        compiler_params=_cparams("parallel"),
        name="s5_core",
    )(u_t, w1, w2t, pw, aa)


def _s5_out_kernel(y_ref, h_ref, wglu_ref, wout_ref, g_ref, o_ref):
    g = _gelu_tanh(y_ref[0].astype(F32))
    z = jnp.dot(g.astype(BF16), wglu_ref[...], preferred_element_type=F32)
    g2 = g * _sigmoid(z)
    mix = jnp.dot(g2.astype(BF16), wout_ref[...], preferred_element_type=F32)
    o_ref[...] = h_ref[...] + _rms(mix, g_ref[...])


def _s5_out(y_t, h, w_glu, w_out, gain, rc):
    n, d = h.shape
    chunks = n // S5_CHUNK
    out = pl.pallas_call(
        _s5_out_kernel,
        grid=(chunks // rc, S5_CHUNK),
        in_specs=[pl.BlockSpec((1, rc, d), lambda i, t: (t, i, 0)),
                  pl.BlockSpec((rc, d), lambda i, t: (i, t)),
                  pl.BlockSpec((d, d), lambda i, t: (0, 0)),
                  pl.BlockSpec((d, d), lambda i, t: (0, 0)),
                  pl.BlockSpec((1, d), lambda i, t: (0, 0))],
        out_specs=pl.BlockSpec((rc, d), lambda i, t: (i, t)),
        out_shape=jax.ShapeDtypeStruct((chunks, S5_CHUNK * d), F32),
        compiler_params=_cparams("parallel", "parallel"),
        name="s5_out",
    )(y_t, h.reshape(chunks, S5_CHUNK * d), w_glu, w_out, gain.reshape(1, d))
    return out.reshape(n, d)


def _swiglu(x, wg, wu, wd, chunk):
    n_chunks = wg.shape[-1] // chunk

    def gate_up(c):
        cols = slice(c * chunk, (c + 1) * chunk)
        return (jnp.dot(x, wg[:, cols], preferred_element_type=F32),
                jnp.dot(x, wu[:, cols], preferred_element_type=F32))

    y = None
    ahead = 2
    queue = [gate_up(c) for c in range(min(ahead, n_chunks))]
    for c in range(n_chunks):
        a, u = queue.pop(0)
        if c + ahead < n_chunks:
            queue.append(gate_up(c + ahead))
        act = (a * _sigmoid(a) * u).astype(BF16)
        part = jnp.dot(act, wd[c * chunk:(c + 1) * chunk, :], preferred_element_type=F32)
        y = part if y is None else y + part
    return y


def _ffn_kernel(h_ref, gpre_ref, gpost_ref, wg_ref, wu_ref, wd_ref, o_ref, hn_s, acc_s,
                *, chunk):
    f = pl.program_id(1)

    @pl.when(f == 0)
    def _():
        hn_s[...] = _rms(h_ref[...], gpre_ref[...]).astype(BF16)
        acc_s[...] = jnp.zeros_like(acc_s)

    acc_s[...] += _swiglu(hn_s[...], wg_ref, wu_ref, wd_ref, chunk)

    @pl.when(f == pl.num_programs(1) - 1)
    def _():
        o_ref[...] = h_ref[...] + _rms(acc_s[...], gpost_ref[...])


def _ffn(h, gain_pre, gain_post, w_gate, w_up, w_down, tm, tf):
    n, d = h.shape
    d_ff = w_gate.shape[1]
    mode = pl.Buffered(1) if tf == d_ff else pl.Buffered(2)
    return pl.pallas_call(
        functools.partial(_ffn_kernel, chunk=256),
        grid=(n // tm, d_ff // tf),
        in_specs=[pl.BlockSpec((tm, d), lambda i, f: (i, 0)),
                  pl.BlockSpec((1, d), lambda i, f: (0, 0)),
                  pl.BlockSpec((1, d), lambda i, f: (0, 0)),
                  pl.BlockSpec((d, tf), lambda i, f: (0, f), pipeline_mode=mode),
                  pl.BlockSpec((d, tf), lambda i, f: (0, f), pipeline_mode=mode),
                  pl.BlockSpec((tf, d), lambda i, f: (f, 0), pipeline_mode=mode)],
        out_specs=pl.BlockSpec((tm, d), lambda i, f: (i, 0)),
        out_shape=jax.ShapeDtypeStruct((n, d), F32),
        scratch_shapes=[pltpu.VMEM((tm, d), BF16), pltpu.VMEM((tm, d), F32)],
        compiler_params=_cparams("parallel", "arbitrary"),
        name="ffn",
    )(h, gain_pre.reshape(1, d), gain_post.reshape(1, d), w_gate, w_up, w_down)


def _moe_kernel(h_ref, gpre_ref, gpost_ref, wr_ref, wg_ref, wu_ref, wd_ref, o_ref,
                hn_s, gate_s, pos_s, post_s, xc_s, yc_s, xg_s, col_s, cnt_s,
                *, n_exp, rt, strip):
    e = pl.program_id(1)
    f = pl.program_id(2)
    tb, d = hn_s.shape
    n_f = pl.num_programs(2)
    n_strips = tb // strip

    def put_counts(r, total):
        lane1 = lax.broadcasted_iota(jnp.int32, total.shape, 1)
        for x in range(n_exp):
            cnt_s[r * n_exp + x] = jnp.sum(jnp.where(lane1 == x, total, 0.0)).astype(jnp.int32)

    @pl.when((e == 0) & (f == 0))
    def _():
        before = (lax.broadcasted_iota(jnp.int32, (strip, strip), 1)
                  < lax.broadcasted_iota(jnp.int32, (strip, strip), 0)).astype(BF16)

        def route(r, total):
            put_counts(r, total)
            rows = pl.ds(pl.multiple_of(r * strip, strip), strip)
            xn = _rms(h_ref[rows, :], gpre_ref[...])
            hn_s[rows, :] = xn.astype(BF16)
            o_ref[rows, :] = jnp.zeros((strip, d), F32)
            x_hi = xn.astype(BF16)
            x_lo = (xn - x_hi.astype(F32)).astype(BF16)
            logits = (jnp.dot(x_hi, wr_ref[0], preferred_element_type=F32)
                      + jnp.dot(x_lo, wr_ref[0], preferred_element_type=F32)
                      + jnp.dot(x_hi, wr_ref[1], preferred_element_type=F32))
            lane = lax.broadcasted_iota(jnp.int32, logits.shape, 1)
            neg = jnp.float32(-jnp.inf)
            logits = jnp.where(lane < n_exp, logits, neg)
            m1 = jnp.max(logits, axis=-1, keepdims=True)
            i1 = jnp.min(jnp.where(logits == m1, lane, LANES), axis=-1, keepdims=True)
            rest = jnp.where(lane == i1, neg, logits)
            m2 = jnp.max(rest, axis=-1, keepdims=True)
            i2 = jnp.min(jnp.where(rest == m2, lane, LANES), axis=-1, keepdims=True)
            e2 = jnp.exp(m2 - m1)
            gate_s[rows, :] = (jnp.where(lane == i1, 1.0 / (1.0 + e2), 0.0)
                               + jnp.where(lane == i2, e2 / (1.0 + e2), 0.0))
            sel = ((lane == i1) | (lane == i2)).astype(F32)
            earlier = jnp.dot(before, sel.astype(BF16), preferred_element_type=F32) + total
            pos_s[rows, :] = jnp.where(sel > 0, earlier, -1.0)
            return total + jnp.sum(sel, axis=0, keepdims=True)

        total = lax.fori_loop(0, n_strips, route, jnp.zeros((1, LANES), F32))
        put_counts(n_strips, total)

        for r in range(tb // strip):
            post_s[:, r * strip:(r + 1) * strip] = pos_s[r * strip:(r + 1) * strip, :].T

    @pl.when(f == n_f - 1)
    def _():
        lane = lax.broadcasted_iota(jnp.int32, (tb, LANES), 1)
        col_s[0] = jnp.sum(jnp.where(lane == e, pos_s[...], 0.0), axis=-1, keepdims=True)
        col_s[1] = jnp.sum(jnp.where(lane == e, gate_s[...], 0.0), axis=-1, keepdims=True)

    n_tiles = (cnt_s[n_strips * n_exp + e] + rt - 1) // rt

    def tile(i, _):
        r0 = pl.multiple_of(i * rt, 16)

        def holds(r):
            return ((cnt_s[r * n_exp + e] < r0 + rt) & (cnt_s[(r + 1) * n_exp + e] > r0))

        @pl.when(f == 0)
        def _():
            xg_s[...] = jnp.zeros_like(xg_s)
            slot = (lax.broadcasted_iota(jnp.int32, (rt, strip), 0) + r0).astype(F32)
            for r in range(n_strips):
                @pl.when(holds(r))
                def _():
                    cols = slice(r * strip, (r + 1) * strip)
                    pick = (post_s[pl.ds(e, 1), cols] == slot).astype(BF16)
                    xg_s[...] += jnp.dot(pick, hn_s[cols, :], preferred_element_type=F32)
            xc_s[pl.ds(r0, rt), :] = xg_s[...].astype(BF16)

        xc = xc_s[pl.ds(r0, rt), :]
        a = jnp.dot(xc, wg_ref[0], preferred_element_type=F32)
        u = jnp.dot(xc, wu_ref[0], preferred_element_type=F32)
        act = (a * _sigmoid(a) * u).astype(BF16)
        y = jnp.dot(act, wd_ref[0], preferred_element_type=F32)
        prev = jnp.where(f == 0, 0.0, yc_s[pl.ds(r0, rt), :])
        yc_s[pl.ds(r0, rt), :] = prev + y

        @pl.when(f == n_f - 1)
        def _():
            yc = yc_s[pl.ds(r0, rt), :].astype(BF16)
            slot = (lax.broadcasted_iota(jnp.int32, (strip, rt), 1) + r0).astype(F32)
            for r in range(n_strips):
                @pl.when(holds(r))
                def _():
                    rows = slice(r * strip, (r + 1) * strip)
                    put = (col_s[0, rows, :] == slot).astype(BF16)
                    back = jnp.dot(put, yc, preferred_element_type=F32)
                    o_ref[rows, :] += col_s[1, rows, :] * back
        return 0

    lax.fori_loop(0, n_tiles, tile, 0)

    @pl.when((e == n_exp - 1) & (f == n_f - 1))
    def _():
        def finish(r, _):
            rows = pl.ds(pl.multiple_of(r * strip, strip), strip)
            o_ref[rows, :] = h_ref[rows, :] + _rms(o_ref[rows, :], gpost_ref[...])
            return 0

        lax.fori_loop(0, tb // strip, finish, 0)


def _moe(h, gain_pre, gain_post, w_router, w_gate, w_up, w_down, tb, tf, rt):
    n, d = h.shape
    n_exp, _, d_ff = w_gate.shape
    w_r = jnp.pad(w_router, ((0, 0), (0, LANES - n_exp)))
    w_hi = w_r.astype(BF16)
    w_r = jnp.stack([w_hi, (w_r - w_hi.astype(F32)).astype(BF16)])
    once = pl.Buffered(1)
    strip = 256
    cap = pl.cdiv(tb, rt) * rt
    return pl.pallas_call(
        functools.partial(_moe_kernel, n_exp=n_exp, rt=rt, strip=strip),
        grid=(n // tb, n_exp, d_ff // tf),
        in_specs=[pl.BlockSpec((tb, d), lambda i, e, f: (i, 0), pipeline_mode=once),
                  pl.BlockSpec((1, d), lambda i, e, f: (0, 0)),
                  pl.BlockSpec((1, d), lambda i, e, f: (0, 0)),
                  pl.BlockSpec((2, d, LANES), lambda i, e, f: (0, 0, 0)),
                  pl.BlockSpec((1, d, tf), lambda i, e, f: (e, 0, f)),
                  pl.BlockSpec((1, d, tf), lambda i, e, f: (e, 0, f)),
                  pl.BlockSpec((1, tf, d), lambda i, e, f: (e, f, 0))],
        out_specs=pl.BlockSpec((tb, d), lambda i, e, f: (i, 0), pipeline_mode=once),
        out_shape=jax.ShapeDtypeStruct((n, d), F32),
        scratch_shapes=[pltpu.VMEM((tb, d), BF16),
                        pltpu.VMEM((tb, LANES), F32),
                        pltpu.VMEM((tb, LANES), F32),
                        pltpu.VMEM((LANES, tb), F32),
                        pltpu.VMEM((cap, d), BF16),
                        pltpu.VMEM((cap, d), F32),
                        pltpu.VMEM((rt, d), F32),
                        pltpu.VMEM((2, tb, 1), F32),
                        pltpu.SMEM(((tb // strip + 1) * n_exp,), jnp.int32)],
        compiler_params=pltpu.CompilerParams(
            dimension_semantics=("parallel", "arbitrary", "arbitrary"),
            vmem_limit_bytes=MOE_VMEM_LIMIT_BYTES),
        name="moe_ffn",
    )(h, gain_pre.reshape(1, d), gain_post.reshape(1, d), w_r, w_gate, w_up, w_down)


def _qkv_kernel(h_ref, g_ref, w_ref, qg_ref, kg_ref, cs_ref, sn_ref, bd_ref,
                q_ref, k_ref, v_ref):
    xn = _rms(h_ref[...], g_ref[...]).astype(BF16)
    qkv = jnp.dot(xn, w_ref[...], preferred_element_type=F32)
    cs = cs_ref[...]
    sn = sn_ref[...]
    bd = bd_ref[...]
    lane = lax.broadcasted_iota(jnp.int32, cs.shape, 1)
    first_half = (lane % HEAD_DIM) < (HEAD_DIM // 2)
    scale = math.log2(math.e) / math.sqrt(HEAD_DIM)

    def norm_rope(x, gain):
        ms = jnp.dot(x * x, bd, preferred_element_type=F32)
        y = x * lax.rsqrt(ms + NORM_EPS) * gain
        partner = jnp.where(first_half,
                            pltpu.roll(y, LANES - HEAD_DIM // 2, axis=1),
                            pltpu.roll(y, HEAD_DIM // 2, axis=1))
        return y * cs + partner * sn

    n_q_tiles = N_HEADS * HEAD_DIM // LANES
    for t in range(n_q_tiles):
        y = norm_rope(qkv[:, t * LANES:(t + 1) * LANES], qg_ref[...]) * scale
        q_ref[0, 2 * t] = y[:, 0:HEAD_DIM].astype(BF16)
        q_ref[0, 2 * t + 1] = y[:, HEAD_DIM:LANES].astype(BF16)
    k0 = N_HEADS * HEAD_DIM
    for t in range(N_KV_HEADS * HEAD_DIM // LANES):
        y = norm_rope(qkv[:, k0 + t * LANES:k0 + (t + 1) * LANES], kg_ref[...])
        k_ref[0, 2 * t] = y[:, 0:HEAD_DIM].astype(BF16)
        k_ref[0, 2 * t + 1] = y[:, HEAD_DIM:LANES].astype(BF16)
    v0 = (N_HEADS + N_KV_HEADS) * HEAD_DIM
    ones = jnp.ones((qkv.shape[0], LANES - HEAD_DIM), BF16)
    for j in range(N_KV_HEADS):
        vj = qkv[:, v0 + j * HEAD_DIM:v0 + (j + 1) * HEAD_DIM].astype(BF16)
        v_ref[0, j] = jnp.concatenate([vj, ones], axis=-1)


def _rope_tables(seq):
    axis_dim = HEAD_DIM // 2
    freqs = ROPE_THETA ** (-jnp.arange(0, axis_dim, 2, dtype=F32) / axis_dim)
    rows = seq // GRID_W
    row_ang = jnp.arange(rows, dtype=F32)[:, None] * freqs
    col_ang = jnp.arange(GRID_W, dtype=F32)[:, None] * freqs
    ang = jnp.concatenate([
        jnp.broadcast_to(row_ang[:, None, :], (rows, GRID_W, freqs.shape[0])),
        jnp.broadcast_to(col_ang[None, :, :], (rows, GRID_W, freqs.shape[0]))], axis=-1)
    ang = ang.reshape(seq, HEAD_DIM // 2)
    cos, sin = jnp.cos(ang), jnp.sin(ang)
    cs = jnp.tile(jnp.concatenate([cos, cos], axis=-1), (1, LANES // HEAD_DIM))
    sn = jnp.tile(jnp.concatenate([-sin, sin], axis=-1), (1, LANES // HEAD_DIM))
    return cs, sn


def _qkv(h, gain, w_qkv, q_gain, k_gain, bsz, seq, tm):
    n, d = h.shape
    width = w_qkv.shape[1]
    perm = jnp.concatenate([jnp.arange(0, HEAD_DIM, 2), jnp.arange(1, HEAD_DIM, 2)])
    n_rot = N_HEADS + N_KV_HEADS
    cols = (jnp.arange(n_rot)[:, None] * HEAD_DIM + perm[None, :]).reshape(-1)
    cols = jnp.concatenate([cols, jnp.arange(n_rot * HEAD_DIM, width)])
    w = w_qkv[:, cols].astype(BF16)
    qg = jnp.tile(q_gain[perm], LANES // HEAD_DIM).reshape(1, LANES)
    kg = jnp.tile(k_gain[perm], LANES // HEAD_DIM).reshape(1, LANES)
    cs, sn = _rope_tables(seq)
    blk = jnp.arange(LANES) // HEAD_DIM
    bd = (blk[:, None] == blk[None, :]).astype(F32) / HEAD_DIM
    per_seq = seq // tm
    return pl.pallas_call(
        _qkv_kernel,
        grid=(n // tm,),
        in_specs=[pl.BlockSpec((tm, d), lambda i: (i, 0)),
                  pl.BlockSpec((1, d), lambda i: (0, 0)),
                  pl.BlockSpec((d, width), lambda i: (0, 0)),
                  pl.BlockSpec((1, LANES), lambda i: (0, 0)),
                  pl.BlockSpec((1, LANES), lambda i: (0, 0)),
                  pl.BlockSpec((tm, LANES), lambda i: (i % per_seq, 0)),
                  pl.BlockSpec((tm, LANES), lambda i: (i % per_seq, 0)),
                  pl.BlockSpec((LANES, LANES), lambda i: (0, 0))],
        out_specs=[pl.BlockSpec((1, N_HEADS, tm, HEAD_DIM),
                                lambda i: (i // per_seq, 0, i % per_seq, 0)),
                   pl.BlockSpec((1, N_KV_HEADS, tm, HEAD_DIM),
                                lambda i: (i // per_seq, 0, i % per_seq, 0)),
                   pl.BlockSpec((1, N_KV_HEADS, tm, LANES),
                                lambda i: (i // per_seq, 0, i % per_seq, 0))],
        out_shape=[jax.ShapeDtypeStruct((bsz, N_HEADS, seq, HEAD_DIM), BF16),
                   jax.ShapeDtypeStruct((bsz, N_KV_HEADS, seq, HEAD_DIM), BF16),
                   jax.ShapeDtypeStruct((bsz, N_KV_HEADS, seq, LANES), BF16)],
        compiler_params=_cparams("parallel"),
        name="qkv_rope",
    )(h, gain.reshape(1, d), w, qg, kg, cs, sn, bd)


def _attn_kernel(q_ref, k_ref, v_ref, o_ref, m_s, acc_s, s_buf, p_buf, a_buf,
                 *, tq, tk, rc, unroll):
    seq = k_ref.shape[2]
    chunks_per_head = tq // rc
    n_chunks = Q_PER_KV * chunks_per_head
    n_steps = (seq // tk) * n_chunks

    m_s[...] = jnp.full_like(m_s, -jnp.inf)
    acc_s[...] = jnp.zeros_like(acc_s)

    def where(n):
        c = n % n_chunks
        return (pl.multiple_of((n // n_chunks) * tk, tk), c // chunks_per_head,
                pl.multiple_of((c % chunks_per_head) * rc, rc))

    def scores(n):
        k0, g, r0 = where(n)
        q = q_ref[0, g, pl.ds(r0, rc), :]
        s_buf[...] = lax.dot_general(q, k_ref[0, 0, pl.ds(k0, tk), :], _NT,
                                     preferred_element_type=F32)

    def softmax(n):
        _, g, r0 = where(n)
        s = s_buf[...]
        m_prev = m_s[g, pl.ds(r0, rc), :]
        m_new = jnp.maximum(m_prev, jnp.max(s, axis=-1, keepdims=True))
        a_buf[...] = jnp.exp2(m_prev - m_new)
        for t in range(tk // LANES):
            p_buf[:, t * LANES:(t + 1) * LANES] = jnp.exp2(
                s[:, t * LANES:(t + 1) * LANES] - m_new).astype(BF16)
        m_s[g, pl.ds(r0, rc), :] = m_new

    def values(n):
        k0, g, r0 = where(n)
        pv = jnp.dot(p_buf[...], v_ref[0, 0, pl.ds(k0, tk), :],
                     preferred_element_type=F32)
        acc_s[g, pl.ds(r0, rc), :] = a_buf[...] * acc_s[g, pl.ds(r0, rc), :] + pv

    scores(0)
    softmax(0)
    scores(1)

    def body(n, _):
        values(n)
        softmax(n + 1)
        scores(n + 2)
        return 0

    lax.fori_loop(0, n_steps - 2, body, 0, unroll=unroll)
    values(n_steps - 2)
    softmax(n_steps - 1)
    values(n_steps - 1)

    outs = []
    for g in range(Q_PER_KV):
        acc = acc_s[g]
        o = acc / pltpu.roll(acc, HEAD_DIM, axis=1)
        outs.append(o[:, 0:HEAD_DIM])
    o_ref[...] = jnp.concatenate(outs, axis=-1).astype(o_ref.dtype)


def _attention(q, k, v, tq, tk, rc):
    bsz, _, seq, _ = q.shape
    n_q = seq // tq
    n_steps = (seq // tk) * Q_PER_KV * (tq // rc)
    unroll = min(32, n_steps - 2)
    return pl.pallas_call(
        functools.partial(_attn_kernel, tq=tq, tk=tk, rc=rc, unroll=unroll),
        grid=(bsz, N_KV_HEADS, n_q),
        in_specs=[pl.BlockSpec((1, Q_PER_KV, tq, HEAD_DIM), lambda b, j, i: (b, j, i, 0)),
                  pl.BlockSpec((1, 1, seq, HEAD_DIM), lambda b, j, i: (b, j, 0, 0)),
                  pl.BlockSpec((1, 1, seq, LANES), lambda b, j, i: (b, j, 0, 0))],
        out_specs=pl.BlockSpec((tq, Q_PER_KV * HEAD_DIM), lambda b, j, i: (b * n_q + i, j)),
        out_shape=jax.ShapeDtypeStruct((bsz * seq, N_HEADS * HEAD_DIM), BF16),
        scratch_shapes=[pltpu.VMEM((Q_PER_KV, tq, LANES), F32),
                        pltpu.VMEM((Q_PER_KV, tq, LANES), F32),
                        pltpu.VMEM((rc, tk), F32),
                        pltpu.VMEM((rc, tk), BF16),
                        pltpu.VMEM((rc, LANES), F32)],
        compiler_params=_cparams("parallel", "parallel", "parallel"),
        name="flash_attn",
    )(q, k, v)


def _proj_res_kernel(x_ref, h_ref, w_ref, g_ref, o_ref):
    mix = jnp.dot(x_ref[...], w_ref[...], preferred_element_type=F32)
    o_ref[...] = h_ref[...] + _rms(mix, g_ref[...])


def _proj_res(x, h, w, gain, tm):
    n, d = h.shape
    k = x.shape[1]
    return pl.pallas_call(
        _proj_res_kernel,
        grid=(n // tm,),
        in_specs=[pl.BlockSpec((tm, k), lambda i: (i, 0)),
                  pl.BlockSpec((tm, d), lambda i: (i, 0)),
                  pl.BlockSpec((k, d), lambda i: (0, 0)),
                  pl.BlockSpec((1, d), lambda i: (0, 0))],
        out_specs=pl.BlockSpec((tm, d), lambda i: (i, 0)),
        out_shape=jax.ShapeDtypeStruct((n, d), F32),
        compiler_params=_cparams("parallel"),
        name="proj_res",
    )(x, h, w, gain.reshape(1, d))


def _s5_layer(h, bsz, seq, gains, w_in, lam_re, lam_im, log_step, b_re, b_im, c_re, c_im,
              d_skip, w_glu, w_out):
    seg_tokens = S5_CHUNK * S5_SEG_CHUNKS
    nseg = seq // seg_tokens
    nseq = bsz * nseg
    rc = min(512, h.shape[0] // S5_CHUNK)
    u_t = _norm_matmul(h, gains[0], w_in.astype(BF16), rc=rc)
    w1, w2t, pw_p, aa_p = _s5_prep(lam_re, lam_im, log_step, b_re, b_im, c_re, c_im, d_skip)
    y_t = _s5_core(u_t, w1, w2t, pw_p, aa_p, nseq, nseg)
    return _s5_out(y_t, h, w_glu.astype(BF16), w_out.astype(BF16), gains[1], rc=rc)


def _attn_layer(h, bsz, seq, gains, w_qkv, q_gain, k_gain, w_out):
    q, k, v = _qkv(h, gains[0], w_qkv, q_gain, k_gain, bsz, seq, tm=512)
    o = _attention(q, k, v, tq=min(4096, seq), tk=512, rc=512)
    return _proj_res(o, h, w_out.astype(BF16), gains[1], tm=512)


def kernel(x, norm_gains, ssm_w_in, ssm_lambda_re, ssm_lambda_im, ssm_log_step, ssm_b_re,
           ssm_b_im, ssm_c_re, ssm_c_im, ssm_d, ssm_w_glu, ssm_w_out, ffn_w_gate, ffn_w_up,
           ffn_w_down, attn_w_qkv, attn_q_gain, attn_k_gain, attn_w_out, moe_w_router,
           moe_w_gate, moe_w_up, moe_w_down):
    bsz, seq, d = x.shape
    depth = norm_gains.shape[0]
    h = x.reshape(bsz * seq, d)
    for i in range(depth):
        j = i // 2
        g = norm_gains[i]
        if i % 2 == 0:
            h = _s5_layer(h, bsz, seq, g, ssm_w_in[j], ssm_lambda_re[j], ssm_lambda_im[j],
                          ssm_log_step[j], ssm_b_re[j], ssm_b_im[j], ssm_c_re[j], ssm_c_im[j],
                          ssm_d[j], ssm_w_glu[j], ssm_w_out[j])
            h = _ffn(h, g[2], g[3], ffn_w_gate[j].astype(BF16), ffn_w_up[j].astype(BF16),
                     ffn_w_down[j].astype(BF16), tm=1024, tf=ffn_w_gate.shape[-1])
        else:
            h = _attn_layer(h, bsz, seq, g, attn_w_qkv[j], attn_q_gain[j], attn_k_gain[j],
                            attn_w_out[j])
            h = _moe(h, g[2], g[3], moe_w_router[j], moe_w_gate[j].astype(BF16),
                     moe_w_up[j].astype(BF16), moe_w_down[j].astype(BF16),
                     tb=2048, tf=896, rt=256)
    return h.reshape(bsz, seq, d)
```

```python
import functools
import math

import jax
import jax.numpy as jnp
from jax import lax
from jax.experimental import pallas as pl
from jax.experimental.pallas import tpu as pltpu

F32 = jnp.float32
BF16 = jnp.bfloat16
NORM_EPS = 1e-6
ROPE_THETA = 10000.0
GRID_W = 64
N_HEADS = 16
N_KV_HEADS = 4
HEAD_DIM = 64
Q_PER_KV = N_HEADS // N_KV_HEADS
SSM_GROUP = 16
SSM_STATE = 64
S5_CHUNK = 16
S5_SEG_CHUNKS = 64
TOP_K = 2
LANES = 128
VMEM_LIMIT_BYTES = 56 * 1024 * 1024
MOE_VMEM_LIMIT_BYTES = 60 * 1024 * 1024

_NT = (((1,), (1,)), ((), ()))


def _cparams(*sem):
    return pltpu.CompilerParams(dimension_semantics=sem, vmem_limit_bytes=VMEM_LIMIT_BYTES)


def _rms(x, gain):
    return x * lax.rsqrt(jnp.mean(x * x, axis=-1, keepdims=True) + NORM_EPS) * gain


def _sigmoid(x):
    return 1.0 / (1.0 + jnp.exp(-x))


def _gelu_tanh(x):
    return x * (0.5 * (1.0 + jnp.tanh(math.sqrt(2.0 / math.pi) * (x + 0.044715 * (x * x * x)))))


def _norm_matmul_kernel(x_ref, g_ref, w_ref, o_ref):
    xn = _rms(x_ref[...], g_ref[...]).astype(BF16)
    o_ref[0] = jnp.dot(xn, w_ref[...], preferred_element_type=F32).astype(o_ref.dtype)


def _norm_matmul(x, gain, w, rc):
    n, d = x.shape
    m = w.shape[1]
    chunks = n // S5_CHUNK
    return pl.pallas_call(
        _norm_matmul_kernel,
        grid=(chunks // rc, S5_CHUNK),
        in_specs=[pl.BlockSpec((rc, d), lambda i, t: (i, t)),
                  pl.BlockSpec((1, d), lambda i, t: (0, 0)),
                  pl.BlockSpec((d, m), lambda i, t: (0, 0))],
        out_specs=pl.BlockSpec((1, rc, m), lambda i, t: (t, i, 0)),
        out_shape=jax.ShapeDtypeStruct((S5_CHUNK, chunks, m), BF16),
        compiler_params=_cparams("parallel", "parallel"),
        name="norm_matmul",
    )(x.reshape(chunks, S5_CHUNK * d), gain.reshape(1, d), w)


def _s5_prep_kernel(lr_ref, li_ref, ls_ref, bt_re_ref, bt_im_ref, c_re_ref, c_im_ref, dt_ref,
                    w1_ref, w2t_ref, pw_ref, aa_ref):
    t_n, s_n, p_n = S5_CHUNK, SSM_GROUP, SSM_STATE
    pair = 2 * s_n
    rows = t_n * pair
    hi = lax.Precision.HIGHEST

    rr = lax.broadcasted_iota(jnp.int32, (rows, rows), 0)
    cc = lax.broadcasted_iota(jnp.int32, (rows, rows), 1)
    same_group = ((rr // s_n) % 2) == ((cc // s_n) % 2)
    tvec = lax.broadcasted_iota(jnp.int32, (t_n, 1), 0).astype(F32)
    jvec = lax.broadcasted_iota(jnp.int32, (S5_SEG_CHUNKS, 1), 0).astype(F32)
    row8 = lax.broadcasted_iota(jnp.int32, (8, LANES), 0)
    own_lanes = ((lax.broadcasted_iota(jnp.int32, (rows, LANES), 1) // p_n)
                 == ((lax.broadcasted_iota(jnp.int32, (rows, LANES), 0) // s_n) % 2))

    def lanes_by_group(tab0, tab1):
        lane = lax.broadcasted_iota(jnp.int32, tab0.shape, 1)
        return jnp.where(lane < p_n, tab0, tab1)

    m_total = None
    for d in range(2):
        lsr, lsi, q_re, q_im = [], [], [], []
        for g in range(2):
            lr = lr_ref[0, d, g:g + 1, :]
            li = li_ref[0, d, g:g + 1, :]
            step = jnp.exp(ls_ref[0, d, g:g + 1, :])
            lsr.append(lr * step)
            lsi.append(li * step)
            mag = jnp.exp(lsr[g])
            a_re, a_im = mag * jnp.cos(lsi[g]), mag * jnp.sin(lsi[g])
            nr, ni = a_re - 1.0, a_im
            den = lr * lr + li * li
            q_re.append((nr * lr + ni * li) / den)
            q_im.append((ni * lr - nr * li) / den)

        def cpow(k, g):
            mag = jnp.exp(lsr[g] * k)
            ang = lsi[g] * k
            return mag * jnp.cos(ang), mag * jnp.sin(ang)

        def table(k):
            t0, t1 = cpow(k, 0), cpow(k, 1)
            return tuple(
                jnp.concatenate([jnp.broadcast_to(tg[part][t:t + 1, :], (s_n, LANES))
                                 for t in range(t_n) for tg in (t0, t1)], axis=0)
                for part in range(2))

        br, bi = bt_re_ref[0, d], bt_im_ref[0, d]
        qr = jnp.concatenate([jnp.broadcast_to(q_re[g], (s_n, LANES)) for g in range(2)], axis=0)
        qi = jnp.concatenate([jnp.broadcast_to(q_im[g], (s_n, LANES)) for g in range(2)], axis=0)
        bb_re = qr * br - qi * bi
        bb_im = qr * bi + qi * br
        cr, ci = c_re_ref[0, d], c_im_ref[0, d]

        def outer(x_re, x_im, k):
            pe_re, pe_im = table(k)
            xe_re = jnp.concatenate([x_re] * t_n, axis=0)
            xe_im = jnp.concatenate([x_im] * t_n, axis=0)
            return xe_re * pe_re - xe_im * pe_im, xe_re * pe_im + xe_im * pe_re

        if d == 0:
            l_re, l_im = outer(bb_re, bb_im, -tvec)
            rt_re, rt_im = outer(cr, ci, tvec)
            ws_re, ws_im = outer(bb_re, bb_im, (t_n - 1.0) - tvec)
            wy_re, wy_im = outer(cr, ci, tvec + 1.0)
            mask = (rr // pair) <= (cc // pair)
            kseg = t_n * jvec
        else:
            l_re, l_im = outer(bb_re, bb_im, tvec)
            rt_re, rt_im = outer(cr, ci, -tvec)
            ws_re, ws_im = l_re, l_im
            wy_re, wy_im = outer(cr, ci, t_n - tvec)
            mask = (rr // pair) >= (cc // pair)
            kseg = t_n * ((S5_SEG_CHUNKS - 1.0) - jvec)
        kern = (lax.dot_general(l_re[:, 0:p_n], rt_re[:, 0:p_n], _NT, precision=hi,
                                preferred_element_type=F32)
                - lax.dot_general(l_im[:, 0:p_n], rt_im[:, 0:p_n], _NT, precision=hi,
                                  preferred_element_type=F32))
        kern = jnp.where(mask & same_group, kern, 0.0)
        m_total = kern if m_total is None else m_total + kern

        ycols = rows
        for k2, val in ((2 * d, ws_re), (2 * d + 1, ws_im)):
            w1_ref[0, :, ycols + k2 * LANES:ycols + (k2 + 1) * LANES] = jnp.where(
                own_lanes, val, 0.0).astype(w1_ref.dtype)
        for k2, val in ((2 * d, wy_re), (2 * d + 1, -wy_im)):
            w2t_ref[0, :, k2 * LANES:(k2 + 1) * LANES] = jnp.where(
                own_lanes, val, 0.0).astype(w2t_ref.dtype)
        p0, p1 = cpow(kseg, 0), cpow(kseg, 1)
        pw_ref[0, :, 2 * d * LANES:(2 * d + 1) * LANES] = lanes_by_group(p0[0], p1[0])
        pw_ref[0, :, (2 * d + 1) * LANES:(2 * d + 2) * LANES] = lanes_by_group(p0[1], p1[1])
        one = jnp.ones((1, 1), F32)
        c0, c1 = cpow(float(t_n) * one, 0), cpow(float(t_n) * one, 1)
        s0, s1 = (cpow(float(t_n * S5_SEG_CHUNKS) * one, 0),
                  cpow(float(t_n * S5_SEG_CHUNKS) * one, 1))
        for part in range(2):
            chunk_pow = lanes_by_group(c0[part], c1[part])
            seg_pow = lanes_by_group(s0[part], s1[part])
            aa_ref[0, :, (2 * d + part) * LANES:(2 * d + part + 1) * LANES] = jnp.where(
                row8 == 0, chunk_pow, jnp.where(row8 == 1, seg_pow, 0.0))

    w1_ref[0, :, 0:rows] = (m_total + jnp.where(rr == cc, dt_ref[0], 0.0)).astype(w1_ref.dtype)


def _s5_prep(lam_re, lam_im, log_step, b_re, b_im, c_re, c_im, d_skip):
    g_n = lam_re.shape[1]
    pairs = g_n // 2
    pair = 2 * SSM_GROUP
    rows = S5_CHUNK * pair

    def states(x):
        x = jnp.transpose(x.reshape(2, pairs, 2, SSM_STATE), (1, 0, 2, 3))
        return jnp.tile(x, (1, 1, 1, LANES // SSM_STATE))

    def per_channel(x):
        x = jnp.transpose(x.reshape(2, pairs, pair, SSM_STATE), (1, 0, 2, 3))
        return jnp.tile(x, (1, 1, 1, LANES // SSM_STATE))

    ls = jnp.transpose(log_step.reshape(2, pairs, 2), (1, 0, 2))[..., None]
    bt_re = per_channel(jnp.swapaxes(b_re, 2, 3))
    bt_im = per_channel(jnp.swapaxes(b_im, 2, 3))
    dt = jnp.tile(d_skip.reshape(pairs, 1, pair), (1, 1, S5_CHUNK))

    def spec(*blk):
        return pl.BlockSpec((1,) + blk, lambda g: (g,) + (0,) * len(blk))

    return pl.pallas_call(
        _s5_prep_kernel,
        grid=(pairs,),
        in_specs=[spec(2, 2, LANES), spec(2, 2, LANES), spec(2, 2, 1),
                  spec(2, pair, LANES), spec(2, pair, LANES),
                  spec(2, pair, LANES), spec(2, pair, LANES), spec(1, rows)],
        out_specs=[spec(rows, rows + 4 * LANES), spec(rows, 4 * LANES),
                   spec(S5_SEG_CHUNKS, 4 * LANES), spec(8, 4 * LANES)],
        out_shape=[jax.ShapeDtypeStruct((pairs, rows, rows + 4 * LANES), BF16),
                   jax.ShapeDtypeStruct((pairs, rows, 4 * LANES), BF16),
                   jax.ShapeDtypeStruct((pairs, S5_SEG_CHUNKS, 4 * LANES), F32),
                   jax.ShapeDtypeStruct((pairs, 8, 4 * LANES), F32)],
        compiler_params=_cparams("parallel"),
        name="s5_prep",
    )(states(lam_re), states(lam_im), ls, bt_re, bt_im, per_channel(c_re), per_channel(c_im), dt)


S5_PAIR_LANES = 2 * SSM_GROUP
S5_PAIRS_PER_SLAB = LANES // S5_PAIR_LANES
S5_SEG_PITCH = S5_SEG_CHUNKS + 8
S5_SEG_BASE = 8


def _s5_core_kernel(x_ref, w1_ref, w2t_ref, pw_ref, aa_ref, y_ref, z_s, zs_s, x_s, u_s, xb_s,
                    *, nseq, nseg, strip):
    n_chunks = S5_SEG_CHUNKS
    rows = n_chunks * nseq
    ycols = u_s.shape[1]
    per_tile = LANES // S5_PAIR_LANES
    lane_grp = lax.broadcasted_iota(jnp.int32, (strip, LANES), 1) // S5_PAIR_LANES
    zero = jnp.zeros((nseq, LANES), F32)

    def seg_rows(j):
        return pl.ds(S5_SEG_BASE + j, nseq, stride=S5_SEG_PITCH)

    for q in range(S5_PAIRS_PER_SLAB):
        def gather(r, _, q=q):
            rws = pl.ds(pl.multiple_of(r * strip, strip), strip)
            for j in range(ycols // LANES):
                acc = None
                for i in range(per_tile):
                    xt = x_ref[per_tile * j + i, rws, :].astype(F32)
                    shift = (S5_PAIR_LANES * (i - q)) % LANES
                    if shift:
                        xt = pltpu.roll(xt, shift, axis=1)
                    acc = xt if acc is None else jnp.where(lane_grp == i, xt, acc)
                u_s[rws, j * LANES:(j + 1) * LANES] = acc.astype(BF16)
            return 0

        lax.fori_loop(0, rows // strip, gather, 0)
        z_s[...] = jnp.dot(u_s[...], w1_ref[q], preferred_element_type=F32)

        for b in range(nseq):
            dst = slice(S5_SEG_BASE + b * S5_SEG_PITCH, S5_SEG_BASE + b * S5_SEG_PITCH + n_chunks)
            for k in range(4):
                zs_s[k, dst, :] = z_s[b * n_chunks:(b + 1) * n_chunks,
                                      ycols + k * LANES:ycols + (k + 1) * LANES]
        x_s[0, seg_rows(0), :] = zero
        x_s[1, seg_rows(0), :] = zero
        x_s[2, seg_rows(n_chunks - 1), :] = zero
        x_s[3, seg_rows(n_chunks - 1), :] = zero

        aa = aa_ref[q]
        a_fr, a_fi = aa[0:1, 0:LANES], aa[0:1, LANES:2 * LANES]
        a_br, a_bi = aa[0:1, 2 * LANES:3 * LANES], aa[0:1, 3 * LANES:4 * LANES]
        g_fr, g_fi = aa[1:2, 0:LANES], aa[1:2, LANES:2 * LANES]
        g_br, g_bi = aa[1:2, 2 * LANES:3 * LANES], aa[1:2, 3 * LANES:4 * LANES]

        def fwd(j, carry):
            xr, xi = carry
            nr = a_fr * xr - a_fi * xi + zs_s[0, seg_rows(j), :]
            ni = a_fr * xi + a_fi * xr + zs_s[1, seg_rows(j), :]
            x_s[0, seg_rows(j + 1), :] = nr
            x_s[1, seg_rows(j + 1), :] = ni
            return nr, ni

        def bwd(i, carry):
            xr, xi = carry
            j = n_chunks - 1 - i
            nr = a_br * xr - a_bi * xi + zs_s[2, seg_rows(j), :]
            ni = a_br * xi + a_bi * xr + zs_s[3, seg_rows(j), :]
            x_s[2, seg_rows(j - 1), :] = nr
            x_s[3, seg_rows(j - 1), :] = ni
            return nr, ni

        ef_r, ef_i = lax.fori_loop(0, n_chunks, fwd, (zero, zero))
        eb_r, eb_i = lax.fori_loop(0, n_chunks, bwd, (zero, zero))

        def seg_carries(e_r, e_i, g_r, g_i, reverse):
            out_r = [None] * nseq
            out_i = [None] * nseq
            for b in range(nseq // nseg):
                c_r = jnp.zeros((1, LANES), F32)
                c_i = jnp.zeros((1, LANES), F32)
                order = range(nseg - 1, -1, -1) if reverse else range(nseg)
                for s in order:
                    r = b * nseg + s
                    out_r[r], out_i[r] = c_r, c_i
                    n_r = g_r * c_r - g_i * c_i + e_r[r:r + 1, :]
                    n_i = g_r * c_i + g_i * c_r + e_i[r:r + 1, :]
                    c_r, c_i = n_r, n_i
            return out_r, out_i

        cf_r, cf_i = seg_carries(ef_r, ef_i, g_fr, g_fi, False)
        cb_r, cb_i = seg_carries(eb_r, eb_i, g_br, g_bi, True)

        p = pw_ref[q]
        p_fr, p_fi = p[:, 0:LANES], p[:, LANES:2 * LANES]
        p_br, p_bi = p[:, 2 * LANES:3 * LANES], p[:, 3 * LANES:4 * LANES]
        for b in range(nseq):
            src = slice(S5_SEG_BASE + b * S5_SEG_PITCH, S5_SEG_BASE + b * S5_SEG_PITCH + n_chunks)
            dst = slice(b * n_chunks, (b + 1) * n_chunks)
            xb_s[dst, 0:LANES] = (x_s[0, src, :] + (p_fr * cf_r[b] - p_fi * cf_i[b])).astype(BF16)
            xb_s[dst, LANES:2 * LANES] = (
                x_s[1, src, :] + (p_fr * cf_i[b] + p_fi * cf_r[b])).astype(BF16)
            xb_s[dst, 2 * LANES:3 * LANES] = (
                x_s[2, src, :] + (p_br * cb_r[b] - p_bi * cb_i[b])).astype(BF16)
            xb_s[dst, 3 * LANES:4 * LANES] = (
                x_s[3, src, :] + (p_br * cb_i[b] + p_bi * cb_r[b])).astype(BF16)

        z_s[:, 0:ycols] += lax.dot_general(xb_s[...], w2t_ref[q], _NT,
                                           preferred_element_type=F32)

        def scatter(r, _, q=q):
            rws = pl.ds(pl.multiple_of(r * strip, strip), strip)
            for j in range(ycols // LANES):
                yq = z_s[rws, j * LANES:(j + 1) * LANES]
                for i in range(per_tile):
                    shift = (S5_PAIR_LANES * (q - i)) % LANES
                    yt = pltpu.roll(yq, shift, axis=1) if shift else yq
                    lanes_q = slice(q * S5_PAIR_LANES, (q + 1) * S5_PAIR_LANES)
                    y_ref[per_tile * j + i, rws, lanes_q] = yt[:, lanes_q].astype(y_ref.dtype)
            return 0

        lax.fori_loop(0, rows // strip, scatter, 0)


def _s5_core(u_t, w1, w2t, pw, aa, nseq, nseg):
    t_n, rows, d = u_t.shape
    width = w1.shape[1]
    pps = S5_PAIRS_PER_SLAB
    slab_rows = S5_SEG_BASE + nseq * S5_SEG_PITCH
    return pl.pallas_call(
        functools.partial(_s5_core_kernel, nseq=nseq, nseg=nseg, strip=128),
        grid=(d // LANES,),
        in_specs=[pl.BlockSpec((t_n, rows, LANES), lambda o: (0, 0, o)),
                  pl.BlockSpec((pps, width, w1.shape[2]), lambda o: (o, 0, 0)),
                  pl.BlockSpec((pps, width, w2t.shape[2]), lambda o: (o, 0, 0)),
                  pl.BlockSpec((pps, S5_SEG_CHUNKS, pw.shape[2]), lambda o: (o, 0, 0)),
                  pl.BlockSpec((pps, 8, aa.shape[2]), lambda o: (o, 0, 0))],
        out_specs=pl.BlockSpec((t_n, rows, LANES), lambda o: (0, 0, o)),
        out_shape=jax.ShapeDtypeStruct((t_n, rows, d), BF16),
        scratch_shapes=[pltpu.VMEM((rows, w1.shape[2]), F32),
                        pltpu.VMEM((4, slab_rows, LANES), F32),
                        pltpu.VMEM((4, slab_rows, LANES), F32),
                        pltpu.VMEM((rows, width), BF16),
                        pltpu.VMEM((rows, 4 * LANES), BF16)],
        compiler_params=_cparams("parallel"),
        name="s5_core",
    )(u_t, w1, w2t, pw, aa)


def _s5_out_kernel(y_ref, h_ref, wglu_ref, wout_ref, g_ref, o_ref):
    g = _gelu_tanh(y_ref[0].astype(F32))
    z = jnp.dot(g.astype(BF16), wglu_ref[...], preferred_element_type=F32)
    g2 = g * _sigmoid(z)
    mix = jnp.dot(g2.astype(BF16), wout_ref[...], preferred_element_type=F32)
    o_ref[...] = h_ref[...] + _rms(mix, g_ref[...])


def _s5_out(y_t, h, w_glu, w_out, gain, rc):
    n, d = h.shape
    chunks = n // S5_CHUNK
    out = pl.pallas_call(
        _s5_out_kernel,
        grid=(chunks // rc, S5_CHUNK),
        in_specs=[pl.BlockSpec((1, rc, d), lambda i, t: (t, i, 0)),
                  pl.BlockSpec((rc, d), lambda i, t: (i, t)),
                  pl.BlockSpec((d, d), lambda i, t: (0, 0)),
                  pl.BlockSpec((d, d), lambda i, t: (0, 0)),
                  pl.BlockSpec((1, d), lambda i, t: (0, 0))],
        out_specs=pl.BlockSpec((rc, d), lambda i, t: (i, t)),
        out_shape=jax.ShapeDtypeStruct((chunks, S5_CHUNK * d), F32),
        compiler_params=_cparams("parallel", "parallel"),
        name="s5_out",
    )(y_t, h.reshape(chunks, S5_CHUNK * d), w_glu, w_out, gain.reshape(1, d))
    return out.reshape(n, d)


def _swiglu(x, wg, wu, wd, chunk):
    n_chunks = wg.shape[-1] // chunk

    def gate_up(c):
        cols = slice(c * chunk, (c + 1) * chunk)
        return (jnp.dot(x, wg[:, cols], preferred_element_type=F32),
                jnp.dot(x, wu[:, cols], preferred_element_type=F32))

    y = None
    ahead = 2
    queue = [gate_up(c) for c in range(min(ahead, n_chunks))]
    for c in range(n_chunks):
        a, u = queue.pop(0)
        if c + ahead < n_chunks:
            queue.append(gate_up(c + ahead))
        act = (a * _sigmoid(a) * u).astype(BF16)
        part = jnp.dot(act, wd[c * chunk:(c + 1) * chunk, :], preferred_element_type=F32)
        y = part if y is None else y + part
    return y


def _ffn_kernel(h_ref, gpre_ref, gpost_ref, wg_ref, wu_ref, wd_ref, o_ref, hn_s, acc_s,
                *, chunk):
    f = pl.program_id(1)

    @pl.when(f == 0)
    def _():
        hn_s[...] = _rms(h_ref[...], gpre_ref[...]).astype(BF16)
        acc_s[...] = jnp.zeros_like(acc_s)

    acc_s[...] += _swiglu(hn_s[...], wg_ref, wu_ref, wd_ref, chunk)

    @pl.when(f == pl.num_programs(1) - 1)
    def _():
        o_ref[...] = h_ref[...] + _rms(acc_s[...], gpost_ref[...])


def _ffn(h, gain_pre, gain_post, w_gate, w_up, w_down, tm, tf):
    n, d = h.shape
    d_ff = w_gate.shape[1]
    mode = pl.Buffered(1) if tf == d_ff else pl.Buffered(2)
    return pl.pallas_call(
        functools.partial(_ffn_kernel, chunk=256),
        grid=(n // tm, d_ff // tf),
        in_specs=[pl.BlockSpec((tm, d), lambda i, f: (i, 0)),
                  pl.BlockSpec((1, d), lambda i, f: (0, 0)),
                  pl.BlockSpec((1, d), lambda i, f: (0, 0)),
                  pl.BlockSpec((d, tf), lambda i, f: (0, f), pipeline_mode=mode),
                  pl.BlockSpec((d, tf), lambda i, f: (0, f), pipeline_mode=mode),
                  pl.BlockSpec((tf, d), lambda i, f: (f, 0), pipeline_mode=mode)],
        out_specs=pl.BlockSpec((tm, d), lambda i, f: (i, 0)),
        out_shape=jax.ShapeDtypeStruct((n, d), F32),
        scratch_shapes=[pltpu.VMEM((tm, d), BF16), pltpu.VMEM((tm, d), F32)],
        compiler_params=_cparams("parallel", "arbitrary"),
        name="ffn",
    )(h, gain_pre.reshape(1, d), gain_post.reshape(1, d), w_gate, w_up, w_down)


def _moe_kernel(h_ref, gpre_ref, gpost_ref, wr_ref, wg_ref, wu_ref, wd_ref, o_ref,
                hn_s, gate_s, pos_s, post_s, xc_s, yc_s, xg_s, col_s, cnt_s,
                *, n_exp, rt, strip):
    e = pl.program_id(1)
    f = pl.program_id(2)
    tb, d = hn_s.shape
    n_f = pl.num_programs(2)
    n_strips = tb // strip

    def put_counts(r, total):
        lane1 = lax.broadcasted_iota(jnp.int32, total.shape, 1)
        for x in range(n_exp):
            cnt_s[r * n_exp + x] = jnp.sum(jnp.where(lane1 == x, total, 0.0)).astype(jnp.int32)

    @pl.when((e == 0) & (f == 0))
    def _():
        before = (lax.broadcasted_iota(jnp.int32, (strip, strip), 1)
                  < lax.broadcasted_iota(jnp.int32, (strip, strip), 0)).astype(BF16)

        def route(r, total):
            put_counts(r, total)
            rows = pl.ds(pl.multiple_of(r * strip, strip), strip)
            xn = _rms(h_ref[rows, :], gpre_ref[...])
            hn_s[rows, :] = xn.astype(BF16)
            o_ref[rows, :] = jnp.zeros((strip, d), F32)
            x_hi = xn.astype(BF16)
            x_lo = (xn - x_hi.astype(F32)).astype(BF16)
            logits = (jnp.dot(x_hi, wr_ref[0], preferred_element_type=F32)
                      + jnp.dot(x_lo, wr_ref[0], preferred_element_type=F32)
                      + jnp.dot(x_hi, wr_ref[1], preferred_element_type=F32))
            lane = lax.broadcasted_iota(jnp.int32, logits.shape, 1)
            neg = jnp.float32(-jnp.inf)
            logits = jnp.where(lane < n_exp, logits, neg)
            m1 = jnp.max(logits, axis=-1, keepdims=True)
            i1 = jnp.min(jnp.where(logits == m1, lane, LANES), axis=-1, keepdims=True)
            rest = jnp.where(lane == i1, neg, logits)
            m2 = jnp.max(rest, axis=-1, keepdims=True)
            i2 = jnp.min(jnp.where(rest == m2, lane, LANES), axis=-1, keepdims=True)
            e2 = jnp.exp(m2 - m1)
            gate_s[rows, :] = (jnp.where(lane == i1, 1.0 / (1.0 + e2), 0.0)
                               + jnp.where(lane == i2, e2 / (1.0 + e2), 0.0))
            sel = ((lane == i1) | (lane == i2)).astype(F32)
            earlier = jnp.dot(before, sel.astype(BF16), preferred_element_type=F32) + total
            pos_s[rows, :] = jnp.where(sel > 0, earlier, -1.0)
            return total + jnp.sum(sel, axis=0, keepdims=True)

        total = lax.fori_loop(0, n_strips, route, jnp.zeros((1, LANES), F32))
        put_counts(n_strips, total)

        for r in range(tb // strip):
            post_s[:, r * strip:(r + 1) * strip] = pos_s[r * strip:(r + 1) * strip, :].T

    @pl.when(f == n_f - 1)
    def _():
        lane = lax.broadcasted_iota(jnp.int32, (tb, LANES), 1)
        col_s[0] = jnp.sum(jnp.where(lane == e, pos_s[...], 0.0), axis=-1, keepdims=True)
        col_s[1] = jnp.sum(jnp.where(lane == e, gate_s[...], 0.0), axis=-1, keepdims=True)

    n_tiles = (cnt_s[n_strips * n_exp + e] + rt - 1) // rt

    def tile(i, _):
        r0 = pl.multiple_of(i * rt, 16)

        def holds(r):
            return ((cnt_s[r * n_exp + e] < r0 + rt) & (cnt_s[(r + 1) * n_exp + e] > r0))

        @pl.when(f == 0)
        def _():
            xg_s[...] = jnp.zeros_like(xg_s)
            slot = (lax.broadcasted_iota(jnp.int32, (rt, strip), 0) + r0).astype(F32)
            for r in range(n_strips):
                @pl.when(holds(r))
                def _():
                    cols = slice(r * strip, (r + 1) * strip)
                    pick = (post_s[pl.ds(e, 1), cols] == slot).astype(BF16)
                    xg_s[...] += jnp.dot(pick, hn_s[cols, :], preferred_element_type=F32)
            xc_s[pl.ds(r0, rt), :] = xg_s[...].astype(BF16)

        xc = xc_s[pl.ds(r0, rt), :]
        a = jnp.dot(xc, wg_ref[0], preferred_element_type=F32)
        u = jnp.dot(xc, wu_ref[0], preferred_element_type=F32)
        act = (a * _sigmoid(a) * u).astype(BF16)
        y = jnp.dot(act, wd_ref[0], preferred_element_type=F32)
        prev = jnp.where(f == 0, 0.0, yc_s[pl.ds(r0, rt), :])
        yc_s[pl.ds(r0, rt), :] = prev + y

        @pl.when(f == n_f - 1)
        def _():
            yc = yc_s[pl.ds(r0, rt), :].astype(BF16)
            slot = (lax.broadcasted_iota(jnp.int32, (strip, rt), 1) + r0).astype(F32)
            for r in range(n_strips):
                @pl.when(holds(r))
                def _():
                    rows = slice(r * strip, (r + 1) * strip)
                    put = (col_s[0, rows, :] == slot).astype(BF16)
                    back = jnp.dot(put, yc, preferred_element_type=F32)
                    o_ref[rows, :] += col_s[1, rows, :] * back
        return 0

    lax.fori_loop(0, n_tiles, tile, 0)

    @pl.when((e == n_exp - 1) & (f == n_f - 1))
    def _():
        def finish(r, _):
            rows = pl.ds(pl.multiple_of(r * strip, strip), strip)
            o_ref[rows, :] = h_ref[rows, :] + _rms(o_ref[rows, :], gpost_ref[...])
            return 0

        lax.fori_loop(0, tb // strip, finish, 0)


def _moe(h, gain_pre, gain_post, w_router, w_gate, w_up, w_down, tb, tf, rt):
    n, d = h.shape
    n_exp, _, d_ff = w_gate.shape
    w_r = jnp.pad(w_router, ((0, 0), (0, LANES - n_exp)))
    w_hi = w_r.astype(BF16)
    w_r = jnp.stack([w_hi, (w_r - w_hi.astype(F32)).astype(BF16)])
    once = pl.Buffered(1)
    strip = 256
    cap = pl.cdiv(tb, rt) * rt
    return pl.pallas_call(
        functools.partial(_moe_kernel, n_exp=n_exp, rt=rt, strip=strip),
        grid=(n // tb, n_exp, d_ff // tf),
        in_specs=[pl.BlockSpec((tb, d), lambda i, e, f: (i, 0), pipeline_mode=once),
                  pl.BlockSpec((1, d), lambda i, e, f: (0, 0)),
                  pl.BlockSpec((1, d), lambda i, e, f: (0, 0)),
                  pl.BlockSpec((2, d, LANES), lambda i, e, f: (0, 0, 0)),
                  pl.BlockSpec((1, d, tf), lambda i, e, f: (e, 0, f)),
                  pl.BlockSpec((1, d, tf), lambda i, e, f: (e, 0, f)),
                  pl.BlockSpec((1, tf, d), lambda i, e, f: (e, f, 0))],
        out_specs=pl.BlockSpec((tb, d), lambda i, e, f: (i, 0), pipeline_mode=once),
        out_shape=jax.ShapeDtypeStruct((n, d), F32),
        scratch_shapes=[pltpu.VMEM((tb, d), BF16),
                        pltpu.VMEM((tb, LANES), F32),
                        pltpu.VMEM((tb, LANES), F32),
                        pltpu.VMEM((LANES, tb), F32),
                        pltpu.VMEM((cap, d), BF16),
                        pltpu.VMEM((cap, d), F32),
                        pltpu.VMEM((rt, d), F32),
                        pltpu.VMEM((2, tb, 1), F32),
                        pltpu.SMEM(((tb // strip + 1) * n_exp,), jnp.int32)],
        compiler_params=pltpu.CompilerParams(
            dimension_semantics=("parallel", "arbitrary", "arbitrary"),
            vmem_limit_bytes=MOE_VMEM_LIMIT_BYTES),
        name="moe_ffn",
    )(h, gain_pre.reshape(1, d), gain_post.reshape(1, d), w_r, w_gate, w_up, w_down)


def _qkv_kernel(h_ref, g_ref, w_ref, qg_ref, kg_ref, cs_ref, sn_ref, bd_ref,
                q_ref, k_ref, v_ref):
    xn = _rms(h_ref[...], g_ref[...]).astype(BF16)
    qkv = jnp.dot(xn, w_ref[...], preferred_element_type=F32)
    cs = cs_ref[...]
    sn = sn_ref[...]
    bd = bd_ref[...]
    lane = lax.broadcasted_iota(jnp.int32, cs.shape, 1)
    first_half = (lane % HEAD_DIM) < (HEAD_DIM // 2)
    scale = math.log2(math.e) / math.sqrt(HEAD_DIM)

    def norm_rope(x, gain):
        ms = jnp.dot(x * x, bd, preferred_element_type=F32)
        y = x * lax.rsqrt(ms + NORM_EPS) * gain
        partner = jnp.where(first_half,
                            pltpu.roll(y, LANES - HEAD_DIM // 2, axis=1),
                            pltpu.roll(y, HEAD_DIM // 2, axis=1))
        return y * cs + partner * sn

    n_q_tiles = N_HEADS * HEAD_DIM // LANES
    for t in range(n_q_tiles):
        y = norm_rope(qkv[:, t * LANES:(t + 1) * LANES], qg_ref[...]) * scale
        q_ref[0, 2 * t] = y[:, 0:HEAD_DIM].astype(BF16)
        q_ref[0, 2 * t + 1] = y[:, HEAD_DIM:LANES].astype(BF16)
    k0 = N_HEADS * HEAD_DIM
    for t in range(N_KV_HEADS * HEAD_DIM // LANES):
        y = norm_rope(qkv[:, k0 + t * LANES:k0 + (t + 1) * LANES], kg_ref[...])
        k_ref[0, 2 * t] = y[:, 0:HEAD_DIM].astype(BF16)
        k_ref[0, 2 * t + 1] = y[:, HEAD_DIM:LANES].astype(BF16)
    v0 = (N_HEADS + N_KV_HEADS) * HEAD_DIM
    ones = jnp.ones((qkv.shape[0], LANES - HEAD_DIM), BF16)
    for j in range(N_KV_HEADS):
        vj = qkv[:, v0 + j * HEAD_DIM:v0 + (j + 1) * HEAD_DIM].astype(BF16)
        v_ref[0, j] = jnp.concatenate([vj, ones], axis=-1)


def _rope_tables(seq):
    axis_dim = HEAD_DIM // 2
    freqs = ROPE_THETA ** (-jnp.arange(0, axis_dim, 2, dtype=F32) / axis_dim)
    rows = seq // GRID_W
    row_ang = jnp.arange(rows, dtype=F32)[:, None] * freqs
    col_ang = jnp.arange(GRID_W, dtype=F32)[:, None] * freqs
    ang = jnp.concatenate([
        jnp.broadcast_to(row_ang[:, None, :], (rows, GRID_W, freqs.shape[0])),
        jnp.broadcast_to(col_ang[None, :, :], (rows, GRID_W, freqs.shape[0]))], axis=-1)
    ang = ang.reshape(seq, HEAD_DIM // 2)
    cos, sin = jnp.cos(ang), jnp.sin(ang)
    cs = jnp.tile(jnp.concatenate([cos, cos], axis=-1), (1, LANES // HEAD_DIM))
    sn = jnp.tile(jnp.concatenate([-sin, sin], axis=-1), (1, LANES // HEAD_DIM))
    return cs, sn


def _qkv(h, gain, w_qkv, q_gain, k_gain, bsz, seq, tm):
    n, d = h.shape
    width = w_qkv.shape[1]
    perm = jnp.concatenate([jnp.arange(0, HEAD_DIM, 2), jnp.arange(1, HEAD_DIM, 2)])
    n_rot = N_HEADS + N_KV_HEADS
    cols = (jnp.arange(n_rot)[:, None] * HEAD_DIM + perm[None, :]).reshape(-1)
    cols = jnp.concatenate([cols, jnp.arange(n_rot * HEAD_DIM, width)])
    w = w_qkv[:, cols].astype(BF16)
    qg = jnp.tile(q_gain[perm], LANES // HEAD_DIM).reshape(1, LANES)
    kg = jnp.tile(k_gain[perm], LANES // HEAD_DIM).reshape(1, LANES)
    cs, sn = _rope_tables(seq)
    blk = jnp.arange(LANES) // HEAD_DIM
    bd = (blk[:, None] == blk[None, :]).astype(F32) / HEAD_DIM
    per_seq = seq // tm
    return pl.pallas_call(
        _qkv_kernel,
        grid=(n // tm,),
        in_specs=[pl.BlockSpec((tm, d), lambda i: (i, 0)),
                  pl.BlockSpec((1, d), lambda i: (0, 0)),
                  pl.BlockSpec((d, width), lambda i: (0, 0)),
                  pl.BlockSpec((1, LANES), lambda i: (0, 0)),
                  pl.BlockSpec((1, LANES), lambda i: (0, 0)),
                  pl.BlockSpec((tm, LANES), lambda i: (i % per_seq, 0)),
                  pl.BlockSpec((tm, LANES), lambda i: (i % per_seq, 0)),
                  pl.BlockSpec((LANES, LANES), lambda i: (0, 0))],
        out_specs=[pl.BlockSpec((1, N_HEADS, tm, HEAD_DIM),
                                lambda i: (i // per_seq, 0, i % per_seq, 0)),
                   pl.BlockSpec((1, N_KV_HEADS, tm, HEAD_DIM),
                                lambda i: (i // per_seq, 0, i % per_seq, 0)),
                   pl.BlockSpec((1, N_KV_HEADS, tm, LANES),
                                lambda i: (i // per_seq, 0, i % per_seq, 0))],
        out_shape=[jax.ShapeDtypeStruct((bsz, N_HEADS, seq, HEAD_DIM), BF16),
                   jax.ShapeDtypeStruct((bsz, N_KV_HEADS, seq, HEAD_DIM), BF16),
                   jax.ShapeDtypeStruct((bsz, N_KV_HEADS, seq, LANES), BF16)],
        compiler_params=_cparams("parallel"),
        name="qkv_rope",
    )(h, gain.reshape(1, d), w, qg, kg, cs, sn, bd)


def _attn_kernel(q_ref, k_ref, v_ref, o_ref, m_s, acc_s, s_buf, p_buf, a_buf,
                 *, tq, tk, rc, unroll):
    seq = k_ref.shape[2]
    chunks_per_head = tq // rc
    n_chunks = Q_PER_KV * chunks_per_head
    n_steps = (seq // tk) * n_chunks

    m_s[...] = jnp.full_like(m_s, -jnp.inf)
    acc_s[...] = jnp.zeros_like(acc_s)

    def where(n):
        c = n % n_chunks
        return (pl.multiple_of((n // n_chunks) * tk, tk), c // chunks_per_head,
                pl.multiple_of((c % chunks_per_head) * rc, rc))

    def scores(n):
        k0, g, r0 = where(n)
        q = q_ref[0, g, pl.ds(r0, rc), :]
        s_buf[...] = lax.dot_general(q, k_ref[0, 0, pl.ds(k0, tk), :], _NT,
                                     preferred_element_type=F32)

    def softmax(n):
        _, g, r0 = where(n)
        s = s_buf[...]
        m_prev = m_s[g, pl.ds(r0, rc), :]
        m_new = jnp.maximum(m_prev, jnp.max(s, axis=-1, keepdims=True))
        a_buf[...] = jnp.exp2(m_prev - m_new)
        for t in range(tk // LANES):
            p_buf[:, t * LANES:(t + 1) * LANES] = jnp.exp2(
                s[:, t * LANES:(t + 1) * LANES] - m_new).astype(BF16)
        m_s[g, pl.ds(r0, rc), :] = m_new

    def values(n):
        k0, g, r0 = where(n)
        pv = jnp.dot(p_buf[...], v_ref[0, 0, pl.ds(k0, tk), :],
                     preferred_element_type=F32)
        acc_s[g, pl.ds(r0, rc), :] = a_buf[...] * acc_s[g, pl.ds(r0, rc), :] + pv

    scores(0)
    softmax(0)
    scores(1)

    def body(n, _):
        values(n)
        softmax(n + 1)
        scores(n + 2)
        return 0

    lax.fori_loop(0, n_steps - 2, body, 0, unroll=unroll)
    values(n_steps - 2)
    softmax(n_steps - 1)
    values(n_steps - 1)

    outs = []
    for g in range(Q_PER_KV):
        acc = acc_s[g]
        o = acc / pltpu.roll(acc, HEAD_DIM, axis=1)
        outs.append(o[:, 0:HEAD_DIM])
    o_ref[...] = jnp.concatenate(outs, axis=-1).astype(o_ref.dtype)


def _attention(q, k, v, tq, tk, rc):
    bsz, _, seq, _ = q.shape
    n_q = seq // tq
    n_steps = (seq // tk) * Q_PER_KV * (tq // rc)
    unroll = min(48, n_steps - 2)
    return pl.pallas_call(
        functools.partial(_attn_kernel, tq=tq, tk=tk, rc=rc, unroll=unroll),
        grid=(bsz, N_KV_HEADS, n_q),
        in_specs=[pl.BlockSpec((1, Q_PER_KV, tq, HEAD_DIM), lambda b, j, i: (b, j, i, 0)),
                  pl.BlockSpec((1, 1, seq, HEAD_DIM), lambda b, j, i: (b, j, 0, 0)),
                  pl.BlockSpec((1, 1, seq, LANES), lambda b, j, i: (b, j, 0, 0))],
        out_specs=pl.BlockSpec((tq, Q_PER_KV * HEAD_DIM), lambda b, j, i: (b * n_q + i, j)),
        out_shape=jax.ShapeDtypeStruct((bsz * seq, N_HEADS * HEAD_DIM), BF16),
        scratch_shapes=[pltpu.VMEM((Q_PER_KV, tq, LANES), F32),
                        pltpu.VMEM((Q_PER_KV, tq, LANES), F32),
                        pltpu.VMEM((rc, tk), F32),
                        pltpu.VMEM((rc, tk), BF16),
                        pltpu.VMEM((rc, LANES), F32)],
        compiler_params=_cparams("parallel", "parallel", "parallel"),
        name="flash_attn",
    )(q, k, v)


def _proj_res_kernel(x_ref, h_ref, w_ref, g_ref, o_ref):
    mix = jnp.dot(x_ref[...], w_ref[...], preferred_element_type=F32)
    o_ref[...] = h_ref[...] + _rms(mix, g_ref[...])


def _proj_res(x, h, w, gain, tm):
    n, d = h.shape
    k = x.shape[1]
    return pl.pallas_call(
        _proj_res_kernel,
        grid=(n // tm,),
        in_specs=[pl.BlockSpec((tm, k), lambda i: (i, 0)),
                  pl.BlockSpec((tm, d), lambda i: (i, 0)),
                  pl.BlockSpec((k, d), lambda i: (0, 0)),
                  pl.BlockSpec((1, d), lambda i: (0, 0))],
        out_specs=pl.BlockSpec((tm, d), lambda i: (i, 0)),
        out_shape=jax.ShapeDtypeStruct((n, d), F32),
        compiler_params=_cparams("parallel"),
        name="proj_res",
    )(x, h, w, gain.reshape(1, d))


def _s5_layer(h, bsz, seq, gains, w_in, lam_re, lam_im, log_step, b_re, b_im, c_re, c_im,
              d_skip, w_glu, w_out):
    seg_tokens = S5_CHUNK * S5_SEG_CHUNKS
    nseg = seq // seg_tokens
    nseq = bsz * nseg
    rc = min(512, h.shape[0] // S5_CHUNK)
    u_t = _norm_matmul(h, gains[0], w_in.astype(BF16), rc=rc)
    w1, w2t, pw_p, aa_p = _s5_prep(lam_re, lam_im, log_step, b_re, b_im, c_re, c_im, d_skip)
    y_t = _s5_core(u_t, w1, w2t, pw_p, aa_p, nseq, nseg)
    return _s5_out(y_t, h, w_glu.astype(BF16), w_out.astype(BF16), gains[1], rc=rc)


def _attn_layer(h, bsz, seq, gains, w_qkv, q_gain, k_gain, w_out):
    q, k, v = _qkv(h, gains[0], w_qkv, q_gain, k_gain, bsz, seq, tm=512)
    o = _attention(q, k, v, tq=min(4096, seq), tk=512, rc=512)
    return _proj_res(o, h, w_out.astype(BF16), gains[1], tm=512)


def kernel(x, norm_gains, ssm_w_in, ssm_lambda_re, ssm_lambda_im, ssm_log_step, ssm_b_re,
           ssm_b_im, ssm_c_re, ssm_c_im, ssm_d, ssm_w_glu, ssm_w_out, ffn_w_gate, ffn_w_up,
           ffn_w_down, attn_w_qkv, attn_q_gain, attn_k_gain, attn_w_out, moe_w_router,
           moe_w_gate, moe_w_up, moe_w_down):
    bsz, seq, d = x.shape
    depth = norm_gains.shape[0]
    h = x.reshape(bsz * seq, d)
    for i in range(depth):
        j = i // 2
        g = norm_gains[i]
        if i % 2 == 0:
            h = _s5_layer(h, bsz, seq, g, ssm_w_in[j], ssm_lambda_re[j], ssm_lambda_im[j],
                          ssm_log_step[j], ssm_b_re[j], ssm_b_im[j], ssm_c_re[j], ssm_c_im[j],
                          ssm_d[j], ssm_w_glu[j], ssm_w_out[j])
            h = _ffn(h, g[2], g[3], ffn_w_gate[j].astype(BF16), ffn_w_up[j].astype(BF16),
                     ffn_w_down[j].astype(BF16), tm=1024, tf=ffn_w_gate.shape[-1])
        else:
            h = _attn_layer(h, bsz, seq, g, attn_w_qkv[j], attn_q_gain[j], attn_k_gain[j],
                            attn_w_out[j])
            h = _moe(h, g[2], g[3], moe_w_router[j], moe_w_gate[j].astype(BF16),
                     moe_w_up[j].astype(BF16), moe_w_down[j].astype(BF16),
                     tb=2048, tf=896, rt=256)
    return h.reshape(bsz, seq, d)
```
